```python
import math
import jax
import jax.numpy as jnp
from jax import lax
import numpy as np

D_MODEL = 1024
BATCH = 8
SEQ = 4096
DEPTH = 2

N_HEADS = 8
N_KV_HEADS = 2
HEAD_DIM = 64
Q_PER_KV = N_HEADS // N_KV_HEADS
ATTN_WIDTH = N_HEADS * HEAD_DIM
KV_WIDTH = N_KV_HEADS * HEAD_DIM
N_BRANCH = 3
CMP_LEN = 32
CMP_STRIDE = 16
CMP_HIDDEN = 128
SEL_LEN = 64
SEL_TOPK = 16
N_LOCAL_SEL = 2
WINDOW = 512
Q_BLOCK = 64
CONV_WIDTH = D_MODEL // 2
CONV_KERNEL = 31
MIX_WIDTH = ATTN_WIDTH + CONV_WIDTH
IN_WIDTH = ATTN_WIDTH + 6 * KV_WIDTH + N_HEADS * N_BRANCH + 2 * CONV_WIDTH
ROPE_THETA = 500000.0
ROPE_DIM = HEAD_DIM // 4
D_FF = 7 * D_MODEL // 2
N_EXPERTS = 8
TOP_K = 2
MOE_BLOCK = 128
EPS = 1e-6

kernel_name = "hybrid_nsa_conformer_moe"


def rms_norm(x, g):
    xf = x.astype(jnp.float32)
    y = xf * lax.rsqrt(jnp.mean(xf * xf, axis=-1, keepdims=True) + EPS)
    return (y * g.astype(jnp.float32)).astype(x.dtype)


def layer_norm(x, g, b):
    xf = x.astype(jnp.float32)
    xc = xf - jnp.mean(xf, axis=-1, keepdims=True)
    y = xc * lax.rsqrt(jnp.mean(xc * xc, axis=-1, keepdims=True) + EPS)
    return (y * g.astype(jnp.float32) + b.astype(jnp.float32)).astype(x.dtype)


def masked_softmax(s, mask):
    s = jnp.where(mask, s.astype(jnp.float32), -jnp.inf)
    m = jnp.max(s, axis=-1, keepdims=True)
    m = jnp.where(jnp.isfinite(m), m, 0.0)
    p = jnp.exp(s - m)
    return p / jnp.maximum(jnp.sum(p, axis=-1, keepdims=True), jnp.finfo(jnp.float32).tiny)


def rope_partial(x, pos):
    half = ROPE_DIM // 2
    inv_freq = ROPE_THETA ** (-2.0 * jnp.arange(half, dtype=jnp.float32) / ROPE_DIM)
    ang = pos.astype(jnp.float32)[..., None, None] * inv_freq
    cos, sin = jnp.cos(ang), jnp.sin(ang)
    xr = x[..., :ROPE_DIM].astype(jnp.float32)
    x1, x2 = xr[..., :half], xr[..., half:]
    rot = jnp.concatenate([x1 * cos - x2 * sin, x2 * cos + x1 * sin], axis=-1).astype(x.dtype)
    return jnp.concatenate([rot, x[..., ROPE_DIM:]], axis=-1)


def swiglu(x, wg, wu, wd):
    return (jax.nn.silu(x @ wg) * (x @ wu)) @ wd


def in_proj_split_points():
    sizes = [ATTN_WIDTH] + [KV_WIDTH] * 6 + [N_HEADS * N_BRANCH, 2 * CONV_WIDTH]
    return [int(v) for v in np.cumsum(sizes)[:-1]]


def n_cmp_blocks(s):
    return (s - CMP_LEN) // CMP_STRIDE + 1


def cmp_block_ends(s):
    return np.arange(n_cmp_blocks(s)) * CMP_STRIDE + CMP_LEN - 1


def selection_map(s):
    c0 = np.arange(n_cmp_blocks(s)) * CMP_STRIDE
    s0 = np.arange(s // SEL_LEN) * SEL_LEN
    ov = np.minimum(c0[:, None] + CMP_LEN, s0[None, :] + SEL_LEN) - np.maximum(c0[:, None], s0[None, :])
    return jnp.asarray(np.clip(ov, 0, None) / CMP_LEN, dtype=jnp.float32)


def compress_blocks(kv, pos_emb, w1, w2):
    b, s = kv.shape[0], kv.shape[1]
    n_cmp = n_cmp_blocks(s)
    idx = np.arange(n_cmp)[:, None] * CMP_STRIDE + np.arange(CMP_LEN)[None, :]
    blocks = kv[:, idx] + pos_emb[None, None, :, None, :]
    blocks = blocks.transpose(0, 1, 3, 2, 4).reshape(b, n_cmp, N_KV_HEADS, CMP_LEN * HEAD_DIM)
    hid = jax.nn.gelu(jnp.einsum('bnkf,fh->bnkh', blocks, w1))
    return jnp.einsum('bnkh,hd->bnkd', hid, w2)


def nsa_attention(q, k_cmp, v_cmp, k_slc, v_slc, k_win, v_win, gates):
    b, s = q.shape[0], q.shape[1]
    n_sel = s // SEL_LEN
    top_n = min(SEL_TOPK, n_sel)
    scale = HEAD_DIM ** -0.5
    q = q.reshape(b, s, N_KV_HEADS, Q_PER_KV, HEAD_DIM)
    gates = gates.reshape(b, s, N_KV_HEADS, Q_PER_KV, N_BRANCH)
    cmp_end = jnp.asarray(cmp_block_ends(s))
    sel_map = selection_map(s)
    k_blk = k_slc.reshape(b, n_sel, SEL_LEN, N_KV_HEADS, HEAD_DIM).transpose(0, 3, 1, 2, 4)
    v_blk = v_slc.reshape(b, n_sel, SEL_LEN, N_KV_HEADS, HEAD_DIM).transpose(0, 3, 1, 2, 4)
    pad = ((0, 0), (WINDOW, 0), (0, 0), (0, 0))
    k_wpad = jnp.pad(k_win, pad)
    v_wpad = jnp.pad(v_win, pad)
    b_ix = jnp.arange(b)[:, None, None, None]
    h_ix = jnp.arange(N_KV_HEADS)[None, None, :, None]
    blk_ids = jnp.arange(n_sel)

    def query_block(c):
        t0 = c * Q_BLOCK
        t = t0 + jnp.arange(Q_BLOCK)
        qc = lax.dynamic_slice_in_dim(q, t0, Q_BLOCK, axis=1)
        gc = lax.dynamic_slice_in_dim(gates, t0, Q_BLOCK, axis=1)
        s_c = jnp.einsum('bqkgd,bnkd->bqkgn', qc, k_cmp) * scale
        mask_c = (cmp_end[None, :] <= t[:, None])[None, :, None, None, :]
        p_c = masked_softmax(s_c, mask_c)
        o_c = jnp.einsum('bqkgn,bnkd->bqkgd', p_c.astype(v_cmp.dtype), v_cmp)
        imp = jnp.einsum('bqkgn,nj->bqkj', p_c, sel_map)
        cur = (t // SEL_LEN)[:, None]
        causal_blk = blk_ids[None, :] <= cur
        forced = (blk_ids[None, :] == 0) | (causal_blk & (blk_ids[None, :] > cur - N_LOCAL_SEL))
        score = jnp.where(forced[None, :, None, :], jnp.inf,
                          jnp.where(causal_blk[None, :, None, :], imp, -jnp.inf))
        _, sel_idx = lax.top_k(score, top_n)
        k_sel = k_blk[b_ix, h_ix, sel_idx].reshape(b, Q_BLOCK, N_KV_HEADS, top_n * SEL_LEN, HEAD_DIM)
        v_sel = v_blk[b_ix, h_ix, sel_idx].reshape(b, Q_BLOCK, N_KV_HEADS, top_n * SEL_LEN, HEAD_DIM)
        tok = (sel_idx[..., None] * SEL_LEN + jnp.arange(SEL_LEN)).reshape(
            b, Q_BLOCK, N_KV_HEADS, 1, top_n * SEL_LEN)
        mask_s = tok <= t[None, :, None, None, None]
        s_s = jnp.einsum('bqkgd,bqkmd->bqkgm', qc, k_sel) * scale
        p_s = masked_softmax(s_s, mask_s)
        o_s = jnp.einsum('bqkgm,bqkmd->bqkgd', p_s.astype(v_sel.dtype), v_sel)
        k_w = lax.dynamic_slice_in_dim(k_wpad, t0, WINDOW + Q_BLOCK, axis=1)
        v_w = lax.dynamic_slice_in_dim(v_wpad, t0, WINDOW + Q_BLOCK, axis=1)
        src = t0 - WINDOW + jnp.arange(WINDOW + Q_BLOCK)
        mask_w = ((src[None, :] >= 0) & (src[None, :] <= t[:, None])
                  & (src[None, :] > t[:, None] - WINDOW))[None, :, None, None, :]
        s_w = jnp.einsum('bqkgd,bwkd->bqkgw', qc, k_w) * scale
        p_w = masked_softmax(s_w, mask_w)
        o_w = jnp.einsum('bqkgw,bwkd->bqkgd', p_w.astype(v_w.dtype), v_w)
        return gc[..., 0:1] * o_c + gc[..., 1:2] * o_s + gc[..., 2:3] * o_w

    out = lax.map(query_block, jnp.arange(s // Q_BLOCK))
    return out.transpose(1, 0, 2, 3, 4, 5).reshape(b, s, ATTN_WIDTH)


def conformer_conv(u, conv_w, conv_b, ln_g, ln_b):
    a, g = jnp.split(u, 2, axis=-1)
    h = a * jax.nn.sigmoid(g)
    h = jnp.pad(h, ((0, 0), (CONV_KERNEL - 1, 0), (0, 0)))
    y = lax.conv_general_dilated(h, conv_w[:, None, :], window_strides=(1,), padding='VALID',
                                 dimension_numbers=('NWC', 'WIO', 'NWC'),
                                 feature_group_count=CONV_WIDTH)
    y = y + conv_b
    return jax.nn.silu(layer_norm(y, ln_g, ln_b))


def moe_swiglu(h, router, wg, wu, wd):
    b, s, d = h.shape
    n_tok = b * s
    xt = h.reshape(n_tok, d)
    logits = (xt @ router).astype(jnp.float32)
    top_logit, top_e = lax.top_k(logits, TOP_K)
    gate = jax.nn.softmax(top_logit, axis=-1)
    n_asg = n_tok * TOP_K
    e_flat = top_e.reshape(n_asg)
    tok_flat = jnp.broadcast_to(jnp.arange(n_tok)[:, None], (n_tok, TOP_K)).reshape(n_asg)
    g_flat = gate.reshape(n_asg)
    order = jnp.argsort(e_flat)
    e_s, tok_s, g_s = e_flat[order], tok_flat[order], g_flat[order]
    counts = jnp.bincount(e_flat, length=N_EXPERTS)
    padded = (counts + MOE_BLOCK - 1) // MOE_BLOCK * MOE_BLOCK
    start = jnp.cumsum(counts) - counts
    pend = jnp.cumsum(padded)
    pstart = pend - padded
    dest = pstart[e_s] + jnp.arange(n_asg) - start[e_s]
    n_blocks = -(-n_asg // MOE_BLOCK) + N_EXPERTS
    n_slots = n_blocks * MOE_BLOCK
    slot_tok = jnp.full((n_slots,), n_tok, jnp.int32).at[dest].set(tok_s.astype(jnp.int32))
    slot_gate = jnp.zeros((n_slots,), jnp.float32).at[dest].set(g_s)
    blk_e = jnp.clip(jnp.searchsorted(pend, jnp.arange(n_blocks) * MOE_BLOCK, side='right'),
                     0, N_EXPERTS - 1)
    x_pad = jnp.concatenate([xt, jnp.zeros((1, d), xt.dtype)], axis=0)
    xs = x_pad[slot_tok].reshape(n_blocks, MOE_BLOCK, d)

    def expert_block(args):
        xb, e = args
        return swiglu(xb, wg[e], wu[e], wd[e])

    ys = lax.map(expert_block, (xs, blk_e)).reshape(n_slots, d)
    ys = ys * slot_gate[:, None].astype(ys.dtype)
    out = jax.ops.segment_sum(ys, slot_tok, num_segments=n_tok + 1)[:n_tok]
    return out.reshape(b, s, d)


def setup_inputs(seed: int = 0) -> dict:
    key = jax.random.key(seed)
    keys = list(jax.random.split(key, 32))

    def nrm(i, shape, scale):
        return scale * jax.random.normal(keys[i], shape, jnp.float32)

    n_dense = (DEPTH + 1) // 2
    n_moe = DEPTH // 2
    x = nrm(0, (BATCH, SEQ, D_MODEL), 1.0)
    offsets = jax.random.randint(keys[1], (BATCH, 1), 0, 2048, dtype=jnp.int32)
    positions = offsets + jnp.arange(SEQ, dtype=jnp.int32)[None, :]
    return {
        "x": x,
        "positions": positions,
        "attn_norm_g": 1.0 + nrm(2, (DEPTH, D_MODEL), 0.02),
        "ffn_norm_g": 1.0 + nrm(3, (DEPTH, D_MODEL), 0.02),
        "w_in": nrm(4, (DEPTH, D_MODEL, IN_WIDTH), D_MODEL ** -0.5),
        "w_out": nrm(5, (DEPTH, MIX_WIDTH, D_MODEL), MIX_WIDTH ** -0.5),
        "q_norm_g": 1.0 + nrm(6, (DEPTH, HEAD_DIM), 0.02),
        "k_norm_g": 1.0 + nrm(7, (DEPTH, N_BRANCH, HEAD_DIM), 0.02),
        "cmp_pos_k": nrm(8, (DEPTH, CMP_LEN, HEAD_DIM), 0.1),
        "cmp_w1_k": nrm(9, (DEPTH, CMP_LEN * HEAD_DIM, CMP_HIDDEN), (CMP_LEN * HEAD_DIM) ** -0.5),
        "cmp_w2_k": nrm(10, (DEPTH, CMP_HIDDEN, HEAD_DIM), CMP_HIDDEN ** -0.5),
        "cmp_pos_v": nrm(11, (DEPTH, CMP_LEN, HEAD_DIM), 0.1),
        "cmp_w1_v": nrm(12, (DEPTH, CMP_LEN * HEAD_DIM, CMP_HIDDEN), (CMP_LEN * HEAD_DIM) ** -0.5),
        "cmp_w2_v": nrm(13, (DEPTH, CMP_HIDDEN, HEAD_DIM), CMP_HIDDEN ** -0.5),
        "conv_w": nrm(14, (DEPTH, CONV_KERNEL, CONV_WIDTH), CONV_KERNEL ** -0.5),
        "conv_b": nrm(15, (DEPTH, CONV_WIDTH), 0.02),
        "conv_ln_g": 1.0 + nrm(16, (DEPTH, CONV_WIDTH), 0.02),
        "conv_ln_b": nrm(17, (DEPTH, CONV_WIDTH), 0.02),
        "ffn_w_gate": nrm(18, (n_dense, D_MODEL, D_FF), D_MODEL ** -0.5),
        "ffn_w_up": nrm(19, (n_dense, D_MODEL, D_FF), D_MODEL ** -0.5),
        "ffn_w_down": nrm(20, (n_dense, D_FF, D_MODEL), D_FF ** -0.5),
        "moe_router": nrm(21, (n_moe, D_MODEL, N_EXPERTS), D_MODEL ** -0.5),
        "moe_w_gate": nrm(22, (n_moe, N_EXPERTS, D_MODEL, D_FF), D_MODEL ** -0.5),
        "moe_w_up": nrm(23, (n_moe, N_EXPERTS, D_MODEL, D_FF), D_MODEL ** -0.5),
        "moe_w_down": nrm(24, (n_moe, N_EXPERTS, D_FF, D_MODEL), D_FF ** -0.5),
    }


def reference(x, positions, attn_norm_g, ffn_norm_g, w_in, w_out, q_norm_g, k_norm_g,
              cmp_pos_k, cmp_w1_k, cmp_w2_k, cmp_pos_v, cmp_w1_v, cmp_w2_v,
              conv_w, conv_b, conv_ln_g, conv_ln_b,
              ffn_w_gate, ffn_w_up, ffn_w_down,
              moe_router, moe_w_gate, moe_w_up, moe_w_down):
    b, s, _ = x.shape
    pos_cmp = positions[:, cmp_block_ends(s)]
    split_points = in_proj_split_points()
    for layer in range(DEPTH):
        h = rms_norm(x, attn_norm_g[layer])
        proj = jnp.einsum('bsd,de->bse', h, w_in[layer])
        q, k_c, v_c, k_s, v_s, k_w, v_w, g_logit, u = jnp.split(proj, split_points, axis=-1)
        q = rope_partial(rms_norm(q.reshape(b, s, N_HEADS, HEAD_DIM), q_norm_g[layer]), positions)
        kv_shape = (b, s, N_KV_HEADS, HEAD_DIM)
        k_cmp = compress_blocks(k_c.reshape(kv_shape), cmp_pos_k[layer], cmp_w1_k[layer], cmp_w2_k[layer])
        k_cmp = rope_partial(rms_norm(k_cmp, k_norm_g[layer, 0]), pos_cmp)
        v_cmp = compress_blocks(v_c.reshape(kv_shape), cmp_pos_v[layer], cmp_w1_v[layer], cmp_w2_v[layer])
        k_slc = rope_partial(rms_norm(k_s.reshape(kv_shape), k_norm_g[layer, 1]), positions)
        k_win = rope_partial(rms_norm(k_w.reshape(kv_shape), k_norm_g[layer, 2]), positions)
        gates = jax.nn.sigmoid(g_logit.astype(jnp.float32)).astype(x.dtype).reshape(b, s, N_HEADS, N_BRANCH)
        attn_out = nsa_attention(q, k_cmp, v_cmp, k_slc, v_s.reshape(kv_shape),
                                 k_win, v_w.reshape(kv_shape), gates)
        conv_out = conformer_conv(u, conv_w[layer], conv_b[layer], conv_ln_g[layer], conv_ln_b[layer])
        mixed = jnp.concatenate([attn_out, conv_out], axis=-1)
        x = x + jnp.einsum('bsm,md->bsd', mixed, w_out[layer])
        h = rms_norm(x, ffn_norm_g[layer])
        if layer % 2 == 0:
            i = layer // 2
            x = x + swiglu(h, ffn_w_gate[i], ffn_w_up[i], ffn_w_down[i])
        else:
            i = layer // 2
            x = x + moe_swiglu(h, moe_router[i], moe_w_gate[i], moe_w_up[i], moe_w_down[i])
    return x
```

```python
import functools
import math

import jax
import jax.numpy as jnp
import numpy as np
from jax import lax
from jax.experimental import pallas as pl
from jax.experimental.pallas import tpu as pltpu

F32 = jnp.float32
BF16 = jnp.bfloat16

N_HEADS = 8
N_KV_HEADS = 2
Q_PER_KV = N_HEADS // N_KV_HEADS
HEAD_DIM = 64
N_BRANCH = 3
CMP_LEN = 32
CMP_STRIDE = 16
SEL_LEN = 64
SEL_TOPK = 16
N_LOCAL_SEL = 2
WINDOW = 512
CONV_KERNEL = 31
ROPE_THETA = 500000.0
ROPE_DIM = HEAD_DIM // 4
TOP_K = 2
EPS = 1e-6

LANES = 128
SUBLANES = 8
HEADS_PER_VREG = LANES // HEAD_DIM
VMEM_LIMIT = 56 * 1024 * 1024

TM_PROJ = 512
TQ = 128
KC = 512
TS_CONV = 512
CH_CONV = 32
HALO = 32
TM_FFN = 1024
TF_FFN = 512


def _cparams(sem):
    return pltpu.CompilerParams(dimension_semantics=sem, vmem_limit_bytes=VMEM_LIMIT)


def _dot(a, b):
    return jnp.dot(a, b, preferred_element_type=F32)


def _dot_nt(a, b):
    return lax.dot_general(a, b, (((1,), (1,)), ((), ())), preferred_element_type=F32)


def _split_bf16(x):
    hi = x.astype(BF16)
    lo = (x - hi.astype(F32)).astype(BF16)
    return hi, lo


def _rms_rows(x, g):
    ms = jnp.mean(x * x, axis=-1, keepdims=True)
    return x * lax.rsqrt(ms + EPS) * g


def _head_block_ones():
    r = lax.broadcasted_iota(jnp.int32, (LANES, LANES), 0) // HEAD_DIM
    c = lax.broadcasted_iota(jnp.int32, (LANES, LANES), 1) // HEAD_DIM
    return jnp.where(r == c, 1.0, 0.0).astype(BF16)


def _head_norm_rope(xg, gain, cos, sin, ones_bd):
    hi, lo = _split_bf16(xg * xg)
    ms = (_dot(hi, ones_bd) + _dot(lo, ones_bd)) * (1.0 / HEAD_DIM)
    y = xg * lax.rsqrt(ms + EPS) * gain
    lane = lax.broadcasted_iota(jnp.int32, y.shape, 1) % HEAD_DIM
    half = ROPE_DIM // 2
    partner = jnp.where(lane < half, pltpu.roll(y, LANES - half, 1), pltpu.roll(y, half, 1))
    return y * cos + partner * sin


Q_W = N_HEADS * HEAD_DIM
KV_W = N_KV_HEADS * HEAD_DIM
SEG_Q = 0
SEG_KV = Q_W
SEG_GATE = SEG_KV + 6 * KV_W
SEG_UA = SEG_GATE + N_KV_HEADS * LANES
GATES_PER_KV = Q_PER_KV * N_BRANCH


def _in_proj_kernel(x_ref, g_ref, w_ref, cos_ref, sin_ref, qg_ref, ksg_ref, kwg_ref,
                    q_ref, kc_ref, vc_ref, ks_ref, vs_ref, kw_ref, vw_ref, gate_ref, glu_ref,
                    *, conv_w):
    h = _rms_rows(x_ref[...], g_ref[...]).astype(BF16)
    cos = cos_ref[...]
    sin = sin_ref[...]
    ones_bd = _head_block_ones()

    def seg(lo, width):
        return _dot(h, w_ref[:, lo:lo + width])

    def put_heads(ref, first, val):
        for j in range(HEADS_PER_VREG):
            ref[first + j] = val[:, j * HEAD_DIM:(j + 1) * HEAD_DIM].astype(ref.dtype)

    scale = HEAD_DIM ** -0.5
    qg = qg_ref[...]
    for c in range(Q_W // LANES):
        y = _head_norm_rope(seg(SEG_Q + c * LANES, LANES), qg, cos, sin, ones_bd) * scale
        put_heads(q_ref, c * HEADS_PER_VREG, y)
    kc_ref[...] = seg(SEG_KV, KV_W)
    vc_ref[...] = seg(SEG_KV + KV_W, KV_W)
    put_heads(ks_ref, 0, _head_norm_rope(seg(SEG_KV + 2 * KV_W, KV_W), ksg_ref[...], cos, sin, ones_bd))
    put_heads(vs_ref, 0, seg(SEG_KV + 3 * KV_W, KV_W))
    put_heads(kw_ref, 0, _head_norm_rope(seg(SEG_KV + 4 * KV_W, KV_W), kwg_ref[...], cos, sin, ones_bd))
    put_heads(vw_ref, 0, seg(SEG_KV + 5 * KV_W, KV_W))
    gate_ref[...] = jax.nn.sigmoid(seg(SEG_GATE, N_KV_HEADS * LANES))
    a = seg(SEG_UA, conv_w)
    g = seg(SEG_UA + conv_w, conv_w)
    glu_ref[...] = a * jax.nn.sigmoid(g)


def _in_proj(x2, g, w_perm, cos_t, sin_t, qg, ksg, kwg):
    t, d = x2.shape
    conv_w = (w_perm.shape[1] - SEG_UA) // 2
    tm = min(TM_PROJ, t)
    row = lambda i: (i, 0)
    const = lambda i: (0, 0)
    head_row = lambda i: (0, i, 0)
    out_shape = [
        jax.ShapeDtypeStruct((N_HEADS, t, HEAD_DIM), BF16),
        jax.ShapeDtypeStruct((t, KV_W), F32),
        jax.ShapeDtypeStruct((t, KV_W), F32),
        jax.ShapeDtypeStruct((N_KV_HEADS, t, HEAD_DIM), BF16),
        jax.ShapeDtypeStruct((N_KV_HEADS, t, HEAD_DIM), BF16),
        jax.ShapeDtypeStruct((N_KV_HEADS, t, HEAD_DIM), BF16),
        jax.ShapeDtypeStruct((N_KV_HEADS, t, HEAD_DIM), BF16),
        jax.ShapeDtypeStruct((t, N_KV_HEADS * LANES), F32),
        jax.ShapeDtypeStruct((t, conv_w), F32),
    ]
    kv_spec = pl.BlockSpec((N_KV_HEADS, tm, HEAD_DIM), head_row)
    out_specs = [
        pl.BlockSpec((N_HEADS, tm, HEAD_DIM), head_row),
        pl.BlockSpec((tm, KV_W), row), pl.BlockSpec((tm, KV_W), row),
        kv_spec, kv_spec, kv_spec, kv_spec,
        pl.BlockSpec((tm, N_KV_HEADS * LANES), row),
        pl.BlockSpec((tm, conv_w), row),
    ]
    in_specs = [
        pl.BlockSpec((tm, d), row), pl.BlockSpec((1, d), const),
        pl.BlockSpec(w_perm.shape, const),
        pl.BlockSpec((tm, LANES), row), pl.BlockSpec((tm, LANES), row),
        pl.BlockSpec((1, LANES), const), pl.BlockSpec((1, LANES), const), pl.BlockSpec((1, LANES), const),
    ]
    return pl.pallas_call(
        functools.partial(_in_proj_kernel, conv_w=conv_w),
        grid=(t // tm,), in_specs=in_specs, out_specs=out_specs, out_shape=out_shape,
        compiler_params=_cparams(("parallel",)), name="in_proj",
    )(x2, g, w_perm, cos_t, sin_t, qg, ksg, kwg)


def _gelu_tanh(x):
    c = math.sqrt(2.0 / math.pi)
    return 0.5 * x * (1.0 + jnp.tanh(c * (x + 0.044715 * (x * x * x))))


def _compress_kernel(k_ref, v_ref, w1ak_ref, w1bk_ref, w2k_ref, pak_ref, pbk_ref,
                     w1av_ref, w1bv_ref, w2v_ref, pav_ref, pbv_ref,
                     kg_ref, cos_ref, sin_ref, ko_ref, vo_ref):
    def mlp(x_ref, w1a_ref, w1b_ref, w2_ref, pa_ref, pb_ref):
        x = x_ref[...]
        first = _dot((x + pa_ref[...]).astype(BF16), w1a_ref[...])
        second = _dot((x + pb_ref[...]).astype(BF16), w1b_ref[...])
        n = first.shape[0]
        hid = first + pltpu.roll(second, n - 1, 0)
        return _dot(_gelu_tanh(hid).astype(BF16), w2_ref[...])

    kc = mlp(k_ref, w1ak_ref, w1bk_ref, w2k_ref, pak_ref, pbk_ref)
    kc = _head_norm_rope(kc, kg_ref[...], cos_ref[...], sin_ref[...], _head_block_ones())
    vc = mlp(v_ref, w1av_ref, w1bv_ref, w2v_ref, pav_ref, pbv_ref)
    for j in range(N_KV_HEADS):
        ko_ref[0, j] = kc[:, j * HEAD_DIM:(j + 1) * HEAD_DIM].astype(ko_ref.dtype)
        vo_ref[0, j] = vc[:, j * HEAD_DIM:(j + 1) * HEAD_DIM].astype(vo_ref.dtype)


def _compress(kc2, vc2, wk, wv, kg, cosc, sinc, b, ncp):
    feat = kc2.shape[1]
    const = lambda i: (0, 0)
    row = lambda i: (i, 0)

    def wspecs(ws):
        return [pl.BlockSpec(w.shape, const) for w in ws]

    out = jax.ShapeDtypeStruct((b, N_KV_HEADS, ncp, HEAD_DIM), BF16)
    ospec = pl.BlockSpec((1, N_KV_HEADS, ncp, HEAD_DIM), lambda i: (i, 0, 0, 0))
    return pl.pallas_call(
        _compress_kernel, grid=(b,),
        in_specs=[pl.BlockSpec((ncp, feat), row), pl.BlockSpec((ncp, feat), row)]
        + wspecs(wk) + wspecs(wv)
        + [pl.BlockSpec((1, LANES), const), pl.BlockSpec((ncp, LANES), row), pl.BlockSpec((ncp, LANES), row)],
        out_specs=[ospec, ospec], out_shape=[out, out],
        compiler_params=_cparams(("parallel",)), name="compress",
    )(kc2, vc2, *wk, *wv, kg, cosc, sinc)


def _softmax_rows(s):
    m = jnp.max(s, axis=-1, keepdims=True)
    m = jnp.where(m == -jnp.inf, 0.0, m)
    p = jnp.exp(s - m)
    l = jnp.sum(p, axis=-1, keepdims=True)
    return p * (1.0 / jnp.maximum(l, jnp.finfo(F32).tiny))


def _attn_kernel(q_ref, kc_ref, vc_ref, ks_ref, vs_ref, kw_ref, vw_ref, gate_ref, selmap_ref,
                 expand_ref, o_ref, *, seq, tq, kc_len, top_n):
    i = pl.program_id(2)
    t0 = i * tq
    rows = Q_PER_KV * tq
    n_sel = seq // SEL_LEN
    q2 = q_ref[...].reshape(rows, HEAD_DIM)
    t_row = t0 + (lax.broadcasted_iota(jnp.int32, (rows, 1), 0) & (tq - 1))

    kcmp = kc_ref[0, 0]
    ncp = kcmp.shape[0]
    s_c = _dot_nt(q2, kcmp)
    cmp_end = lax.broadcasted_iota(jnp.int32, (1, ncp), 1) * CMP_STRIDE + (CMP_LEN - 1)
    p_c = _softmax_rows(jnp.where(cmp_end <= t_row, s_c, -jnp.inf))
    o_c = _dot(p_c.astype(BF16), vc_ref[0, 0])

    p_hi, p_lo = _split_bf16(jnp.sum(p_c.reshape(Q_PER_KV, tq, ncp), axis=0))
    selmap = selmap_ref[...]
    imp = _dot_nt(selmap, p_hi) + _dot_nt(selmap, p_lo)
    blk = lax.broadcasted_iota(jnp.int32, (n_sel, tq), 0)
    cur = (t0 + lax.broadcasted_iota(jnp.int32, (n_sel, tq), 1)) // SEL_LEN
    causal_blk = blk <= cur
    forced = (blk == 0) | (causal_blk & (blk > cur - N_LOCAL_SEL))
    score = jnp.where(forced, jnp.inf, jnp.where(causal_blk, imp, -jnp.inf))
    sub = lax.broadcasted_iota(jnp.int32, (SUBLANES, tq), 0)
    groups = [score[g * SUBLANES:(g + 1) * SUBLANES, :] for g in range(n_sel // SUBLANES)]
    ranks = [jnp.zeros((SUBLANES, tq), F32) for _ in groups]
    for jp in range(n_sel):
        other = jnp.broadcast_to(score[jp:jp + 1, :], (SUBLANES, tq))
        for g, sg in enumerate(groups):
            first = g * SUBLANES
            if first > jp:
                inc = jnp.where(other >= sg, 1.0, 0.0)
            elif first + SUBLANES - 1 <= jp:
                inc = jnp.where(other > sg, 1.0, 0.0)
            else:
                inc = jnp.where(other > sg, 1.0, jnp.where((other == sg) & (sub > jp - first), 1.0, 0.0))
            ranks[g] = ranks[g] + inc
    chosen = jnp.where(jnp.concatenate(ranks, axis=0) < top_n, 1.0, 0.0)
    chosen_q = jnp.transpose(chosen).astype(BF16)

    def sel_chunk(c, carry):
        m, l, acc = carry
        k0 = pl.multiple_of(c * kc_len, kc_len)
        s = _dot_nt(q2, ks_ref[0, pl.ds(k0, kc_len), :])
        picked = _dot(chosen_q, expand_ref[c])
        key = k0 + lax.broadcasted_iota(jnp.int32, (1, kc_len), 1)
        picked = jnp.broadcast_to(picked[None], (Q_PER_KV, tq, kc_len)).reshape(rows, kc_len)
        s = jnp.where((picked > 0.5) & (key <= t_row), s, -jnp.inf)
        m_new = jnp.maximum(m, jnp.max(s, axis=-1, keepdims=True))
        m_safe = jnp.where(m_new == -jnp.inf, 0.0, m_new)
        alpha = jnp.exp(m - m_safe)
        p = jnp.exp(s - m_safe)
        l = alpha * l + jnp.sum(p, axis=-1, keepdims=True)
        acc = alpha * acc + _dot(p.astype(BF16), vs_ref[0, pl.ds(k0, kc_len), :])
        return m_new, l, acc

    n_chunks = (t0 + tq + kc_len - 1) // kc_len
    init = (jnp.full((rows, 1), -jnp.inf, F32), jnp.zeros((rows, 1), F32), jnp.zeros((rows, HEAD_DIM), F32))
    _, l_s, acc_s = lax.fori_loop(0, n_chunks, sel_chunk, init)
    o_s = acc_s * (1.0 / jnp.maximum(l_s, jnp.finfo(F32).tiny))

    span = min(WINDOW + tq, seq)
    w0 = pl.multiple_of(jnp.maximum(t0 - WINDOW, 0), tq)
    s_w = _dot_nt(q2, kw_ref[0, pl.ds(w0, span), :])
    key_w = w0 + lax.broadcasted_iota(jnp.int32, (1, span), 1)
    p_w = _softmax_rows(jnp.where((key_w <= t_row) & (key_w > t_row - WINDOW), s_w, -jnp.inf))
    o_w = _dot(p_w.astype(BF16), vw_ref[0, pl.ds(w0, span), :])

    gates = gate_ref[...]

    def gate_col(br):
        cols = [gates[:, g * N_BRANCH + br:g * N_BRANCH + br + 1] for g in range(Q_PER_KV)]
        return jnp.concatenate(cols, axis=0)

    o = gate_col(0) * o_c + gate_col(1) * o_s + gate_col(2) * o_w
    o3 = o.reshape(Q_PER_KV, tq, HEAD_DIM)
    o_ref[...] = jnp.concatenate([o3[g] for g in range(Q_PER_KV)], axis=-1).astype(o_ref.dtype)


def _attention(q, kcmp, vcmp, ks, vs, kw, vw, gates, selmap_t, expand, b, seq):
    t = b * seq
    tq = min(TQ, seq)
    kc_len = min(KC, seq)
    nq = seq // tq
    ncp = kcmp.shape[2]
    n_sel = seq // SEL_LEN
    top_n = min(SEL_TOPK, n_sel)
    cmp_spec = pl.BlockSpec((1, 1, ncp, HEAD_DIM), lambda bi, kh, i: (bi, kh, 0, 0))
    seq_spec = pl.BlockSpec((1, seq, HEAD_DIM), lambda bi, kh, i: (kh, bi, 0))
    return pl.pallas_call(
        functools.partial(_attn_kernel, seq=seq, tq=tq, kc_len=kc_len, top_n=top_n),
        grid=(b, N_KV_HEADS, nq),
        in_specs=[
            pl.BlockSpec((Q_PER_KV, tq, HEAD_DIM), lambda bi, kh, i: (kh, bi * nq + i, 0)),
            cmp_spec, cmp_spec, seq_spec, seq_spec, seq_spec, seq_spec,
            pl.BlockSpec((tq, LANES), lambda bi, kh, i: (bi * nq + i, kh)),
            pl.BlockSpec(selmap_t.shape, lambda bi, kh, i: (0, 0)),
            pl.BlockSpec(expand.shape, lambda bi, kh, i: (0, 0, 0)),
        ],
        out_specs=pl.BlockSpec((tq, Q_PER_KV * HEAD_DIM), lambda bi, kh, i: (bi * nq + i, kh)),
        out_shape=jax.ShapeDtypeStruct((t, N_HEADS * HEAD_DIM), BF16),
        compiler_params=_cparams(("parallel", "parallel", "arbitrary")), name="nsa_attention",
    )(q, kcmp, vcmp, ks, vs, kw, vw, gates, selmap_t, expand)


def _conv_kernel(glu_ref, w_ref, b_ref, lg_ref, lb_ref, o_ref, ext_ref, *, ts):
    i = pl.program_id(1)

    @pl.when(i == 0)
    def _():
        ext_ref[0:HALO, :] = jnp.zeros((HALO, ext_ref.shape[1]), F32)

    @pl.when(i > 0)
    def _():
        ext_ref[0:HALO, :] = ext_ref[ts:ts + HALO, :]

    ext_ref[HALO:HALO + ts, :] = glu_ref[...]
    w = w_ref[...]
    first_tap = HALO - (CONV_KERNEL - 1)
    for c in range(ts // CH_CONV):
        base = c * CH_CONV + first_tap
        acc = w[0:1, :] * ext_ref[base:base + CH_CONV, :]
        for k in range(1, CONV_KERNEL):
            acc = acc + w[k:k + 1, :] * ext_ref[base + k:base + k + CH_CONV, :]
        y = acc + b_ref[...]
        yc = y - jnp.mean(y, axis=-1, keepdims=True)
        yn = yc * lax.rsqrt(jnp.mean(yc * yc, axis=-1, keepdims=True) + EPS)
        z = yn * lg_ref[...] + lb_ref[...]
        o_ref[c * CH_CONV:(c + 1) * CH_CONV, :] = (z * jax.nn.sigmoid(z)).astype(o_ref.dtype)


def _conv(glu, w, bias, lg, lb, b, seq):
    t, cw = glu.shape
    ts = min(TS_CONV, seq)
    ns = seq // ts
    const = lambda bi, i: (0, 0)
    row = lambda bi, i: (bi * ns + i, 0)
    return pl.pallas_call(
        functools.partial(_conv_kernel, ts=ts), grid=(b, ns),
        in_specs=[pl.BlockSpec((ts, cw), row), pl.BlockSpec(w.shape, const),
                  pl.BlockSpec((1, cw), const), pl.BlockSpec((1, cw), const), pl.BlockSpec((1, cw), const)],
        out_specs=pl.BlockSpec((ts, cw), row),
        out_shape=jax.ShapeDtypeStruct((t, cw), BF16),
        scratch_shapes=[pltpu.VMEM((ts + HALO, cw), F32)],
        compiler_params=_cparams(("arbitrary", "arbitrary")), name="conformer_conv",
    )(glu, w, bias, lg, lb)


def _top2_gates(logits, n_experts):
    lane = lax.broadcasted_iota(jnp.int32, logits.shape, 1)
    x = jnp.where(lane < n_experts, logits, -jnp.inf)
    m1 = jnp.max(x, axis=-1, keepdims=True)
    i1 = jnp.min(jnp.where(x == m1, lane, LANES), axis=-1, keepdims=True)
    x2 = jnp.where(lane == i1, -jnp.inf, x)
    m2 = jnp.max(x2, axis=-1, keepdims=True)
    i2 = jnp.min(jnp.where(x2 == m2, lane, LANES), axis=-1, keepdims=True)
    e2 = jnp.exp(m2 - m1)
    inv = 1.0 / (1.0 + e2)
    return jnp.where(lane == i1, inv, jnp.where(lane == i2, e2 * inv, 0.0))


def _out_proj_kernel(*refs, n_experts):
    if n_experts:
        attn_ref, conv_ref, wo_ref, x_ref, g_ref, rt_ref, xo_ref, h_ref, gate_ref = refs
    else:
        attn_ref, conv_ref, wo_ref, x_ref, g_ref, xo_ref, h_ref = refs
    aw = attn_ref.shape[1]
    x = x_ref[...] + _dot(attn_ref[...], wo_ref[0:aw, :]) + _dot(conv_ref[...], wo_ref[aw:, :])
    xo_ref[...] = x
    h = _rms_rows(x, g_ref[...])
    h_ref[...] = h.astype(h_ref.dtype)
    if n_experts:
        h_hi, h_lo = _split_bf16(h)
        r_hi = rt_ref[0]
        r_lo = rt_ref[1]
        logits = _dot(h_hi, r_hi) + (_dot(h_hi, r_lo) + _dot(h_lo, r_hi))
        gate_ref[...] = _top2_gates(logits, n_experts)


def _out_proj(attn, conv, wo, x2, g, router_split=None, n_experts=0):
    t, d = x2.shape
    tm = min(TM_PROJ, t)
    row = lambda i: (i, 0)
    const = lambda i: (0, 0)
    in_specs = [pl.BlockSpec((tm, attn.shape[1]), row), pl.BlockSpec((tm, conv.shape[1]), row),
                pl.BlockSpec(wo.shape, const), pl.BlockSpec((tm, d), row), pl.BlockSpec((1, d), const)]
    out_shape = [jax.ShapeDtypeStruct((t, d), F32), jax.ShapeDtypeStruct((t, d), BF16)]
    out_specs = [pl.BlockSpec((tm, d), row), pl.BlockSpec((tm, d), row)]
    args = [attn, conv, wo, x2, g]
    if n_experts:
        in_specs.append(pl.BlockSpec(router_split.shape, lambda i: (0, 0, 0)))
        out_shape.append(jax.ShapeDtypeStruct((t, LANES), F32))
        out_specs.append(pl.BlockSpec((tm, LANES), row))
        args.append(router_split)
    return pl.pallas_call(
        functools.partial(_out_proj_kernel, n_experts=n_experts),
        grid=(t // tm,), in_specs=in_specs, out_specs=out_specs, out_shape=out_shape,
        compiler_params=_cparams(("parallel",)), name="out_proj",
    )(*args)


def _ffn_kernel(h_ref, x_ref, wg_ref, wu_ref, wd_ref, o_ref):
    f = pl.program_id(1)

    @pl.when(f == 0)
    def _():
        o_ref[...] = x_ref[...]

    h = h_ref[...]
    a = _dot(h, wg_ref[...])
    u = _dot(h, wu_ref[...])
    act = (a * jax.nn.sigmoid(a)) * u
    o_ref[...] += _dot(act.astype(BF16), wd_ref[...])


def _ffn(h, x2, wg, wu, wd):
    t, d = x2.shape
    dff = wg.shape[1]
    tm = min(TM_FFN, t)
    tf = TF_FFN
    return pl.pallas_call(
        _ffn_kernel, grid=(t // tm, dff // tf),
        in_specs=[pl.BlockSpec((tm, d), lambda i, f: (i, 0)), pl.BlockSpec((tm, d), lambda i, f: (i, 0)),
                  pl.BlockSpec((d, tf), lambda i, f: (0, f)), pl.BlockSpec((d, tf), lambda i, f: (0, f)),
                  pl.BlockSpec((tf, d), lambda i, f: (f, 0))],
        out_specs=pl.BlockSpec((tm, d), lambda i, f: (i, 0)),
        out_shape=jax.ShapeDtypeStruct((t, d), F32),
        compiler_params=_cparams(("parallel", "arbitrary")), name="ffn",
    )(h, x2, wg, wu, wd)


def _moe_kernel(h_ref, x_ref, gate_ref, wg_ref, wu_ref, wd_ref, o_ref):
    e = pl.program_id(1)
    f = pl.program_id(2)

    @pl.when((e == 0) & (f == 0))
    def _():
        o_ref[...] = x_ref[...]

    gates = gate_ref[...]
    lane = lax.broadcasted_iota(jnp.int32, gates.shape, 1)
    g_e = jnp.sum(jnp.where(lane == e, gates, 0.0), axis=-1, keepdims=True)
    h = h_ref[...]
    a = _dot(h, wg_ref[0])
    u = _dot(h, wu_ref[0])
    act = (a * jax.nn.sigmoid(a)) * u * g_e
    o_ref[...] += _dot(act.astype(BF16), wd_ref[0])


def _moe(h, x2, gates, wg, wu, wd):
    t, d = x2.shape
    n_e, _, dff = wg.shape
    tm = min(TM_FFN, t)
    tf = TF_FFN
    tok = lambda i, e, f: (i, 0)
    return pl.pallas_call(
        _moe_kernel, grid=(t // tm, n_e, dff // tf),
        in_specs=[pl.BlockSpec((tm, d), tok), pl.BlockSpec((tm, d), tok), pl.BlockSpec((tm, LANES), tok),
                  pl.BlockSpec((1, d, tf), lambda i, e, f: (e, 0, f)),
                  pl.BlockSpec((1, d, tf), lambda i, e, f: (e, 0, f)),
                  pl.BlockSpec((1, tf, d), lambda i, e, f: (e, f, 0))],
        out_specs=pl.BlockSpec((tm, d), tok),
        out_shape=jax.ShapeDtypeStruct((t, d), F32),
        compiler_params=_cparams(("parallel", "arbitrary", "arbitrary")), name="moe",
    )(h, x2, gates, wg, wu, wd)


def _rope_tables(pos):
    half = ROPE_DIM // 2
    inv_freq = ROPE_THETA ** (-2.0 * jnp.arange(half, dtype=F32) / ROPE_DIM)
    ang = pos.astype(F32).reshape(-1, 1) * inv_freq
    cos, sin = jnp.cos(ang), jnp.sin(ang)
    n = ang.shape[0]
    rest = HEAD_DIM - ROPE_DIM
    cos_h = jnp.concatenate([cos, cos, jnp.ones((n, rest), F32)], axis=-1)
    sin_h = jnp.concatenate([-sin, sin, jnp.zeros((n, rest), F32)], axis=-1)
    return jnp.tile(cos_h, (1, HEADS_PER_VREG)), jnp.tile(sin_h, (1, HEADS_PER_VREG))


def _permute_w_in(w, conv_w):
    d = w.shape[0]
    kv_end = Q_W + 6 * KV_W
    g = w[:, kv_end:kv_end + N_HEADS * N_BRANCH]
    pad = jnp.zeros((d, LANES - GATES_PER_KV), w.dtype)
    gate_cols = []
    for kh in range(N_KV_HEADS):
        gate_cols += [g[:, kh * GATES_PER_KV:(kh + 1) * GATES_PER_KV], pad]
    u = w[:, kv_end + N_HEADS * N_BRANCH:]
    return jnp.concatenate([w[:, :kv_end]] + gate_cols + [u], axis=1).astype(BF16)


def _compress_weights(pos_emb, w1, w2):
    hidden = w1.shape[1]
    eye = jnp.eye(N_KV_HEADS, dtype=w1.dtype)
    w1r = w1.reshape(CMP_LEN, HEAD_DIM, hidden)
    halves = []
    for part in (w1r[:CMP_STRIDE], w1r[CMP_STRIDE:]):
        full = jnp.einsum('ldj,hg->lhdgj', part, eye)
        halves.append(full.reshape(CMP_STRIDE * N_KV_HEADS * HEAD_DIM, N_KV_HEADS * hidden).astype(BF16))
    w2p = jnp.einsum('jd,hg->hjgd', w2, eye).reshape(N_KV_HEADS * hidden, N_KV_HEADS * HEAD_DIM).astype(BF16)
    pos = []
    for part in (pos_emb[:CMP_STRIDE], pos_emb[CMP_STRIDE:]):
        pos.append(jnp.broadcast_to(part[:, None, :], (CMP_STRIDE, N_KV_HEADS, HEAD_DIM)).reshape(1, -1))
    return [halves[0], halves[1], w2p, pos[0], pos[1]]


def _selection_map_t(seq):
    ncp = seq // CMP_STRIDE
    n_cmp = (seq - CMP_LEN) // CMP_STRIDE + 1
    c0 = np.arange(ncp) * CMP_STRIDE
    s0 = np.arange(seq // SEL_LEN) * SEL_LEN
    ov = np.minimum(c0[None, :] + CMP_LEN, s0[:, None] + SEL_LEN) - np.maximum(c0[None, :], s0[:, None])
    m = np.clip(ov, 0, None) / CMP_LEN
    m[:, n_cmp:] = 0.0
    return jnp.asarray(m, dtype=BF16)


def _expand_map(seq, kc_len):
    key_blk = np.arange(seq) // SEL_LEN
    e = (np.arange(seq // SEL_LEN)[:, None] == key_blk[None, :]).astype(np.float32)
    e = e.reshape(seq // SEL_LEN, seq // kc_len, kc_len).transpose(1, 0, 2)
    return jnp.asarray(e, dtype=BF16)


def kernel(x, positions, attn_norm_g, ffn_norm_g, w_in, w_out, q_norm_g, k_norm_g, cmp_pos_k, cmp_w1_k, cmp_w2_k, cmp_pos_v, cmp_w1_v, cmp_w2_v, conv_w, conv_b, conv_ln_g, conv_ln_b, ffn_w_gate, ffn_w_up, ffn_w_down, moe_router, moe_w_gate, moe_w_up, moe_w_down):
    b, seq, d = x.shape
    t = b * seq
    depth = w_in.shape[0]
    cw = conv_w.shape[2]
    ncp = seq // CMP_STRIDE
    n_cmp = (seq - CMP_LEN) // CMP_STRIDE + 1
    assert seq % max(TQ, KC, TS_CONV) == 0 and seq >= WINDOW + TQ

    cos_t, sin_t = _rope_tables(positions)
    cmp_end = np.minimum(np.arange(ncp) * CMP_STRIDE + CMP_LEN - 1, seq - 1)
    cos_c, sin_c = _rope_tables(positions[:, cmp_end])
    selmap_t = _selection_map_t(seq)
    expand = _expand_map(seq, min(KC, seq))
    tile2 = lambda v: jnp.tile(v.reshape(1, HEAD_DIM), (1, HEADS_PER_VREG))

    x2 = x.reshape(t, d)
    for layer in range(depth):
        w_perm = _permute_w_in(w_in[layer], cw)
        q, kc, vc, ks, vs, kw, vw, gates, glu = _in_proj(
            x2, attn_norm_g[layer].reshape(1, d), w_perm, cos_t, sin_t,
            tile2(q_norm_g[layer]), tile2(k_norm_g[layer, 1]), tile2(k_norm_g[layer, 2]))
        kcmp, vcmp = _compress(
            kc.reshape(t // CMP_STRIDE, CMP_STRIDE * KV_W), vc.reshape(t // CMP_STRIDE, CMP_STRIDE * KV_W),
            _compress_weights(cmp_pos_k[layer], cmp_w1_k[layer], cmp_w2_k[layer]),
            _compress_weights(cmp_pos_v[layer], cmp_w1_v[layer], cmp_w2_v[layer]),
            tile2(k_norm_g[layer, 0]), cos_c, sin_c, b, ncp)
        attn = _attention(q, kcmp, vcmp, ks, vs, kw, vw, gates, selmap_t, expand, b, seq)
        conv = _conv(glu, conv_w[layer], conv_b[layer].reshape(1, cw), conv_ln_g[layer].reshape(1, cw),
                     conv_ln_b[layer].reshape(1, cw), b, seq)
        wo = w_out[layer].astype(BF16)
        g2 = ffn_norm_g[layer].reshape(1, d)
        i = layer // 2
        if layer % 2 == 0:
            x2, h = _out_proj(attn, conv, wo, x2, g2)
            x2 = _ffn(h, x2, ffn_w_gate[i].astype(BF16), ffn_w_up[i].astype(BF16), ffn_w_down[i].astype(BF16))
        else:
            n_e = moe_router.shape[2]
            r = jnp.pad(moe_router[i], ((0, 0), (0, LANES - n_e)))
            r_hi = r.astype(BF16)
            r_lo = (r - r_hi.astype(F32)).astype(BF16)
            x2, h, route = _out_proj(attn, conv, wo, x2, g2, jnp.stack([r_hi, r_lo]), n_e)
            x2 = _moe(h, x2, route, moe_w_gate[i].astype(BF16), moe_w_up[i].astype(BF16),
                      moe_w_down[i].astype(BF16))
    return x2.reshape(b, seq, d)
```

```python
import functools
import math

import jax
import jax.numpy as jnp
import numpy as np
from jax import lax
from jax.experimental import pallas as pl
from jax.experimental.pallas import tpu as pltpu

F32 = jnp.float32
BF16 = jnp.bfloat16

N_HEADS = 8
N_KV_HEADS = 2
Q_PER_KV = N_HEADS // N_KV_HEADS
HEAD_DIM = 64
N_BRANCH = 3
CMP_LEN = 32
CMP_STRIDE = 16
SEL_LEN = 64
SEL_TOPK = 16
N_LOCAL_SEL = 2
WINDOW = 512
CONV_KERNEL = 31
ROPE_THETA = 500000.0
ROPE_DIM = HEAD_DIM // 4
TOP_K = 2
EPS = 1e-6

LANES = 128
SUBLANES = 8
LOG2_E = math.log2(math.e)
NEG_BIG = -(2.0 ** 100)
HEADS_PER_VREG = LANES // HEAD_DIM
VMEM_LIMIT = 56 * 1024 * 1024

TM_PROJ = 512
TQ = 128
KC = 512
TS_CONV = 512
CH_CONV = 32
HALO = 32
TM_FFN = 1024
TF_FFN = 512


def _cparams(sem):
    return pltpu.CompilerParams(dimension_semantics=sem, vmem_limit_bytes=VMEM_LIMIT)


def _dot(a, b):
    return jnp.dot(a, b, preferred_element_type=F32)


def _dot_nt(a, b):
    return lax.dot_general(a, b, (((1,), (1,)), ((), ())), preferred_element_type=F32)


def _split_bf16(x):
    hi = x.astype(BF16)
    lo = (x - hi.astype(F32)).astype(BF16)
    return hi, lo


def _rms_rows(x, g):
    ms = jnp.mean(x * x, axis=-1, keepdims=True)
    return x * lax.rsqrt(ms + EPS) * g


def _head_block_ones():
    r = lax.broadcasted_iota(jnp.int32, (LANES, LANES), 0) // HEAD_DIM
    c = lax.broadcasted_iota(jnp.int32, (LANES, LANES), 1) // HEAD_DIM
    return jnp.where(r == c, 1.0, 0.0).astype(BF16)


def _head_norm_rope(xg, gain, cos, sin, ones_bd):
    hi, lo = _split_bf16(xg * xg)
    ms = (_dot(hi, ones_bd) + _dot(lo, ones_bd)) * (1.0 / HEAD_DIM)
    y = xg * lax.rsqrt(ms + EPS) * gain
    lane = lax.broadcasted_iota(jnp.int32, y.shape, 1) % HEAD_DIM
    half = ROPE_DIM // 2
    partner = jnp.where(lane < half, pltpu.roll(y, LANES - half, 1), pltpu.roll(y, half, 1))
    return y * cos + partner * sin


Q_W = N_HEADS * HEAD_DIM
KV_W = N_KV_HEADS * HEAD_DIM
SEG_Q = 0
SEG_KV = Q_W
SEG_GATE = SEG_KV + 6 * KV_W
SEG_UA = SEG_GATE + N_KV_HEADS * LANES
GATES_PER_KV = Q_PER_KV * N_BRANCH


def _in_proj_kernel(x_ref, g_ref, w_ref, cos_ref, sin_ref, qg_ref, ksg_ref, kwg_ref,
                    q_ref, kc_ref, vc_ref, ks_ref, vs_ref, kw_ref, vw_ref, gate_ref, glu_ref,
                    *, conv_w, seq):
    h = _rms_rows(x_ref[...], g_ref[...]).astype(BF16)
    tm = h.shape[0]
    cos = cos_ref[...]
    sin = sin_ref[...]
    ones_bd = _head_block_ones()
    lane = lax.broadcasted_iota(jnp.int32, (tm, LANES), 1)
    tok = lax.rem(pl.program_id(0) * tm, seq) + lax.broadcasted_iota(jnp.int32, (tm, 1), 0)
    block_aug = jnp.where(lane - HEAD_DIM == tok // SEL_LEN, NEG_BIG, 0.0)
    ones_aug = jnp.where(lane == HEAD_DIM, 1.0, 0.0)

    def seg(lo, width):
        return _dot(h, w_ref[:, lo:lo + width])

    def put_heads(ref, first, val, aug):
        for j in range(HEADS_PER_VREG):
            head = val if j == 0 else pltpu.roll(val, LANES - j * HEAD_DIM, 1)
            ref[first + j] = jnp.where(lane < HEAD_DIM, head, aug).astype(ref.dtype)

    scale = HEAD_DIM ** -0.5 * LOG2_E
    qg = qg_ref[...]
    for c in range(Q_W // LANES):
        y = _head_norm_rope(seg(SEG_Q + c * LANES, LANES), qg, cos, sin, ones_bd) * scale
        put_heads(q_ref, c * HEADS_PER_VREG, y, 0.0)
    kc_ref[...] = seg(SEG_KV, KV_W)
    vc_ref[...] = seg(SEG_KV + KV_W, KV_W)
    put_heads(ks_ref, 0, _head_norm_rope(seg(SEG_KV + 2 * KV_W, KV_W), ksg_ref[...], cos, sin, ones_bd), block_aug)
    put_heads(vs_ref, 0, seg(SEG_KV + 3 * KV_W, KV_W), ones_aug)
    put_heads(kw_ref, 0, _head_norm_rope(seg(SEG_KV + 4 * KV_W, KV_W), kwg_ref[...], cos, sin, ones_bd), 0.0)
    put_heads(vw_ref, 0, seg(SEG_KV + 5 * KV_W, KV_W), ones_aug)
    gate_ref[...] = jax.nn.sigmoid(seg(SEG_GATE, N_KV_HEADS * LANES))
    a = seg(SEG_UA, conv_w)
    g = seg(SEG_UA + conv_w, conv_w)
    glu_ref[...] = a * jax.nn.sigmoid(g)


def _in_proj(x2, g, w_perm, cos_t, sin_t, qg, ksg, kwg, seq):
    t, d = x2.shape
    conv_w = (w_perm.shape[1] - SEG_UA) // 2
    tm = min(TM_PROJ, t)
    row = lambda i: (i, 0)
    const = lambda i: (0, 0)
    head_row = lambda i: (0, i, 0)
    out_shape = [
        jax.ShapeDtypeStruct((N_HEADS, t, LANES), BF16),
        jax.ShapeDtypeStruct((t, KV_W), F32),
        jax.ShapeDtypeStruct((t, KV_W), F32),
        jax.ShapeDtypeStruct((N_KV_HEADS, t, LANES), BF16),
        jax.ShapeDtypeStruct((N_KV_HEADS, t, LANES), BF16),
        jax.ShapeDtypeStruct((N_KV_HEADS, t, LANES), BF16),
        jax.ShapeDtypeStruct((N_KV_HEADS, t, LANES), BF16),
        jax.ShapeDtypeStruct((t, N_KV_HEADS * LANES), F32),
        jax.ShapeDtypeStruct((t, conv_w), F32),
    ]
    kv_spec = pl.BlockSpec((N_KV_HEADS, tm, LANES), head_row)
    out_specs = [
        pl.BlockSpec((N_HEADS, tm, LANES), head_row),
        pl.BlockSpec((tm, KV_W), row), pl.BlockSpec((tm, KV_W), row),
        kv_spec, kv_spec, kv_spec, kv_spec,
        pl.BlockSpec((tm, N_KV_HEADS * LANES), row),
        pl.BlockSpec((tm, conv_w), row),
    ]
    in_specs = [
        pl.BlockSpec((tm, d), row), pl.BlockSpec((1, d), const),
        pl.BlockSpec(w_perm.shape, const),
        pl.BlockSpec((tm, LANES), row), pl.BlockSpec((tm, LANES), row),
        pl.BlockSpec((1, LANES), const), pl.BlockSpec((1, LANES), const), pl.BlockSpec((1, LANES), const),
    ]
    return pl.pallas_call(
        functools.partial(_in_proj_kernel, conv_w=conv_w, seq=seq),
        grid=(t // tm,), in_specs=in_specs, out_specs=out_specs, out_shape=out_shape,
        compiler_params=_cparams(("parallel",)), name="in_proj",
    )(x2, g, w_perm, cos_t, sin_t, qg, ksg, kwg)


def _gelu_tanh(x):
    c = math.sqrt(2.0 / math.pi)
    return 0.5 * x * (1.0 + jnp.tanh(c * (x + 0.044715 * (x * x * x))))


def _compress_kernel(k_ref, v_ref, w1ak_ref, w1bk_ref, w2k_ref, pak_ref, pbk_ref,
                     w1av_ref, w1bv_ref, w2v_ref, pav_ref, pbv_ref,
                     kg_ref, cos_ref, sin_ref, ko_ref, vo_ref):
    def mlp(x_ref, w1a_ref, w1b_ref, w2_ref, pa_ref, pb_ref):
        x = x_ref[...]
        first = _dot((x + pa_ref[...]).astype(BF16), w1a_ref[...])
        second = _dot((x + pb_ref[...]).astype(BF16), w1b_ref[...])
        n = first.shape[0]
        hid = first + pltpu.roll(second, n - 1, 0)
        return _dot(_gelu_tanh(hid).astype(BF16), w2_ref[...])

    kc = mlp(k_ref, w1ak_ref, w1bk_ref, w2k_ref, pak_ref, pbk_ref)
    kc = _head_norm_rope(kc, kg_ref[...], cos_ref[...], sin_ref[...], _head_block_ones())
    vc = mlp(v_ref, w1av_ref, w1bv_ref, w2v_ref, pav_ref, pbv_ref)
    lane = lax.broadcasted_iota(jnp.int32, kc.shape, 1)
    for j in range(N_KV_HEADS):
        for val, ref in ((kc, ko_ref), (vc, vo_ref)):
            head = val if j == 0 else pltpu.roll(val, LANES - j * HEAD_DIM, 1)
            ref[0, j] = jnp.where(lane < HEAD_DIM, head, 0.0).astype(ref.dtype)


def _compress(kc2, vc2, wk, wv, kg, cosc, sinc, b, ncp):
    feat = kc2.shape[1]
    const = lambda i: (0, 0)
    row = lambda i: (i, 0)

    def wspecs(ws):
        return [pl.BlockSpec(w.shape, const) for w in ws]

    out = jax.ShapeDtypeStruct((b, N_KV_HEADS, ncp, LANES), BF16)
    ospec = pl.BlockSpec((1, N_KV_HEADS, ncp, LANES), lambda i: (i, 0, 0, 0))
    return pl.pallas_call(
        _compress_kernel, grid=(b,),
        in_specs=[pl.BlockSpec((ncp, feat), row), pl.BlockSpec((ncp, feat), row)]
        + wspecs(wk) + wspecs(wv)
        + [pl.BlockSpec((1, LANES), const), pl.BlockSpec((ncp, LANES), row), pl.BlockSpec((ncp, LANES), row)],
        out_specs=[ospec, ospec], out_shape=[out, out],
        compiler_params=_cparams(("parallel",)), name="compress",
    )(kc2, vc2, *wk, *wv, kg, cosc, sinc)


def _softmax2_rows(s):
    m = jnp.max(s, axis=-1, keepdims=True)
    m = jnp.where(m == -jnp.inf, 0.0, m)
    p = jnp.exp2(s - m)
    l = jnp.sum(p, axis=-1, keepdims=True)
    return p * (1.0 / jnp.maximum(l, jnp.finfo(F32).tiny))


def _attn_kernel(q_ref, kc_ref, vc_ref, ks_ref, vs_ref, kw_ref, vw_ref, gate_ref, selmap_ref,
                 o_ref, *, seq, tq, kc_len, top_n):
    i = pl.program_id(2)
    t0 = i * tq
    rows = Q_PER_KV * tq
    n_sel = seq // SEL_LEN
    q2 = q_ref[...].reshape(rows, LANES)
    t_row = t0 + (lax.broadcasted_iota(jnp.int32, (rows, 1), 0) & (tq - 1))
    t_tok = t0 + lax.broadcasted_iota(jnp.int32, (tq, 1), 0)

    kcmp = kc_ref[0, 0]
    ncp = kcmp.shape[0]
    s_c = _dot_nt(q2, kcmp)
    cmp_end = lax.broadcasted_iota(jnp.int32, (1, ncp), 1) * CMP_STRIDE + (CMP_LEN - 1)
    p_c = _softmax2_rows(jnp.where(cmp_end <= t_row, s_c, -jnp.inf))
    o_c = _dot(p_c.astype(BF16), vc_ref[0, 0])

    p_hi, p_lo = _split_bf16(jnp.sum(p_c.reshape(Q_PER_KV, tq, ncp), axis=0))
    selmap = selmap_ref[...]
    imp = _dot_nt(selmap, p_hi) + _dot_nt(selmap, p_lo)
    blk = lax.broadcasted_iota(jnp.int32, (n_sel, tq), 0)
    cur = (t0 + lax.broadcasted_iota(jnp.int32, (n_sel, tq), 1)) // SEL_LEN
    causal_blk = blk <= cur
    forced = (blk == 0) | (causal_blk & (blk > cur - N_LOCAL_SEL))
    score = jnp.where(forced, jnp.inf, jnp.where(causal_blk, imp, -jnp.inf))
    sub = lax.broadcasted_iota(jnp.int32, (SUBLANES, tq), 0)
    groups = [score[g * SUBLANES:(g + 1) * SUBLANES, :] for g in range(n_sel // SUBLANES)]
    ranks = [jnp.zeros((SUBLANES, tq), F32) for _ in groups]
    for jp in range(n_sel):
        other = jnp.broadcast_to(score[jp:jp + 1, :], (SUBLANES, tq))
        for g, sg in enumerate(groups):
            first = g * SUBLANES
            if first > jp:
                inc = jnp.where(other >= sg, 1.0, 0.0)
            elif first + SUBLANES - 1 <= jp:
                inc = jnp.where(other > sg, 1.0, 0.0)
            else:
                inc = jnp.where(other > sg, 1.0, jnp.where((other == sg) & (sub > jp - first), 1.0, 0.0))
            ranks[g] = ranks[g] + inc
    not_chosen = jnp.where(jnp.concatenate(ranks, axis=0) < top_n, 0.0, 1.0)
    flag_rows = [jnp.zeros((HEAD_DIM, tq), F32), not_chosen]
    if HEAD_DIM + n_sel < LANES:
        flag_rows.append(jnp.zeros((LANES - HEAD_DIM - n_sel, tq), F32))
    flags = jnp.transpose(jnp.concatenate(flag_rows, axis=0))
    q_sel = (q2.reshape(Q_PER_KV, tq, LANES) + flags.astype(BF16)[None]).reshape(rows, LANES)

    def masked_scores(s, bias):
        return (s.reshape(Q_PER_KV, tq, s.shape[-1]) + bias[None]).reshape(s.shape)

    def denominator(acc):
        return acc[:, HEAD_DIM:HEAD_DIM + 1]

    def sel_chunk(k0, carry, bias):
        m, acc = carry
        s = _dot_nt(q_sel, ks_ref[0, pl.ds(k0, kc_len), :])
        if bias is not None:
            s = masked_scores(s, bias)
        m_new = jnp.maximum(m, jnp.max(s, axis=-1, keepdims=True))
        p = jnp.exp2((s - m_new).astype(BF16))
        acc = jnp.exp2(m - m_new) * acc + _dot(p, vs_ref[0, pl.ds(k0, kc_len), :])
        return m_new, acc

    n_full = t0 // kc_len
    init = (jnp.full((rows, 1), -jnp.inf, F32), jnp.zeros((rows, LANES), F32))
    carry = lax.fori_loop(
        0, n_full, lambda c, cr: sel_chunk(pl.multiple_of(c * kc_len, kc_len), cr, None), init)
    kd = pl.multiple_of(n_full * kc_len, kc_len)
    key_d = kd + lax.broadcasted_iota(jnp.int32, (1, kc_len), 1)
    _, acc_s = sel_chunk(kd, carry, jnp.where(key_d <= t_tok, 0.0, NEG_BIG))
    o_s = acc_s * (1.0 / denominator(acc_s))

    span = min(WINDOW + tq, seq)
    w0 = pl.multiple_of(jnp.maximum(t0 - WINDOW, 0), tq)
    key_w = w0 + lax.broadcasted_iota(jnp.int32, (1, span), 1)
    bias_w = jnp.where((key_w <= t_tok) & (key_w > t_tok - WINDOW), 0.0, NEG_BIG)
    s_w = masked_scores(_dot_nt(q2, kw_ref[0, pl.ds(w0, span), :]), bias_w)
    p_w = jnp.exp2((s_w - jnp.max(s_w, axis=-1, keepdims=True)).astype(BF16))
    acc_w = _dot(p_w, vw_ref[0, pl.ds(w0, span), :])
    o_w = acc_w * (1.0 / denominator(acc_w))

    gates = gate_ref[...]

    def gate_col(br):
        cols = [gates[:, g * N_BRANCH + br:g * N_BRANCH + br + 1] for g in range(Q_PER_KV)]
        return jnp.concatenate(cols, axis=0)

    o = gate_col(0) * o_c + gate_col(1) * o_s + gate_col(2) * o_w
    o3 = o.reshape(Q_PER_KV, tq, LANES)
    lane = lax.broadcasted_iota(jnp.int32, (tq, LANES), 1)
    pairs = [jnp.where(lane < HEAD_DIM, o3[g], pltpu.roll(o3[g + 1], HEAD_DIM, 1))
             for g in range(0, Q_PER_KV, HEADS_PER_VREG)]
    o_ref[...] = jnp.concatenate(pairs, axis=-1).astype(o_ref.dtype)


def _attention(q, kcmp, vcmp, ks, vs, kw, vw, gates, selmap_t, b, seq):
    t = b * seq
    tq = min(TQ, seq)
    kc_len = min(KC, seq)
    nq = seq // tq
    ncp = kcmp.shape[2]
    n_sel = seq // SEL_LEN
    assert HEAD_DIM + n_sel <= LANES, "selection-block flags must fit beside the head dims"
    top_n = min(SEL_TOPK, n_sel)
    cmp_spec = pl.BlockSpec((1, 1, ncp, LANES), lambda bi, kh, i: (bi, kh, 0, 0))
    seq_spec = pl.BlockSpec((1, seq, LANES), lambda bi, kh, i: (kh, bi, 0))
    return pl.pallas_call(
        functools.partial(_attn_kernel, seq=seq, tq=tq, kc_len=kc_len, top_n=top_n),
        grid=(b, N_KV_HEADS, nq),
        in_specs=[
            pl.BlockSpec((Q_PER_KV, tq, LANES), lambda bi, kh, i: (kh, bi * nq + i, 0)),
            cmp_spec, cmp_spec, seq_spec, seq_spec, seq_spec, seq_spec,
            pl.BlockSpec((tq, LANES), lambda bi, kh, i: (bi * nq + i, kh)),
            pl.BlockSpec(selmap_t.shape, lambda bi, kh, i: (0, 0)),
        ],
        out_specs=pl.BlockSpec((tq, Q_PER_KV * HEAD_DIM), lambda bi, kh, i: (bi * nq + i, kh)),
        out_shape=jax.ShapeDtypeStruct((t, N_HEADS * HEAD_DIM), BF16),
        compiler_params=_cparams(("parallel", "parallel", "arbitrary")), name="nsa_attention",
    )(q, kcmp, vcmp, ks, vs, kw, vw, gates, selmap_t)


def _conv_kernel(glu_ref, w_ref, b_ref, lg_ref, lb_ref, o_ref, ext_ref, *, ts):
    i = pl.program_id(1)

    @pl.when(i == 0)
    def _():
        ext_ref[0:HALO, :] = jnp.zeros((HALO, ext_ref.shape[1]), F32)

    @pl.when(i > 0)
    def _():
        ext_ref[0:HALO, :] = ext_ref[ts:ts + HALO, :]

    ext_ref[HALO:HALO + ts, :] = glu_ref[...]
    w = w_ref[...]
    first_tap = HALO - (CONV_KERNEL - 1)
    for c in range(ts // CH_CONV):
        base = c * CH_CONV + first_tap
        acc = w[0:1, :] * ext_ref[base:base + CH_CONV, :]
        for k in range(1, CONV_KERNEL):
            acc = acc + w[k:k + 1, :] * ext_ref[base + k:base + k + CH_CONV, :]
        y = acc + b_ref[...]
        yc = y - jnp.mean(y, axis=-1, keepdims=True)
        yn = yc * lax.rsqrt(jnp.mean(yc * yc, axis=-1, keepdims=True) + EPS)
        z = yn * lg_ref[...] + lb_ref[...]
        o_ref[c * CH_CONV:(c + 1) * CH_CONV, :] = (z * jax.nn.sigmoid(z)).astype(o_ref.dtype)


def _conv(glu, w, bias, lg, lb, b, seq):
    t, cw = glu.shape
    ts = min(TS_CONV, seq)
    ns = seq // ts
    const = lambda bi, i: (0, 0)
    row = lambda bi, i: (bi * ns + i, 0)
    return pl.pallas_call(
        functools.partial(_conv_kernel, ts=ts), grid=(b, ns),
        in_specs=[pl.BlockSpec((ts, cw), row), pl.BlockSpec(w.shape, const),
                  pl.BlockSpec((1, cw), const), pl.BlockSpec((1, cw), const), pl.BlockSpec((1, cw), const)],
        out_specs=pl.BlockSpec((ts, cw), row),
        out_shape=jax.ShapeDtypeStruct((t, cw), BF16),
        scratch_shapes=[pltpu.VMEM((ts + HALO, cw), F32)],
        compiler_params=_cparams(("arbitrary", "arbitrary")), name="conformer_conv",
    )(glu, w, bias, lg, lb)


def _top2_gates(logits, n_experts):
    lane = lax.broadcasted_iota(jnp.int32, logits.shape, 1)
    x = jnp.where(lane < n_experts, logits, -jnp.inf)
    m1 = jnp.max(x, axis=-1, keepdims=True)
    i1 = jnp.min(jnp.where(x == m1, lane, LANES), axis=-1, keepdims=True)
    x2 = jnp.where(lane == i1, -jnp.inf, x)
    m2 = jnp.max(x2, axis=-1, keepdims=True)
    i2 = jnp.min(jnp.where(x2 == m2, lane, LANES), axis=-1, keepdims=True)
    e2 = jnp.exp(m2 - m1)
    inv = 1.0 / (1.0 + e2)
    return jnp.where(lane == i1, inv, jnp.where(lane == i2, e2 * inv, 0.0))


def _out_proj_kernel(*refs, n_experts):
    if n_experts:
        attn_ref, conv_ref, wo_ref, x_ref, g_ref, rt_ref, xo_ref, h_ref, gate_ref = refs
    else:
        attn_ref, conv_ref, wo_ref, x_ref, g_ref, xo_ref, h_ref = refs
    aw = attn_ref.shape[1]
    x = x_ref[...] + _dot(attn_ref[...], wo_ref[0:aw, :]) + _dot(conv_ref[...], wo_ref[aw:, :])
    xo_ref[...] = x
    h = _rms_rows(x, g_ref[...])
    h_ref[...] = h.astype(h_ref.dtype)
    if n_experts:
        h_hi, h_lo = _split_bf16(h)
        r_hi = rt_ref[0]
        r_lo = rt_ref[1]
        logits = _dot(h_hi, r_hi) + (_dot(h_hi, r_lo) + _dot(h_lo, r_hi))
        gate_ref[...] = _top2_gates(logits, n_experts)


def _out_proj(attn, conv, wo, x2, g, router_split=None, n_experts=0):
    t, d = x2.shape
    tm = min(TM_PROJ, t)
    row = lambda i: (i, 0)
    const = lambda i: (0, 0)
    in_specs = [pl.BlockSpec((tm, attn.shape[1]), row), pl.BlockSpec((tm, conv.shape[1]), row),
                pl.BlockSpec(wo.shape, const), pl.BlockSpec((tm, d), row), pl.BlockSpec((1, d), const)]
    out_shape = [jax.ShapeDtypeStruct((t, d), F32), jax.ShapeDtypeStruct((t, d), BF16)]
    out_specs = [pl.BlockSpec((tm, d), row), pl.BlockSpec((tm, d), row)]
    args = [attn, conv, wo, x2, g]
    if n_experts:
        in_specs.append(pl.BlockSpec(router_split.shape, lambda i: (0, 0, 0)))
        out_shape.append(jax.ShapeDtypeStruct((t, LANES), F32))
        out_specs.append(pl.BlockSpec((tm, LANES), row))
        args.append(router_split)
    return pl.pallas_call(
        functools.partial(_out_proj_kernel, n_experts=n_experts),
        grid=(t // tm,), in_specs=in_specs, out_specs=out_specs, out_shape=out_shape,
        compiler_params=_cparams(("parallel",)), name="out_proj",
    )(*args)


def _ffn_kernel(h_ref, x_ref, wg_ref, wu_ref, wd_ref, o_ref):
    f = pl.program_id(1)

    @pl.when(f == 0)
    def _():
        o_ref[...] = x_ref[...]

    h = h_ref[...]
    a = _dot(h, wg_ref[...])
    u = _dot(h, wu_ref[...])
    act = (a * jax.nn.sigmoid(a)) * u
    o_ref[...] += _dot(act.astype(BF16), wd_ref[...])


def _ffn(h, x2, wg, wu, wd):
    t, d = x2.shape
    dff = wg.shape[1]
    tm = min(TM_FFN, t)
    tf = TF_FFN
    return pl.pallas_call(
        _ffn_kernel, grid=(t // tm, dff // tf),
        in_specs=[pl.BlockSpec((tm, d), lambda i, f: (i, 0)), pl.BlockSpec((tm, d), lambda i, f: (i, 0)),
                  pl.BlockSpec((d, tf), lambda i, f: (0, f)), pl.BlockSpec((d, tf), lambda i, f: (0, f)),
                  pl.BlockSpec((tf, d), lambda i, f: (f, 0))],
        out_specs=pl.BlockSpec((tm, d), lambda i, f: (i, 0)),
        out_shape=jax.ShapeDtypeStruct((t, d), F32),
        compiler_params=_cparams(("parallel", "arbitrary")), name="ffn",
    )(h, x2, wg, wu, wd)


def _moe_kernel(h_ref, x_ref, gate_ref, wg_ref, wu_ref, wd_ref, o_ref):
    e = pl.program_id(1)
    f = pl.program_id(2)

    @pl.when((e == 0) & (f == 0))
    def _():
        o_ref[...] = x_ref[...]

    gates = gate_ref[...]
    lane = lax.broadcasted_iota(jnp.int32, gates.shape, 1)
    g_e = jnp.sum(jnp.where(lane == e, gates, 0.0), axis=-1, keepdims=True)
    h = h_ref[...]
    a = _dot(h, wg_ref[0])
    u = _dot(h, wu_ref[0])
    act = (a * jax.nn.sigmoid(a)) * u * g_e
    o_ref[...] += _dot(act.astype(BF16), wd_ref[0])


def _moe(h, x2, gates, wg, wu, wd):
    t, d = x2.shape
    n_e, _, dff = wg.shape
    tm = min(TM_FFN, t)
    tf = TF_FFN
    tok = lambda i, e, f: (i, 0)
    return pl.pallas_call(
        _moe_kernel, grid=(t // tm, n_e, dff // tf),
        in_specs=[pl.BlockSpec((tm, d), tok), pl.BlockSpec((tm, d), tok), pl.BlockSpec((tm, LANES), tok),
                  pl.BlockSpec((1, d, tf), lambda i, e, f: (e, 0, f)),
                  pl.BlockSpec((1, d, tf), lambda i, e, f: (e, 0, f)),
                  pl.BlockSpec((1, tf, d), lambda i, e, f: (e, f, 0))],
        out_specs=pl.BlockSpec((tm, d), tok),
        out_shape=jax.ShapeDtypeStruct((t, d), F32),
        compiler_params=_cparams(("parallel", "arbitrary", "arbitrary")), name="moe",
    )(h, x2, gates, wg, wu, wd)


def _rope_tables(pos):
    half = ROPE_DIM // 2
    inv_freq = ROPE_THETA ** (-2.0 * jnp.arange(half, dtype=F32) / ROPE_DIM)
    ang = pos.astype(F32).reshape(-1, 1) * inv_freq
    cos, sin = jnp.cos(ang), jnp.sin(ang)
    n = ang.shape[0]
    rest = HEAD_DIM - ROPE_DIM
    cos_h = jnp.concatenate([cos, cos, jnp.ones((n, rest), F32)], axis=-1)
    sin_h = jnp.concatenate([-sin, sin, jnp.zeros((n, rest), F32)], axis=-1)
    return jnp.tile(cos_h, (1, HEADS_PER_VREG)), jnp.tile(sin_h, (1, HEADS_PER_VREG))


def _permute_w_in(w, conv_w):
    d = w.shape[0]
    kv_end = Q_W + 6 * KV_W
    g = w[:, kv_end:kv_end + N_HEADS * N_BRANCH]
    pad = jnp.zeros((d, LANES - GATES_PER_KV), w.dtype)
    gate_cols = []
    for kh in range(N_KV_HEADS):
        gate_cols += [g[:, kh * GATES_PER_KV:(kh + 1) * GATES_PER_KV], pad]
    u = w[:, kv_end + N_HEADS * N_BRANCH:]
    return jnp.concatenate([w[:, :kv_end]] + gate_cols + [u], axis=1).astype(BF16)


def _compress_weights(pos_emb, w1, w2):
    hidden = w1.shape[1]
    eye = jnp.eye(N_KV_HEADS, dtype=w1.dtype)
    w1r = w1.reshape(CMP_LEN, HEAD_DIM, hidden)
    halves = []
    for part in (w1r[:CMP_STRIDE], w1r[CMP_STRIDE:]):
        full = jnp.einsum('ldj,hg->lhdgj', part, eye)
        halves.append(full.reshape(CMP_STRIDE * N_KV_HEADS * HEAD_DIM, N_KV_HEADS * hidden).astype(BF16))
    w2p = jnp.einsum('jd,hg->hjgd', w2, eye).reshape(N_KV_HEADS * hidden, N_KV_HEADS * HEAD_DIM).astype(BF16)
    pos = []
    for part in (pos_emb[:CMP_STRIDE], pos_emb[CMP_STRIDE:]):
        pos.append(jnp.broadcast_to(part[:, None, :], (CMP_STRIDE, N_KV_HEADS, HEAD_DIM)).reshape(1, -1))
    return [halves[0], halves[1], w2p, pos[0], pos[1]]


def _selection_map_t(seq):
    ncp = seq // CMP_STRIDE
    n_cmp = (seq - CMP_LEN) // CMP_STRIDE + 1
    c0 = np.arange(ncp) * CMP_STRIDE
    s0 = np.arange(seq // SEL_LEN) * SEL_LEN
    ov = np.minimum(c0[None, :] + CMP_LEN, s0[:, None] + SEL_LEN) - np.maximum(c0[None, :], s0[:, None])
    m = np.clip(ov, 0, None) / CMP_LEN
    m[:, n_cmp:] = 0.0
    return jnp.asarray(m, dtype=BF16)


def kernel(x, positions, attn_norm_g, ffn_norm_g, w_in, w_out, q_norm_g, k_norm_g, cmp_pos_k, cmp_w1_k, cmp_w2_k, cmp_pos_v, cmp_w1_v, cmp_w2_v, conv_w, conv_b, conv_ln_g, conv_ln_b, ffn_w_gate, ffn_w_up, ffn_w_down, moe_router, moe_w_gate, moe_w_up, moe_w_down):
    b, seq, d = x.shape
    t = b * seq
    depth = w_in.shape[0]
    cw = conv_w.shape[2]
    ncp = seq // CMP_STRIDE
    n_cmp = (seq - CMP_LEN) // CMP_STRIDE + 1
    assert seq % max(TQ, KC, TS_CONV) == 0 and seq >= WINDOW + TQ

    cos_t, sin_t = _rope_tables(positions)
    cmp_end = np.minimum(np.arange(ncp) * CMP_STRIDE + CMP_LEN - 1, seq - 1)
    cos_c, sin_c = _rope_tables(positions[:, cmp_end])
    selmap_t = _selection_map_t(seq)
    tile2 = lambda v: jnp.tile(v.reshape(1, HEAD_DIM), (1, HEADS_PER_VREG))

    x2 = x.reshape(t, d)
    for layer in range(depth):
        w_perm = _permute_w_in(w_in[layer], cw)
        q, kc, vc, ks, vs, kw, vw, gates, glu = _in_proj(
            x2, attn_norm_g[layer].reshape(1, d), w_perm, cos_t, sin_t,
            tile2(q_norm_g[layer]), tile2(k_norm_g[layer, 1]), tile2(k_norm_g[layer, 2]), seq)
        kcmp, vcmp = _compress(
            kc.reshape(t // CMP_STRIDE, CMP_STRIDE * KV_W), vc.reshape(t // CMP_STRIDE, CMP_STRIDE * KV_W),
            _compress_weights(cmp_pos_k[layer], cmp_w1_k[layer], cmp_w2_k[layer]),
            _compress_weights(cmp_pos_v[layer], cmp_w1_v[layer], cmp_w2_v[layer]),
            tile2(k_norm_g[layer, 0]), cos_c, sin_c, b, ncp)
        attn = _attention(q, kcmp, vcmp, ks, vs, kw, vw, gates, selmap_t, b, seq)
        conv = _conv(glu, conv_w[layer], conv_b[layer].reshape(1, cw), conv_ln_g[layer].reshape(1, cw),
                     conv_ln_b[layer].reshape(1, cw), b, seq)
        wo = w_out[layer].astype(BF16)
        g2 = ffn_norm_g[layer].reshape(1, d)
        i = layer // 2
        if layer % 2 == 0:
            x2, h = _out_proj(attn, conv, wo, x2, g2)
            x2 = _ffn(h, x2, ffn_w_gate[i].astype(BF16), ffn_w_up[i].astype(BF16), ffn_w_down[i].astype(BF16))
        else:
            n_e = moe_router.shape[2]
            r = jnp.pad(moe_router[i], ((0, 0), (0, LANES - n_e)))
            r_hi = r.astype(BF16)
            r_lo = (r - r_hi.astype(F32)).astype(BF16)
            x2, h, route = _out_proj(attn, conv, wo, x2, g2, jnp.stack([r_hi, r_lo]), n_e)
            x2 = _moe(h, x2, route, moe_w_gate[i].astype(BF16), moe_w_up[i].astype(BF16),
                      moe_w_down[i].astype(BF16))
    return x2.reshape(b, seq, d)
```

```python
import functools
import math

import jax
import jax.numpy as jnp
import numpy as np
from jax import lax
from jax.experimental import pallas as pl
from jax.experimental.pallas import tpu as pltpu

F32 = jnp.float32
BF16 = jnp.bfloat16

N_HEADS = 8
N_KV_HEADS = 2
Q_PER_KV = N_HEADS // N_KV_HEADS
HEAD_DIM = 64
N_BRANCH = 3
CMP_LEN = 32
CMP_STRIDE = 16
SEL_LEN = 64
SEL_TOPK = 16
N_LOCAL_SEL = 2
WINDOW = 512
CONV_KERNEL = 31
ROPE_THETA = 500000.0
ROPE_DIM = HEAD_DIM // 4
TOP_K = 2
EPS = 1e-6

LANES = 128
SUBLANES = 8
LOG2_E = math.log2(math.e)
NEG_BIG = -(2.0 ** 100)
HEADS_PER_VREG = LANES // HEAD_DIM
VMEM_LIMIT = 56 * 1024 * 1024

TM_PROJ = 512
TQ = 128
KC = 512
TS_CONV = 512
CH_CONV = 32
HALO = 32
TM_FFN = 1024
TF_FFN = 512
CT_MOE = 512
BG_MOE = 256
BM_MOE = 512


def _cparams(sem):
    return pltpu.CompilerParams(dimension_semantics=sem, vmem_limit_bytes=VMEM_LIMIT)


def _dot(a, b):
    return jnp.dot(a, b, preferred_element_type=F32)


def _dot_nt(a, b):
    return lax.dot_general(a, b, (((1,), (1,)), ((), ())), preferred_element_type=F32)


def _split_bf16(x):
    hi = x.astype(BF16)
    lo = (x - hi.astype(F32)).astype(BF16)
    return hi, lo


def _rms_rows(x, g):
    ms = jnp.mean(x * x, axis=-1, keepdims=True)
    return x * lax.rsqrt(ms + EPS) * g


def _head_block_ones():
    r = lax.broadcasted_iota(jnp.int32, (LANES, LANES), 0) // HEAD_DIM
    c = lax.broadcasted_iota(jnp.int32, (LANES, LANES), 1) // HEAD_DIM
    return jnp.where(r == c, 1.0, 0.0).astype(BF16)


def _head_norm_rope(xg, gain, cos, sin, ones_bd):
    hi, lo = _split_bf16(xg * xg)
    ms = (_dot(hi, ones_bd) + _dot(lo, ones_bd)) * (1.0 / HEAD_DIM)
    y = xg * lax.rsqrt(ms + EPS) * gain
    lane = lax.broadcasted_iota(jnp.int32, y.shape, 1) % HEAD_DIM
    half = ROPE_DIM // 2
    partner = jnp.where(lane < half, pltpu.roll(y, LANES - half, 1), pltpu.roll(y, half, 1))
    return y * cos + partner * sin


Q_W = N_HEADS * HEAD_DIM
KV_W = N_KV_HEADS * HEAD_DIM
SEG_Q = 0
SEG_KV = Q_W
SEG_GATE = SEG_KV + 6 * KV_W
SEG_UA = SEG_GATE + N_KV_HEADS * LANES
GATES_PER_KV = Q_PER_KV * N_BRANCH


def _in_proj_kernel(x_ref, g_ref, w_ref, cos_ref, sin_ref, qg_ref, ksg_ref, kwg_ref,
                    q_ref, kc_ref, vc_ref, ks_ref, vs_ref, kw_ref, vw_ref, gate_ref, glu_ref,
                    *, conv_w, seq):
    h = _rms_rows(x_ref[...], g_ref[...]).astype(BF16)
    tm = h.shape[0]
    cos = cos_ref[...]
    sin = sin_ref[...]
    ones_bd = _head_block_ones()
    lane = lax.broadcasted_iota(jnp.int32, (tm, LANES), 1)
    tok = lax.rem(pl.program_id(0) * tm, seq) + lax.broadcasted_iota(jnp.int32, (tm, 1), 0)
    block_aug = jnp.where(lane - HEAD_DIM == tok // SEL_LEN, NEG_BIG, 0.0)
    ones_aug = jnp.where(lane == HEAD_DIM, 1.0, 0.0)

    def seg(lo, width):
        return _dot(h, w_ref[:, lo:lo + width])

    def put_heads(ref, first, val, aug):
        for j in range(HEADS_PER_VREG):
            head = val if j == 0 else pltpu.roll(val, LANES - j * HEAD_DIM, 1)
            ref[first + j] = jnp.where(lane < HEAD_DIM, head, aug).astype(ref.dtype)

    scale = HEAD_DIM ** -0.5 * LOG2_E
    qg = qg_ref[...]
    for c in range(Q_W // LANES):
        y = _head_norm_rope(seg(SEG_Q + c * LANES, LANES), qg, cos, sin, ones_bd) * scale
        put_heads(q_ref, c * HEADS_PER_VREG, y, 0.0)
    kc_ref[...] = seg(SEG_KV, KV_W)
    vc_ref[...] = seg(SEG_KV + KV_W, KV_W)
    put_heads(ks_ref, 0, _head_norm_rope(seg(SEG_KV + 2 * KV_W, KV_W), ksg_ref[...], cos, sin, ones_bd), block_aug)
    put_heads(vs_ref, 0, seg(SEG_KV + 3 * KV_W, KV_W), ones_aug)
    put_heads(kw_ref, 0, _head_norm_rope(seg(SEG_KV + 4 * KV_W, KV_W), kwg_ref[...], cos, sin, ones_bd), 0.0)
    put_heads(vw_ref, 0, seg(SEG_KV + 5 * KV_W, KV_W), ones_aug)
    gate_ref[...] = jax.nn.sigmoid(seg(SEG_GATE, N_KV_HEADS * LANES))
    a = seg(SEG_UA, conv_w)
    g = seg(SEG_UA + conv_w, conv_w)
    glu_ref[...] = a * jax.nn.sigmoid(g)


def _in_proj(x2, g, w_perm, cos_t, sin_t, qg, ksg, kwg, seq):
    t, d = x2.shape
    conv_w = (w_perm.shape[1] - SEG_UA) // 2
    tm = min(TM_PROJ, t)
    row = lambda i: (i, 0)
    const = lambda i: (0, 0)
    head_row = lambda i: (0, i, 0)
    out_shape = [
        jax.ShapeDtypeStruct((N_HEADS, t, LANES), BF16),
        jax.ShapeDtypeStruct((t, KV_W), F32),
        jax.ShapeDtypeStruct((t, KV_W), F32),
        jax.ShapeDtypeStruct((N_KV_HEADS, t, LANES), BF16),
        jax.ShapeDtypeStruct((N_KV_HEADS, t, LANES), BF16),
        jax.ShapeDtypeStruct((N_KV_HEADS, t, LANES), BF16),
        jax.ShapeDtypeStruct((N_KV_HEADS, t, LANES), BF16),
        jax.ShapeDtypeStruct((t, N_KV_HEADS * LANES), F32),
        jax.ShapeDtypeStruct((t, conv_w), F32),
    ]
    kv_spec = pl.BlockSpec((N_KV_HEADS, tm, LANES), head_row)
    out_specs = [
        pl.BlockSpec((N_HEADS, tm, LANES), head_row),
        pl.BlockSpec((tm, KV_W), row), pl.BlockSpec((tm, KV_W), row),
        kv_spec, kv_spec, kv_spec, kv_spec,
        pl.BlockSpec((tm, N_KV_HEADS * LANES), row),
        pl.BlockSpec((tm, conv_w), row),
    ]
    in_specs = [
        pl.BlockSpec((tm, d), row), pl.BlockSpec((1, d), const),
        pl.BlockSpec(w_perm.shape, const),
        pl.BlockSpec((tm, LANES), row), pl.BlockSpec((tm, LANES), row),
        pl.BlockSpec((1, LANES), const), pl.BlockSpec((1, LANES), const), pl.BlockSpec((1, LANES), const),
    ]
    return pl.pallas_call(
        functools.partial(_in_proj_kernel, conv_w=conv_w, seq=seq),
        grid=(t // tm,), in_specs=in_specs, out_specs=out_specs, out_shape=out_shape,
        compiler_params=_cparams(("parallel",)), name="in_proj",
    )(x2, g, w_perm, cos_t, sin_t, qg, ksg, kwg)


def _gelu_tanh(x):
    c = math.sqrt(2.0 / math.pi)
    return 0.5 * x * (1.0 + jnp.tanh(c * (x + 0.044715 * (x * x * x))))


def _compress_kernel(k_ref, v_ref, w1ak_ref, w1bk_ref, w2k_ref, pak_ref, pbk_ref,
                     w1av_ref, w1bv_ref, w2v_ref, pav_ref, pbv_ref,
                     kg_ref, cos_ref, sin_ref, ko_ref, vo_ref):
    def mlp(x_ref, w1a_ref, w1b_ref, w2_ref, pa_ref, pb_ref):
        x = x_ref[...]
        first = _dot((x + pa_ref[...]).astype(BF16), w1a_ref[...])
        second = _dot((x + pb_ref[...]).astype(BF16), w1b_ref[...])
        n = first.shape[0]
        hid = first + pltpu.roll(second, n - 1, 0)
        return _dot(_gelu_tanh(hid).astype(BF16), w2_ref[...])

    kc = mlp(k_ref, w1ak_ref, w1bk_ref, w2k_ref, pak_ref, pbk_ref)
    kc = _head_norm_rope(kc, kg_ref[...], cos_ref[...], sin_ref[...], _head_block_ones())
    vc = mlp(v_ref, w1av_ref, w1bv_ref, w2v_ref, pav_ref, pbv_ref)
    lane = lax.broadcasted_iota(jnp.int32, kc.shape, 1)
    for j in range(N_KV_HEADS):
        for val, ref in ((kc, ko_ref), (vc, vo_ref)):
            head = val if j == 0 else pltpu.roll(val, LANES - j * HEAD_DIM, 1)
            ref[0, j] = jnp.where(lane < HEAD_DIM, head, 0.0).astype(ref.dtype)


def _compress(kc2, vc2, wk, wv, kg, cosc, sinc, b, ncp):
    feat = kc2.shape[1]
    const = lambda i: (0, 0)
    row = lambda i: (i, 0)

    def wspecs(ws):
        return [pl.BlockSpec(w.shape, const) for w in ws]

    out = jax.ShapeDtypeStruct((b, N_KV_HEADS, ncp, LANES), BF16)
    ospec = pl.BlockSpec((1, N_KV_HEADS, ncp, LANES), lambda i: (i, 0, 0, 0))
    return pl.pallas_call(
        _compress_kernel, grid=(b,),
        in_specs=[pl.BlockSpec((ncp, feat), row), pl.BlockSpec((ncp, feat), row)]
        + wspecs(wk) + wspecs(wv)
        + [pl.BlockSpec((1, LANES), const), pl.BlockSpec((ncp, LANES), row), pl.BlockSpec((ncp, LANES), row)],
        out_specs=[ospec, ospec], out_shape=[out, out],
        compiler_params=_cparams(("parallel",)), name="compress",
    )(kc2, vc2, *wk, *wv, kg, cosc, sinc)


def _softmax2_rows(s):
    m = jnp.max(s, axis=-1, keepdims=True)
    m = jnp.where(m == -jnp.inf, 0.0, m)
    p = jnp.exp2(s - m)
    l = jnp.sum(p, axis=-1, keepdims=True)
    return p * (1.0 / jnp.maximum(l, jnp.finfo(F32).tiny))


def _attn_kernel(q_ref, kc_ref, vc_ref, ks_ref, vs_ref, kw_ref, vw_ref, gate_ref, selmap_ref,
                 o_ref, *, seq, tq, kc_len, top_n):
    i = pl.program_id(2)
    t0 = i * tq
    rows = Q_PER_KV * tq
    n_sel = seq // SEL_LEN
    q2 = q_ref[...].reshape(rows, LANES)
    t_row = t0 + (lax.broadcasted_iota(jnp.int32, (rows, 1), 0) & (tq - 1))
    t_tok = t0 + lax.broadcasted_iota(jnp.int32, (tq, 1), 0)

    kcmp = kc_ref[0, 0]
    ncp = kcmp.shape[0]
    s_c = _dot_nt(q2, kcmp)
    cmp_end = lax.broadcasted_iota(jnp.int32, (1, ncp), 1) * CMP_STRIDE + (CMP_LEN - 1)
    p_c = _softmax2_rows(jnp.where(cmp_end <= t_row, s_c, -jnp.inf))
    o_c = _dot(p_c.astype(BF16), vc_ref[0, 0])

    p_hi, p_lo = _split_bf16(jnp.sum(p_c.reshape(Q_PER_KV, tq, ncp), axis=0))
    selmap = selmap_ref[...]
    imp = _dot_nt(selmap, p_hi) + _dot_nt(selmap, p_lo)
    blk = lax.broadcasted_iota(jnp.int32, (n_sel, tq), 0)
    cur = (t0 + lax.broadcasted_iota(jnp.int32, (n_sel, tq), 1)) // SEL_LEN
    causal_blk = blk <= cur
    forced = (blk == 0) | (causal_blk & (blk > cur - N_LOCAL_SEL))
    score = jnp.where(forced, jnp.inf, jnp.where(causal_blk, imp, -jnp.inf))
    sub = lax.broadcasted_iota(jnp.int32, (SUBLANES, tq), 0)
    groups = [score[g * SUBLANES:(g + 1) * SUBLANES, :] for g in range(n_sel // SUBLANES)]
    ranks = [jnp.zeros((SUBLANES, tq), F32) for _ in groups]
    for jp in range(n_sel):
        other = jnp.broadcast_to(score[jp:jp + 1, :], (SUBLANES, tq))
        for g, sg in enumerate(groups):
            first = g * SUBLANES
            if first > jp:
                inc = jnp.where(other >= sg, 1.0, 0.0)
            elif first + SUBLANES - 1 <= jp:
                inc = jnp.where(other > sg, 1.0, 0.0)
            else:
                inc = jnp.where(other > sg, 1.0, jnp.where((other == sg) & (sub > jp - first), 1.0, 0.0))
            ranks[g] = ranks[g] + inc
    not_chosen = jnp.where(jnp.concatenate(ranks, axis=0) < top_n, 0.0, 1.0)
    flag_rows = [jnp.zeros((HEAD_DIM, tq), F32), not_chosen]
    if HEAD_DIM + n_sel < LANES:
        flag_rows.append(jnp.zeros((LANES - HEAD_DIM - n_sel, tq), F32))
    flags = jnp.transpose(jnp.concatenate(flag_rows, axis=0))
    q_sel = (q2.reshape(Q_PER_KV, tq, LANES) + flags.astype(BF16)[None]).reshape(rows, LANES)

    def masked_scores(s, bias):
        return (s.reshape(Q_PER_KV, tq, s.shape[-1]) + bias[None]).reshape(s.shape)

    def denominator(acc):
        return acc[:, HEAD_DIM:HEAD_DIM + 1]

    def sel_chunk(k0, carry, bias):
        m, acc = carry
        s = _dot_nt(q_sel, ks_ref[0, pl.ds(k0, kc_len), :])
        if bias is not None:
            s = masked_scores(s, bias)
        m_new = jnp.maximum(m, jnp.max(s, axis=-1, keepdims=True))
        p = jnp.exp2((s - m_new).astype(BF16))
        acc = jnp.exp2(m - m_new) * acc + _dot(p, vs_ref[0, pl.ds(k0, kc_len), :])
        return m_new, acc

    n_full = t0 // kc_len
    init = (jnp.full((rows, 1), -jnp.inf, F32), jnp.zeros((rows, LANES), F32))
    carry = lax.fori_loop(
        0, n_full, lambda c, cr: sel_chunk(pl.multiple_of(c * kc_len, kc_len), cr, None), init)
    kd = pl.multiple_of(n_full * kc_len, kc_len)
    key_d = kd + lax.broadcasted_iota(jnp.int32, (1, kc_len), 1)
    _, acc_s = sel_chunk(kd, carry, jnp.where(key_d <= t_tok, 0.0, NEG_BIG))
    o_s = acc_s * (1.0 / denominator(acc_s))

    span = min(WINDOW + tq, seq)
    w0 = pl.multiple_of(jnp.maximum(t0 - WINDOW, 0), tq)
    key_w = w0 + lax.broadcasted_iota(jnp.int32, (1, span), 1)
    bias_w = jnp.where((key_w <= t_tok) & (key_w > t_tok - WINDOW), 0.0, NEG_BIG)
    s_w = masked_scores(_dot_nt(q2, kw_ref[0, pl.ds(w0, span), :]), bias_w)
    p_w = jnp.exp2((s_w - jnp.max(s_w, axis=-1, keepdims=True)).astype(BF16))
    acc_w = _dot(p_w, vw_ref[0, pl.ds(w0, span), :])
    o_w = acc_w * (1.0 / denominator(acc_w))

    gates = gate_ref[...]

    def gate_col(br):
        cols = [gates[:, g * N_BRANCH + br:g * N_BRANCH + br + 1] for g in range(Q_PER_KV)]
        return jnp.concatenate(cols, axis=0)

    o = gate_col(0) * o_c + gate_col(1) * o_s + gate_col(2) * o_w
    o3 = o.reshape(Q_PER_KV, tq, LANES)
    lane = lax.broadcasted_iota(jnp.int32, (tq, LANES), 1)
    pairs = [jnp.where(lane < HEAD_DIM, o3[g], pltpu.roll(o3[g + 1], HEAD_DIM, 1))
             for g in range(0, Q_PER_KV, HEADS_PER_VREG)]
    o_ref[...] = jnp.concatenate(pairs, axis=-1).astype(o_ref.dtype)


def _attention(q, kcmp, vcmp, ks, vs, kw, vw, gates, selmap_t, b, seq):
    t = b * seq
    tq = min(TQ, seq)
    kc_len = min(KC, seq)
    nq = seq // tq
    ncp = kcmp.shape[2]
    n_sel = seq // SEL_LEN
    assert HEAD_DIM + n_sel <= LANES, "selection-block flags must fit beside the head dims"
    top_n = min(SEL_TOPK, n_sel)
    cmp_spec = pl.BlockSpec((1, 1, ncp, LANES), lambda bi, kh, i: (bi, kh, 0, 0))
    seq_spec = pl.BlockSpec((1, seq, LANES), lambda bi, kh, i: (kh, bi, 0))
    return pl.pallas_call(
        functools.partial(_attn_kernel, seq=seq, tq=tq, kc_len=kc_len, top_n=top_n),
        grid=(b, N_KV_HEADS, nq),
        in_specs=[
            pl.BlockSpec((Q_PER_KV, tq, LANES), lambda bi, kh, i: (kh, bi * nq + i, 0)),
            cmp_spec, cmp_spec, seq_spec, seq_spec, seq_spec, seq_spec,
            pl.BlockSpec((tq, LANES), lambda bi, kh, i: (bi * nq + i, kh)),
            pl.BlockSpec(selmap_t.shape, lambda bi, kh, i: (0, 0)),
        ],
        out_specs=pl.BlockSpec((tq, Q_PER_KV * HEAD_DIM), lambda bi, kh, i: (bi * nq + i, kh)),
        out_shape=jax.ShapeDtypeStruct((t, N_HEADS * HEAD_DIM), BF16),
        compiler_params=_cparams(("parallel", "parallel", "arbitrary")), name="nsa_attention",
    )(q, kcmp, vcmp, ks, vs, kw, vw, gates, selmap_t)


def _conv_kernel(glu_ref, w_ref, b_ref, lg_ref, lb_ref, o_ref, ext_ref, *, ts):
    i = pl.program_id(1)

    @pl.when(i == 0)
    def _():
        ext_ref[0:HALO, :] = jnp.zeros((HALO, ext_ref.shape[1]), F32)

    @pl.when(i > 0)
    def _():
        ext_ref[0:HALO, :] = ext_ref[ts:ts + HALO, :]

    ext_ref[HALO:HALO + ts, :] = glu_ref[...]
    w = w_ref[...]
    first_tap = HALO - (CONV_KERNEL - 1)
    for c in range(ts // CH_CONV):
        base = c * CH_CONV + first_tap
        acc = w[0:1, :] * ext_ref[base:base + CH_CONV, :]
        for k in range(1, CONV_KERNEL):
            acc = acc + w[k:k + 1, :] * ext_ref[base + k:base + k + CH_CONV, :]
        y = acc + b_ref[...]
        yc = y - jnp.mean(y, axis=-1, keepdims=True)
        yn = yc * lax.rsqrt(jnp.mean(yc * yc, axis=-1, keepdims=True) + EPS)
        z = yn * lg_ref[...] + lb_ref[...]
        o_ref[c * CH_CONV:(c + 1) * CH_CONV, :] = (z * jax.nn.sigmoid(z)).astype(o_ref.dtype)


def _conv(glu, w, bias, lg, lb, b, seq):
    t, cw = glu.shape
    ts = min(TS_CONV, seq)
    ns = seq // ts
    const = lambda bi, i: (0, 0)
    row = lambda bi, i: (bi * ns + i, 0)
    return pl.pallas_call(
        functools.partial(_conv_kernel, ts=ts), grid=(b, ns),
        in_specs=[pl.BlockSpec((ts, cw), row), pl.BlockSpec(w.shape, const),
                  pl.BlockSpec((1, cw), const), pl.BlockSpec((1, cw), const), pl.BlockSpec((1, cw), const)],
        out_specs=pl.BlockSpec((ts, cw), row),
        out_shape=jax.ShapeDtypeStruct((t, cw), BF16),
        scratch_shapes=[pltpu.VMEM((ts + HALO, cw), F32)],
        compiler_params=_cparams(("arbitrary", "arbitrary")), name="conformer_conv",
    )(glu, w, bias, lg, lb)


def _top2_gates(logits, n_experts):
    lane = lax.broadcasted_iota(jnp.int32, logits.shape, 1)
    x = jnp.where(lane < n_experts, logits, -jnp.inf)
    m1 = jnp.max(x, axis=-1, keepdims=True)
    i1 = jnp.min(jnp.where(x == m1, lane, LANES), axis=-1, keepdims=True)
    x2 = jnp.where(lane == i1, -jnp.inf, x)
    m2 = jnp.max(x2, axis=-1, keepdims=True)
    i2 = jnp.min(jnp.where(x2 == m2, lane, LANES), axis=-1, keepdims=True)
    e2 = jnp.exp(m2 - m1)
    inv = 1.0 / (1.0 + e2)
    return jnp.where(lane == i1, inv, jnp.where(lane == i2, e2 * inv, 0.0))


def _out_proj_kernel(*refs, n_experts):
    if n_experts:
        attn_ref, conv_ref, wo_ref, x_ref, g_ref, rt_ref, xo_ref, h_ref, gate_ref = refs
    else:
        attn_ref, conv_ref, wo_ref, x_ref, g_ref, xo_ref, h_ref = refs
    aw = attn_ref.shape[1]
    x = x_ref[...] + _dot(attn_ref[...], wo_ref[0:aw, :]) + _dot(conv_ref[...], wo_ref[aw:, :])
    xo_ref[...] = x
    h = _rms_rows(x, g_ref[...])
    h_ref[...] = h.astype(h_ref.dtype)
    if n_experts:
        h_hi, h_lo = _split_bf16(h)
        r_hi = rt_ref[0]
        r_lo = rt_ref[1]
        logits = _dot(h_hi, r_hi) + (_dot(h_hi, r_lo) + _dot(h_lo, r_hi))
        gate_ref[...] = _top2_gates(logits, n_experts)


def _out_proj(attn, conv, wo, x2, g, router_split=None, n_experts=0):
    t, d = x2.shape
    tm = min(TM_PROJ, t)
    row = lambda i: (i, 0)
    const = lambda i: (0, 0)
    in_specs = [pl.BlockSpec((tm, attn.shape[1]), row), pl.BlockSpec((tm, conv.shape[1]), row),
                pl.BlockSpec(wo.shape, const), pl.BlockSpec((tm, d), row), pl.BlockSpec((1, d), const)]
    out_shape = [jax.ShapeDtypeStruct((t, d), F32), jax.ShapeDtypeStruct((t, d), BF16)]
    out_specs = [pl.BlockSpec((tm, d), row), pl.BlockSpec((tm, d), row)]
    args = [attn, conv, wo, x2, g]
    if n_experts:
        in_specs.append(pl.BlockSpec(router_split.shape, lambda i: (0, 0, 0)))
        out_shape.append(jax.ShapeDtypeStruct((t, LANES), F32))
        out_specs.append(pl.BlockSpec((tm, LANES), row))
        args.append(router_split)
    return pl.pallas_call(
        functools.partial(_out_proj_kernel, n_experts=n_experts),
        grid=(t // tm,), in_specs=in_specs, out_specs=out_specs, out_shape=out_shape,
        compiler_params=_cparams(("parallel",)), name="out_proj",
    )(*args)


def _ffn_kernel(h_ref, x_ref, wg_ref, wu_ref, wd_ref, o_ref):
    f = pl.program_id(1)

    @pl.when(f == 0)
    def _():
        o_ref[...] = x_ref[...]

    h = h_ref[...]
    a = _dot(h, wg_ref[...])
    u = _dot(h, wu_ref[...])
    act = (a * jax.nn.sigmoid(a)) * u
    o_ref[...] += _dot(act.astype(BF16), wd_ref[...])


def _ffn(h, x2, wg, wu, wd):
    t, d = x2.shape
    dff = wg.shape[1]
    tm = min(TM_FFN, t)
    tf = TF_FFN
    return pl.pallas_call(
        _ffn_kernel, grid=(t // tm, dff // tf),
        in_specs=[pl.BlockSpec((tm, d), lambda i, f: (i, 0)), pl.BlockSpec((tm, d), lambda i, f: (i, 0)),
                  pl.BlockSpec((d, tf), lambda i, f: (0, f)), pl.BlockSpec((d, tf), lambda i, f: (0, f)),
                  pl.BlockSpec((tf, d), lambda i, f: (f, 0))],
        out_specs=pl.BlockSpec((tm, d), lambda i, f: (i, 0)),
        out_shape=jax.ShapeDtypeStruct((t, d), F32),
        compiler_params=_cparams(("parallel", "arbitrary")), name="ffn",
    )(h, x2, wg, wu, wd)


def _route_scan_kernel(g_ref, pos_t_ref, pos_ref, cnt_ref, tot_ref, carry_ref, *, n_experts):
    c = pl.program_id(0)

    @pl.when(c == 0)
    def _():
        carry_ref[...] = jnp.zeros_like(carry_ref)

    ct = g_ref.shape[0]
    routed = g_ref[...] > 0.0
    a = jnp.where(routed, 1.0, 0.0)
    earlier = lax.broadcasted_iota(jnp.int32, (ct, ct), 1) < lax.broadcasted_iota(jnp.int32, (ct, ct), 0)
    base = carry_ref[...]
    pos = jnp.where(routed, _dot(jnp.where(earlier, 1.0, 0.0).astype(BF16), a.astype(BF16)) + base, -1.0)
    pos_ref[...] = pos
    pos_t_ref[0] = jnp.transpose(pos)[0:n_experts, :]
    cnt_ref[0] = base
    total = base + jnp.sum(a, axis=0, keepdims=True)
    carry_ref[...] = total
    tot_ref[...] = total


def _route_scan(gates, n_experts):
    t = gates.shape[0]
    ct = CT_MOE
    nch = t // ct
    return pl.pallas_call(
        functools.partial(_route_scan_kernel, n_experts=n_experts), grid=(nch,),
        in_specs=[pl.BlockSpec((ct, LANES), lambda c: (c, 0))],
        out_specs=[pl.BlockSpec((1, n_experts, ct), lambda c: (c, 0, 0)),
                   pl.BlockSpec((ct, LANES), lambda c: (c, 0)),
                   pl.BlockSpec((1, 1, LANES), lambda c: (c, 0, 0)),
                   pl.BlockSpec((1, LANES), lambda c: (0, 0))],
        out_shape=[jax.ShapeDtypeStruct((nch, n_experts, ct), F32),
                   jax.ShapeDtypeStruct((t, LANES), F32),
                   jax.ShapeDtypeStruct((nch, 1, LANES), F32),
                   jax.ShapeDtypeStruct((1, LANES), F32)],
        scratch_shapes=[pltpu.VMEM((1, LANES), F32)],
        compiler_params=_cparams(("arbitrary",)), name="route_scan",
    )(gates)


def _moe_gather_kernel(blk_ref, chk_ref, exp_ref, first_ref, valid_ref,
                       h_ref, pos_t_ref, pstart_ref, xs_ref, acc_ref):
    w = pl.program_id(0)

    @pl.when(first_ref[w] == 1)
    def _():
        acc_ref[...] = jnp.zeros_like(acc_ref)

    @pl.when(valid_ref[w] == 1)
    def _():
        pos = pos_t_ref[0]
        slot = jnp.where(pos >= 0.0, pos + pstart_ref[...], -1.0)
        sub = lax.broadcasted_iota(jnp.int32, pos.shape, 0)
        slot_e = jnp.sum(jnp.where(sub == exp_ref[w], slot, 0.0), axis=0, keepdims=True)
        bg = acc_ref.shape[0]
        target = (blk_ref[w] * bg + lax.broadcasted_iota(jnp.int32, (bg, 1), 0)).astype(F32)
        onehot = jnp.where(slot_e == target, 1.0, 0.0).astype(BF16)
        acc_ref[...] += _dot(onehot, h_ref[...])

    xs_ref[...] = acc_ref[...].astype(xs_ref.dtype)


def _moe_gather(h, pos_t, pstart_col, items, n_slots):
    t, d = h.shape
    n_items = items[0].shape[0]
    n_e = pos_t.shape[1]
    grid_spec = pltpu.PrefetchScalarGridSpec(
        num_scalar_prefetch=5, grid=(n_items,),
        in_specs=[pl.BlockSpec((CT_MOE, d), lambda w, blk, chk, *_: (chk[w], 0)),
                  pl.BlockSpec((1, n_e, CT_MOE), lambda w, blk, chk, *_: (chk[w], 0, 0)),
                  pl.BlockSpec((n_e, 1), lambda w, *_: (0, 0))],
        out_specs=pl.BlockSpec((BG_MOE, d), lambda w, blk, *_: (blk[w], 0)),
        scratch_shapes=[pltpu.VMEM((BG_MOE, d), F32)])
    return pl.pallas_call(
        _moe_gather_kernel, grid_spec=grid_spec,
        out_shape=jax.ShapeDtypeStruct((n_slots, d), BF16),
        compiler_params=_cparams(("arbitrary",)), name="moe_gather",
    )(*items, h, pos_t, pstart_col)


def _moe_ffn_kernel(exp_ref, valid_ref, xs_ref, wg_ref, wu_ref, wd_ref, ys_ref, acc_ref):
    j = pl.program_id(0)
    f = pl.program_id(1)

    @pl.when(f == 0)
    def _():
        acc_ref[...] = jnp.zeros_like(acc_ref)

    @pl.when(valid_ref[j] == 1)
    def _():
        xs = xs_ref[...]
        a = _dot(xs, wg_ref[0])
        u = _dot(xs, wu_ref[0])
        acc_ref[...] += _dot(((a * jax.nn.sigmoid(a)) * u).astype(BF16), wd_ref[0])

    @pl.when(f == pl.num_programs(1) - 1)
    def _():
        ys_ref[...] = acc_ref[...].astype(ys_ref.dtype)


def _moe_ffn(xs, blk_expert, blk_valid, wg, wu, wd):
    n_slots, d = xs.shape
    dff = wg.shape[2]
    tf = TF_FFN
    nf = dff // tf

    def fidx(j, f, valid):
        return f * valid[j] + (nf - 1) * (1 - valid[j])

    grid_spec = pltpu.PrefetchScalarGridSpec(
        num_scalar_prefetch=2, grid=(n_slots // BM_MOE, nf),
        in_specs=[pl.BlockSpec((BM_MOE, d), lambda j, f, e, v: (j, 0)),
                  pl.BlockSpec((1, d, tf), lambda j, f, e, v: (e[j], 0, fidx(j, f, v))),
                  pl.BlockSpec((1, d, tf), lambda j, f, e, v: (e[j], 0, fidx(j, f, v))),
                  pl.BlockSpec((1, tf, d), lambda j, f, e, v: (e[j], fidx(j, f, v), 0))],
        out_specs=pl.BlockSpec((BM_MOE, d), lambda j, f, e, v: (j, 0)),
        scratch_shapes=[pltpu.VMEM((BM_MOE, d), F32)])
    return pl.pallas_call(
        _moe_ffn_kernel, grid_spec=grid_spec,
        out_shape=jax.ShapeDtypeStruct((n_slots, d), BF16),
        compiler_params=_cparams(("parallel", "arbitrary")), name="moe_ffn",
    )(blk_expert, blk_valid, xs, wg, wu, wd)


def _moe_combine_kernel(tile_ref, blk_ref, exp_ref, first_ref, valid_ref,
                        x_ref, pos_ref, g_ref, pstart_ref, ys_ref, o_ref):
    w = pl.program_id(0)

    @pl.when(first_ref[w] == 1)
    def _():
        o_ref[...] = x_ref[...]

    @pl.when(valid_ref[w] == 1)
    def _():
        pos = pos_ref[...]
        slot = jnp.where(pos >= 0.0, pos + pstart_ref[...], -1.0)
        mine = lax.broadcasted_iota(jnp.int32, pos.shape, 1) == exp_ref[w]
        slot_e = jnp.sum(jnp.where(mine, slot, 0.0), axis=-1, keepdims=True)
        gate_e = jnp.sum(jnp.where(mine, g_ref[...], 0.0), axis=-1, keepdims=True)
        bs = ys_ref.shape[0]
        target = (blk_ref[w] * bs + lax.broadcasted_iota(jnp.int32, (1, bs), 1)).astype(F32)
        onehot = jnp.where(slot_e == target, 1.0, 0.0).astype(BF16)
        o_ref[...] += gate_e * _dot(onehot, ys_ref[...])


def _moe_combine(x2, pos, gates, pstart_row, ys, items):
    t, d = x2.shape
    n_items = items[0].shape[0]
    tok = lambda w, tile, *_: (tile[w], 0)
    grid_spec = pltpu.PrefetchScalarGridSpec(
        num_scalar_prefetch=5, grid=(n_items,),
        in_specs=[pl.BlockSpec((CT_MOE, d), tok), pl.BlockSpec((CT_MOE, LANES), tok),
                  pl.BlockSpec((CT_MOE, LANES), tok), pl.BlockSpec((1, LANES), lambda w, *_: (0, 0)),
                  pl.BlockSpec((BG_MOE, d), lambda w, tile, blk, *_: (blk[w], 0))],
        out_specs=pl.BlockSpec((CT_MOE, d), tok))
    return pl.pallas_call(
        _moe_combine_kernel, grid_spec=grid_spec,
        out_shape=jax.ShapeDtypeStruct((t, d), F32),
        compiler_params=_cparams(("arbitrary",)), name="moe_combine",
    )(*items, x2, pos, gates, pstart_row, ys)


def _work_items(group_id, lo, hi, n_items_max):
    n = jnp.maximum(hi - lo + 1, 0)
    ends = jnp.cumsum(n)
    total = ends[-1]
    w = jnp.arange(n_items_max, dtype=jnp.int32)
    wc = jnp.minimum(w, total - 1)
    g = jnp.searchsorted(ends, wc, side='right').astype(jnp.int32)
    k = wc - (ends[g] - n[g])
    valid = (w < total).astype(jnp.int32)
    return group_id[g], lo[g] + k, g, ((k == 0) & (w < total)).astype(jnp.int32), valid


def _moe_routed(h, x2, gates, wg, wu, wd):
    t, d = x2.shape
    n_e = wg.shape[0]
    nch = t // CT_MOE
    n_slots = t * TOP_K + n_e * BM_MOE
    nbg = n_slots // BG_MOE
    pos_t, pos, cnt, tot = _route_scan(gates, n_e)

    counts = tot[0, :n_e].astype(jnp.int32)
    padded = (counts + BM_MOE - 1) // BM_MOE * BM_MOE
    pend = jnp.cumsum(padded)
    pstart = pend - padded
    cum = cnt[:, 0, :n_e].astype(jnp.int32)
    cum_end = jnp.concatenate([cum[1:], counts[None]], axis=0)

    sb = jnp.arange(nbg, dtype=jnp.int32)
    sb_e = jnp.minimum(jnp.searchsorted(pend, sb * BG_MOE, side='right'), n_e - 1).astype(jnp.int32)
    p_lo = sb * BG_MOE - pstart[sb_e]
    has_rows = (sb * BG_MOE < pend[-1]) & (p_lo < counts[sb_e])
    c_lo = jnp.sum(cum_end.T[sb_e] <= p_lo[:, None], axis=1).astype(jnp.int32)
    c_hi = jnp.sum(cum.T[sb_e] <= (p_lo + BG_MOE - 1)[:, None], axis=1).astype(jnp.int32) - 1
    c_lo = jnp.where(has_rows, c_lo, 0)
    c_hi = jnp.where(has_rows, c_hi, 0)
    g_blk, g_chk, g_grp, g_first, g_valid = _work_items(sb, c_lo, c_hi, nbg + n_e * nch)
    xs = _moe_gather(h, pos_t, pstart.astype(F32).reshape(n_e, 1),
                     (g_blk, g_chk, sb_e[g_grp], g_first, g_valid), n_slots)

    mb = jnp.arange(n_slots // BM_MOE, dtype=jnp.int32) * BM_MOE
    mb_e = jnp.minimum(jnp.searchsorted(pend, mb, side='right'), n_e - 1).astype(jnp.int32)
    mb_valid = ((mb < pend[-1]) & (mb - pstart[mb_e] < counts[mb_e])).astype(jnp.int32)
    ys = _moe_ffn(xs, mb_e, mb_valid, wg, wu, wd)

    tile = jnp.repeat(jnp.arange(nch, dtype=jnp.int32), n_e)
    te = jnp.tile(jnp.arange(n_e, dtype=jnp.int32), nch)
    s_lo = (pstart[None, :] + cum).reshape(-1)
    s_hi = (pstart[None, :] + cum_end).reshape(-1) - 1
    b_lo = s_lo // BG_MOE
    b_hi = jnp.where(s_hi >= s_lo, s_hi // BG_MOE, b_lo - 1)
    grp = jnp.arange(nch * n_e, dtype=jnp.int32)
    c_grp, c_blk, _, _, c_valid = _work_items(grp, b_lo, b_hi, nbg + n_e * nch)
    c_tile = tile[c_grp]
    c_first = jnp.concatenate([jnp.ones((1,), jnp.int32), (c_tile[1:] != c_tile[:-1]).astype(jnp.int32)])
    pstart_row = jnp.zeros((1, LANES), F32).at[0, :n_e].set(pstart.astype(F32))
    return _moe_combine(x2, pos, gates, pstart_row, ys, (c_tile, c_blk, te[c_grp], c_first, c_valid))


def _rope_tables(pos):
    half = ROPE_DIM // 2
    inv_freq = ROPE_THETA ** (-2.0 * jnp.arange(half, dtype=F32) / ROPE_DIM)
    ang = pos.astype(F32).reshape(-1, 1) * inv_freq
    cos, sin = jnp.cos(ang), jnp.sin(ang)
    n = ang.shape[0]
    rest = HEAD_DIM - ROPE_DIM
    cos_h = jnp.concatenate([cos, cos, jnp.ones((n, rest), F32)], axis=-1)
    sin_h = jnp.concatenate([-sin, sin, jnp.zeros((n, rest), F32)], axis=-1)
    return jnp.tile(cos_h, (1, HEADS_PER_VREG)), jnp.tile(sin_h, (1, HEADS_PER_VREG))


def _permute_w_in(w, conv_w):
    d = w.shape[0]
    kv_end = Q_W + 6 * KV_W
    g = w[:, kv_end:kv_end + N_HEADS * N_BRANCH]
    pad = jnp.zeros((d, LANES - GATES_PER_KV), w.dtype)
    gate_cols = []
    for kh in range(N_KV_HEADS):
        gate_cols += [g[:, kh * GATES_PER_KV:(kh + 1) * GATES_PER_KV], pad]
    u = w[:, kv_end + N_HEADS * N_BRANCH:]
    return jnp.concatenate([w[:, :kv_end]] + gate_cols + [u], axis=1).astype(BF16)


def _compress_weights(pos_emb, w1, w2):
    hidden = w1.shape[1]
    eye = jnp.eye(N_KV_HEADS, dtype=w1.dtype)
    w1r = w1.reshape(CMP_LEN, HEAD_DIM, hidden)
    halves = []
    for part in (w1r[:CMP_STRIDE], w1r[CMP_STRIDE:]):
        full = jnp.einsum('ldj,hg->lhdgj', part, eye)
        halves.append(full.reshape(CMP_STRIDE * N_KV_HEADS * HEAD_DIM, N_KV_HEADS * hidden).astype(BF16))
    w2p = jnp.einsum('jd,hg->hjgd', w2, eye).reshape(N_KV_HEADS * hidden, N_KV_HEADS * HEAD_DIM).astype(BF16)
    pos = []
    for part in (pos_emb[:CMP_STRIDE], pos_emb[CMP_STRIDE:]):
        pos.append(jnp.broadcast_to(part[:, None, :], (CMP_STRIDE, N_KV_HEADS, HEAD_DIM)).reshape(1, -1))
    return [halves[0], halves[1], w2p, pos[0], pos[1]]


def _selection_map_t(seq):
    ncp = seq // CMP_STRIDE
    n_cmp = (seq - CMP_LEN) // CMP_STRIDE + 1
    c0 = np.arange(ncp) * CMP_STRIDE
    s0 = np.arange(seq // SEL_LEN) * SEL_LEN
    ov = np.minimum(c0[None, :] + CMP_LEN, s0[:, None] + SEL_LEN) - np.maximum(c0[None, :], s0[:, None])
    m = np.clip(ov, 0, None) / CMP_LEN
    m[:, n_cmp:] = 0.0
    return jnp.asarray(m, dtype=BF16)


def kernel(x, positions, attn_norm_g, ffn_norm_g, w_in, w_out, q_norm_g, k_norm_g, cmp_pos_k, cmp_w1_k, cmp_w2_k, cmp_pos_v, cmp_w1_v, cmp_w2_v, conv_w, conv_b, conv_ln_g, conv_ln_b, ffn_w_gate, ffn_w_up, ffn_w_down, moe_router, moe_w_gate, moe_w_up, moe_w_down):
    b, seq, d = x.shape
    t = b * seq
    depth = w_in.shape[0]
    cw = conv_w.shape[2]
    ncp = seq // CMP_STRIDE
    n_cmp = (seq - CMP_LEN) // CMP_STRIDE + 1
    assert seq % max(TQ, KC, TS_CONV) == 0 and seq >= WINDOW + TQ

    cos_t, sin_t = _rope_tables(positions)
    cmp_end = np.minimum(np.arange(ncp) * CMP_STRIDE + CMP_LEN - 1, seq - 1)
    cos_c, sin_c = _rope_tables(positions[:, cmp_end])
    selmap_t = _selection_map_t(seq)
    tile2 = lambda v: jnp.tile(v.reshape(1, HEAD_DIM), (1, HEADS_PER_VREG))

    x2 = x.reshape(t, d)
    for layer in range(depth):
        w_perm = _permute_w_in(w_in[layer], cw)
        q, kc, vc, ks, vs, kw, vw, gates, glu = _in_proj(
            x2, attn_norm_g[layer].reshape(1, d), w_perm, cos_t, sin_t,
            tile2(q_norm_g[layer]), tile2(k_norm_g[layer, 1]), tile2(k_norm_g[layer, 2]), seq)
        kcmp, vcmp = _compress(
            kc.reshape(t // CMP_STRIDE, CMP_STRIDE * KV_W), vc.reshape(t // CMP_STRIDE, CMP_STRIDE * KV_W),
            _compress_weights(cmp_pos_k[layer], cmp_w1_k[layer], cmp_w2_k[layer]),
            _compress_weights(cmp_pos_v[layer], cmp_w1_v[layer], cmp_w2_v[layer]),
            tile2(k_norm_g[layer, 0]), cos_c, sin_c, b, ncp)
        attn = _attention(q, kcmp, vcmp, ks, vs, kw, vw, gates, selmap_t, b, seq)
        conv = _conv(glu, conv_w[layer], conv_b[layer].reshape(1, cw), conv_ln_g[layer].reshape(1, cw),
                     conv_ln_b[layer].reshape(1, cw), b, seq)
        wo = w_out[layer].astype(BF16)
        g2 = ffn_norm_g[layer].reshape(1, d)
        i = layer // 2
        if layer % 2 == 0:
            x2, h = _out_proj(attn, conv, wo, x2, g2)
            x2 = _ffn(h, x2, ffn_w_gate[i].astype(BF16), ffn_w_up[i].astype(BF16), ffn_w_down[i].astype(BF16))
        else:
            n_e = moe_router.shape[2]
            r = jnp.pad(moe_router[i], ((0, 0), (0, LANES - n_e)))
            r_hi = r.astype(BF16)
            r_lo = (r - r_hi.astype(F32)).astype(BF16)
            x2, h, route = _out_proj(attn, conv, wo, x2, g2, jnp.stack([r_hi, r_lo]), n_e)
            x2 = _moe_routed(h, x2, route, moe_w_gate[i].astype(BF16), moe_w_up[i].astype(BF16),
                             moe_w_down[i].astype(BF16))
    return x2.reshape(b, seq, d)
```

```python
import functools
import math

import jax
import jax.numpy as jnp
import numpy as np
from jax import lax
from jax.experimental import pallas as pl
from jax.experimental.pallas import tpu as pltpu

F32 = jnp.float32
BF16 = jnp.bfloat16

N_HEADS = 8
N_KV_HEADS = 2
Q_PER_KV = N_HEADS // N_KV_HEADS
HEAD_DIM = 64
N_BRANCH = 3
CMP_LEN = 32
CMP_STRIDE = 16
SEL_LEN = 64
SEL_TOPK = 16
N_LOCAL_SEL = 2
WINDOW = 512
CONV_KERNEL = 31
ROPE_THETA = 500000.0
ROPE_DIM = HEAD_DIM // 4
TOP_K = 2
EPS = 1e-6

LANES = 128
SUBLANES = 8
LOG2_E = math.log2(math.e)
NEG_BIG = -(2.0 ** 100)
MAX_SCORE_BOUND = 50.0
SCORE_BOUND_MARGIN = 1.02
HEADS_PER_VREG = LANES // HEAD_DIM
VMEM_LIMIT = 56 * 1024 * 1024

TM_PROJ = 512
TQ = 128
KC = 512
TS_CONV = 512
CH_CONV = 32
HALO = 32
TM_FFN = 1024
TF_FFN = 512
CT_MOE = 512
BG_MOE = 256
BM_MOE = 512


def _cparams(sem):
    return pltpu.CompilerParams(dimension_semantics=sem, vmem_limit_bytes=VMEM_LIMIT)


def _dot(a, b):
    return jnp.dot(a, b, preferred_element_type=F32)


def _dot_nt(a, b):
    return lax.dot_general(a, b, (((1,), (1,)), ((), ())), preferred_element_type=F32)


def _split_bf16(x):
    hi = x.astype(BF16)
    lo = (x - hi.astype(F32)).astype(BF16)
    return hi, lo


def _rms_rows(x, g):
    ms = jnp.mean(x * x, axis=-1, keepdims=True)
    return x * lax.rsqrt(ms + EPS) * g


def _head_block_ones():
    r = lax.broadcasted_iota(jnp.int32, (LANES, LANES), 0) // HEAD_DIM
    c = lax.broadcasted_iota(jnp.int32, (LANES, LANES), 1) // HEAD_DIM
    return jnp.where(r == c, 1.0, 0.0).astype(BF16)


def _head_norm_rope(xg, gain, cos, sin, ones_bd):
    hi, lo = _split_bf16(xg * xg)
    ms = (_dot(hi, ones_bd) + _dot(lo, ones_bd)) * (1.0 / HEAD_DIM)
    y = xg * lax.rsqrt(ms + EPS) * gain
    lane = lax.broadcasted_iota(jnp.int32, y.shape, 1) % HEAD_DIM
    half = ROPE_DIM // 2
    partner = jnp.where(lane < half, pltpu.roll(y, LANES - half, 1), pltpu.roll(y, half, 1))
    return y * cos + partner * sin


Q_W = N_HEADS * HEAD_DIM
KV_W = N_KV_HEADS * HEAD_DIM
SEG_Q = 0
SEG_KV = Q_W
SEG_GATE = SEG_KV + 6 * KV_W
SEG_UA = SEG_GATE + N_KV_HEADS * LANES
GATES_PER_KV = Q_PER_KV * N_BRANCH


def _in_proj_kernel(x_ref, g_ref, w_ref, cos_ref, sin_ref, qg_ref, ksg_ref, kwg_ref,
                    q_ref, kc_ref, vc_ref, ks_ref, vs_ref, kw_ref, vw_ref, gate_ref, glu_ref,
                    *, conv_w, seq):
    h = _rms_rows(x_ref[...], g_ref[...]).astype(BF16)
    tm = h.shape[0]
    cos = cos_ref[...]
    sin = sin_ref[...]
    ones_bd = _head_block_ones()
    lane = lax.broadcasted_iota(jnp.int32, (tm, LANES), 1)
    tok = lax.rem(pl.program_id(0) * tm, seq) + lax.broadcasted_iota(jnp.int32, (tm, 1), 0)
    block_aug = jnp.where(lane - HEAD_DIM == tok // SEL_LEN, NEG_BIG, 0.0)
    ones_aug = jnp.where(lane == HEAD_DIM, 1.0, 0.0)

    def seg(lo, width):
        return _dot(h, w_ref[:, lo:lo + width])

    def put_heads(ref, first, val, aug):
        for j in range(HEADS_PER_VREG):
            head = val if j == 0 else pltpu.roll(val, LANES - j * HEAD_DIM, 1)
            ref[first + j] = jnp.where(lane < HEAD_DIM, head, aug).astype(ref.dtype)

    scale = HEAD_DIM ** -0.5 * LOG2_E
    qg = qg_ref[...]
    for c in range(Q_W // LANES):
        y = _head_norm_rope(seg(SEG_Q + c * LANES, LANES), qg, cos, sin, ones_bd) * scale
        put_heads(q_ref, c * HEADS_PER_VREG, y, 0.0)
    kc_ref[...] = seg(SEG_KV, KV_W)
    vc_ref[...] = seg(SEG_KV + KV_W, KV_W)
    put_heads(ks_ref, 0, _head_norm_rope(seg(SEG_KV + 2 * KV_W, KV_W), ksg_ref[...], cos, sin, ones_bd), block_aug)
    put_heads(vs_ref, 0, seg(SEG_KV + 3 * KV_W, KV_W), ones_aug)
    put_heads(kw_ref, 0, _head_norm_rope(seg(SEG_KV + 4 * KV_W, KV_W), kwg_ref[...], cos, sin, ones_bd), 0.0)
    put_heads(vw_ref, 0, seg(SEG_KV + 5 * KV_W, KV_W), ones_aug)
    gate_ref[...] = jax.nn.sigmoid(seg(SEG_GATE, N_KV_HEADS * LANES))
    a = seg(SEG_UA, conv_w)
    g = seg(SEG_UA + conv_w, conv_w)
    glu_ref[...] = a * jax.nn.sigmoid(g)


def _in_proj(x2, g, w_perm, cos_t, sin_t, qg, ksg, kwg, seq):
    t, d = x2.shape
    conv_w = (w_perm.shape[1] - SEG_UA) // 2
    tm = min(TM_PROJ, t)
    row = lambda i: (i, 0)
    const = lambda i: (0, 0)
    head_row = lambda i: (0, i, 0)
    out_shape = [
        jax.ShapeDtypeStruct((N_HEADS, t, LANES), BF16),
        jax.ShapeDtypeStruct((t, KV_W), F32),
        jax.ShapeDtypeStruct((t, KV_W), F32),
        jax.ShapeDtypeStruct((N_KV_HEADS, t, LANES), BF16),
        jax.ShapeDtypeStruct((N_KV_HEADS, t, LANES), BF16),
        jax.ShapeDtypeStruct((N_KV_HEADS, t, LANES), BF16),
        jax.ShapeDtypeStruct((N_KV_HEADS, t, LANES), BF16),
        jax.ShapeDtypeStruct((t, N_KV_HEADS * LANES), F32),
        jax.ShapeDtypeStruct((t, conv_w), F32),
    ]
    kv_spec = pl.BlockSpec((N_KV_HEADS, tm, LANES), head_row)
    out_specs = [
        pl.BlockSpec((N_HEADS, tm, LANES), head_row),
        pl.BlockSpec((tm, KV_W), row), pl.BlockSpec((tm, KV_W), row),
        kv_spec, kv_spec, kv_spec, kv_spec,
        pl.BlockSpec((tm, N_KV_HEADS * LANES), row),
        pl.BlockSpec((tm, conv_w), row),
    ]
    in_specs = [
        pl.BlockSpec((tm, d), row), pl.BlockSpec((1, d), const),
        pl.BlockSpec(w_perm.shape, const),
        pl.BlockSpec((tm, LANES), row), pl.BlockSpec((tm, LANES), row),
        pl.BlockSpec((1, LANES), const), pl.BlockSpec((1, LANES), const), pl.BlockSpec((1, LANES), const),
    ]
    return pl.pallas_call(
        functools.partial(_in_proj_kernel, conv_w=conv_w, seq=seq),
        grid=(t // tm,), in_specs=in_specs, out_specs=out_specs, out_shape=out_shape,
        compiler_params=_cparams(("parallel",)), name="in_proj",
    )(x2, g, w_perm, cos_t, sin_t, qg, ksg, kwg)


def _gelu_tanh(x):
    c = math.sqrt(2.0 / math.pi)
    return 0.5 * x * (1.0 + jnp.tanh(c * (x + 0.044715 * (x * x * x))))


def _compress_kernel(k_ref, v_ref, w1ak_ref, w1bk_ref, w2k_ref, pak_ref, pbk_ref,
                     w1av_ref, w1bv_ref, w2v_ref, pav_ref, pbv_ref,
                     kg_ref, cos_ref, sin_ref, ko_ref, vo_ref):
    def mlp(x_ref, w1a_ref, w1b_ref, w2_ref, pa_ref, pb_ref):
        x = x_ref[...]
        first = _dot((x + pa_ref[...]).astype(BF16), w1a_ref[...])
        second = _dot((x + pb_ref[...]).astype(BF16), w1b_ref[...])
        n = first.shape[0]
        hid = first + pltpu.roll(second, n - 1, 0)
        return _dot(_gelu_tanh(hid).astype(BF16), w2_ref[...])

    kc = mlp(k_ref, w1ak_ref, w1bk_ref, w2k_ref, pak_ref, pbk_ref)
    kc = _head_norm_rope(kc, kg_ref[...], cos_ref[...], sin_ref[...], _head_block_ones())
    vc = mlp(v_ref, w1av_ref, w1bv_ref, w2v_ref, pav_ref, pbv_ref)
    lane = lax.broadcasted_iota(jnp.int32, kc.shape, 1)
    for j in range(N_KV_HEADS):
        for val, ref in ((kc, ko_ref), (vc, vo_ref)):
            head = val if j == 0 else pltpu.roll(val, LANES - j * HEAD_DIM, 1)
            ref[0, j] = jnp.where(lane < HEAD_DIM, head, 0.0).astype(ref.dtype)


def _compress(kc2, vc2, wk, wv, kg, cosc, sinc, b, ncp):
    feat = kc2.shape[1]
    const = lambda i: (0, 0)
    row = lambda i: (i, 0)

    def wspecs(ws):
        return [pl.BlockSpec(w.shape, const) for w in ws]

    out = jax.ShapeDtypeStruct((b, N_KV_HEADS, ncp, LANES), BF16)
    ospec = pl.BlockSpec((1, N_KV_HEADS, ncp, LANES), lambda i: (i, 0, 0, 0))
    return pl.pallas_call(
        _compress_kernel, grid=(b,),
        in_specs=[pl.BlockSpec((ncp, feat), row), pl.BlockSpec((ncp, feat), row)]
        + wspecs(wk) + wspecs(wv)
        + [pl.BlockSpec((1, LANES), const), pl.BlockSpec((ncp, LANES), row), pl.BlockSpec((ncp, LANES), row)],
        out_specs=[ospec, ospec], out_shape=[out, out],
        compiler_params=_cparams(("parallel",)), name="compress",
    )(kc2, vc2, *wk, *wv, kg, cosc, sinc)


def _attn_kernel(off_ref, q_ref, kc_ref, vc_ref, ks_ref, vs_ref, kw_ref, vw_ref, gate_ref, selmap_ref,
                 o_ref, *, seq, tq, kc_len, top_n, bounded):
    i = pl.program_id(2)
    t0 = i * tq
    rows = Q_PER_KV * tq
    n_sel = seq // SEL_LEN
    q2 = q_ref[...].reshape(rows, LANES)
    t_row = t0 + (lax.broadcasted_iota(jnp.int32, (rows, 1), 0) & (tq - 1))
    t_tok = t0 + lax.broadcasted_iota(jnp.int32, (tq, 1), 0)
    neg_offset = -off_ref[0] if bounded else 0.0

    def add_bias(s, bias):
        return (s.reshape(s.shape[0] // tq, tq, s.shape[1]) + bias[None]).reshape(s.shape)

    kcmp = kc_ref[0, 0]
    ncp = kcmp.shape[0]
    s_c = _dot_nt(q2, kcmp)
    cmp_end = lax.broadcasted_iota(jnp.int32, (1, ncp), 1) * CMP_STRIDE + (CMP_LEN - 1)
    if bounded:
        e_c = jnp.exp2(add_bias(s_c, jnp.where(cmp_end <= t_tok, neg_offset, NEG_BIG)))
    else:
        s_c = jnp.where(cmp_end <= t_row, s_c, -jnp.inf)
        m_c = jnp.max(s_c, axis=-1, keepdims=True)
        e_c = jnp.exp2(s_c - jnp.where(m_c == -jnp.inf, 0.0, m_c))
    p_c = e_c * (1.0 / jnp.maximum(jnp.sum(e_c, axis=-1, keepdims=True), jnp.finfo(F32).tiny))
    o_c = _dot(p_c.astype(BF16), vc_ref[0, 0])

    def denominator(acc):
        return acc[:, HEAD_DIM:HEAD_DIM + 1]

    span = min(WINDOW + tq, seq)
    w0 = pl.multiple_of(jnp.maximum(t0 - WINDOW, 0), tq)
    key_w = w0 + lax.broadcasted_iota(jnp.int32, (1, span), 1)
    bias_w = jnp.where((key_w <= t_tok) & (key_w > t_tok - WINDOW), neg_offset, NEG_BIG)
    s_w = add_bias(_dot_nt(q2, kw_ref[0, pl.ds(w0, span), :]), bias_w)
    if bounded:
        p_w = jnp.exp2(s_w).astype(BF16)
    else:
        p_w = jnp.exp2((s_w - jnp.max(s_w, axis=-1, keepdims=True)).astype(BF16))
    acc_w = _dot(p_w, vw_ref[0, pl.ds(w0, span), :])
    o_w = acc_w * (1.0 / denominator(acc_w))

    p_hi, p_lo = _split_bf16(jnp.sum(p_c.reshape(Q_PER_KV, tq, ncp), axis=0))
    selmap = selmap_ref[...]
    imp = _dot_nt(selmap, p_hi) + _dot_nt(selmap, p_lo)
    blk = lax.broadcasted_iota(jnp.int32, (n_sel, tq), 0)
    cur = (t0 + lax.broadcasted_iota(jnp.int32, (n_sel, tq), 1)) // SEL_LEN
    causal_blk = blk <= cur
    forced = (blk == 0) | (causal_blk & (blk > cur - N_LOCAL_SEL))
    score = jnp.where(forced, jnp.inf, jnp.where(causal_blk, imp, -jnp.inf))
    sub = lax.broadcasted_iota(jnp.int32, (SUBLANES, tq), 0)
    groups = [score[g * SUBLANES:(g + 1) * SUBLANES, :] for g in range(n_sel // SUBLANES)]
    ranks = [jnp.zeros((SUBLANES, tq), F32) for _ in groups]
    for jp in range(n_sel):
        other = jnp.broadcast_to(score[jp:jp + 1, :], (SUBLANES, tq))
        for g, sg in enumerate(groups):
            first = g * SUBLANES
            if first > jp:
                inc = jnp.where(other >= sg, 1.0, 0.0)
            elif first + SUBLANES - 1 <= jp:
                inc = jnp.where(other > sg, 1.0, 0.0)
            else:
                inc = jnp.where(other > sg, 1.0, jnp.where((other == sg) & (sub > jp - first), 1.0, 0.0))
            ranks[g] = ranks[g] + inc
    chosen_flag = neg_offset * (1.0 / NEG_BIG)
    block_flags = jnp.where(jnp.concatenate(ranks, axis=0) < top_n, chosen_flag, 1.0)
    flag_rows = [jnp.zeros((HEAD_DIM, tq), F32), block_flags]
    if HEAD_DIM + n_sel < LANES:
        flag_rows.append(jnp.zeros((LANES - HEAD_DIM - n_sel, tq), F32))
    flags = jnp.transpose(jnp.concatenate(flag_rows, axis=0))
    q_sel = (q2.reshape(Q_PER_KV, tq, LANES) + flags.astype(BF16)[None]).reshape(rows, LANES)

    def sel_chunk(k0, carry, bias):
        s = _dot_nt(q_sel, ks_ref[0, pl.ds(k0, kc_len), :])
        if bias is not None:
            s = add_bias(s, bias)
        v = vs_ref[0, pl.ds(k0, kc_len), :]
        if bounded:
            (acc,) = carry
            return (acc + _dot(jnp.exp2(s).astype(BF16), v),)
        m, acc = carry
        m_new = jnp.maximum(m, jnp.max(s, axis=-1, keepdims=True))
        p = jnp.exp2((s - m_new).astype(BF16))
        return m_new, jnp.exp2(m - m_new) * acc + _dot(p, v)

    n_full = t0 // kc_len
    init = (jnp.zeros((rows, LANES), F32),)
    if not bounded:
        init = (jnp.full((rows, 1), -jnp.inf, F32),) + init
    carry = lax.fori_loop(
        0, n_full, lambda c, cr: sel_chunk(pl.multiple_of(c * kc_len, kc_len), cr, None), init)
    kd = pl.multiple_of(n_full * kc_len, kc_len)
    key_d = kd + lax.broadcasted_iota(jnp.int32, (1, kc_len), 1)
    acc_s = sel_chunk(kd, carry, jnp.where(key_d <= t_tok, 0.0, NEG_BIG))[-1]
    o_s = acc_s * (1.0 / denominator(acc_s))

    gates = gate_ref[...]

    def gate_col(br):
        cols = [gates[:, g * N_BRANCH + br:g * N_BRANCH + br + 1] for g in range(Q_PER_KV)]
        return jnp.concatenate(cols, axis=0)

    o = gate_col(0) * o_c + gate_col(1) * o_s + gate_col(2) * o_w
    o3 = o.reshape(Q_PER_KV, tq, LANES)
    lane = lax.broadcasted_iota(jnp.int32, (tq, LANES), 1)
    pairs = [jnp.where(lane < HEAD_DIM, o3[g], pltpu.roll(o3[g + 1], HEAD_DIM, 1))
             for g in range(0, Q_PER_KV, HEADS_PER_VREG)]
    o_ref[...] = jnp.concatenate(pairs, axis=-1).astype(o_ref.dtype)


def _attention(score_bound, q, kcmp, vcmp, ks, vs, kw, vw, gates, selmap_t, b, seq):
    t = b * seq
    tq = min(TQ, seq)
    kc_len = min(KC, seq)
    nq = seq // tq
    ncp = kcmp.shape[2]
    n_sel = seq // SEL_LEN
    assert HEAD_DIM + n_sel <= LANES, "selection-block flags must fit beside the head dims"
    top_n = min(SEL_TOPK, n_sel)
    cmp_spec = pl.BlockSpec((1, 1, ncp, LANES), lambda bi, kh, i: (bi, kh, 0, 0))
    seq_spec = pl.BlockSpec((1, seq, LANES), lambda bi, kh, i: (kh, bi, 0))

    def run(bounded):
        return pl.pallas_call(
            functools.partial(_attn_kernel, seq=seq, tq=tq, kc_len=kc_len, top_n=top_n, bounded=bounded),
            grid=(b, N_KV_HEADS, nq),
            in_specs=[
                pl.BlockSpec(memory_space=pltpu.SMEM),
                pl.BlockSpec((Q_PER_KV, tq, LANES), lambda bi, kh, i: (kh, bi * nq + i, 0)),
                cmp_spec, cmp_spec, seq_spec, seq_spec, seq_spec, seq_spec,
                pl.BlockSpec((tq, LANES), lambda bi, kh, i: (bi * nq + i, kh)),
                pl.BlockSpec(selmap_t.shape, lambda bi, kh, i: (0, 0)),
            ],
            out_specs=pl.BlockSpec((tq, Q_PER_KV * HEAD_DIM), lambda bi, kh, i: (bi * nq + i, kh)),
            out_shape=jax.ShapeDtypeStruct((t, N_HEADS * HEAD_DIM), BF16),
            compiler_params=_cparams(("parallel", "parallel", "arbitrary")),
            name="nsa_attention" if bounded else "nsa_attention_running_max",
        )(score_bound, q, kcmp, vcmp, ks, vs, kw, vw, gates, selmap_t)

    return lax.cond(score_bound[0] < MAX_SCORE_BOUND, lambda: run(True), lambda: run(False))


def _conv_kernel(glu_ref, w_ref, b_ref, lg_ref, lb_ref, o_ref, ext_ref, *, ts):
    i = pl.program_id(1)

    @pl.when(i == 0)
    def _():
        ext_ref[0:HALO, :] = jnp.zeros((HALO, ext_ref.shape[1]), F32)

    @pl.when(i > 0)
    def _():
        ext_ref[0:HALO, :] = ext_ref[ts:ts + HALO, :]

    ext_ref[HALO:HALO + ts, :] = glu_ref[...]
    w = w_ref[...]
    first_tap = HALO - (CONV_KERNEL - 1)
    for c in range(ts // CH_CONV):
        base = c * CH_CONV + first_tap
        acc = w[0:1, :] * ext_ref[base:base + CH_CONV, :]
        for k in range(1, CONV_KERNEL):
            acc = acc + w[k:k + 1, :] * ext_ref[base + k:base + k + CH_CONV, :]
        y = acc + b_ref[...]
        yc = y - jnp.mean(y, axis=-1, keepdims=True)
        yn = yc * lax.rsqrt(jnp.mean(yc * yc, axis=-1, keepdims=True) + EPS)
        z = yn * lg_ref[...] + lb_ref[...]
        o_ref[c * CH_CONV:(c + 1) * CH_CONV, :] = (z * jax.nn.sigmoid(z)).astype(o_ref.dtype)


def _conv(glu, w, bias, lg, lb, b, seq):
    t, cw = glu.shape
    ts = min(TS_CONV, seq)
    ns = seq // ts
    const = lambda bi, i: (0, 0)
    row = lambda bi, i: (bi * ns + i, 0)
    return pl.pallas_call(
        functools.partial(_conv_kernel, ts=ts), grid=(b, ns),
        in_specs=[pl.BlockSpec((ts, cw), row), pl.BlockSpec(w.shape, const),
                  pl.BlockSpec((1, cw), const), pl.BlockSpec((1, cw), const), pl.BlockSpec((1, cw), const)],
        out_specs=pl.BlockSpec((ts, cw), row),
        out_shape=jax.ShapeDtypeStruct((t, cw), BF16),
        scratch_shapes=[pltpu.VMEM((ts + HALO, cw), F32)],
        compiler_params=_cparams(("arbitrary", "arbitrary")), name="conformer_conv",
    )(glu, w, bias, lg, lb)


def _top2_gates(logits, n_experts):
    lane = lax.broadcasted_iota(jnp.int32, logits.shape, 1)
    x = jnp.where(lane < n_experts, logits, -jnp.inf)
    m1 = jnp.max(x, axis=-1, keepdims=True)
    i1 = jnp.min(jnp.where(x == m1, lane, LANES), axis=-1, keepdims=True)
    x2 = jnp.where(lane == i1, -jnp.inf, x)
    m2 = jnp.max(x2, axis=-1, keepdims=True)
    i2 = jnp.min(jnp.where(x2 == m2, lane, LANES), axis=-1, keepdims=True)
    e2 = jnp.exp(m2 - m1)
    inv = 1.0 / (1.0 + e2)
    return jnp.where(lane == i1, inv, jnp.where(lane == i2, e2 * inv, 0.0))


def _out_proj_kernel(*refs, n_experts):
    if n_experts:
        attn_ref, conv_ref, wo_ref, x_ref, g_ref, rt_ref, xo_ref, h_ref, gate_ref = refs
    else:
        attn_ref, conv_ref, wo_ref, x_ref, g_ref, xo_ref, h_ref = refs
    aw = attn_ref.shape[1]
    x = x_ref[...] + _dot(attn_ref[...], wo_ref[0:aw, :]) + _dot(conv_ref[...], wo_ref[aw:, :])
    xo_ref[...] = x
    h = _rms_rows(x, g_ref[...])
    h_ref[...] = h.astype(h_ref.dtype)
    if n_experts:
        h_hi, h_lo = _split_bf16(h)
        r_hi = rt_ref[0]
        r_lo = rt_ref[1]
        logits = _dot(h_hi, r_hi) + (_dot(h_hi, r_lo) + _dot(h_lo, r_hi))
        gate_ref[...] = _top2_gates(logits, n_experts)


def _out_proj(attn, conv, wo, x2, g, router_split=None, n_experts=0):
    t, d = x2.shape
    tm = min(TM_PROJ, t)
    row = lambda i: (i, 0)
    const = lambda i: (0, 0)
    in_specs = [pl.BlockSpec((tm, attn.shape[1]), row), pl.BlockSpec((tm, conv.shape[1]), row),
                pl.BlockSpec(wo.shape, const), pl.BlockSpec((tm, d), row), pl.BlockSpec((1, d), const)]
    out_shape = [jax.ShapeDtypeStruct((t, d), F32), jax.ShapeDtypeStruct((t, d), BF16)]
    out_specs = [pl.BlockSpec((tm, d), row), pl.BlockSpec((tm, d), row)]
    args = [attn, conv, wo, x2, g]
    if n_experts:
        in_specs.append(pl.BlockSpec(router_split.shape, lambda i: (0, 0, 0)))
        out_shape.append(jax.ShapeDtypeStruct((t, LANES), F32))
        out_specs.append(pl.BlockSpec((tm, LANES), row))
        args.append(router_split)
    return pl.pallas_call(
        functools.partial(_out_proj_kernel, n_experts=n_experts),
        grid=(t // tm,), in_specs=in_specs, out_specs=out_specs, out_shape=out_shape,
        compiler_params=_cparams(("parallel",)), name="out_proj",
    )(*args)


def _ffn_kernel(h_ref, x_ref, wg_ref, wu_ref, wd_ref, o_ref):
    f = pl.program_id(1)

    @pl.when(f == 0)
    def _():
        o_ref[...] = x_ref[...]

    h = h_ref[...]
    a = _dot(h, wg_ref[...])
    u = _dot(h, wu_ref[...])
    act = (a * jax.nn.sigmoid(a)) * u
    o_ref[...] += _dot(act.astype(BF16), wd_ref[...])


def _ffn(h, x2, wg, wu, wd):
    t, d = x2.shape
    dff = wg.shape[1]
    tm = min(TM_FFN, t)
    tf = TF_FFN
    return pl.pallas_call(
        _ffn_kernel, grid=(t // tm, dff // tf),
        in_specs=[pl.BlockSpec((tm, d), lambda i, f: (i, 0)), pl.BlockSpec((tm, d), lambda i, f: (i, 0)),
                  pl.BlockSpec((d, tf), lambda i, f: (0, f)), pl.BlockSpec((d, tf), lambda i, f: (0, f)),
                  pl.BlockSpec((tf, d), lambda i, f: (f, 0))],
        out_specs=pl.BlockSpec((tm, d), lambda i, f: (i, 0)),
        out_shape=jax.ShapeDtypeStruct((t, d), F32),
        compiler_params=_cparams(("parallel", "arbitrary")), name="ffn",
    )(h, x2, wg, wu, wd)


def _route_scan_kernel(g_ref, pos_t_ref, pos_ref, cnt_ref, tot_ref, carry_ref, *, n_experts):
    c = pl.program_id(0)

    @pl.when(c == 0)
    def _():
        carry_ref[...] = jnp.zeros_like(carry_ref)

    ct = g_ref.shape[0]
    routed = g_ref[...] > 0.0
    a = jnp.where(routed, 1.0, 0.0)
    earlier = lax.broadcasted_iota(jnp.int32, (ct, ct), 1) < lax.broadcasted_iota(jnp.int32, (ct, ct), 0)
    base = carry_ref[...]
    pos = jnp.where(routed, _dot(jnp.where(earlier, 1.0, 0.0).astype(BF16), a.astype(BF16)) + base, -1.0)
    pos_ref[...] = pos
    pos_t_ref[0] = jnp.transpose(pos)[0:n_experts, :]
    cnt_ref[0] = base
    total = base + jnp.sum(a, axis=0, keepdims=True)
    carry_ref[...] = total
    tot_ref[...] = total


def _route_scan(gates, n_experts):
    t = gates.shape[0]
    ct = CT_MOE
    nch = t // ct
    return pl.pallas_call(
        functools.partial(_route_scan_kernel, n_experts=n_experts), grid=(nch,),
        in_specs=[pl.BlockSpec((ct, LANES), lambda c: (c, 0))],
        out_specs=[pl.BlockSpec((1, n_experts, ct), lambda c: (c, 0, 0)),
                   pl.BlockSpec((ct, LANES), lambda c: (c, 0)),
                   pl.BlockSpec((1, 1, LANES), lambda c: (c, 0, 0)),
                   pl.BlockSpec((1, LANES), lambda c: (0, 0))],
        out_shape=[jax.ShapeDtypeStruct((nch, n_experts, ct), F32),
                   jax.ShapeDtypeStruct((t, LANES), F32),
                   jax.ShapeDtypeStruct((nch, 1, LANES), F32),
                   jax.ShapeDtypeStruct((1, LANES), F32)],
        scratch_shapes=[pltpu.VMEM((1, LANES), F32)],
        compiler_params=_cparams(("arbitrary",)), name="route_scan",
    )(gates)


def _moe_gather_kernel(blk_ref, chk_ref, exp_ref, first_ref, valid_ref,
                       h_ref, pos_t_ref, pstart_ref, xs_ref, acc_ref):
    w = pl.program_id(0)

    @pl.when(first_ref[w] == 1)
    def _():
        acc_ref[...] = jnp.zeros_like(acc_ref)

    @pl.when(valid_ref[w] == 1)
    def _():
        pos = pos_t_ref[0]
        slot = jnp.where(pos >= 0.0, pos + pstart_ref[...], -1.0)
        sub = lax.broadcasted_iota(jnp.int32, pos.shape, 0)
        slot_e = jnp.sum(jnp.where(sub == exp_ref[w], slot, 0.0), axis=0, keepdims=True)
        bg = acc_ref.shape[0]
        target = (blk_ref[w] * bg + lax.broadcasted_iota(jnp.int32, (bg, 1), 0)).astype(F32)
        onehot = jnp.where(slot_e == target, 1.0, 0.0).astype(BF16)
        acc_ref[...] += _dot(onehot, h_ref[...])

    xs_ref[...] = acc_ref[...].astype(xs_ref.dtype)


def _moe_gather(h, pos_t, pstart_col, items, n_slots):
    t, d = h.shape
    n_items = items[0].shape[0]
    n_e = pos_t.shape[1]
    grid_spec = pltpu.PrefetchScalarGridSpec(
        num_scalar_prefetch=5, grid=(n_items,),
        in_specs=[pl.BlockSpec((CT_MOE, d), lambda w, blk, chk, *_: (chk[w], 0)),
                  pl.BlockSpec((1, n_e, CT_MOE), lambda w, blk, chk, *_: (chk[w], 0, 0)),
                  pl.BlockSpec((n_e, 1), lambda w, *_: (0, 0))],
        out_specs=pl.BlockSpec((BG_MOE, d), lambda w, blk, *_: (blk[w], 0)),
        scratch_shapes=[pltpu.VMEM((BG_MOE, d), F32)])
    return pl.pallas_call(
        _moe_gather_kernel, grid_spec=grid_spec,
        out_shape=jax.ShapeDtypeStruct((n_slots, d), BF16),
        compiler_params=_cparams(("arbitrary",)), name="moe_gather",
    )(*items, h, pos_t, pstart_col)


def _moe_ffn_kernel(exp_ref, valid_ref, xs_ref, wg_ref, wu_ref, wd_ref, ys_ref, acc_ref):
    j = pl.program_id(0)
    f = pl.program_id(1)

    @pl.when(f == 0)
    def _():
        acc_ref[...] = jnp.zeros_like(acc_ref)

    @pl.when(valid_ref[j] == 1)
    def _():
        xs = xs_ref[...]
        a = _dot(xs, wg_ref[0])
        u = _dot(xs, wu_ref[0])
        acc_ref[...] += _dot(((a * jax.nn.sigmoid(a)) * u).astype(BF16), wd_ref[0])

    @pl.when(f == pl.num_programs(1) - 1)
    def _():
        ys_ref[...] = acc_ref[...].astype(ys_ref.dtype)


def _moe_ffn(xs, blk_expert, blk_valid, wg, wu, wd):
    n_slots, d = xs.shape
    dff = wg.shape[2]
    tf = TF_FFN
    nf = dff // tf

    def fidx(j, f, valid):
        return f * valid[j] + (nf - 1) * (1 - valid[j])

    grid_spec = pltpu.PrefetchScalarGridSpec(
        num_scalar_prefetch=2, grid=(n_slots // BM_MOE, nf),
        in_specs=[pl.BlockSpec((BM_MOE, d), lambda j, f, e, v: (j, 0)),
                  pl.BlockSpec((1, d, tf), lambda j, f, e, v: (e[j], 0, fidx(j, f, v))),
                  pl.BlockSpec((1, d, tf), lambda j, f, e, v: (e[j], 0, fidx(j, f, v))),
                  pl.BlockSpec((1, tf, d), lambda j, f, e, v: (e[j], fidx(j, f, v), 0))],
        out_specs=pl.BlockSpec((BM_MOE, d), lambda j, f, e, v: (j, 0)),
        scratch_shapes=[pltpu.VMEM((BM_MOE, d), F32)])
    return pl.pallas_call(
        _moe_ffn_kernel, grid_spec=grid_spec,
        out_shape=jax.ShapeDtypeStruct((n_slots, d), BF16),
        compiler_params=_cparams(("parallel", "arbitrary")), name="moe_ffn",
    )(blk_expert, blk_valid, xs, wg, wu, wd)


def _moe_combine_kernel(tile_ref, blk_ref, exp_ref, first_ref, valid_ref,
                        x_ref, pos_ref, g_ref, pstart_ref, ys_ref, o_ref):
    w = pl.program_id(0)

    @pl.when(first_ref[w] == 1)
    def _():
        o_ref[...] = x_ref[...]

    @pl.when(valid_ref[w] == 1)
    def _():
        pos = pos_ref[...]
        slot = jnp.where(pos >= 0.0, pos + pstart_ref[...], -1.0)
        mine = lax.broadcasted_iota(jnp.int32, pos.shape, 1) == exp_ref[w]
        slot_e = jnp.sum(jnp.where(mine, slot, 0.0), axis=-1, keepdims=True)
        gate_e = jnp.sum(jnp.where(mine, g_ref[...], 0.0), axis=-1, keepdims=True)
        bs = ys_ref.shape[0]
        target = (blk_ref[w] * bs + lax.broadcasted_iota(jnp.int32, (1, bs), 1)).astype(F32)
        onehot = jnp.where(slot_e == target, 1.0, 0.0).astype(BF16)
        o_ref[...] += gate_e * _dot(onehot, ys_ref[...])


def _moe_combine(x2, pos, gates, pstart_row, ys, items):
    t, d = x2.shape
    n_items = items[0].shape[0]
    tok = lambda w, tile, *_: (tile[w], 0)
    grid_spec = pltpu.PrefetchScalarGridSpec(
        num_scalar_prefetch=5, grid=(n_items,),
        in_specs=[pl.BlockSpec((CT_MOE, d), tok), pl.BlockSpec((CT_MOE, LANES), tok),
                  pl.BlockSpec((CT_MOE, LANES), tok), pl.BlockSpec((1, LANES), lambda w, *_: (0, 0)),
                  pl.BlockSpec((BG_MOE, d), lambda w, tile, blk, *_: (blk[w], 0))],
        out_specs=pl.BlockSpec((CT_MOE, d), tok))
    return pl.pallas_call(
        _moe_combine_kernel, grid_spec=grid_spec,
        out_shape=jax.ShapeDtypeStruct((t, d), F32),
        compiler_params=_cparams(("arbitrary",)), name="moe_combine",
    )(*items, x2, pos, gates, pstart_row, ys)


def _work_items(group_id, lo, hi, n_items_max):
    n = jnp.maximum(hi - lo + 1, 0)
    ends = jnp.cumsum(n)
    total = ends[-1]
    w = jnp.arange(n_items_max, dtype=jnp.int32)
    wc = jnp.minimum(w, total - 1)
    g = jnp.searchsorted(ends, wc, side='right').astype(jnp.int32)
    k = wc - (ends[g] - n[g])
    valid = (w < total).astype(jnp.int32)
    return group_id[g], lo[g] + k, g, ((k == 0) & (w < total)).astype(jnp.int32), valid


def _moe_routed(h, x2, gates, wg, wu, wd):
    t, d = x2.shape
    n_e = wg.shape[0]
    nch = t // CT_MOE
    n_slots = t * TOP_K + n_e * BM_MOE
    nbg = n_slots // BG_MOE
    pos_t, pos, cnt, tot = _route_scan(gates, n_e)

    counts = tot[0, :n_e].astype(jnp.int32)
    padded = (counts + BM_MOE - 1) // BM_MOE * BM_MOE
    pend = jnp.cumsum(padded)
    pstart = pend - padded
    cum = cnt[:, 0, :n_e].astype(jnp.int32)
    cum_end = jnp.concatenate([cum[1:], counts[None]], axis=0)

    sb = jnp.arange(nbg, dtype=jnp.int32)
    sb_e = jnp.minimum(jnp.searchsorted(pend, sb * BG_MOE, side='right'), n_e - 1).astype(jnp.int32)
    p_lo = sb * BG_MOE - pstart[sb_e]
    has_rows = (sb * BG_MOE < pend[-1]) & (p_lo < counts[sb_e])
    c_lo = jnp.sum(cum_end.T[sb_e] <= p_lo[:, None], axis=1).astype(jnp.int32)
    c_hi = jnp.sum(cum.T[sb_e] <= (p_lo + BG_MOE - 1)[:, None], axis=1).astype(jnp.int32) - 1
    c_lo = jnp.where(has_rows, c_lo, 0)
    c_hi = jnp.where(has_rows, c_hi, 0)
    g_blk, g_chk, g_grp, g_first, g_valid = _work_items(sb, c_lo, c_hi, nbg + n_e * nch)
    xs = _moe_gather(h, pos_t, pstart.astype(F32).reshape(n_e, 1),
                     (g_blk, g_chk, sb_e[g_grp], g_first, g_valid), n_slots)

    mb = jnp.arange(n_slots // BM_MOE, dtype=jnp.int32) * BM_MOE
    mb_e = jnp.minimum(jnp.searchsorted(pend, mb, side='right'), n_e - 1).astype(jnp.int32)
    mb_valid = ((mb < pend[-1]) & (mb - pstart[mb_e] < counts[mb_e])).astype(jnp.int32)
    ys = _moe_ffn(xs, mb_e, mb_valid, wg, wu, wd)

    tile = jnp.repeat(jnp.arange(nch, dtype=jnp.int32), n_e)
    te = jnp.tile(jnp.arange(n_e, dtype=jnp.int32), nch)
    s_lo = (pstart[None, :] + cum).reshape(-1)
    s_hi = (pstart[None, :] + cum_end).reshape(-1) - 1
    b_lo = s_lo // BG_MOE
    b_hi = jnp.where(s_hi >= s_lo, s_hi // BG_MOE, b_lo - 1)
    grp = jnp.arange(nch * n_e, dtype=jnp.int32)
    c_grp, c_blk, _, _, c_valid = _work_items(grp, b_lo, b_hi, nbg + n_e * nch)
    c_tile = tile[c_grp]
    c_first = jnp.concatenate([jnp.ones((1,), jnp.int32), (c_tile[1:] != c_tile[:-1]).astype(jnp.int32)])
    pstart_row = jnp.zeros((1, LANES), F32).at[0, :n_e].set(pstart.astype(F32))
    return _moe_combine(x2, pos, gates, pstart_row, ys, (c_tile, c_blk, te[c_grp], c_first, c_valid))


def _rope_tables(pos):
    half = ROPE_DIM // 2
    inv_freq = ROPE_THETA ** (-2.0 * jnp.arange(half, dtype=F32) / ROPE_DIM)
    ang = pos.astype(F32).reshape(-1, 1) * inv_freq
    cos, sin = jnp.cos(ang), jnp.sin(ang)
    n = ang.shape[0]
    rest = HEAD_DIM - ROPE_DIM
    cos_h = jnp.concatenate([cos, cos, jnp.ones((n, rest), F32)], axis=-1)
    sin_h = jnp.concatenate([-sin, sin, jnp.zeros((n, rest), F32)], axis=-1)
    return jnp.tile(cos_h, (1, HEADS_PER_VREG)), jnp.tile(sin_h, (1, HEADS_PER_VREG))


def _permute_w_in(w, conv_w):
    d = w.shape[0]
    kv_end = Q_W + 6 * KV_W
    g = w[:, kv_end:kv_end + N_HEADS * N_BRANCH]
    pad = jnp.zeros((d, LANES - GATES_PER_KV), w.dtype)
    gate_cols = []
    for kh in range(N_KV_HEADS):
        gate_cols += [g[:, kh * GATES_PER_KV:(kh + 1) * GATES_PER_KV], pad]
    u = w[:, kv_end + N_HEADS * N_BRANCH:]
    return jnp.concatenate([w[:, :kv_end]] + gate_cols + [u], axis=1).astype(BF16)


def _compress_weights(pos_emb, w1, w2):
    hidden = w1.shape[1]
    eye = jnp.eye(N_KV_HEADS, dtype=w1.dtype)
    w1r = w1.reshape(CMP_LEN, HEAD_DIM, hidden)
    halves = []
    for part in (w1r[:CMP_STRIDE], w1r[CMP_STRIDE:]):
        full = jnp.einsum('ldj,hg->lhdgj', part, eye)
        halves.append(full.reshape(CMP_STRIDE * N_KV_HEADS * HEAD_DIM, N_KV_HEADS * hidden).astype(BF16))
    w2p = jnp.einsum('jd,hg->hjgd', w2, eye).reshape(N_KV_HEADS * hidden, N_KV_HEADS * HEAD_DIM).astype(BF16)
    pos = []
    for part in (pos_emb[:CMP_STRIDE], pos_emb[CMP_STRIDE:]):
        pos.append(jnp.broadcast_to(part[:, None, :], (CMP_STRIDE, N_KV_HEADS, HEAD_DIM)).reshape(1, -1))
    return [halves[0], halves[1], w2p, pos[0], pos[1]]


def _selection_map_t(seq):
    ncp = seq // CMP_STRIDE
    n_cmp = (seq - CMP_LEN) // CMP_STRIDE + 1
    c0 = np.arange(ncp) * CMP_STRIDE
    s0 = np.arange(seq // SEL_LEN) * SEL_LEN
    ov = np.minimum(c0[None, :] + CMP_LEN, s0[:, None] + SEL_LEN) - np.maximum(c0[None, :], s0[:, None])
    m = np.clip(ov, 0, None) / CMP_LEN
    m[:, n_cmp:] = 0.0
    return jnp.asarray(m, dtype=BF16)


def kernel(x, positions, attn_norm_g, ffn_norm_g, w_in, w_out, q_norm_g, k_norm_g, cmp_pos_k, cmp_w1_k, cmp_w2_k, cmp_pos_v, cmp_w1_v, cmp_w2_v, conv_w, conv_b, conv_ln_g, conv_ln_b, ffn_w_gate, ffn_w_up, ffn_w_down, moe_router, moe_w_gate, moe_w_up, moe_w_down):
    b, seq, d = x.shape
    t = b * seq
    depth = w_in.shape[0]
    cw = conv_w.shape[2]
    ncp = seq // CMP_STRIDE
    n_cmp = (seq - CMP_LEN) // CMP_STRIDE + 1
    assert seq % max(TQ, KC, TS_CONV) == 0 and seq >= WINDOW + TQ

    cos_t, sin_t = _rope_tables(positions)
    cmp_end = np.minimum(np.arange(ncp) * CMP_STRIDE + CMP_LEN - 1, seq - 1)
    cos_c, sin_c = _rope_tables(positions[:, cmp_end])
    selmap_t = _selection_map_t(seq)
    tile2 = lambda v: jnp.tile(v.reshape(1, HEAD_DIM), (1, HEADS_PER_VREG))

    x2 = x.reshape(t, d)
    for layer in range(depth):
        w_perm = _permute_w_in(w_in[layer], cw)
        q, kc, vc, ks, vs, kw, vw, gates, glu = _in_proj(
            x2, attn_norm_g[layer].reshape(1, d), w_perm, cos_t, sin_t,
            tile2(q_norm_g[layer]), tile2(k_norm_g[layer, 1]), tile2(k_norm_g[layer, 2]), seq)
        kcmp, vcmp = _compress(
            kc.reshape(t // CMP_STRIDE, CMP_STRIDE * KV_W), vc.reshape(t // CMP_STRIDE, CMP_STRIDE * KV_W),
            _compress_weights(cmp_pos_k[layer], cmp_w1_k[layer], cmp_w2_k[layer]),
            _compress_weights(cmp_pos_v[layer], cmp_w1_v[layer], cmp_w2_v[layer]),
            tile2(k_norm_g[layer, 0]), cos_c, sin_c, b, ncp)
        score_bound = (HEAD_DIM ** 0.5 * LOG2_E * SCORE_BOUND_MARGIN * jnp.max(jnp.abs(q_norm_g[layer]))
                       * jnp.max(jnp.abs(k_norm_g[layer]))).astype(F32).reshape(1)
        attn = _attention(score_bound, q, kcmp, vcmp, ks, vs, kw, vw, gates, selmap_t, b, seq)
        conv = _conv(glu, conv_w[layer], conv_b[layer].reshape(1, cw), conv_ln_g[layer].reshape(1, cw),
                     conv_ln_b[layer].reshape(1, cw), b, seq)
        wo = w_out[layer].astype(BF16)
        g2 = ffn_norm_g[layer].reshape(1, d)
        i = layer // 2
        if layer % 2 == 0:
            x2, h = _out_proj(attn, conv, wo, x2, g2)
            x2 = _ffn(h, x2, ffn_w_gate[i].astype(BF16), ffn_w_up[i].astype(BF16), ffn_w_down[i].astype(BF16))
        else:
            n_e = moe_router.shape[2]
            r = jnp.pad(moe_router[i], ((0, 0), (0, LANES - n_e)))
            r_hi = r.astype(BF16)
            r_lo = (r - r_hi.astype(F32)).astype(BF16)
            x2, h, route = _out_proj(attn, conv, wo, x2, g2, jnp.stack([r_hi, r_lo]), n_e)
            x2 = _moe_routed(h, x2, route, moe_w_gate[i].astype(BF16), moe_w_up[i].astype(BF16),
                             moe_w_down[i].astype(BF16))
    return x2.reshape(b, seq, d)
```

```python
import functools
import math

import jax
import jax.numpy as jnp
import numpy as np
from jax import lax
from jax.experimental import pallas as pl
from jax.experimental.pallas import tpu as pltpu

F32 = jnp.float32
BF16 = jnp.bfloat16

N_HEADS = 8
N_KV_HEADS = 2
Q_PER_KV = N_HEADS // N_KV_HEADS
HEAD_DIM = 64
N_BRANCH = 3
CMP_LEN = 32
CMP_STRIDE = 16
SEL_LEN = 64
SEL_TOPK = 16
N_LOCAL_SEL = 2
WINDOW = 512
CONV_KERNEL = 31
ROPE_THETA = 500000.0
ROPE_DIM = HEAD_DIM // 4
TOP_K = 2
EPS = 1e-6

LANES = 128
SUBLANES = 8
LOG2_E = math.log2(math.e)
NEG_BIG = -(2.0 ** 100)
MAX_SCORE_BOUND = 50.0
SCORE_BOUND_MARGIN = 1.02
HEADS_PER_VREG = LANES // HEAD_DIM
VMEM_LIMIT = 56 * 1024 * 1024

TM_PROJ = 512
TQ = 128
KC = 512
TS_CONV = 512
CH_CONV = 32
HALO = 32
TM_FFN = 1024
TF_FFN = 512
CT_MOE = 512
BG_MOE = 256
BM_MOE = 512


def _cparams(sem):
    return pltpu.CompilerParams(dimension_semantics=sem, vmem_limit_bytes=VMEM_LIMIT)


def _dot(a, b):
    return jnp.dot(a, b, preferred_element_type=F32)


def _dot_nt(a, b):
    return lax.dot_general(a, b, (((1,), (1,)), ((), ())), preferred_element_type=F32)


def _split_bf16(x):
    hi = x.astype(BF16)
    lo = (x - hi.astype(F32)).astype(BF16)
    return hi, lo


def _rms_rows(x, g):
    ms = jnp.mean(x * x, axis=-1, keepdims=True)
    return x * lax.rsqrt(ms + EPS) * g


def _head_block_ones():
    r = lax.broadcasted_iota(jnp.int32, (LANES, LANES), 0) // HEAD_DIM
    c = lax.broadcasted_iota(jnp.int32, (LANES, LANES), 1) // HEAD_DIM
    return jnp.where(r == c, 1.0, 0.0).astype(BF16)


def _head_norm_rope(xg, gain, cos, sin, ones_bd):
    ms = _dot((xg * xg).astype(BF16), ones_bd) * (1.0 / HEAD_DIM)
    y = xg * lax.rsqrt(ms + EPS) * gain
    lane = lax.broadcasted_iota(jnp.int32, y.shape, 1) % HEAD_DIM
    half = ROPE_DIM // 2
    partner = jnp.where(lane < half, pltpu.roll(y, LANES - half, 1), pltpu.roll(y, half, 1))
    return y * cos + partner * sin


Q_W = N_HEADS * HEAD_DIM
KV_W = N_KV_HEADS * HEAD_DIM
SEG_Q = 0
SEG_KV = Q_W
SEG_GATE = SEG_KV + 6 * KV_W
SEG_UA = SEG_GATE + N_KV_HEADS * LANES
GATES_PER_KV = Q_PER_KV * N_BRANCH


def _in_proj_kernel(x_ref, g_ref, w_ref, cos_ref, sin_ref, qg_ref, ksg_ref, kwg_ref,
                    q_ref, kc_ref, vc_ref, ks_ref, vs_ref, kw_ref, vw_ref, gate_ref, glu_ref,
                    *, conv_w, seq):
    h = _rms_rows(x_ref[...], g_ref[...]).astype(BF16)
    tm = h.shape[0]
    cos = cos_ref[...]
    sin = sin_ref[...]
    ones_bd = _head_block_ones()
    lane = lax.broadcasted_iota(jnp.int32, (tm, LANES), 1)
    tok = lax.rem(pl.program_id(0) * tm, seq) + lax.broadcasted_iota(jnp.int32, (tm, 1), 0)
    block_aug = jnp.where(lane - HEAD_DIM == tok // SEL_LEN, NEG_BIG, 0.0)
    ones_aug = jnp.where(lane == HEAD_DIM, 1.0, 0.0)

    def seg(lo, width):
        return _dot(h, w_ref[:, lo:lo + width])

    def put_heads(ref, first, val, aug):
        for j in range(HEADS_PER_VREG):
            head = val if j == 0 else pltpu.roll(val, LANES - j * HEAD_DIM, 1)
            ref[first + j] = jnp.where(lane < HEAD_DIM, head, aug).astype(ref.dtype)

    scale = HEAD_DIM ** -0.5 * LOG2_E
    qg = qg_ref[...]
    for c in range(Q_W // LANES):
        y = _head_norm_rope(seg(SEG_Q + c * LANES, LANES), qg, cos, sin, ones_bd) * scale
        put_heads(q_ref, c * HEADS_PER_VREG, y, 0.0)
    kc_ref[...] = seg(SEG_KV, KV_W)
    vc_ref[...] = seg(SEG_KV + KV_W, KV_W)
    put_heads(ks_ref, 0, _head_norm_rope(seg(SEG_KV + 2 * KV_W, KV_W), ksg_ref[...], cos, sin, ones_bd), block_aug)
    put_heads(vs_ref, 0, seg(SEG_KV + 3 * KV_W, KV_W), ones_aug)
    put_heads(kw_ref, 0, _head_norm_rope(seg(SEG_KV + 4 * KV_W, KV_W), kwg_ref[...], cos, sin, ones_bd), 0.0)
    put_heads(vw_ref, 0, seg(SEG_KV + 5 * KV_W, KV_W), ones_aug)
    gate_ref[...] = jax.nn.sigmoid(seg(SEG_GATE, N_KV_HEADS * LANES))
    a = seg(SEG_UA, conv_w)
    g = seg(SEG_UA + conv_w, conv_w)
    glu_ref[...] = a * jax.nn.sigmoid(g)


def _in_proj(x2, g, w_perm, cos_t, sin_t, qg, ksg, kwg, seq):
    t, d = x2.shape
    conv_w = (w_perm.shape[1] - SEG_UA) // 2
    tm = min(TM_PROJ, t)
    row = lambda i: (i, 0)
    const = lambda i: (0, 0)
    head_row = lambda i: (0, i, 0)
    out_shape = [
        jax.ShapeDtypeStruct((N_HEADS, t, LANES), BF16),
        jax.ShapeDtypeStruct((t, KV_W), F32),
        jax.ShapeDtypeStruct((t, KV_W), F32),
        jax.ShapeDtypeStruct((N_KV_HEADS, t, LANES), BF16),
        jax.ShapeDtypeStruct((N_KV_HEADS, t, LANES), BF16),
        jax.ShapeDtypeStruct((N_KV_HEADS, t, LANES), BF16),
        jax.ShapeDtypeStruct((N_KV_HEADS, t, LANES), BF16),
        jax.ShapeDtypeStruct((t, N_KV_HEADS * LANES), F32),
        jax.ShapeDtypeStruct((t, conv_w), F32),
    ]
    kv_spec = pl.BlockSpec((N_KV_HEADS, tm, LANES), head_row)
    out_specs = [
        pl.BlockSpec((N_HEADS, tm, LANES), head_row),
        pl.BlockSpec((tm, KV_W), row), pl.BlockSpec((tm, KV_W), row),
        kv_spec, kv_spec, kv_spec, kv_spec,
        pl.BlockSpec((tm, N_KV_HEADS * LANES), row),
        pl.BlockSpec((tm, conv_w), row),
    ]
    in_specs = [
        pl.BlockSpec((tm, d), row), pl.BlockSpec((1, d), const),
        pl.BlockSpec(w_perm.shape, const),
        pl.BlockSpec((tm, LANES), row), pl.BlockSpec((tm, LANES), row),
        pl.BlockSpec((1, LANES), const), pl.BlockSpec((1, LANES), const), pl.BlockSpec((1, LANES), const),
    ]
    return pl.pallas_call(
        functools.partial(_in_proj_kernel, conv_w=conv_w, seq=seq),
        grid=(t // tm,), in_specs=in_specs, out_specs=out_specs, out_shape=out_shape,
        compiler_params=_cparams(("parallel",)), name="in_proj",
    )(x2, g, w_perm, cos_t, sin_t, qg, ksg, kwg)


def _gelu_tanh(x):
    c = math.sqrt(2.0 / math.pi)
    return 0.5 * x * (1.0 + jnp.tanh(c * (x + 0.044715 * (x * x * x))))


def _compress_kernel(k_ref, v_ref, w1ak_ref, w1bk_ref, w2k_ref, pak_ref, pbk_ref,
                     w1av_ref, w1bv_ref, w2v_ref, pav_ref, pbv_ref,
                     kg_ref, cos_ref, sin_ref, ko_ref, vo_ref):
    def mlp(x_ref, w1a_ref, w1b_ref, w2_ref, pa_ref, pb_ref):
        n = x_ref.shape[0] // CMP_STRIDE
        first = second = None
        for l in range(CMP_STRIDE):
            x = x_ref[pl.ds(l, n, stride=CMP_STRIDE), :]
            cols = slice(l * KV_W, (l + 1) * KV_W)
            fa = _dot((x + pa_ref[:, cols]).astype(BF16), w1a_ref[cols, :])
            fb = _dot((x + pb_ref[:, cols]).astype(BF16), w1b_ref[cols, :])
            first = fa if first is None else first + fa
            second = fb if second is None else second + fb
        hid = first + pltpu.roll(second, n - 1, 0)
        return _dot(_gelu_tanh(hid).astype(BF16), w2_ref[...])

    kc = mlp(k_ref, w1ak_ref, w1bk_ref, w2k_ref, pak_ref, pbk_ref)
    kc = _head_norm_rope(kc, kg_ref[...], cos_ref[...], sin_ref[...], _head_block_ones())
    vc = mlp(v_ref, w1av_ref, w1bv_ref, w2v_ref, pav_ref, pbv_ref)
    lane = lax.broadcasted_iota(jnp.int32, kc.shape, 1)
    for j in range(N_KV_HEADS):
        for val, ref in ((kc, ko_ref), (vc, vo_ref)):
            head = val if j == 0 else pltpu.roll(val, LANES - j * HEAD_DIM, 1)
            ref[0, j] = jnp.where(lane < HEAD_DIM, head, 0.0).astype(ref.dtype)


def _compress(kc, vc, wk, wv, kg, cosc, sinc, b, ncp):
    seq = kc.shape[0] // b
    const = lambda i: (0, 0)
    row = lambda i: (i, 0)

    def wspecs(ws):
        return [pl.BlockSpec(w.shape, const) for w in ws]

    out = jax.ShapeDtypeStruct((b, N_KV_HEADS, ncp, LANES), BF16)
    ospec = pl.BlockSpec((1, N_KV_HEADS, ncp, LANES), lambda i: (i, 0, 0, 0))
    return pl.pallas_call(
        _compress_kernel, grid=(b,),
        in_specs=[pl.BlockSpec((seq, KV_W), row), pl.BlockSpec((seq, KV_W), row)]
        + wspecs(wk) + wspecs(wv)
        + [pl.BlockSpec((1, LANES), const), pl.BlockSpec((ncp, LANES), row), pl.BlockSpec((ncp, LANES), row)],
        out_specs=[ospec, ospec], out_shape=[out, out],
        compiler_params=_cparams(("parallel",)), name="compress",
    )(kc, vc, *wk, *wv, kg, cosc, sinc)


def _attn_kernel(off_ref, q_ref, kc_ref, vc_ref, ks_ref, vs_ref, kw_ref, vw_ref, gate_ref, selmap_ref,
                 o_ref, *, seq, tq, kc_len, top_n, bounded):
    i = pl.program_id(2)
    t0 = i * tq
    rows = Q_PER_KV * tq
    n_sel = seq // SEL_LEN
    q2 = q_ref[...].reshape(rows, LANES)
    t_row = t0 + (lax.broadcasted_iota(jnp.int32, (rows, 1), 0) & (tq - 1))
    t_tok = t0 + lax.broadcasted_iota(jnp.int32, (tq, 1), 0)
    neg_offset = -off_ref[0] if bounded else 0.0

    def add_bias(s, bias):
        return (s.reshape(s.shape[0] // tq, tq, s.shape[1]) + bias[None]).reshape(s.shape)

    kcmp = kc_ref[0, 0]
    ncp = kcmp.shape[0]
    s_c = _dot_nt(q2, kcmp)
    cmp_end = lax.broadcasted_iota(jnp.int32, (1, ncp), 1) * CMP_STRIDE + (CMP_LEN - 1)
    if bounded:
        e_c = jnp.exp2(add_bias(s_c, jnp.where(cmp_end <= t_tok, neg_offset, NEG_BIG)))
    else:
        s_c = jnp.where(cmp_end <= t_row, s_c, -jnp.inf)
        m_c = jnp.max(s_c, axis=-1, keepdims=True)
        e_c = jnp.exp2(s_c - jnp.where(m_c == -jnp.inf, 0.0, m_c))
    p_c = e_c * (1.0 / jnp.maximum(jnp.sum(e_c, axis=-1, keepdims=True), jnp.finfo(F32).tiny))
    o_c = _dot(p_c.astype(BF16), vc_ref[0, 0])

    def denominator(acc):
        return acc[:, HEAD_DIM:HEAD_DIM + 1]

    span = min(WINDOW + tq, seq)
    w0 = pl.multiple_of(jnp.maximum(t0 - WINDOW, 0), tq)
    key_w = w0 + lax.broadcasted_iota(jnp.int32, (1, span), 1)
    bias_w = jnp.where((key_w <= t_tok) & (key_w > t_tok - WINDOW), neg_offset, NEG_BIG)
    s_w = add_bias(_dot_nt(q2, kw_ref[0, pl.ds(w0, span), :]), bias_w)
    if bounded:
        p_w = jnp.exp2(s_w).astype(BF16)
    else:
        p_w = jnp.exp2((s_w - jnp.max(s_w, axis=-1, keepdims=True)).astype(BF16))
    acc_w = _dot(p_w, vw_ref[0, pl.ds(w0, span), :])
    o_w = acc_w * (1.0 / denominator(acc_w))

    p_hi, p_lo = _split_bf16(jnp.sum(p_c.reshape(Q_PER_KV, tq, ncp), axis=0))
    selmap = selmap_ref[...]
    imp = _dot_nt(selmap, p_hi) + _dot_nt(selmap, p_lo)
    blk = lax.broadcasted_iota(jnp.int32, (n_sel, tq), 0)
    cur = (t0 + lax.broadcasted_iota(jnp.int32, (n_sel, tq), 1)) // SEL_LEN
    causal_blk = blk <= cur
    forced = (blk == 0) | (causal_blk & (blk > cur - N_LOCAL_SEL))
    score = jnp.where(forced, jnp.inf, jnp.where(causal_blk, imp, -jnp.inf))
    sub = lax.broadcasted_iota(jnp.int32, (SUBLANES, tq), 0)
    groups = [score[g * SUBLANES:(g + 1) * SUBLANES, :] for g in range(n_sel // SUBLANES)]
    ranks = [jnp.zeros((SUBLANES, tq), F32) for _ in groups]
    for jp in range(n_sel):
        other = jnp.broadcast_to(score[jp:jp + 1, :], (SUBLANES, tq))
        for g, sg in enumerate(groups):
            first = g * SUBLANES
            if first > jp:
                inc = jnp.where(other >= sg, 1.0, 0.0)
            elif first + SUBLANES - 1 <= jp:
                inc = jnp.where(other > sg, 1.0, 0.0)
            else:
                inc = jnp.where(other > sg, 1.0, jnp.where((other == sg) & (sub > jp - first), 1.0, 0.0))
            ranks[g] = ranks[g] + inc
    chosen_flag = neg_offset * (1.0 / NEG_BIG)
    block_flags = jnp.where(jnp.concatenate(ranks, axis=0) < top_n, chosen_flag, 1.0)
    flag_rows = [jnp.zeros((HEAD_DIM, tq), F32), block_flags]
    if HEAD_DIM + n_sel < LANES:
        flag_rows.append(jnp.zeros((LANES - HEAD_DIM - n_sel, tq), F32))
    flags = jnp.transpose(jnp.concatenate(flag_rows, axis=0))
    q_sel = (q2.reshape(Q_PER_KV, tq, LANES) + flags.astype(BF16)[None]).reshape(rows, LANES)

    def sel_chunk(k0, carry, bias):
        s = _dot_nt(q_sel, ks_ref[0, pl.ds(k0, kc_len), :])
        if bias is not None:
            s = add_bias(s, bias)
        v = vs_ref[0, pl.ds(k0, kc_len), :]
        if bounded:
            (acc,) = carry
            return (acc + _dot(jnp.exp2(s).astype(BF16), v),)
        m, acc = carry
        m_new = jnp.maximum(m, jnp.max(s, axis=-1, keepdims=True))
        p = jnp.exp2((s - m_new).astype(BF16))
        return m_new, jnp.exp2(m - m_new) * acc + _dot(p, v)

    n_full = t0 // kc_len
    init = (jnp.zeros((rows, LANES), F32),)
    if not bounded:
        init = (jnp.full((rows, 1), -jnp.inf, F32),) + init
    carry = lax.fori_loop(
        0, n_full, lambda c, cr: sel_chunk(pl.multiple_of(c * kc_len, kc_len), cr, None), init)
    kd = pl.multiple_of(n_full * kc_len, kc_len)
    key_d = kd + lax.broadcasted_iota(jnp.int32, (1, kc_len), 1)
    acc_s = sel_chunk(kd, carry, jnp.where(key_d <= t_tok, 0.0, NEG_BIG))[-1]
    o_s = acc_s * (1.0 / denominator(acc_s))

    gates = gate_ref[...]

    def gate_col(br):
        cols = [gates[:, g * N_BRANCH + br:g * N_BRANCH + br + 1] for g in range(Q_PER_KV)]
        return jnp.concatenate(cols, axis=0)

    o = gate_col(0) * o_c + gate_col(1) * o_s + gate_col(2) * o_w
    o3 = o.reshape(Q_PER_KV, tq, LANES)
    lane = lax.broadcasted_iota(jnp.int32, (tq, LANES), 1)
    pairs = [jnp.where(lane < HEAD_DIM, o3[g], pltpu.roll(o3[g + 1], HEAD_DIM, 1))
             for g in range(0, Q_PER_KV, HEADS_PER_VREG)]
    o_ref[...] = jnp.concatenate(pairs, axis=-1).astype(o_ref.dtype)


def _attention(score_bound, q, kcmp, vcmp, ks, vs, kw, vw, gates, selmap_t, b, seq):
    t = b * seq
    tq = min(TQ, seq)
    kc_len = min(KC, seq)
    nq = seq // tq
    ncp = kcmp.shape[2]
    n_sel = seq // SEL_LEN
    assert HEAD_DIM + n_sel <= LANES, "selection-block flags must fit beside the head dims"
    top_n = min(SEL_TOPK, n_sel)
    cmp_spec = pl.BlockSpec((1, 1, ncp, LANES), lambda bi, kh, i: (bi, kh, 0, 0))
    seq_spec = pl.BlockSpec((1, seq, LANES), lambda bi, kh, i: (kh, bi, 0))

    def run(bounded):
        return pl.pallas_call(
            functools.partial(_attn_kernel, seq=seq, tq=tq, kc_len=kc_len, top_n=top_n, bounded=bounded),
            grid=(b, N_KV_HEADS, nq),
            in_specs=[
                pl.BlockSpec(memory_space=pltpu.SMEM),
                pl.BlockSpec((Q_PER_KV, tq, LANES), lambda bi, kh, i: (kh, bi * nq + i, 0)),
                cmp_spec, cmp_spec, seq_spec, seq_spec, seq_spec, seq_spec,
                pl.BlockSpec((tq, LANES), lambda bi, kh, i: (bi * nq + i, kh)),
                pl.BlockSpec(selmap_t.shape, lambda bi, kh, i: (0, 0)),
            ],
            out_specs=pl.BlockSpec((tq, Q_PER_KV * HEAD_DIM), lambda bi, kh, i: (bi * nq + i, kh)),
            out_shape=jax.ShapeDtypeStruct((t, N_HEADS * HEAD_DIM), BF16),
            compiler_params=_cparams(("parallel", "parallel", "arbitrary")),
            name="nsa_attention" if bounded else "nsa_attention_running_max",
        )(score_bound, q, kcmp, vcmp, ks, vs, kw, vw, gates, selmap_t)

    return lax.cond(score_bound[0] < MAX_SCORE_BOUND, lambda: run(True), lambda: run(False))


def _conv_kernel(glu_ref, w_ref, b_ref, lg_ref, lb_ref, o_ref, ext_ref, shift_ref, *, ts):
    i = pl.program_id(1)

    @pl.when(i == 0)
    def _():
        ext_ref[0:HALO, :] = jnp.zeros((HALO, ext_ref.shape[1]), F32)

    @pl.when(i > 0)
    def _():
        ext_ref[0:HALO, :] = ext_ref[ts:ts + HALO, :]

    ext_ref[HALO:HALO + ts, :] = glu_ref[...]
    n_shift = shift_ref.shape[1]
    for r in range(1, SUBLANES):
        shift_ref[r - 1] = ext_ref[r:r + n_shift, :]

    def rows_from(o):
        r = o % SUBLANES
        if r == 0:
            return ext_ref[o:o + CH_CONV, :]
        return shift_ref[r - 1, o - r:o - r + CH_CONV, :]

    w = w_ref[...]
    first_tap = HALO - (CONV_KERNEL - 1)
    for c in range(ts // CH_CONV):
        base = c * CH_CONV + first_tap
        acc = w[0:1, :] * rows_from(base)
        for k in range(1, CONV_KERNEL):
            acc = acc + w[k:k + 1, :] * rows_from(base + k)
        y = acc + b_ref[...]
        yc = y - jnp.mean(y, axis=-1, keepdims=True)
        yn = yc * lax.rsqrt(jnp.mean(yc * yc, axis=-1, keepdims=True) + EPS)
        z = yn * lg_ref[...] + lb_ref[...]
        o_ref[c * CH_CONV:(c + 1) * CH_CONV, :] = (z * jax.nn.sigmoid(z)).astype(o_ref.dtype)


def _conv(glu, w, bias, lg, lb, b, seq):
    t, cw = glu.shape
    ts = min(TS_CONV, seq)
    ns = seq // ts
    const = lambda bi, i: (0, 0)
    row = lambda bi, i: (bi * ns + i, 0)
    return pl.pallas_call(
        functools.partial(_conv_kernel, ts=ts), grid=(b, ns),
        in_specs=[pl.BlockSpec((ts, cw), row), pl.BlockSpec(w.shape, const),
                  pl.BlockSpec((1, cw), const), pl.BlockSpec((1, cw), const), pl.BlockSpec((1, cw), const)],
        out_specs=pl.BlockSpec((ts, cw), row),
        out_shape=jax.ShapeDtypeStruct((t, cw), BF16),
        scratch_shapes=[pltpu.VMEM((ts + HALO, cw), F32),
                        pltpu.VMEM((SUBLANES - 1, ts + HALO - SUBLANES, cw), F32)],
        compiler_params=_cparams(("arbitrary", "arbitrary")), name="conformer_conv",
    )(glu, w, bias, lg, lb)


def _top2_gates(logits, n_experts):
    lane = lax.broadcasted_iota(jnp.int32, logits.shape, 1)
    x = jnp.where(lane < n_experts, logits, -jnp.inf)
    m1 = jnp.max(x, axis=-1, keepdims=True)
    i1 = jnp.min(jnp.where(x == m1, lane, LANES), axis=-1, keepdims=True)
    x2 = jnp.where(lane == i1, -jnp.inf, x)
    m2 = jnp.max(x2, axis=-1, keepdims=True)
    i2 = jnp.min(jnp.where(x2 == m2, lane, LANES), axis=-1, keepdims=True)
    e2 = jnp.exp(m2 - m1)
    inv = 1.0 / (1.0 + e2)
    return jnp.where(lane == i1, inv, jnp.where(lane == i2, e2 * inv, 0.0))


def _out_proj_kernel(*refs, n_experts):
    if n_experts:
        attn_ref, conv_ref, wo_ref, x_ref, g_ref, rt_ref, xo_ref, h_ref, gate_ref = refs
    else:
        attn_ref, conv_ref, wo_ref, x_ref, g_ref, xo_ref, h_ref = refs
    aw = attn_ref.shape[1]
    x = x_ref[...] + _dot(attn_ref[...], wo_ref[0:aw, :]) + _dot(conv_ref[...], wo_ref[aw:, :])
    xo_ref[...] = x
    h = _rms_rows(x, g_ref[...])
    h_ref[...] = h.astype(h_ref.dtype)
    if n_experts:
        h_hi, h_lo = _split_bf16(h)
        r_hi = rt_ref[0]
        r_lo = rt_ref[1]
        logits = _dot(h_hi, r_hi) + (_dot(h_hi, r_lo) + _dot(h_lo, r_hi))
        gate_ref[...] = _top2_gates(logits, n_experts)


def _out_proj(attn, conv, wo, x2, g, router_split=None, n_experts=0):
    t, d = x2.shape
    tm = min(TM_PROJ, t)
    row = lambda i: (i, 0)
    const = lambda i: (0, 0)
    in_specs = [pl.BlockSpec((tm, attn.shape[1]), row), pl.BlockSpec((tm, conv.shape[1]), row),
                pl.BlockSpec(wo.shape, const), pl.BlockSpec((tm, d), row), pl.BlockSpec((1, d), const)]
    out_shape = [jax.ShapeDtypeStruct((t, d), F32), jax.ShapeDtypeStruct((t, d), BF16)]
    out_specs = [pl.BlockSpec((tm, d), row), pl.BlockSpec((tm, d), row)]
    args = [attn, conv, wo, x2, g]
    if n_experts:
        in_specs.append(pl.BlockSpec(router_split.shape, lambda i: (0, 0, 0)))
        out_shape.append(jax.ShapeDtypeStruct((t, LANES), F32))
        out_specs.append(pl.BlockSpec((tm, LANES), row))
        args.append(router_split)
    return pl.pallas_call(
        functools.partial(_out_proj_kernel, n_experts=n_experts),
        grid=(t // tm,), in_specs=in_specs, out_specs=out_specs, out_shape=out_shape,
        compiler_params=_cparams(("parallel",)), name="out_proj",
    )(*args)


def _ffn_kernel(h_ref, x_ref, wg_ref, wu_ref, wd_ref, o_ref):
    f = pl.program_id(1)

    @pl.when(f == 0)
    def _():
        o_ref[...] = x_ref[...]

    h = h_ref[...]
    a = _dot(h, wg_ref[...])
    u = _dot(h, wu_ref[...])
    act = (a * jax.nn.sigmoid(a)) * u
    o_ref[...] += _dot(act.astype(BF16), wd_ref[...])


def _ffn(h, x2, wg, wu, wd):
    t, d = x2.shape
    dff = wg.shape[1]
    tm = min(TM_FFN, t)
    tf = TF_FFN
    return pl.pallas_call(
        _ffn_kernel, grid=(t // tm, dff // tf),
        in_specs=[pl.BlockSpec((tm, d), lambda i, f: (i, 0)), pl.BlockSpec((tm, d), lambda i, f: (i, 0)),
                  pl.BlockSpec((d, tf), lambda i, f: (0, f)), pl.BlockSpec((d, tf), lambda i, f: (0, f)),
                  pl.BlockSpec((tf, d), lambda i, f: (f, 0))],
        out_specs=pl.BlockSpec((tm, d), lambda i, f: (i, 0)),
        out_shape=jax.ShapeDtypeStruct((t, d), F32),
        compiler_params=_cparams(("parallel", "arbitrary")), name="ffn",
    )(h, x2, wg, wu, wd)


def _route_scan_kernel(g_ref, pos_t_ref, pos_ref, cnt_ref, tot_ref, carry_ref, *, n_experts):
    c = pl.program_id(0)

    @pl.when(c == 0)
    def _():
        carry_ref[...] = jnp.zeros_like(carry_ref)

    ct = g_ref.shape[0]
    routed = g_ref[...] > 0.0
    a = jnp.where(routed, 1.0, 0.0)
    earlier = lax.broadcasted_iota(jnp.int32, (ct, ct), 1) < lax.broadcasted_iota(jnp.int32, (ct, ct), 0)
    base = carry_ref[...]
    pos = jnp.where(routed, _dot(jnp.where(earlier, 1.0, 0.0).astype(BF16), a.astype(BF16)) + base, -1.0)
    pos_ref[...] = pos
    pos_t_ref[0] = jnp.transpose(pos)[0:n_experts, :]
    cnt_ref[0] = base
    total = base + jnp.sum(a, axis=0, keepdims=True)
    carry_ref[...] = total
    tot_ref[...] = total


def _route_scan(gates, n_experts):
    t = gates.shape[0]
    ct = CT_MOE
    nch = t // ct
    return pl.pallas_call(
        functools.partial(_route_scan_kernel, n_experts=n_experts), grid=(nch,),
        in_specs=[pl.BlockSpec((ct, LANES), lambda c: (c, 0))],
        out_specs=[pl.BlockSpec((1, n_experts, ct), lambda c: (c, 0, 0)),
                   pl.BlockSpec((ct, LANES), lambda c: (c, 0)),
                   pl.BlockSpec((1, 1, LANES), lambda c: (c, 0, 0)),
                   pl.BlockSpec((1, LANES), lambda c: (0, 0))],
        out_shape=[jax.ShapeDtypeStruct((nch, n_experts, ct), F32),
                   jax.ShapeDtypeStruct((t, LANES), F32),
                   jax.ShapeDtypeStruct((nch, 1, LANES), F32),
                   jax.ShapeDtypeStruct((1, LANES), F32)],
        scratch_shapes=[pltpu.VMEM((1, LANES), F32)],
        compiler_params=_cparams(("arbitrary",)), name="route_scan",
    )(gates)


def _moe_gather_kernel(blk_ref, chk_ref, exp_ref, first_ref, valid_ref,
                       h_ref, pos_t_ref, pstart_ref, xs_ref):
    w = pl.program_id(0)

    @pl.when(first_ref[w] == 1)
    def _():
        xs_ref[...] = jnp.zeros_like(xs_ref)

    @pl.when(valid_ref[w] == 1)
    def _():
        pos = pos_t_ref[0]
        slot = jnp.where(pos >= 0.0, pos + pstart_ref[...], -1.0)
        sub = lax.broadcasted_iota(jnp.int32, pos.shape, 0)
        slot_e = jnp.sum(jnp.where(sub == exp_ref[w], slot, 0.0), axis=0, keepdims=True)
        bg = xs_ref.shape[0]
        target = (blk_ref[w] * bg + lax.broadcasted_iota(jnp.int32, (bg, 1), 0)).astype(F32)
        onehot = jnp.where(slot_e == target, 1.0, 0.0).astype(BF16)
        xs_ref[...] += _dot(onehot, h_ref[...]).astype(xs_ref.dtype)


def _moe_gather(h, pos_t, pstart_col, items, n_slots):
    t, d = h.shape
    n_items = items[0].shape[0]
    n_e = pos_t.shape[1]
    grid_spec = pltpu.PrefetchScalarGridSpec(
        num_scalar_prefetch=5, grid=(n_items,),
        in_specs=[pl.BlockSpec((CT_MOE, d), lambda w, blk, chk, *_: (chk[w], 0)),
                  pl.BlockSpec((1, n_e, CT_MOE), lambda w, blk, chk, *_: (chk[w], 0, 0)),
                  pl.BlockSpec((n_e, 1), lambda w, *_: (0, 0))],
        out_specs=pl.BlockSpec((BG_MOE, d), lambda w, blk, *_: (blk[w], 0)))
    return pl.pallas_call(
        _moe_gather_kernel, grid_spec=grid_spec,
        out_shape=jax.ShapeDtypeStruct((n_slots, d), BF16),
        compiler_params=_cparams(("arbitrary",)), name="moe_gather",
    )(*items, h, pos_t, pstart_col)


def _moe_ffn_kernel(exp_ref, valid_ref, xs_ref, wg_ref, wu_ref, wd_ref, ys_ref, acc_ref):
    j = pl.program_id(0)
    f = pl.program_id(1)

    @pl.when(f == 0)
    def _():
        acc_ref[...] = jnp.zeros_like(acc_ref)

    @pl.when(valid_ref[j] == 1)
    def _():
        xs = xs_ref[...]
        a = _dot(xs, wg_ref[0])
        u = _dot(xs, wu_ref[0])
        acc_ref[...] += _dot(((a * jax.nn.sigmoid(a)) * u).astype(BF16), wd_ref[0])

    @pl.when(f == pl.num_programs(1) - 1)
    def _():
        ys_ref[...] = acc_ref[...].astype(ys_ref.dtype)


def _moe_ffn(xs, blk_expert, blk_valid, wg, wu, wd):
    n_slots, d = xs.shape
    dff = wg.shape[2]
    tf = TF_FFN
    nf = dff // tf

    def fidx(j, f, valid):
        return f * valid[j] + (nf - 1) * (1 - valid[j])

    grid_spec = pltpu.PrefetchScalarGridSpec(
        num_scalar_prefetch=2, grid=(n_slots // BM_MOE, nf),
        in_specs=[pl.BlockSpec((BM_MOE, d), lambda j, f, e, v: (j, 0)),
                  pl.BlockSpec((1, d, tf), lambda j, f, e, v: (e[j], 0, fidx(j, f, v))),
                  pl.BlockSpec((1, d, tf), lambda j, f, e, v: (e[j], 0, fidx(j, f, v))),
                  pl.BlockSpec((1, tf, d), lambda j, f, e, v: (e[j], fidx(j, f, v), 0))],
        out_specs=pl.BlockSpec((BM_MOE, d), lambda j, f, e, v: (j, 0)),
        scratch_shapes=[pltpu.VMEM((BM_MOE, d), F32)])
    return pl.pallas_call(
        _moe_ffn_kernel, grid_spec=grid_spec,
        out_shape=jax.ShapeDtypeStruct((n_slots, d), BF16),
        compiler_params=_cparams(("parallel", "arbitrary")), name="moe_ffn",
    )(blk_expert, blk_valid, xs, wg, wu, wd)


def _moe_combine_kernel(tile_ref, blk_ref, exp_ref, first_ref, valid_ref,
                        x_ref, pos_ref, g_ref, pstart_ref, ys_ref, o_ref):
    w = pl.program_id(0)

    @pl.when(first_ref[w] == 1)
    def _():
        o_ref[...] = x_ref[...]

    @pl.when(valid_ref[w] == 1)
    def _():
        pos = pos_ref[...]
        slot = jnp.where(pos >= 0.0, pos + pstart_ref[...], -1.0)
        mine = lax.broadcasted_iota(jnp.int32, pos.shape, 1) == exp_ref[w]
        slot_e = jnp.sum(jnp.where(mine, slot, 0.0), axis=-1, keepdims=True)
        gate_e = jnp.sum(jnp.where(mine, g_ref[...], 0.0), axis=-1, keepdims=True)
        bs = ys_ref.shape[0]
        target = (blk_ref[w] * bs + lax.broadcasted_iota(jnp.int32, (1, bs), 1)).astype(F32)
        onehot = jnp.where(slot_e == target, 1.0, 0.0).astype(BF16)
        o_ref[...] += gate_e * _dot(onehot, ys_ref[...])


def _moe_combine(x2, pos, gates, pstart_row, ys, items):
    t, d = x2.shape
    n_items = items[0].shape[0]
    tok = lambda w, tile, *_: (tile[w], 0)
    grid_spec = pltpu.PrefetchScalarGridSpec(
        num_scalar_prefetch=5, grid=(n_items,),
        in_specs=[pl.BlockSpec((CT_MOE, d), tok), pl.BlockSpec((CT_MOE, LANES), tok),
                  pl.BlockSpec((CT_MOE, LANES), tok), pl.BlockSpec((1, LANES), lambda w, *_: (0, 0)),
                  pl.BlockSpec((BG_MOE, d), lambda w, tile, blk, *_: (blk[w], 0))],
        out_specs=pl.BlockSpec((CT_MOE, d), tok))
    return pl.pallas_call(
        _moe_combine_kernel, grid_spec=grid_spec,
        out_shape=jax.ShapeDtypeStruct((t, d), F32),
        compiler_params=_cparams(("arbitrary",)), name="moe_combine",
    )(*items, x2, pos, gates, pstart_row, ys)


def _count_le(ascending, x):
    return jnp.sum(ascending[None, :] <= x[:, None], axis=1).astype(jnp.int32)


def _work_items(group_id, lo, hi, n_items_max):
    n = jnp.maximum(hi - lo + 1, 0)
    ends = jnp.cumsum(n)
    total = ends[-1]
    w = jnp.arange(n_items_max, dtype=jnp.int32)
    wc = jnp.minimum(w, total - 1)
    g = _count_le(ends, wc)
    k = wc - (ends[g] - n[g])
    valid = (w < total).astype(jnp.int32)
    return group_id[g], lo[g] + k, g, ((k == 0) & (w < total)).astype(jnp.int32), valid


def _moe_routed(h, x2, gates, wg, wu, wd):
    t, d = x2.shape
    n_e = wg.shape[0]
    nch = t // CT_MOE
    n_slots = t * TOP_K + n_e * BM_MOE
    nbg = n_slots // BG_MOE
    pos_t, pos, cnt, tot = _route_scan(gates, n_e)

    counts = tot[0, :n_e].astype(jnp.int32)
    padded = (counts + BM_MOE - 1) // BM_MOE * BM_MOE
    pend = jnp.cumsum(padded)
    pstart = pend - padded
    cum = cnt[:, 0, :n_e].astype(jnp.int32)
    cum_end = jnp.concatenate([cum[1:], counts[None]], axis=0)

    sb = jnp.arange(nbg, dtype=jnp.int32)
    sb_e = jnp.minimum(_count_le(pend, sb * BG_MOE), n_e - 1)
    p_lo = sb * BG_MOE - pstart[sb_e]
    has_rows = (sb * BG_MOE < pend[-1]) & (p_lo < counts[sb_e])
    c_lo = jnp.sum(cum_end.T[sb_e] <= p_lo[:, None], axis=1).astype(jnp.int32)
    c_hi = jnp.sum(cum.T[sb_e] <= (p_lo + BG_MOE - 1)[:, None], axis=1).astype(jnp.int32) - 1
    c_lo = jnp.where(has_rows, c_lo, 0)
    c_hi = jnp.where(has_rows, c_hi, 0)
    g_blk, g_chk, g_grp, g_first, g_valid = _work_items(sb, c_lo, c_hi, nbg + n_e * nch)
    xs = _moe_gather(h, pos_t, pstart.astype(F32).reshape(n_e, 1),
                     (g_blk, g_chk, sb_e[g_grp], g_first, g_valid), n_slots)

    mb = jnp.arange(n_slots // BM_MOE, dtype=jnp.int32) * BM_MOE
    mb_e = jnp.minimum(_count_le(pend, mb), n_e - 1)
    mb_valid = ((mb < pend[-1]) & (mb - pstart[mb_e] < counts[mb_e])).astype(jnp.int32)
    ys = _moe_ffn(xs, mb_e, mb_valid, wg, wu, wd)

    tile = jnp.repeat(jnp.arange(nch, dtype=jnp.int32), n_e)
    te = jnp.tile(jnp.arange(n_e, dtype=jnp.int32), nch)
    s_lo = (pstart[None, :] + cum).reshape(-1)
    s_hi = (pstart[None, :] + cum_end).reshape(-1) - 1
    b_lo = s_lo // BG_MOE
    b_hi = jnp.where(s_hi >= s_lo, s_hi // BG_MOE, b_lo - 1)
    grp = jnp.arange(nch * n_e, dtype=jnp.int32)
    c_grp, c_blk, _, _, c_valid = _work_items(grp, b_lo, b_hi, nbg + n_e * nch)
    c_tile = tile[c_grp]
    c_first = jnp.concatenate([jnp.ones((1,), jnp.int32), (c_tile[1:] != c_tile[:-1]).astype(jnp.int32)])
    pstart_row = jnp.zeros((1, LANES), F32).at[0, :n_e].set(pstart.astype(F32))
    return _moe_combine(x2, pos, gates, pstart_row, ys, (c_tile, c_blk, te[c_grp], c_first, c_valid))


def _rope_tables(pos):
    half = ROPE_DIM // 2
    inv_freq = ROPE_THETA ** (-2.0 * jnp.arange(half, dtype=F32) / ROPE_DIM)
    ang = pos.astype(F32).reshape(-1, 1) * inv_freq
    cos, sin = jnp.cos(ang), jnp.sin(ang)
    n = ang.shape[0]
    rest = HEAD_DIM - ROPE_DIM
    cos_h = jnp.concatenate([cos, cos, jnp.ones((n, rest), F32)], axis=-1)
    sin_h = jnp.concatenate([-sin, sin, jnp.zeros((n, rest), F32)], axis=-1)
    return jnp.tile(cos_h, (1, HEADS_PER_VREG)), jnp.tile(sin_h, (1, HEADS_PER_VREG))


def _permute_w_in(w, conv_w):
    d = w.shape[0]
    kv_end = Q_W + 6 * KV_W
    g = w[:, kv_end:kv_end + N_HEADS * N_BRANCH]
    pad = jnp.zeros((d, LANES - GATES_PER_KV), w.dtype)
    gate_cols = []
    for kh in range(N_KV_HEADS):
        gate_cols += [g[:, kh * GATES_PER_KV:(kh + 1) * GATES_PER_KV], pad]
    u = w[:, kv_end + N_HEADS * N_BRANCH:]
    return jnp.concatenate([w[:, :kv_end]] + gate_cols + [u], axis=1).astype(BF16)


def _compress_weights(pos_emb, w1, w2):
    hidden = w1.shape[1]
    eye = jnp.eye(N_KV_HEADS, dtype=w1.dtype)
    w1r = w1.reshape(CMP_LEN, HEAD_DIM, hidden)
    halves = []
    for part in (w1r[:CMP_STRIDE], w1r[CMP_STRIDE:]):
        full = jnp.einsum('ldj,hg->lhdgj', part, eye)
        halves.append(full.reshape(CMP_STRIDE * N_KV_HEADS * HEAD_DIM, N_KV_HEADS * hidden).astype(BF16))
    w2p = jnp.einsum('jd,hg->hjgd', w2, eye).reshape(N_KV_HEADS * hidden, N_KV_HEADS * HEAD_DIM).astype(BF16)
    pos = []
    for part in (pos_emb[:CMP_STRIDE], pos_emb[CMP_STRIDE:]):
        pos.append(jnp.broadcast_to(part[:, None, :], (CMP_STRIDE, N_KV_HEADS, HEAD_DIM)).reshape(1, -1))
    return [halves[0], halves[1], w2p, pos[0], pos[1]]


def _selection_map_t(seq):
    ncp = seq // CMP_STRIDE
    n_cmp = (seq - CMP_LEN) // CMP_STRIDE + 1
    c0 = np.arange(ncp) * CMP_STRIDE
    s0 = np.arange(seq // SEL_LEN) * SEL_LEN
    ov = np.minimum(c0[None, :] + CMP_LEN, s0[:, None] + SEL_LEN) - np.maximum(c0[None, :], s0[:, None])
    m = np.clip(ov, 0, None) / CMP_LEN
    m[:, n_cmp:] = 0.0
    return jnp.asarray(m, dtype=BF16)


def kernel(x, positions, attn_norm_g, ffn_norm_g, w_in, w_out, q_norm_g, k_norm_g, cmp_pos_k, cmp_w1_k, cmp_w2_k, cmp_pos_v, cmp_w1_v, cmp_w2_v, conv_w, conv_b, conv_ln_g, conv_ln_b, ffn_w_gate, ffn_w_up, ffn_w_down, moe_router, moe_w_gate, moe_w_up, moe_w_down):
    b, seq, d = x.shape
    t = b * seq
    depth = w_in.shape[0]
    cw = conv_w.shape[2]
    ncp = seq // CMP_STRIDE
    n_cmp = (seq - CMP_LEN) // CMP_STRIDE + 1
    assert seq % max(TQ, KC, TS_CONV) == 0 and seq >= WINDOW + TQ

    cos_t, sin_t = _rope_tables(positions)
    cmp_end = np.minimum(np.arange(ncp) * CMP_STRIDE + CMP_LEN - 1, seq - 1)
    cos_c, sin_c = _rope_tables(positions[:, cmp_end])
    selmap_t = _selection_map_t(seq)
    tile2 = lambda v: jnp.tile(v.reshape(1, HEAD_DIM), (1, HEADS_PER_VREG))

    x2 = x.reshape(t, d)
    for layer in range(depth):
        w_perm = _permute_w_in(w_in[layer], cw)
        q, kc, vc, ks, vs, kw, vw, gates, glu = _in_proj(
            x2, attn_norm_g[layer].reshape(1, d), w_perm, cos_t, sin_t,
            tile2(q_norm_g[layer]), tile2(k_norm_g[layer, 1]), tile2(k_norm_g[layer, 2]), seq)
        kcmp, vcmp = _compress(
            kc, vc,
            _compress_weights(cmp_pos_k[layer], cmp_w1_k[layer], cmp_w2_k[layer]),
            _compress_weights(cmp_pos_v[layer], cmp_w1_v[layer], cmp_w2_v[layer]),
            tile2(k_norm_g[layer, 0]), cos_c, sin_c, b, ncp)
        score_bound = (HEAD_DIM ** 0.5 * LOG2_E * SCORE_BOUND_MARGIN * jnp.max(jnp.abs(q_norm_g[layer]))
                       * jnp.max(jnp.abs(k_norm_g[layer]))).astype(F32).reshape(1)
        attn = _attention(score_bound, q, kcmp, vcmp, ks, vs, kw, vw, gates, selmap_t, b, seq)
        conv = _conv(glu, conv_w[layer], conv_b[layer].reshape(1, cw), conv_ln_g[layer].reshape(1, cw),
                     conv_ln_b[layer].reshape(1, cw), b, seq)
        wo = w_out[layer].astype(BF16)
        g2 = ffn_norm_g[layer].reshape(1, d)
        i = layer // 2
        if layer % 2 == 0:
            x2, h = _out_proj(attn, conv, wo, x2, g2)
            x2 = _ffn(h, x2, ffn_w_gate[i].astype(BF16), ffn_w_up[i].astype(BF16), ffn_w_down[i].astype(BF16))
        else:
            n_e = moe_router.shape[2]
            r = jnp.pad(moe_router[i], ((0, 0), (0, LANES - n_e)))
            r_hi = r.astype(BF16)
            r_lo = (r - r_hi.astype(F32)).astype(BF16)
            x2, h, route = _out_proj(attn, conv, wo, x2, g2, jnp.stack([r_hi, r_lo]), n_e)
            x2 = _moe_routed(h, x2, route, moe_w_gate[i].astype(BF16), moe_w_up[i].astype(BF16),
                             moe_w_down[i].astype(BF16))
    return x2.reshape(b, seq, d)
```

```python
import functools
import math

import jax
import jax.numpy as jnp
import numpy as np
from jax import lax
from jax.experimental import pallas as pl
from jax.experimental.pallas import tpu as pltpu

F32 = jnp.float32
BF16 = jnp.bfloat16

N_HEADS = 8
N_KV_HEADS = 2
Q_PER_KV = N_HEADS // N_KV_HEADS
HEAD_DIM = 64
N_BRANCH = 3
CMP_LEN = 32
CMP_STRIDE = 16
SEL_LEN = 64
SEL_TOPK = 16
N_LOCAL_SEL = 2
WINDOW = 512
CONV_KERNEL = 31
ROPE_THETA = 500000.0
ROPE_DIM = HEAD_DIM // 4
TOP_K = 2
EPS = 1e-6

LANES = 128
SUBLANES = 8
LOG2_E = math.log2(math.e)
NEG_BIG = -(2.0 ** 100)
MAX_SCORE_BOUND = 50.0
SCORE_BOUND_MARGIN = 1.02
HEADS_PER_VREG = LANES // HEAD_DIM
VMEM_LIMIT = 56 * 1024 * 1024

TM_PROJ = 512
TQ = 128
KC = 512
TS_CONV = 512
CH_CONV = 32
HALO = 32
TM_FFN = 1024
TF_FFN = 512
CT_MOE = 1024
BG_MOE = 256
BM_MOE = 512


def _cparams(sem):
    return pltpu.CompilerParams(dimension_semantics=sem, vmem_limit_bytes=VMEM_LIMIT)


def _dot(a, b):
    return jnp.dot(a, b, preferred_element_type=F32)


def _dot_nt(a, b):
    return lax.dot_general(a, b, (((1,), (1,)), ((), ())), preferred_element_type=F32)


def _split_bf16(x):
    hi = x.astype(BF16)
    lo = (x - hi.astype(F32)).astype(BF16)
    return hi, lo


def _rms_rows(x, g):
    ms = jnp.mean(x * x, axis=-1, keepdims=True)
    return x * lax.rsqrt(ms + EPS) * g


def _head_block_ones():
    r = lax.broadcasted_iota(jnp.int32, (LANES, LANES), 0) // HEAD_DIM
    c = lax.broadcasted_iota(jnp.int32, (LANES, LANES), 1) // HEAD_DIM
    return jnp.where(r == c, 1.0, 0.0).astype(BF16)


def _head_norm_rope(xg, gain, cos, sin, ones_bd):
    ms = _dot((xg * xg).astype(BF16), ones_bd) * (1.0 / HEAD_DIM)
    y = xg * lax.rsqrt(ms + EPS) * gain
    lane = lax.broadcasted_iota(jnp.int32, y.shape, 1) % HEAD_DIM
    half = ROPE_DIM // 2
    partner = jnp.where(lane < half, pltpu.roll(y, LANES - half, 1), pltpu.roll(y, half, 1))
    return y * cos + partner * sin


Q_W = N_HEADS * HEAD_DIM
KV_W = N_KV_HEADS * HEAD_DIM
SEG_Q = 0
SEG_KV = Q_W
SEG_GATE = SEG_KV + 6 * KV_W
SEG_UA = SEG_GATE + N_KV_HEADS * LANES
GATES_PER_KV = Q_PER_KV * N_BRANCH


def _in_proj_kernel(x_ref, g_ref, w_ref, cos_ref, sin_ref, qg_ref, ksg_ref, kwg_ref,
                    q_ref, kc_ref, vc_ref, ks_ref, vs_ref, kw_ref, vw_ref, gate_ref, glu_ref,
                    *, conv_w, seq):
    h = _rms_rows(x_ref[...], g_ref[...]).astype(BF16)
    tm = h.shape[0]
    cos = cos_ref[...]
    sin = sin_ref[...]
    ones_bd = _head_block_ones()
    lane = lax.broadcasted_iota(jnp.int32, (tm, LANES), 1)
    tok = lax.rem(pl.program_id(0) * tm, seq) + lax.broadcasted_iota(jnp.int32, (tm, 1), 0)
    block_aug = jnp.where(lane - HEAD_DIM == tok // SEL_LEN, NEG_BIG, 0.0)
    ones_aug = jnp.where(lane == HEAD_DIM, 1.0, 0.0)

    def seg(lo, width):
        return _dot(h, w_ref[:, lo:lo + width])

    def put_heads(ref, first, val, aug):
        for j in range(HEADS_PER_VREG):
            head = val if j == 0 else pltpu.roll(val, LANES - j * HEAD_DIM, 1)
            ref[first + j] = jnp.where(lane < HEAD_DIM, head, aug).astype(ref.dtype)

    scale = HEAD_DIM ** -0.5 * LOG2_E
    qg = qg_ref[...]
    for c in range(Q_W // LANES):
        y = _head_norm_rope(seg(SEG_Q + c * LANES, LANES), qg, cos, sin, ones_bd) * scale
        put_heads(q_ref, c * HEADS_PER_VREG, y, 0.0)
    kc_ref[...] = seg(SEG_KV, KV_W)
    vc_ref[...] = seg(SEG_KV + KV_W, KV_W)
    put_heads(ks_ref, 0, _head_norm_rope(seg(SEG_KV + 2 * KV_W, KV_W), ksg_ref[...], cos, sin, ones_bd), block_aug)
    put_heads(vs_ref, 0, seg(SEG_KV + 3 * KV_W, KV_W), ones_aug)
    put_heads(kw_ref, 0, _head_norm_rope(seg(SEG_KV + 4 * KV_W, KV_W), kwg_ref[...], cos, sin, ones_bd), 0.0)
    put_heads(vw_ref, 0, seg(SEG_KV + 5 * KV_W, KV_W), ones_aug)
    gate_ref[...] = jax.nn.sigmoid(seg(SEG_GATE, N_KV_HEADS * LANES))
    a = seg(SEG_UA, conv_w)
    g = seg(SEG_UA + conv_w, conv_w)
    glu_ref[...] = a * jax.nn.sigmoid(g)


def _in_proj(x2, g, w_perm, cos_t, sin_t, qg, ksg, kwg, seq):
    t, d = x2.shape
    conv_w = (w_perm.shape[1] - SEG_UA) // 2
    tm = min(TM_PROJ, t)
    row = lambda i: (i, 0)
    const = lambda i: (0, 0)
    head_row = lambda i: (0, i, 0)
    out_shape = [
        jax.ShapeDtypeStruct((N_HEADS, t, LANES), BF16),
        jax.ShapeDtypeStruct((t, KV_W), F32),
        jax.ShapeDtypeStruct((t, KV_W), F32),
        jax.ShapeDtypeStruct((N_KV_HEADS, t, LANES), BF16),
        jax.ShapeDtypeStruct((N_KV_HEADS, t, LANES), BF16),
        jax.ShapeDtypeStruct((N_KV_HEADS, t, LANES), BF16),
        jax.ShapeDtypeStruct((N_KV_HEADS, t, LANES), BF16),
        jax.ShapeDtypeStruct((t, N_KV_HEADS * LANES), F32),
        jax.ShapeDtypeStruct((t, conv_w), F32),
    ]
    kv_spec = pl.BlockSpec((N_KV_HEADS, tm, LANES), head_row)
    out_specs = [
        pl.BlockSpec((N_HEADS, tm, LANES), head_row),
        pl.BlockSpec((tm, KV_W), row), pl.BlockSpec((tm, KV_W), row),
        kv_spec, kv_spec, kv_spec, kv_spec,
        pl.BlockSpec((tm, N_KV_HEADS * LANES), row),
        pl.BlockSpec((tm, conv_w), row),
    ]
    in_specs = [
        pl.BlockSpec((tm, d), row), pl.BlockSpec((1, d), const),
        pl.BlockSpec(w_perm.shape, const),
        pl.BlockSpec((tm, LANES), row), pl.BlockSpec((tm, LANES), row),
        pl.BlockSpec((1, LANES), const), pl.BlockSpec((1, LANES), const), pl.BlockSpec((1, LANES), const),
    ]
    return pl.pallas_call(
        functools.partial(_in_proj_kernel, conv_w=conv_w, seq=seq),
        grid=(t // tm,), in_specs=in_specs, out_specs=out_specs, out_shape=out_shape,
        compiler_params=_cparams(("parallel",)), name="in_proj",
    )(x2, g, w_perm, cos_t, sin_t, qg, ksg, kwg)


def _gelu_tanh(x):
    c = math.sqrt(2.0 / math.pi)
    return 0.5 * x * (1.0 + jnp.tanh(c * (x + 0.044715 * (x * x * x))))


def _compress_kernel(k_ref, v_ref, w1ak_ref, w1bk_ref, w2k_ref, pak_ref, pbk_ref,
                     w1av_ref, w1bv_ref, w2v_ref, pav_ref, pbv_ref,
                     kg_ref, cos_ref, sin_ref, ko_ref, vo_ref):
    def mlp(x_ref, w1a_ref, w1b_ref, w2_ref, pa_ref, pb_ref):
        n = x_ref.shape[0] // CMP_STRIDE
        first = second = None
        for l in range(CMP_STRIDE):
            x = x_ref[pl.ds(l, n, stride=CMP_STRIDE), :]
            cols = slice(l * KV_W, (l + 1) * KV_W)
            fa = _dot((x + pa_ref[:, cols]).astype(BF16), w1a_ref[cols, :])
            fb = _dot((x + pb_ref[:, cols]).astype(BF16), w1b_ref[cols, :])
            first = fa if first is None else first + fa
            second = fb if second is None else second + fb
        hid = first + pltpu.roll(second, n - 1, 0)
        return _dot(_gelu_tanh(hid).astype(BF16), w2_ref[...])

    kc = mlp(k_ref, w1ak_ref, w1bk_ref, w2k_ref, pak_ref, pbk_ref)
    kc = _head_norm_rope(kc, kg_ref[...], cos_ref[...], sin_ref[...], _head_block_ones())
    vc = mlp(v_ref, w1av_ref, w1bv_ref, w2v_ref, pav_ref, pbv_ref)
    lane = lax.broadcasted_iota(jnp.int32, kc.shape, 1)
    for j in range(N_KV_HEADS):
        for val, ref in ((kc, ko_ref), (vc, vo_ref)):
            head = val if j == 0 else pltpu.roll(val, LANES - j * HEAD_DIM, 1)
            ref[0, j] = jnp.where(lane < HEAD_DIM, head, 0.0).astype(ref.dtype)


def _compress(kc, vc, wk, wv, kg, cosc, sinc, b, ncp):
    seq = kc.shape[0] // b
    const = lambda i: (0, 0)
    row = lambda i: (i, 0)

    def wspecs(ws):
        return [pl.BlockSpec(w.shape, const) for w in ws]

    out = jax.ShapeDtypeStruct((b, N_KV_HEADS, ncp, LANES), BF16)
    ospec = pl.BlockSpec((1, N_KV_HEADS, ncp, LANES), lambda i: (i, 0, 0, 0))
    return pl.pallas_call(
        _compress_kernel, grid=(b,),
        in_specs=[pl.BlockSpec((seq, KV_W), row), pl.BlockSpec((seq, KV_W), row)]
        + wspecs(wk) + wspecs(wv)
        + [pl.BlockSpec((1, LANES), const), pl.BlockSpec((ncp, LANES), row), pl.BlockSpec((ncp, LANES), row)],
        out_specs=[ospec, ospec], out_shape=[out, out],
        compiler_params=_cparams(("parallel",)), name="compress",
    )(kc, vc, *wk, *wv, kg, cosc, sinc)


def _attn_kernel(off_ref, q_ref, kc_ref, vc_ref, ks_ref, vs_ref, kw_ref, vw_ref, gate_ref, selmap_ref,
                 o_ref, *, seq, tq, kc_len, top_n, bounded):
    i = pl.program_id(2)
    t0 = i * tq
    rows = Q_PER_KV * tq
    n_sel = seq // SEL_LEN
    q2 = q_ref[...].reshape(rows, LANES)
    t_row = t0 + (lax.broadcasted_iota(jnp.int32, (rows, 1), 0) & (tq - 1))
    t_tok = t0 + lax.broadcasted_iota(jnp.int32, (tq, 1), 0)
    neg_offset = -off_ref[0] if bounded else 0.0

    def add_bias(s, bias):
        return (s.reshape(s.shape[0] // tq, tq, s.shape[1]) + bias[None]).reshape(s.shape)

    kcmp = kc_ref[0, 0]
    ncp = kcmp.shape[0]
    s_c = _dot_nt(q2, kcmp)
    cmp_end = lax.broadcasted_iota(jnp.int32, (1, ncp), 1) * CMP_STRIDE + (CMP_LEN - 1)
    if bounded:
        e_c = jnp.exp2(add_bias(s_c, jnp.where(cmp_end <= t_tok, neg_offset, NEG_BIG)))
    else:
        s_c = jnp.where(cmp_end <= t_row, s_c, -jnp.inf)
        m_c = jnp.max(s_c, axis=-1, keepdims=True)
        e_c = jnp.exp2(s_c - jnp.where(m_c == -jnp.inf, 0.0, m_c))
    p_c = e_c * (1.0 / jnp.maximum(jnp.sum(e_c, axis=-1, keepdims=True), jnp.finfo(F32).tiny))
    o_c = _dot(p_c.astype(BF16), vc_ref[0, 0])

    def denominator(acc):
        return acc[:, HEAD_DIM:HEAD_DIM + 1]

    span = min(WINDOW + tq, seq)
    w0 = pl.multiple_of(jnp.maximum(t0 - WINDOW, 0), tq)
    key_w = w0 + lax.broadcasted_iota(jnp.int32, (1, span), 1)
    bias_w = jnp.where((key_w <= t_tok) & (key_w > t_tok - WINDOW), neg_offset, NEG_BIG)
    s_w = add_bias(_dot_nt(q2, kw_ref[0, pl.ds(w0, span), :]), bias_w)
    if bounded:
        p_w = jnp.exp2(s_w).astype(BF16)
    else:
        p_w = jnp.exp2((s_w - jnp.max(s_w, axis=-1, keepdims=True)).astype(BF16))
    acc_w = _dot(p_w, vw_ref[0, pl.ds(w0, span), :])
    o_w = acc_w * (1.0 / denominator(acc_w))

    p_hi, p_lo = _split_bf16(jnp.sum(p_c.reshape(Q_PER_KV, tq, ncp), axis=0))
    selmap = selmap_ref[...]
    imp = _dot_nt(selmap, p_hi) + _dot_nt(selmap, p_lo)
    blk = lax.broadcasted_iota(jnp.int32, (n_sel, tq), 0)
    cur = (t0 + lax.broadcasted_iota(jnp.int32, (n_sel, tq), 1)) // SEL_LEN
    causal_blk = blk <= cur
    forced = (blk == 0) | (causal_blk & (blk > cur - N_LOCAL_SEL))
    score = jnp.where(forced, jnp.inf, jnp.where(causal_blk, imp, -jnp.inf))
    sub = lax.broadcasted_iota(jnp.int32, (SUBLANES, tq), 0)
    groups = [score[g * SUBLANES:(g + 1) * SUBLANES, :] for g in range(n_sel // SUBLANES)]
    ranks = [jnp.zeros((SUBLANES, tq), F32) for _ in groups]
    for jp in range(n_sel):
        other = jnp.broadcast_to(score[jp:jp + 1, :], (SUBLANES, tq))
        for g, sg in enumerate(groups):
            first = g * SUBLANES
            if first > jp:
                inc = jnp.where(other >= sg, 1.0, 0.0)
            elif first + SUBLANES - 1 <= jp:
                inc = jnp.where(other > sg, 1.0, 0.0)
            else:
                inc = jnp.where(other > sg, 1.0, jnp.where((other == sg) & (sub > jp - first), 1.0, 0.0))
            ranks[g] = ranks[g] + inc
    chosen_flag = neg_offset * (1.0 / NEG_BIG)
    block_flags = jnp.where(jnp.concatenate(ranks, axis=0) < top_n, chosen_flag, 1.0)
    flag_rows = [jnp.zeros((HEAD_DIM, tq), F32), block_flags]
    if HEAD_DIM + n_sel < LANES:
        flag_rows.append(jnp.zeros((LANES - HEAD_DIM - n_sel, tq), F32))
    flags = jnp.transpose(jnp.concatenate(flag_rows, axis=0))
    q_sel = (q2.reshape(Q_PER_KV, tq, LANES) + flags.astype(BF16)[None]).reshape(rows, LANES)

    def sel_chunk(k0, carry, bias):
        s = _dot_nt(q_sel, ks_ref[0, pl.ds(k0, kc_len), :])
        if bias is not None:
            s = add_bias(s, bias)
        v = vs_ref[0, pl.ds(k0, kc_len), :]
        if bounded:
            (acc,) = carry
            return (acc + _dot(jnp.exp2(s).astype(BF16), v),)
        m, acc = carry
        m_new = jnp.maximum(m, jnp.max(s, axis=-1, keepdims=True))
        p = jnp.exp2((s - m_new).astype(BF16))
        return m_new, jnp.exp2(m - m_new) * acc + _dot(p, v)

    n_full = t0 // kc_len
    init = (jnp.zeros((rows, LANES), F32),)
    if not bounded:
        init = (jnp.full((rows, 1), -jnp.inf, F32),) + init
    carry = lax.fori_loop(
        0, n_full, lambda c, cr: sel_chunk(pl.multiple_of(c * kc_len, kc_len), cr, None), init)
    kd = pl.multiple_of(n_full * kc_len, kc_len)
    key_d = kd + lax.broadcasted_iota(jnp.int32, (1, kc_len), 1)
    acc_s = sel_chunk(kd, carry, jnp.where(key_d <= t_tok, 0.0, NEG_BIG))[-1]
    o_s = acc_s * (1.0 / denominator(acc_s))

    gates = gate_ref[...]

    def gate_col(br):
        cols = [gates[:, g * N_BRANCH + br:g * N_BRANCH + br + 1] for g in range(Q_PER_KV)]
        return jnp.concatenate(cols, axis=0)

    o = gate_col(0) * o_c + gate_col(1) * o_s + gate_col(2) * o_w
    o3 = o.reshape(Q_PER_KV, tq, LANES)
    lane = lax.broadcasted_iota(jnp.int32, (tq, LANES), 1)
    pairs = [jnp.where(lane < HEAD_DIM, o3[g], pltpu.roll(o3[g + 1], HEAD_DIM, 1))
             for g in range(0, Q_PER_KV, HEADS_PER_VREG)]
    o_ref[...] = jnp.concatenate(pairs, axis=-1).astype(o_ref.dtype)


def _attention(score_bound, q, kcmp, vcmp, ks, vs, kw, vw, gates, selmap_t, b, seq):
    t = b * seq
    tq = min(TQ, seq)
    kc_len = min(KC, seq)
    nq = seq // tq
    ncp = kcmp.shape[2]
    n_sel = seq // SEL_LEN
    assert HEAD_DIM + n_sel <= LANES, "selection-block flags must fit beside the head dims"
    top_n = min(SEL_TOPK, n_sel)
    cmp_spec = pl.BlockSpec((1, 1, ncp, LANES), lambda bi, kh, i: (bi, kh, 0, 0))
    seq_spec = pl.BlockSpec((1, seq, LANES), lambda bi, kh, i: (kh, bi, 0))

    def run(bounded):
        return pl.pallas_call(
            functools.partial(_attn_kernel, seq=seq, tq=tq, kc_len=kc_len, top_n=top_n, bounded=bounded),
            grid=(b, N_KV_HEADS, nq),
            in_specs=[
                pl.BlockSpec(memory_space=pltpu.SMEM),
                pl.BlockSpec((Q_PER_KV, tq, LANES), lambda bi, kh, i: (kh, bi * nq + i, 0)),
                cmp_spec, cmp_spec, seq_spec, seq_spec, seq_spec, seq_spec,
                pl.BlockSpec((tq, LANES), lambda bi, kh, i: (bi * nq + i, kh)),
                pl.BlockSpec(selmap_t.shape, lambda bi, kh, i: (0, 0)),
            ],
            out_specs=pl.BlockSpec((tq, Q_PER_KV * HEAD_DIM), lambda bi, kh, i: (bi * nq + i, kh)),
            out_shape=jax.ShapeDtypeStruct((t, N_HEADS * HEAD_DIM), BF16),
            compiler_params=_cparams(("parallel", "parallel", "arbitrary")),
            name="nsa_attention" if bounded else "nsa_attention_running_max",
        )(score_bound, q, kcmp, vcmp, ks, vs, kw, vw, gates, selmap_t)

    return lax.cond(score_bound[0] < MAX_SCORE_BOUND, lambda: run(True), lambda: run(False))


def _conv_kernel(glu_ref, w_ref, b_ref, lg_ref, lb_ref, o_ref, ext_ref, shift_ref, *, ts):
    i = pl.program_id(1)

    @pl.when(i == 0)
    def _():
        ext_ref[0:HALO, :] = jnp.zeros((HALO, ext_ref.shape[1]), F32)

    @pl.when(i > 0)
    def _():
        ext_ref[0:HALO, :] = ext_ref[ts:ts + HALO, :]

    ext_ref[HALO:HALO + ts, :] = glu_ref[...]
    n_shift = shift_ref.shape[1]
    for r in range(1, SUBLANES):
        shift_ref[r - 1] = ext_ref[r:r + n_shift, :]

    def rows_from(o):
        r = o % SUBLANES
        if r == 0:
            return ext_ref[o:o + CH_CONV, :]
        return shift_ref[r - 1, o - r:o - r + CH_CONV, :]

    w = w_ref[...]
    first_tap = HALO - (CONV_KERNEL - 1)
    for c in range(ts // CH_CONV):
        base = c * CH_CONV + first_tap
        acc = w[0:1, :] * rows_from(base)
        for k in range(1, CONV_KERNEL):
            acc = acc + w[k:k + 1, :] * rows_from(base + k)
        y = acc + b_ref[...]
        yc = y - jnp.mean(y, axis=-1, keepdims=True)
        yn = yc * lax.rsqrt(jnp.mean(yc * yc, axis=-1, keepdims=True) + EPS)
        z = yn * lg_ref[...] + lb_ref[...]
        o_ref[c * CH_CONV:(c + 1) * CH_CONV, :] = (z * jax.nn.sigmoid(z)).astype(o_ref.dtype)


def _conv(glu, w, bias, lg, lb, b, seq):
    t, cw = glu.shape
    ts = min(TS_CONV, seq)
    ns = seq // ts
    const = lambda bi, i: (0, 0)
    row = lambda bi, i: (bi * ns + i, 0)
    return pl.pallas_call(
        functools.partial(_conv_kernel, ts=ts), grid=(b, ns),
        in_specs=[pl.BlockSpec((ts, cw), row), pl.BlockSpec(w.shape, const),
                  pl.BlockSpec((1, cw), const), pl.BlockSpec((1, cw), const), pl.BlockSpec((1, cw), const)],
        out_specs=pl.BlockSpec((ts, cw), row),
        out_shape=jax.ShapeDtypeStruct((t, cw), BF16),
        scratch_shapes=[pltpu.VMEM((ts + HALO, cw), F32),
                        pltpu.VMEM((SUBLANES - 1, ts + HALO - SUBLANES, cw), F32)],
        compiler_params=_cparams(("arbitrary", "arbitrary")), name="conformer_conv",
    )(glu, w, bias, lg, lb)


def _top2_gates(logits, n_experts):
    lane = lax.broadcasted_iota(jnp.int32, logits.shape, 1)
    x = jnp.where(lane < n_experts, logits, -jnp.inf)
    m1 = jnp.max(x, axis=-1, keepdims=True)
    i1 = jnp.min(jnp.where(x == m1, lane, LANES), axis=-1, keepdims=True)
    x2 = jnp.where(lane == i1, -jnp.inf, x)
    m2 = jnp.max(x2, axis=-1, keepdims=True)
    i2 = jnp.min(jnp.where(x2 == m2, lane, LANES), axis=-1, keepdims=True)
    e2 = jnp.exp(m2 - m1)
    inv = 1.0 / (1.0 + e2)
    return jnp.where(lane == i1, inv, jnp.where(lane == i2, e2 * inv, 0.0))


def _out_proj_kernel(*refs, n_experts):
    if n_experts:
        attn_ref, conv_ref, wo_ref, x_ref, g_ref, rt_ref, xo_ref, h_ref, gate_ref = refs
    else:
        attn_ref, conv_ref, wo_ref, x_ref, g_ref, xo_ref, h_ref = refs
    aw = attn_ref.shape[1]
    x = x_ref[...] + _dot(attn_ref[...], wo_ref[0:aw, :]) + _dot(conv_ref[...], wo_ref[aw:, :])
    xo_ref[...] = x
    h = _rms_rows(x, g_ref[...])
    h_ref[...] = h.astype(h_ref.dtype)
    if n_experts:
        h_hi, h_lo = _split_bf16(h)
        r_hi = rt_ref[0]
        r_lo = rt_ref[1]
        logits = _dot(h_hi, r_hi) + (_dot(h_hi, r_lo) + _dot(h_lo, r_hi))
        gate_ref[...] = _top2_gates(logits, n_experts)


def _out_proj(attn, conv, wo, x2, g, router_split=None, n_experts=0):
    t, d = x2.shape
    tm = min(TM_PROJ, t)
    row = lambda i: (i, 0)
    const = lambda i: (0, 0)
    in_specs = [pl.BlockSpec((tm, attn.shape[1]), row), pl.BlockSpec((tm, conv.shape[1]), row),
                pl.BlockSpec(wo.shape, const), pl.BlockSpec((tm, d), row), pl.BlockSpec((1, d), const)]
    out_shape = [jax.ShapeDtypeStruct((t, d), F32), jax.ShapeDtypeStruct((t, d), BF16)]
    out_specs = [pl.BlockSpec((tm, d), row), pl.BlockSpec((tm, d), row)]
    args = [attn, conv, wo, x2, g]
    if n_experts:
        in_specs.append(pl.BlockSpec(router_split.shape, lambda i: (0, 0, 0)))
        out_shape.append(jax.ShapeDtypeStruct((t, LANES), F32))
        out_specs.append(pl.BlockSpec((tm, LANES), row))
        args.append(router_split)
    return pl.pallas_call(
        functools.partial(_out_proj_kernel, n_experts=n_experts),
        grid=(t // tm,), in_specs=in_specs, out_specs=out_specs, out_shape=out_shape,
        compiler_params=_cparams(("parallel",)), name="out_proj",
    )(*args)


def _swiglu_tiles(h, acc, wg, wu, wd, dff, tf):
    for f in range(dff // tf):
        cols = slice(f * tf, (f + 1) * tf)
        a = _dot(h, wg(cols))
        u = _dot(h, wu(cols))
        acc = acc + _dot(((a * jax.nn.sigmoid(a)) * u).astype(BF16), wd(cols))
    return acc


def _ffn_kernel(h_ref, x_ref, wg_ref, wu_ref, wd_ref, o_ref, *, tf):
    o_ref[...] = _swiglu_tiles(h_ref[...], x_ref[...], lambda c: wg_ref[:, c], lambda c: wu_ref[:, c],
                               lambda c: wd_ref[c, :], wg_ref.shape[1], tf)


def _ffn(h, x2, wg, wu, wd):
    t, d = x2.shape
    tm = min(TM_FFN, t)
    row = lambda i: (i, 0)
    resident = lambda w: pl.BlockSpec(w.shape, lambda i: (0, 0), pipeline_mode=pl.Buffered(1))
    return pl.pallas_call(
        functools.partial(_ffn_kernel, tf=TF_FFN), grid=(t // tm,),
        in_specs=[pl.BlockSpec((tm, d), row), pl.BlockSpec((tm, d), row),
                  resident(wg), resident(wu), resident(wd)],
        out_specs=pl.BlockSpec((tm, d), row),
        out_shape=jax.ShapeDtypeStruct((t, d), F32),
        compiler_params=_cparams(("parallel",)), name="ffn",
    )(h, x2, wg, wu, wd)


def _route_scan_kernel(g_ref, pos_t_ref, pos_ref, cnt_ref, tot_ref, carry_ref, *, n_experts):
    c = pl.program_id(0)

    @pl.when(c == 0)
    def _():
        carry_ref[...] = jnp.zeros_like(carry_ref)

    ct = g_ref.shape[0]
    routed = g_ref[...] > 0.0
    a = jnp.where(routed, 1.0, 0.0)
    earlier = lax.broadcasted_iota(jnp.int32, (ct, ct), 1) < lax.broadcasted_iota(jnp.int32, (ct, ct), 0)
    base = carry_ref[...]
    pos = jnp.where(routed, _dot(jnp.where(earlier, 1.0, 0.0).astype(BF16), a.astype(BF16)) + base, -1.0)
    pos_ref[...] = pos
    pos_t_ref[0] = jnp.transpose(pos)[0:n_experts, :]
    cnt_ref[0] = base
    total = base + jnp.sum(a, axis=0, keepdims=True)
    carry_ref[...] = total
    tot_ref[...] = total


def _route_scan(gates, n_experts):
    t = gates.shape[0]
    ct = CT_MOE
    nch = t // ct
    return pl.pallas_call(
        functools.partial(_route_scan_kernel, n_experts=n_experts), grid=(nch,),
        in_specs=[pl.BlockSpec((ct, LANES), lambda c: (c, 0))],
        out_specs=[pl.BlockSpec((1, n_experts, ct), lambda c: (c, 0, 0)),
                   pl.BlockSpec((ct, LANES), lambda c: (c, 0)),
                   pl.BlockSpec((1, 1, LANES), lambda c: (c, 0, 0)),
                   pl.BlockSpec((1, LANES), lambda c: (0, 0))],
        out_shape=[jax.ShapeDtypeStruct((nch, n_experts, ct), F32),
                   jax.ShapeDtypeStruct((t, LANES), F32),
                   jax.ShapeDtypeStruct((nch, 1, LANES), F32),
                   jax.ShapeDtypeStruct((1, LANES), F32)],
        scratch_shapes=[pltpu.VMEM((1, LANES), F32)],
        compiler_params=_cparams(("arbitrary",)), name="route_scan",
    )(gates)


def _moe_gather_kernel(blk_ref, chk_ref, exp_ref, first_ref, valid_ref,
                       h_ref, pos_t_ref, pstart_ref, xs_ref):
    w = pl.program_id(0)

    @pl.when(first_ref[w] == 1)
    def _():
        xs_ref[...] = jnp.zeros_like(xs_ref)

    @pl.when(valid_ref[w] == 1)
    def _():
        pos = pos_t_ref[0]
        slot = jnp.where(pos >= 0.0, pos + pstart_ref[...], -1.0)
        sub = lax.broadcasted_iota(jnp.int32, pos.shape, 0)
        slot_e = jnp.sum(jnp.where(sub == exp_ref[w], slot, 0.0), axis=0, keepdims=True)
        bg = xs_ref.shape[0]
        target = (blk_ref[w] * bg + lax.broadcasted_iota(jnp.int32, (bg, 1), 0)).astype(F32)
        onehot = jnp.where(slot_e == target, 1.0, 0.0).astype(BF16)
        xs_ref[...] += _dot(onehot, h_ref[...]).astype(xs_ref.dtype)


def _moe_gather(h, pos_t, pstart_col, items, n_slots):
    t, d = h.shape
    n_items = items[0].shape[0]
    n_e = pos_t.shape[1]
    grid_spec = pltpu.PrefetchScalarGridSpec(
        num_scalar_prefetch=5, grid=(n_items,),
        in_specs=[pl.BlockSpec((CT_MOE, d), lambda w, blk, chk, *_: (chk[w], 0)),
                  pl.BlockSpec((1, n_e, CT_MOE), lambda w, blk, chk, *_: (chk[w], 0, 0)),
                  pl.BlockSpec((n_e, 1), lambda w, *_: (0, 0))],
        out_specs=pl.BlockSpec((BG_MOE, d), lambda w, blk, *_: (blk[w], 0)))
    return pl.pallas_call(
        _moe_gather_kernel, grid_spec=grid_spec,
        out_shape=jax.ShapeDtypeStruct((n_slots, d), BF16),
        compiler_params=_cparams(("arbitrary",)), name="moe_gather",
    )(*items, h, pos_t, pstart_col)


def _moe_ffn_kernel(exp_ref, valid_ref, xs_ref, wg_ref, wu_ref, wd_ref, ys_ref, *, tf):
    j = pl.program_id(0)

    @pl.when(valid_ref[j] == 1)
    def _():
        zero = jnp.zeros(ys_ref.shape, F32)
        ys = _swiglu_tiles(xs_ref[...], zero, lambda c: wg_ref[0, :, c], lambda c: wu_ref[0, :, c],
                           lambda c: wd_ref[0, c, :], wg_ref.shape[2], tf)
        ys_ref[...] = ys.astype(ys_ref.dtype)

    @pl.when(valid_ref[j] == 0)
    def _():
        ys_ref[...] = jnp.zeros_like(ys_ref)


def _moe_ffn(xs, blk_expert, blk_valid, wg, wu, wd):
    n_slots, d = xs.shape
    expert = lambda w: pl.BlockSpec((1,) + w.shape[1:], lambda j, e, v: (e[j], 0, 0),
                                    pipeline_mode=pl.Buffered(1))
    grid_spec = pltpu.PrefetchScalarGridSpec(
        num_scalar_prefetch=2, grid=(n_slots // BM_MOE,),
        in_specs=[pl.BlockSpec((BM_MOE, d), lambda j, e, v: (j, 0)), expert(wg), expert(wu), expert(wd)],
        out_specs=pl.BlockSpec((BM_MOE, d), lambda j, e, v: (j, 0)))
    return pl.pallas_call(
        functools.partial(_moe_ffn_kernel, tf=TF_FFN), grid_spec=grid_spec,
        out_shape=jax.ShapeDtypeStruct((n_slots, d), BF16),
        compiler_params=_cparams(("arbitrary",)), name="moe_ffn",
    )(blk_expert, blk_valid, xs, wg, wu, wd)


def _moe_combine_kernel(tile_ref, blk_ref, exp_ref, first_ref, valid_ref,
                        x_ref, pos_ref, g_ref, pstart_ref, ys_ref, o_ref):
    w = pl.program_id(0)

    @pl.when(first_ref[w] == 1)
    def _():
        o_ref[...] = x_ref[...]

    @pl.when(valid_ref[w] == 1)
    def _():
        pos = pos_ref[...]
        slot = jnp.where(pos >= 0.0, pos + pstart_ref[...], -1.0)
        mine = lax.broadcasted_iota(jnp.int32, pos.shape, 1) == exp_ref[w]
        slot_e = jnp.sum(jnp.where(mine, slot, 0.0), axis=-1, keepdims=True)
        gate_e = jnp.sum(jnp.where(mine, g_ref[...], 0.0), axis=-1, keepdims=True)
        bs = ys_ref.shape[0]
        target = (blk_ref[w] * bs + lax.broadcasted_iota(jnp.int32, (1, bs), 1)).astype(F32)
        onehot = jnp.where(slot_e == target, 1.0, 0.0).astype(BF16)
        o_ref[...] += gate_e * _dot(onehot, ys_ref[...])


def _moe_combine(x2, pos, gates, pstart_row, ys, items):
    t, d = x2.shape
    n_items = items[0].shape[0]
    tok = lambda w, tile, *_: (tile[w], 0)
    grid_spec = pltpu.PrefetchScalarGridSpec(
        num_scalar_prefetch=5, grid=(n_items,),
        in_specs=[pl.BlockSpec((CT_MOE, d), tok), pl.BlockSpec((CT_MOE, LANES), tok),
                  pl.BlockSpec((CT_MOE, LANES), tok), pl.BlockSpec((1, LANES), lambda w, *_: (0, 0)),
                  pl.BlockSpec((BG_MOE, d), lambda w, tile, blk, *_: (blk[w], 0))],
        out_specs=pl.BlockSpec((CT_MOE, d), tok))
    return pl.pallas_call(
        _moe_combine_kernel, grid_spec=grid_spec,
        out_shape=jax.ShapeDtypeStruct((t, d), F32),
        compiler_params=_cparams(("arbitrary",)), name="moe_combine",
    )(*items, x2, pos, gates, pstart_row, ys)


def _count_le(ascending, x):
    return jnp.sum(ascending[None, :] <= x[:, None], axis=1).astype(jnp.int32)


def _work_items(group_id, lo, hi, n_items_max):
    n = jnp.maximum(hi - lo + 1, 0)
    ends = jnp.cumsum(n)
    total = ends[-1]
    w = jnp.arange(n_items_max, dtype=jnp.int32)
    wc = jnp.minimum(w, total - 1)
    g = _count_le(ends, wc)
    k = wc - (ends[g] - n[g])
    valid = (w < total).astype(jnp.int32)
    return group_id[g], lo[g] + k, g, ((k == 0) & (w < total)).astype(jnp.int32), valid


def _moe_routed(h, x2, gates, wg, wu, wd):
    t, d = x2.shape
    n_e = wg.shape[0]
    nch = t // CT_MOE
    n_slots = t * TOP_K + n_e * BM_MOE
    nbg = n_slots // BG_MOE
    pos_t, pos, cnt, tot = _route_scan(gates, n_e)

    counts = tot[0, :n_e].astype(jnp.int32)
    padded = (counts + BM_MOE - 1) // BM_MOE * BM_MOE
    pend = jnp.cumsum(padded)
    pstart = pend - padded
    cum = cnt[:, 0, :n_e].astype(jnp.int32)
    cum_end = jnp.concatenate([cum[1:], counts[None]], axis=0)

    sb = jnp.arange(nbg, dtype=jnp.int32)
    sb_e = jnp.minimum(_count_le(pend, sb * BG_MOE), n_e - 1)
    p_lo = sb * BG_MOE - pstart[sb_e]
    has_rows = (sb * BG_MOE < pend[-1]) & (p_lo < counts[sb_e])
    c_lo = jnp.sum(cum_end.T[sb_e] <= p_lo[:, None], axis=1).astype(jnp.int32)
    c_hi = jnp.sum(cum.T[sb_e] <= (p_lo + BG_MOE - 1)[:, None], axis=1).astype(jnp.int32) - 1
    c_lo = jnp.where(has_rows, c_lo, 0)
    c_hi = jnp.where(has_rows, c_hi, 0)
    g_blk, g_chk, g_grp, g_first, g_valid = _work_items(sb, c_lo, c_hi, nbg + n_e * nch)
    xs = _moe_gather(h, pos_t, pstart.astype(F32).reshape(n_e, 1),
                     (g_blk, g_chk, sb_e[g_grp], g_first, g_valid), n_slots)

    mb = jnp.arange(n_slots // BM_MOE, dtype=jnp.int32) * BM_MOE
    mb_e = jnp.minimum(_count_le(pend, mb), n_e - 1)
    mb_valid = ((mb < pend[-1]) & (mb - pstart[mb_e] < counts[mb_e])).astype(jnp.int32)
    ys = _moe_ffn(xs, mb_e, mb_valid, wg, wu, wd)

    tile = jnp.repeat(jnp.arange(nch, dtype=jnp.int32), n_e)
    te = jnp.tile(jnp.arange(n_e, dtype=jnp.int32), nch)
    s_lo = (pstart[None, :] + cum).reshape(-1)
    s_hi = (pstart[None, :] + cum_end).reshape(-1) - 1
    b_lo = s_lo // BG_MOE
    b_hi = jnp.where(s_hi >= s_lo, s_hi // BG_MOE, b_lo - 1)
    grp = jnp.arange(nch * n_e, dtype=jnp.int32)
    c_grp, c_blk, _, _, c_valid = _work_items(grp, b_lo, b_hi, nbg + n_e * nch)
    c_tile = tile[c_grp]
    c_first = jnp.concatenate([jnp.ones((1,), jnp.int32), (c_tile[1:] != c_tile[:-1]).astype(jnp.int32)])
    pstart_row = jnp.zeros((1, LANES), F32).at[0, :n_e].set(pstart.astype(F32))
    return _moe_combine(x2, pos, gates, pstart_row, ys, (c_tile, c_blk, te[c_grp], c_first, c_valid))


def _rope_tables(pos):
    half = ROPE_DIM // 2
    inv_freq = ROPE_THETA ** (-2.0 * jnp.arange(half, dtype=F32) / ROPE_DIM)
    ang = pos.astype(F32).reshape(-1, 1) * inv_freq
    cos, sin = jnp.cos(ang), jnp.sin(ang)
    n = ang.shape[0]
    rest = HEAD_DIM - ROPE_DIM
    cos_h = jnp.concatenate([cos, cos, jnp.ones((n, rest), F32)], axis=-1)
    sin_h = jnp.concatenate([-sin, sin, jnp.zeros((n, rest), F32)], axis=-1)
    return jnp.tile(cos_h, (1, HEADS_PER_VREG)), jnp.tile(sin_h, (1, HEADS_PER_VREG))


def _permute_w_in(w, conv_w):
    d = w.shape[0]
    kv_end = Q_W + 6 * KV_W
    g = w[:, kv_end:kv_end + N_HEADS * N_BRANCH]
    pad = jnp.zeros((d, LANES - GATES_PER_KV), w.dtype)
    gate_cols = []
    for kh in range(N_KV_HEADS):
        gate_cols += [g[:, kh * GATES_PER_KV:(kh + 1) * GATES_PER_KV], pad]
    u = w[:, kv_end + N_HEADS * N_BRANCH:]
    return jnp.concatenate([w[:, :kv_end]] + gate_cols + [u], axis=1).astype(BF16)


def _compress_weights(pos_emb, w1, w2):
    hidden = w1.shape[1]
    eye = jnp.eye(N_KV_HEADS, dtype=w1.dtype)
    w1r = w1.reshape(CMP_LEN, HEAD_DIM, hidden)
    halves = []
    for part in (w1r[:CMP_STRIDE], w1r[CMP_STRIDE:]):
        full = jnp.einsum('ldj,hg->lhdgj', part, eye)
        halves.append(full.reshape(CMP_STRIDE * N_KV_HEADS * HEAD_DIM, N_KV_HEADS * hidden).astype(BF16))
    w2p = jnp.einsum('jd,hg->hjgd', w2, eye).reshape(N_KV_HEADS * hidden, N_KV_HEADS * HEAD_DIM).astype(BF16)
    pos = []
    for part in (pos_emb[:CMP_STRIDE], pos_emb[CMP_STRIDE:]):
        pos.append(jnp.broadcast_to(part[:, None, :], (CMP_STRIDE, N_KV_HEADS, HEAD_DIM)).reshape(1, -1))
    return [halves[0], halves[1], w2p, pos[0], pos[1]]


def _selection_map_t(seq):
    ncp = seq // CMP_STRIDE
    n_cmp = (seq - CMP_LEN) // CMP_STRIDE + 1
    c0 = np.arange(ncp) * CMP_STRIDE
    s0 = np.arange(seq // SEL_LEN) * SEL_LEN
    ov = np.minimum(c0[None, :] + CMP_LEN, s0[:, None] + SEL_LEN) - np.maximum(c0[None, :], s0[:, None])
    m = np.clip(ov, 0, None) / CMP_LEN
    m[:, n_cmp:] = 0.0
    return jnp.asarray(m, dtype=BF16)


def kernel(x, positions, attn_norm_g, ffn_norm_g, w_in, w_out, q_norm_g, k_norm_g, cmp_pos_k, cmp_w1_k, cmp_w2_k, cmp_pos_v, cmp_w1_v, cmp_w2_v, conv_w, conv_b, conv_ln_g, conv_ln_b, ffn_w_gate, ffn_w_up, ffn_w_down, moe_router, moe_w_gate, moe_w_up, moe_w_down):
    b, seq, d = x.shape
    t = b * seq
    depth = w_in.shape[0]
    cw = conv_w.shape[2]
    ncp = seq // CMP_STRIDE
    n_cmp = (seq - CMP_LEN) // CMP_STRIDE + 1
    assert seq % max(TQ, KC, TS_CONV) == 0 and seq >= WINDOW + TQ

    cos_t, sin_t = _rope_tables(positions)
    cmp_end = np.minimum(np.arange(ncp) * CMP_STRIDE + CMP_LEN - 1, seq - 1)
    cos_c, sin_c = _rope_tables(positions[:, cmp_end])
    selmap_t = _selection_map_t(seq)
    tile2 = lambda v: jnp.tile(v.reshape(1, HEAD_DIM), (1, HEADS_PER_VREG))

    x2 = x.reshape(t, d)
    for layer in range(depth):
        w_perm = _permute_w_in(w_in[layer], cw)
        q, kc, vc, ks, vs, kw, vw, gates, glu = _in_proj(
            x2, attn_norm_g[layer].reshape(1, d), w_perm, cos_t, sin_t,
            tile2(q_norm_g[layer]), tile2(k_norm_g[layer, 1]), tile2(k_norm_g[layer, 2]), seq)
        kcmp, vcmp = _compress(
            kc, vc,
            _compress_weights(cmp_pos_k[layer], cmp_w1_k[layer], cmp_w2_k[layer]),
            _compress_weights(cmp_pos_v[layer], cmp_w1_v[layer], cmp_w2_v[layer]),
            tile2(k_norm_g[layer, 0]), cos_c, sin_c, b, ncp)
        score_bound = (HEAD_DIM ** 0.5 * LOG2_E * SCORE_BOUND_MARGIN * jnp.max(jnp.abs(q_norm_g[layer]))
                       * jnp.max(jnp.abs(k_norm_g[layer]))).astype(F32).reshape(1)
        attn = _attention(score_bound, q, kcmp, vcmp, ks, vs, kw, vw, gates, selmap_t, b, seq)
        conv = _conv(glu, conv_w[layer], conv_b[layer].reshape(1, cw), conv_ln_g[layer].reshape(1, cw),
                     conv_ln_b[layer].reshape(1, cw), b, seq)
        wo = w_out[layer].astype(BF16)
        g2 = ffn_norm_g[layer].reshape(1, d)
        i = layer // 2
        if layer % 2 == 0:
            x2, h = _out_proj(attn, conv, wo, x2, g2)
            x2 = _ffn(h, x2, ffn_w_gate[i].astype(BF16), ffn_w_up[i].astype(BF16), ffn_w_down[i].astype(BF16))
        else:
            n_e = moe_router.shape[2]
            r = jnp.pad(moe_router[i], ((0, 0), (0, LANES - n_e)))
            r_hi = r.astype(BF16)
            r_lo = (r - r_hi.astype(F32)).astype(BF16)
            x2, h, route = _out_proj(attn, conv, wo, x2, g2, jnp.stack([r_hi, r_lo]), n_e)
            x2 = _moe_routed(h, x2, route, moe_w_gate[i].astype(BF16), moe_w_up[i].astype(BF16),
                             moe_w_down[i].astype(BF16))
    return x2.reshape(b, seq, d)
```

```python
import functools
import math

import jax
import jax.numpy as jnp
import numpy as np
from jax import lax
from jax.experimental import pallas as pl
from jax.experimental.pallas import tpu as pltpu

F32 = jnp.float32
BF16 = jnp.bfloat16

N_HEADS = 8
N_KV_HEADS = 2
Q_PER_KV = N_HEADS // N_KV_HEADS
HEAD_DIM = 64
N_BRANCH = 3
CMP_LEN = 32
CMP_STRIDE = 16
SEL_LEN = 64
SEL_TOPK = 16
N_LOCAL_SEL = 2
WINDOW = 512
CONV_KERNEL = 31
ROPE_THETA = 500000.0
ROPE_DIM = HEAD_DIM // 4
TOP_K = 2
EPS = 1e-6

LANES = 128
SUBLANES = 8
LOG2_E = math.log2(math.e)
NEG_BIG = -(2.0 ** 100)
MAX_SCORE_BOUND = 50.0
SCORE_BOUND_MARGIN = 1.02
HEADS_PER_VREG = LANES // HEAD_DIM
VMEM_LIMIT = 56 * 1024 * 1024

TM_PROJ = 512
TQ = 256
KC = 512
TS_CONV = 512
CH_CONV = 32
HALO = 32
TM_FFN = 1024
TF_FFN = 512
CT_MOE = 1024
BG_MOE = 256
BM_MOE = 512


def _cparams(sem):
    return pltpu.CompilerParams(dimension_semantics=sem, vmem_limit_bytes=VMEM_LIMIT)


def _dot(a, b):
    return jnp.dot(a, b, preferred_element_type=F32)


def _dot_nt(a, b):
    return lax.dot_general(a, b, (((1,), (1,)), ((), ())), preferred_element_type=F32)


def _split_bf16(x):
    hi = x.astype(BF16)
    lo = (x - hi.astype(F32)).astype(BF16)
    return hi, lo


def _rms_rows(x, g):
    ms = jnp.mean(x * x, axis=-1, keepdims=True)
    return x * lax.rsqrt(ms + EPS) * g


def _head_block_ones(width):
    r = lax.broadcasted_iota(jnp.int32, (width, width), 0) // HEAD_DIM
    c = lax.broadcasted_iota(jnp.int32, (width, width), 1) // HEAD_DIM
    return jnp.where(r == c, 1.0, 0.0).astype(BF16)


def _norm_rope(xg, ms, gain, cos, sin):
    y = xg * lax.rsqrt(ms + EPS) * gain
    lane = lax.broadcasted_iota(jnp.int32, y.shape, 1) % HEAD_DIM
    half = ROPE_DIM // 2
    partner = jnp.where(lane < half, pltpu.roll(y, LANES - half, 1), pltpu.roll(y, half, 1))
    return y * cos + partner * sin


def _head_norm_rope(xg, gain, cos, sin, ones_bd):
    ms = _dot((xg * xg).astype(BF16), ones_bd) * (1.0 / HEAD_DIM)
    return _norm_rope(xg, ms, gain, cos, sin)


def _head_norm_rope_pair(xa, xb, gain_a, gain_b, cos, sin, ones_bd2):
    sq = jnp.concatenate([xa * xa, xb * xb], axis=-1).astype(BF16)
    ms = _dot(sq, ones_bd2) * (1.0 / HEAD_DIM)
    return (_norm_rope(xa, ms[:, :LANES], gain_a, cos, sin),
            _norm_rope(xb, ms[:, LANES:], gain_b, cos, sin))


Q_W = N_HEADS * HEAD_DIM
KV_W = N_KV_HEADS * HEAD_DIM
SEG_Q = 0
SEG_KV = Q_W
SEG_GATE = SEG_KV + 6 * KV_W
SEG_UA = SEG_GATE + N_KV_HEADS * LANES
GATES_PER_KV = Q_PER_KV * N_BRANCH


def _in_proj_kernel(x_ref, g_ref, w_ref, cos_ref, sin_ref, qg_ref, ksg_ref, kwg_ref,
                    q_ref, kc_ref, vc_ref, ks_ref, vs_ref, kw_ref, vw_ref, gate_ref, glu_ref,
                    *, conv_w, seq):
    h = _rms_rows(x_ref[...], g_ref[...]).astype(BF16)
    tm = h.shape[0]
    cos = cos_ref[...]
    sin = sin_ref[...]
    ones_bd = _head_block_ones(2 * LANES)
    lane = lax.broadcasted_iota(jnp.int32, (tm, LANES), 1)
    tok = lax.rem(pl.program_id(0) * tm, seq) + lax.broadcasted_iota(jnp.int32, (tm, 1), 0)
    block_aug = jnp.where(lane - HEAD_DIM == tok // SEL_LEN, NEG_BIG, 0.0)
    ones_aug = jnp.where(lane == HEAD_DIM, 1.0, 0.0)

    def put_heads(ref, first, val, aug):
        for j in range(HEADS_PER_VREG):
            head = val if j == 0 else pltpu.roll(val, LANES - j * HEAD_DIM, 1)
            ref[first + j] = jnp.where(lane < HEAD_DIM, head, aug).astype(ref.dtype)

    qkv = _dot(h, w_ref[:, SEG_Q:SEG_GATE])
    group = lambda j: qkv[:, j * LANES:(j + 1) * LANES]
    first_kv = Q_W // LANES
    scale = HEAD_DIM ** -0.5 * LOG2_E
    qg = qg_ref[...]
    for c in range(0, first_kv, 2):
        pair = _head_norm_rope_pair(group(c), group(c + 1), qg, qg, cos, sin, ones_bd)
        for j, y in enumerate(pair):
            put_heads(q_ref, (c + j) * HEADS_PER_VREG, y * scale, 0.0)
    kc_ref[...] = group(first_kv)
    vc_ref[...] = group(first_kv + 1)
    k_sel, k_win = _head_norm_rope_pair(group(first_kv + 2), group(first_kv + 4), ksg_ref[...], kwg_ref[...],
                                        cos, sin, ones_bd)
    put_heads(ks_ref, 0, k_sel, block_aug)
    put_heads(vs_ref, 0, group(first_kv + 3), ones_aug)
    put_heads(kw_ref, 0, k_win, 0.0)
    put_heads(vw_ref, 0, group(first_kv + 5), ones_aug)
    gate_ref[...] = jax.nn.sigmoid(_dot(h, w_ref[:, SEG_GATE:SEG_UA]))
    u = _dot(h, w_ref[:, SEG_UA:])
    glu_ref[...] = u[:, :conv_w] * jax.nn.sigmoid(u[:, conv_w:])


def _in_proj(x2, g, w_perm, cos_t, sin_t, qg, ksg, kwg, seq):
    t, d = x2.shape
    conv_w = (w_perm.shape[1] - SEG_UA) // 2
    tm = min(TM_PROJ, t)
    row = lambda i: (i, 0)
    const = lambda i: (0, 0)
    head_row = lambda i: (0, i, 0)
    out_shape = [
        jax.ShapeDtypeStruct((N_HEADS, t, LANES), BF16),
        jax.ShapeDtypeStruct((t, KV_W), F32),
        jax.ShapeDtypeStruct((t, KV_W), F32),
        jax.ShapeDtypeStruct((N_KV_HEADS, t, LANES), BF16),
        jax.ShapeDtypeStruct((N_KV_HEADS, t, LANES), BF16),
        jax.ShapeDtypeStruct((N_KV_HEADS, t, LANES), BF16),
        jax.ShapeDtypeStruct((N_KV_HEADS, t, LANES), BF16),
        jax.ShapeDtypeStruct((t, N_KV_HEADS * LANES), F32),
        jax.ShapeDtypeStruct((t, conv_w), F32),
    ]
    kv_spec = pl.BlockSpec((N_KV_HEADS, tm, LANES), head_row)
    out_specs = [
        pl.BlockSpec((N_HEADS, tm, LANES), head_row),
        pl.BlockSpec((tm, KV_W), row), pl.BlockSpec((tm, KV_W), row),
        kv_spec, kv_spec, kv_spec, kv_spec,
        pl.BlockSpec((tm, N_KV_HEADS * LANES), row),
        pl.BlockSpec((tm, conv_w), row),
    ]
    in_specs = [
        pl.BlockSpec((tm, d), row), pl.BlockSpec((1, d), const),
        pl.BlockSpec(w_perm.shape, const),
        pl.BlockSpec((tm, LANES), row), pl.BlockSpec((tm, LANES), row),
        pl.BlockSpec((1, LANES), const), pl.BlockSpec((1, LANES), const), pl.BlockSpec((1, LANES), const),
    ]
    return pl.pallas_call(
        functools.partial(_in_proj_kernel, conv_w=conv_w, seq=seq),
        grid=(t // tm,), in_specs=in_specs, out_specs=out_specs, out_shape=out_shape,
        compiler_params=_cparams(("parallel",)), name="in_proj",
    )(x2, g, w_perm, cos_t, sin_t, qg, ksg, kwg)


def _gelu_tanh(x):
    c = math.sqrt(2.0 / math.pi)
    return 0.5 * x * (1.0 + jnp.tanh(c * (x + 0.044715 * (x * x * x))))


def _compress_kernel(k_ref, v_ref, w1ak_ref, w1bk_ref, w2k_ref, pak_ref, pbk_ref,
                     w1av_ref, w1bv_ref, w2v_ref, pav_ref, pbv_ref,
                     kg_ref, cos_ref, sin_ref, ko_ref, vo_ref):
    def mlp(x_ref, w1a_ref, w1b_ref, w2_ref, pa_ref, pb_ref):
        n = x_ref.shape[0] // CMP_STRIDE
        first = second = None
        for l in range(CMP_STRIDE):
            x = x_ref[pl.ds(l, n, stride=CMP_STRIDE), :]
            cols = slice(l * KV_W, (l + 1) * KV_W)
            fa = _dot((x + pa_ref[:, cols]).astype(BF16), w1a_ref[cols, :])
            fb = _dot((x + pb_ref[:, cols]).astype(BF16), w1b_ref[cols, :])
            first = fa if first is None else first + fa
            second = fb if second is None else second + fb
        hid = first + pltpu.roll(second, n - 1, 0)
        return _dot(_gelu_tanh(hid).astype(BF16), w2_ref[...])

    kc = mlp(k_ref, w1ak_ref, w1bk_ref, w2k_ref, pak_ref, pbk_ref)
    kc = _head_norm_rope(kc, kg_ref[...], cos_ref[...], sin_ref[...], _head_block_ones(LANES))
    vc = mlp(v_ref, w1av_ref, w1bv_ref, w2v_ref, pav_ref, pbv_ref)
    lane = lax.broadcasted_iota(jnp.int32, kc.shape, 1)
    for j in range(N_KV_HEADS):
        for val, ref in ((kc, ko_ref), (vc, vo_ref)):
            head = val if j == 0 else pltpu.roll(val, LANES - j * HEAD_DIM, 1)
            ref[0, j] = jnp.where(lane < HEAD_DIM, head, 0.0).astype(ref.dtype)


def _compress(kc, vc, wk, wv, kg, cosc, sinc, b, ncp):
    seq = kc.shape[0] // b
    const = lambda i: (0, 0)
    row = lambda i: (i, 0)

    def wspecs(ws):
        return [pl.BlockSpec(w.shape, const) for w in ws]

    out = jax.ShapeDtypeStruct((b, N_KV_HEADS, ncp, LANES), BF16)
    ospec = pl.BlockSpec((1, N_KV_HEADS, ncp, LANES), lambda i: (i, 0, 0, 0))
    return pl.pallas_call(
        _compress_kernel, grid=(b,),
        in_specs=[pl.BlockSpec((seq, KV_W), row), pl.BlockSpec((seq, KV_W), row)]
        + wspecs(wk) + wspecs(wv)
        + [pl.BlockSpec((1, LANES), const), pl.BlockSpec((ncp, LANES), row), pl.BlockSpec((ncp, LANES), row)],
        out_specs=[ospec, ospec], out_shape=[out, out],
        compiler_params=_cparams(("parallel",)), name="compress",
    )(kc, vc, *wk, *wv, kg, cosc, sinc)


def _attn_kernel(off_ref, q_ref, kc_ref, vc_ref, ks_ref, vs_ref, kw_ref, vw_ref, gate_ref, selmap_ref,
                 o_ref, *, seq, tq, kc_len, top_n, bounded):
    i = pl.program_id(2)
    t0 = i * tq
    rows = Q_PER_KV * tq
    n_sel = seq // SEL_LEN
    q2 = q_ref[...].reshape(rows, LANES)
    t_row = t0 + (lax.broadcasted_iota(jnp.int32, (rows, 1), 0) & (tq - 1))
    t_tok = t0 + lax.broadcasted_iota(jnp.int32, (tq, 1), 0)
    neg_offset = -off_ref[0] if bounded else 0.0

    def add_bias(s, bias):
        return (s.reshape(s.shape[0] // tq, tq, s.shape[1]) + bias[None]).reshape(s.shape)

    kcmp = kc_ref[0, 0]
    ncp = kcmp.shape[0]
    s_c = _dot_nt(q2, kcmp)
    cmp_end = lax.broadcasted_iota(jnp.int32, (1, ncp), 1) * CMP_STRIDE + (CMP_LEN - 1)
    if bounded:
        e_c = jnp.exp2(add_bias(s_c, jnp.where(cmp_end <= t_tok, neg_offset, NEG_BIG)))
    else:
        s_c = jnp.where(cmp_end <= t_row, s_c, -jnp.inf)
        m_c = jnp.max(s_c, axis=-1, keepdims=True)
        e_c = jnp.exp2(s_c - jnp.where(m_c == -jnp.inf, 0.0, m_c))
    p_c = e_c * (1.0 / jnp.maximum(jnp.sum(e_c, axis=-1, keepdims=True), jnp.finfo(F32).tiny))
    o_c = _dot(p_c.astype(BF16), vc_ref[0, 0])

    def denominator(acc):
        return acc[:, HEAD_DIM:HEAD_DIM + 1]

    span = min(WINDOW + tq, seq)
    w0 = pl.multiple_of(jnp.maximum(t0 - WINDOW, 0), tq)
    key_w = w0 + lax.broadcasted_iota(jnp.int32, (1, span), 1)
    bias_w = jnp.where((key_w <= t_tok) & (key_w > t_tok - WINDOW), neg_offset, NEG_BIG)
    s_w = add_bias(_dot_nt(q2, kw_ref[0, pl.ds(w0, span), :]), bias_w)
    if bounded:
        p_w = jnp.exp2(s_w).astype(BF16)
    else:
        p_w = jnp.exp2((s_w - jnp.max(s_w, axis=-1, keepdims=True)).astype(BF16))
    acc_w = _dot(p_w, vw_ref[0, pl.ds(w0, span), :])
    o_w = acc_w * (1.0 / denominator(acc_w))

    p_hi, p_lo = _split_bf16(jnp.sum(p_c.reshape(Q_PER_KV, tq, ncp), axis=0))
    selmap = selmap_ref[...]
    imp = _dot_nt(selmap, p_hi) + _dot_nt(selmap, p_lo)
    blk = lax.broadcasted_iota(jnp.int32, (n_sel, tq), 0)
    cur = (t0 + lax.broadcasted_iota(jnp.int32, (n_sel, tq), 1)) // SEL_LEN
    causal_blk = blk <= cur
    forced = (blk == 0) | (causal_blk & (blk > cur - N_LOCAL_SEL))
    score = jnp.where(forced, jnp.inf, jnp.where(causal_blk, imp, -jnp.inf))
    sub = lax.broadcasted_iota(jnp.int32, (SUBLANES, tq), 0)
    groups = [score[g * SUBLANES:(g + 1) * SUBLANES, :] for g in range(n_sel // SUBLANES)]
    ranks = [jnp.zeros((SUBLANES, tq), F32) for _ in groups]
    for jp in range(n_sel):
        other = jnp.broadcast_to(score[jp:jp + 1, :], (SUBLANES, tq))
        for g, sg in enumerate(groups):
            first = g * SUBLANES
            if first > jp:
                inc = jnp.where(other >= sg, 1.0, 0.0)
            elif first + SUBLANES - 1 <= jp:
                inc = jnp.where(other > sg, 1.0, 0.0)
            else:
                inc = jnp.where(other > sg, 1.0, jnp.where((other == sg) & (sub > jp - first), 1.0, 0.0))
            ranks[g] = ranks[g] + inc
    chosen_flag = neg_offset * (1.0 / NEG_BIG)
    block_flags = jnp.where(jnp.concatenate(ranks, axis=0) < top_n, chosen_flag, 1.0)
    flag_rows = [jnp.zeros((HEAD_DIM, tq), F32), block_flags]
    if HEAD_DIM + n_sel < LANES:
        flag_rows.append(jnp.zeros((LANES - HEAD_DIM - n_sel, tq), F32))
    flags = jnp.transpose(jnp.concatenate(flag_rows, axis=0))
    q_sel = (q2.reshape(Q_PER_KV, tq, LANES) + flags.astype(BF16)[None]).reshape(rows, LANES)

    def sel_chunk(k0, carry, bias):
        s = _dot_nt(q_sel, ks_ref[0, pl.ds(k0, kc_len), :])
        if bias is not None:
            s = add_bias(s, bias)
        v = vs_ref[0, pl.ds(k0, kc_len), :]
        if bounded:
            (acc,) = carry
            return (acc + _dot(jnp.exp2(s).astype(BF16), v),)
        m, acc = carry
        m_new = jnp.maximum(m, jnp.max(s, axis=-1, keepdims=True))
        p = jnp.exp2((s - m_new).astype(BF16))
        return m_new, jnp.exp2(m - m_new) * acc + _dot(p, v)

    n_full = t0 // kc_len
    init = (jnp.zeros((rows, LANES), F32),)
    if not bounded:
        init = (jnp.full((rows, 1), -jnp.inf, F32),) + init
    carry = lax.fori_loop(
        0, n_full, lambda c, cr: sel_chunk(pl.multiple_of(c * kc_len, kc_len), cr, None), init)
    kd = pl.multiple_of(n_full * kc_len, kc_len)
    key_d = kd + lax.broadcasted_iota(jnp.int32, (1, kc_len), 1)
    acc_s = sel_chunk(kd, carry, jnp.where(key_d <= t_tok, 0.0, NEG_BIG))[-1]
    o_s = acc_s * (1.0 / denominator(acc_s))

    gates = gate_ref[...]

    def gate_col(br):
        cols = [gates[:, g * N_BRANCH + br:g * N_BRANCH + br + 1] for g in range(Q_PER_KV)]
        return jnp.concatenate(cols, axis=0)

    o = gate_col(0) * o_c + gate_col(1) * o_s + gate_col(2) * o_w
    o3 = o.reshape(Q_PER_KV, tq, LANES)
    lane = lax.broadcasted_iota(jnp.int32, (tq, LANES), 1)
    pairs = [jnp.where(lane < HEAD_DIM, o3[g], pltpu.roll(o3[g + 1], HEAD_DIM, 1))
             for g in range(0, Q_PER_KV, HEADS_PER_VREG)]
    o_ref[...] = jnp.concatenate(pairs, axis=-1).astype(o_ref.dtype)


def _attention(score_bound, q, kcmp, vcmp, ks, vs, kw, vw, gates, selmap_t, b, seq):
    t = b * seq
    tq = min(TQ, seq)
    kc_len = min(KC, seq)
    nq = seq // tq
    ncp = kcmp.shape[2]
    n_sel = seq // SEL_LEN
    assert HEAD_DIM + n_sel <= LANES, "selection-block flags must fit beside the head dims"
    top_n = min(SEL_TOPK, n_sel)
    cmp_spec = pl.BlockSpec((1, 1, ncp, LANES), lambda bi, kh, i: (bi, kh, 0, 0))
    seq_spec = pl.BlockSpec((1, seq, LANES), lambda bi, kh, i: (kh, bi, 0))

    def run(bounded):
        return pl.pallas_call(
            functools.partial(_attn_kernel, seq=seq, tq=tq, kc_len=kc_len, top_n=top_n, bounded=bounded),
            grid=(b, N_KV_HEADS, nq),
            in_specs=[
                pl.BlockSpec(memory_space=pltpu.SMEM),
                pl.BlockSpec((Q_PER_KV, tq, LANES), lambda bi, kh, i: (kh, bi * nq + i, 0)),
                cmp_spec, cmp_spec, seq_spec, seq_spec, seq_spec, seq_spec,
                pl.BlockSpec((tq, LANES), lambda bi, kh, i: (bi * nq + i, kh)),
                pl.BlockSpec(selmap_t.shape, lambda bi, kh, i: (0, 0)),
            ],
            out_specs=pl.BlockSpec((tq, Q_PER_KV * HEAD_DIM), lambda bi, kh, i: (bi * nq + i, kh)),
            out_shape=jax.ShapeDtypeStruct((t, N_HEADS * HEAD_DIM), BF16),
            compiler_params=_cparams(("parallel", "parallel", "arbitrary")),
            name="nsa_attention" if bounded else "nsa_attention_running_max",
        )(score_bound, q, kcmp, vcmp, ks, vs, kw, vw, gates, selmap_t)

    return lax.cond(score_bound[0] < MAX_SCORE_BOUND, lambda: run(True), lambda: run(False))


def _conv_kernel(glu_ref, w_ref, b_ref, lg_ref, lb_ref, o_ref, ext_ref, shift_ref, *, ts):
    i = pl.program_id(1)

    @pl.when(i == 0)
    def _():
        ext_ref[0:HALO, :] = jnp.zeros((HALO, ext_ref.shape[1]), F32)

    @pl.when(i > 0)
    def _():
        ext_ref[0:HALO, :] = ext_ref[ts:ts + HALO, :]

    ext_ref[HALO:HALO + ts, :] = glu_ref[...]
    n_shift = shift_ref.shape[1]
    for r in range(1, SUBLANES):
        shift_ref[r - 1] = ext_ref[r:r + n_shift, :]

    def rows_from(o):
        r = o % SUBLANES
        if r == 0:
            return ext_ref[o:o + CH_CONV, :]
        return shift_ref[r - 1, o - r:o - r + CH_CONV, :]

    w = w_ref[...]
    first_tap = HALO - (CONV_KERNEL - 1)
    for c in range(ts // CH_CONV):
        base = c * CH_CONV + first_tap
        acc = w[0:1, :] * rows_from(base)
        for k in range(1, CONV_KERNEL):
            acc = acc + w[k:k + 1, :] * rows_from(base + k)
        y = acc + b_ref[...]
        yc = y - jnp.mean(y, axis=-1, keepdims=True)
        yn = yc * lax.rsqrt(jnp.mean(yc * yc, axis=-1, keepdims=True) + EPS)
        z = yn * lg_ref[...] + lb_ref[...]
        o_ref[c * CH_CONV:(c + 1) * CH_CONV, :] = (z * jax.nn.sigmoid(z)).astype(o_ref.dtype)


def _conv(glu, w, bias, lg, lb, b, seq):
    t, cw = glu.shape
    ts = min(TS_CONV, seq)
    ns = seq // ts
    const = lambda bi, i: (0, 0)
    row = lambda bi, i: (bi * ns + i, 0)
    return pl.pallas_call(
        functools.partial(_conv_kernel, ts=ts), grid=(b, ns),
        in_specs=[pl.BlockSpec((ts, cw), row), pl.BlockSpec(w.shape, const),
                  pl.BlockSpec((1, cw), const), pl.BlockSpec((1, cw), const), pl.BlockSpec((1, cw), const)],
        out_specs=pl.BlockSpec((ts, cw), row),
        out_shape=jax.ShapeDtypeStruct((t, cw), BF16),
        scratch_shapes=[pltpu.VMEM((ts + HALO, cw), F32),
                        pltpu.VMEM((SUBLANES - 1, ts + HALO - SUBLANES, cw), F32)],
        compiler_params=_cparams(("arbitrary", "arbitrary")), name="conformer_conv",
    )(glu, w, bias, lg, lb)


def _top2_gates(logits, n_experts):
    lane = lax.broadcasted_iota(jnp.int32, logits.shape, 1)
    x = jnp.where(lane < n_experts, logits, -jnp.inf)
    m1 = jnp.max(x, axis=-1, keepdims=True)
    i1 = jnp.min(jnp.where(x == m1, lane, LANES), axis=-1, keepdims=True)
    x2 = jnp.where(lane == i1, -jnp.inf, x)
    m2 = jnp.max(x2, axis=-1, keepdims=True)
    i2 = jnp.min(jnp.where(x2 == m2, lane, LANES), axis=-1, keepdims=True)
    e2 = jnp.exp(m2 - m1)
    inv = 1.0 / (1.0 + e2)
    return jnp.where(lane == i1, inv, jnp.where(lane == i2, e2 * inv, 0.0))


def _out_proj_kernel(*refs, n_experts):
    if n_experts:
        attn_ref, conv_ref, wo_ref, x_ref, g_ref, rt_ref, xo_ref, h_ref, gate_ref = refs
    else:
        attn_ref, conv_ref, wo_ref, x_ref, g_ref, xo_ref, h_ref = refs
    aw = attn_ref.shape[1]
    x = x_ref[...] + _dot(attn_ref[...], wo_ref[0:aw, :]) + _dot(conv_ref[...], wo_ref[aw:, :])
    xo_ref[...] = x
    h = _rms_rows(x, g_ref[...])
    h_ref[...] = h.astype(h_ref.dtype)
    if n_experts:
        h_hi, h_lo = _split_bf16(h)
        router = rt_ref[...]
        by_hi = _dot(h_hi, router)
        logits = by_hi[:, :LANES] + (by_hi[:, LANES:] + _dot(h_lo, router[:, :LANES]))
        gate_ref[...] = _top2_gates(logits, n_experts)


def _out_proj(attn, conv, wo, x2, g, router_split=None, n_experts=0):
    t, d = x2.shape
    tm = min(TM_PROJ, t)
    row = lambda i: (i, 0)
    const = lambda i: (0, 0)
    in_specs = [pl.BlockSpec((tm, attn.shape[1]), row), pl.BlockSpec((tm, conv.shape[1]), row),
                pl.BlockSpec(wo.shape, const), pl.BlockSpec((tm, d), row), pl.BlockSpec((1, d), const)]
    out_shape = [jax.ShapeDtypeStruct((t, d), F32), jax.ShapeDtypeStruct((t, d), BF16)]
    out_specs = [pl.BlockSpec((tm, d), row), pl.BlockSpec((tm, d), row)]
    args = [attn, conv, wo, x2, g]
    if n_experts:
        in_specs.append(pl.BlockSpec(router_split.shape, const))
        out_shape.append(jax.ShapeDtypeStruct((t, LANES), F32))
        out_specs.append(pl.BlockSpec((tm, LANES), row))
        args.append(router_split)
    return pl.pallas_call(
        functools.partial(_out_proj_kernel, n_experts=n_experts),
        grid=(t // tm,), in_specs=in_specs, out_specs=out_specs, out_shape=out_shape,
        compiler_params=_cparams(("parallel",)), name="out_proj",
    )(*args)


def _swiglu_tiles(h, acc, wg, wu, wd, dff, tf):
    for f in range(dff // tf):
        cols = slice(f * tf, (f + 1) * tf)
        a = _dot(h, wg(cols))
        u = _dot(h, wu(cols))
        acc = acc + _dot(((a * jax.nn.sigmoid(a)) * u).astype(BF16), wd(cols))
    return acc


def _ffn_kernel(h_ref, x_ref, wg_ref, wu_ref, wd_ref, o_ref, *, tf):
    o_ref[...] = _swiglu_tiles(h_ref[...], x_ref[...], lambda c: wg_ref[:, c], lambda c: wu_ref[:, c],
                               lambda c: wd_ref[c, :], wg_ref.shape[1], tf)


def _ffn(h, x2, wg, wu, wd):
    t, d = x2.shape
    tm = min(TM_FFN, t)
    row = lambda i: (i, 0)
    resident = lambda w: pl.BlockSpec(w.shape, lambda i: (0, 0), pipeline_mode=pl.Buffered(1))
    return pl.pallas_call(
        functools.partial(_ffn_kernel, tf=TF_FFN), grid=(t // tm,),
        in_specs=[pl.BlockSpec((tm, d), row), pl.BlockSpec((tm, d), row),
                  resident(wg), resident(wu), resident(wd)],
        out_specs=pl.BlockSpec((tm, d), row),
        out_shape=jax.ShapeDtypeStruct((t, d), F32),
        compiler_params=_cparams(("parallel",)), name="ffn",
    )(h, x2, wg, wu, wd)


def _route_scan_kernel(g_ref, pos_t_ref, pos_ref, cnt_ref, tot_ref, carry_ref, *, n_experts):
    c = pl.program_id(0)

    @pl.when(c == 0)
    def _():
        carry_ref[...] = jnp.zeros_like(carry_ref)

    ct = g_ref.shape[0]
    routed = g_ref[...] > 0.0
    a = jnp.where(routed, 1.0, 0.0)
    earlier = lax.broadcasted_iota(jnp.int32, (ct, ct), 1) < lax.broadcasted_iota(jnp.int32, (ct, ct), 0)
    base = carry_ref[...]
    pos = jnp.where(routed, _dot(jnp.where(earlier, 1.0, 0.0).astype(BF16), a.astype(BF16)) + base, -1.0)
    pos_ref[...] = pos
    pos_t_ref[0] = jnp.transpose(pos)[0:n_experts, :]
    cnt_ref[0] = base
    total = base + jnp.sum(a, axis=0, keepdims=True)
    carry_ref[...] = total
    tot_ref[...] = total


def _route_scan(gates, n_experts):
    t = gates.shape[0]
    ct = CT_MOE
    nch = t // ct
    return pl.pallas_call(
        functools.partial(_route_scan_kernel, n_experts=n_experts), grid=(nch,),
        in_specs=[pl.BlockSpec((ct, LANES), lambda c: (c, 0))],
        out_specs=[pl.BlockSpec((1, n_experts, ct), lambda c: (c, 0, 0)),
                   pl.BlockSpec((ct, LANES), lambda c: (c, 0)),
                   pl.BlockSpec((1, 1, LANES), lambda c: (c, 0, 0)),
                   pl.BlockSpec((1, LANES), lambda c: (0, 0))],
        out_shape=[jax.ShapeDtypeStruct((nch, n_experts, ct), F32),
                   jax.ShapeDtypeStruct((t, LANES), F32),
                   jax.ShapeDtypeStruct((nch, 1, LANES), F32),
                   jax.ShapeDtypeStruct((1, LANES), F32)],
        scratch_shapes=[pltpu.VMEM((1, LANES), F32)],
        compiler_params=_cparams(("arbitrary",)), name="route_scan",
    )(gates)


def _moe_gather_kernel(blk_ref, chk_ref, exp_ref, first_ref, valid_ref,
                       h_ref, pos_t_ref, pstart_ref, xs_ref):
    w = pl.program_id(0)

    @pl.when(first_ref[w] == 1)
    def _():
        xs_ref[...] = jnp.zeros_like(xs_ref)

    @pl.when(valid_ref[w] == 1)
    def _():
        pos = pos_t_ref[0]
        slot = jnp.where(pos >= 0.0, pos + pstart_ref[...], -1.0)
        sub = lax.broadcasted_iota(jnp.int32, pos.shape, 0)
        slot_e = jnp.sum(jnp.where(sub == exp_ref[w], slot, 0.0), axis=0, keepdims=True)
        bg = xs_ref.shape[0]
        target = (blk_ref[w] * bg + lax.broadcasted_iota(jnp.int32, (bg, 1), 0)).astype(F32)
        onehot = jnp.where(slot_e == target, 1.0, 0.0).astype(BF16)
        xs_ref[...] += _dot(onehot, h_ref[...]).astype(xs_ref.dtype)


def _moe_gather(h, pos_t, pstart_col, items, n_slots):
    t, d = h.shape
    n_items = items[0].shape[0]
    n_e = pos_t.shape[1]
    grid_spec = pltpu.PrefetchScalarGridSpec(
        num_scalar_prefetch=5, grid=(n_items,),
        in_specs=[pl.BlockSpec((CT_MOE, d), lambda w, blk, chk, *_: (chk[w], 0)),
                  pl.BlockSpec((1, n_e, CT_MOE), lambda w, blk, chk, *_: (chk[w], 0, 0)),
                  pl.BlockSpec((n_e, 1), lambda w, *_: (0, 0))],
        out_specs=pl.BlockSpec((BG_MOE, d), lambda w, blk, *_: (blk[w], 0)))
    return pl.pallas_call(
        _moe_gather_kernel, grid_spec=grid_spec,
        out_shape=jax.ShapeDtypeStruct((n_slots, d), BF16),
        compiler_params=_cparams(("arbitrary",)), name="moe_gather",
    )(*items, h, pos_t, pstart_col)


def _moe_ffn_kernel(exp_ref, valid_ref, xs_ref, wg_ref, wu_ref, wd_ref, ys_ref, *, tf):
    j = pl.program_id(0)

    @pl.when(valid_ref[j] == 1)
    def _():
        zero = jnp.zeros(ys_ref.shape, F32)
        ys = _swiglu_tiles(xs_ref[...], zero, lambda c: wg_ref[0, :, c], lambda c: wu_ref[0, :, c],
                           lambda c: wd_ref[0, c, :], wg_ref.shape[2], tf)
        ys_ref[...] = ys.astype(ys_ref.dtype)

    @pl.when(valid_ref[j] == 0)
    def _():
        ys_ref[...] = jnp.zeros_like(ys_ref)


def _moe_ffn(xs, blk_expert, blk_valid, wg, wu, wd):
    n_slots, d = xs.shape
    expert = lambda w: pl.BlockSpec((1,) + w.shape[1:], lambda j, e, v: (e[j], 0, 0),
                                    pipeline_mode=pl.Buffered(1))
    grid_spec = pltpu.PrefetchScalarGridSpec(
        num_scalar_prefetch=2, grid=(n_slots // BM_MOE,),
        in_specs=[pl.BlockSpec((BM_MOE, d), lambda j, e, v: (j, 0)), expert(wg), expert(wu), expert(wd)],
        out_specs=pl.BlockSpec((BM_MOE, d), lambda j, e, v: (j, 0)))
    return pl.pallas_call(
        functools.partial(_moe_ffn_kernel, tf=TF_FFN), grid_spec=grid_spec,
        out_shape=jax.ShapeDtypeStruct((n_slots, d), BF16),
        compiler_params=_cparams(("arbitrary",)), name="moe_ffn",
    )(blk_expert, blk_valid, xs, wg, wu, wd)


def _moe_combine_kernel(tile_ref, blk_ref, exp_ref, first_ref, valid_ref,
                        x_ref, pos_ref, g_ref, pstart_ref, ys_ref, o_ref):
    w = pl.program_id(0)

    @pl.when(first_ref[w] == 1)
    def _():
        o_ref[...] = x_ref[...]

    @pl.when(valid_ref[w] == 1)
    def _():
        pos = pos_ref[...]
        slot = jnp.where(pos >= 0.0, pos + pstart_ref[...], -1.0)
        mine = lax.broadcasted_iota(jnp.int32, pos.shape, 1) == exp_ref[w]
        slot_e = jnp.sum(jnp.where(mine, slot, 0.0), axis=-1, keepdims=True)
        gate_e = jnp.sum(jnp.where(mine, g_ref[...], 0.0), axis=-1, keepdims=True)
        bs = ys_ref.shape[0]
        target = (blk_ref[w] * bs + lax.broadcasted_iota(jnp.int32, (1, bs), 1)).astype(F32)
        onehot = jnp.where(slot_e == target, 1.0, 0.0).astype(BF16)
        o_ref[...] += gate_e * _dot(onehot, ys_ref[...])


def _moe_combine(x2, pos, gates, pstart_row, ys, items):
    t, d = x2.shape
    n_items = items[0].shape[0]
    tok = lambda w, tile, *_: (tile[w], 0)
    grid_spec = pltpu.PrefetchScalarGridSpec(
        num_scalar_prefetch=5, grid=(n_items,),
        in_specs=[pl.BlockSpec((CT_MOE, d), tok), pl.BlockSpec((CT_MOE, LANES), tok),
                  pl.BlockSpec((CT_MOE, LANES), tok), pl.BlockSpec((1, LANES), lambda w, *_: (0, 0)),
                  pl.BlockSpec((BG_MOE, d), lambda w, tile, blk, *_: (blk[w], 0))],
        out_specs=pl.BlockSpec((CT_MOE, d), tok))
    return pl.pallas_call(
        _moe_combine_kernel, grid_spec=grid_spec,
        out_shape=jax.ShapeDtypeStruct((t, d), F32),
        compiler_params=_cparams(("arbitrary",)), name="moe_combine",
    )(*items, x2, pos, gates, pstart_row, ys)


def _count_le(ascending, x):
    return jnp.sum(ascending[None, :] <= x[:, None], axis=1).astype(jnp.int32)


def _work_items(group_id, lo, hi, n_items_max):
    n = jnp.maximum(hi - lo + 1, 0)
    ends = jnp.cumsum(n)
    total = ends[-1]
    w = jnp.arange(n_items_max, dtype=jnp.int32)
    wc = jnp.minimum(w, total - 1)
    g = _count_le(ends, wc)
    k = wc - (ends[g] - n[g])
    valid = (w < total).astype(jnp.int32)
    return group_id[g], lo[g] + k, g, ((k == 0) & (w < total)).astype(jnp.int32), valid


def _moe_routed(h, x2, gates, wg, wu, wd):
    t, d = x2.shape
    n_e = wg.shape[0]
    nch = t // CT_MOE
    n_slots = t * TOP_K + n_e * BM_MOE
    nbg = n_slots // BG_MOE
    pos_t, pos, cnt, tot = _route_scan(gates, n_e)

    counts = tot[0, :n_e].astype(jnp.int32)
    padded = (counts + BM_MOE - 1) // BM_MOE * BM_MOE
    pend = jnp.cumsum(padded)
    pstart = pend - padded
    cum = cnt[:, 0, :n_e].astype(jnp.int32)
    cum_end = jnp.concatenate([cum[1:], counts[None]], axis=0)

    sb = jnp.arange(nbg, dtype=jnp.int32)
    sb_e = jnp.minimum(_count_le(pend, sb * BG_MOE), n_e - 1)
    p_lo = sb * BG_MOE - pstart[sb_e]
    has_rows = (sb * BG_MOE < pend[-1]) & (p_lo < counts[sb_e])
    c_lo = jnp.sum(cum_end.T[sb_e] <= p_lo[:, None], axis=1).astype(jnp.int32)
    c_hi = jnp.sum(cum.T[sb_e] <= (p_lo + BG_MOE - 1)[:, None], axis=1).astype(jnp.int32) - 1
    c_lo = jnp.where(has_rows, c_lo, 0)
    c_hi = jnp.where(has_rows, c_hi, 0)
    g_blk, g_chk, g_grp, g_first, g_valid = _work_items(sb, c_lo, c_hi, nbg + n_e * nch)
    xs = _moe_gather(h, pos_t, pstart.astype(F32).reshape(n_e, 1),
                     (g_blk, g_chk, sb_e[g_grp], g_first, g_valid), n_slots)

    mb = jnp.arange(n_slots // BM_MOE, dtype=jnp.int32) * BM_MOE
    mb_e = jnp.minimum(_count_le(pend, mb), n_e - 1)
    mb_valid = ((mb < pend[-1]) & (mb - pstart[mb_e] < counts[mb_e])).astype(jnp.int32)
    ys = _moe_ffn(xs, mb_e, mb_valid, wg, wu, wd)

    tile = jnp.repeat(jnp.arange(nch, dtype=jnp.int32), n_e)
    te = jnp.tile(jnp.arange(n_e, dtype=jnp.int32), nch)
    s_lo = (pstart[None, :] + cum).reshape(-1)
    s_hi = (pstart[None, :] + cum_end).reshape(-1) - 1
    b_lo = s_lo // BG_MOE
    b_hi = jnp.where(s_hi >= s_lo, s_hi // BG_MOE, b_lo - 1)
    grp = jnp.arange(nch * n_e, dtype=jnp.int32)
    c_grp, c_blk, _, _, c_valid = _work_items(grp, b_lo, b_hi, nbg + n_e * nch)
    c_tile = tile[c_grp]
    c_first = jnp.concatenate([jnp.ones((1,), jnp.int32), (c_tile[1:] != c_tile[:-1]).astype(jnp.int32)])
    pstart_row = jnp.zeros((1, LANES), F32).at[0, :n_e].set(pstart.astype(F32))
    return _moe_combine(x2, pos, gates, pstart_row, ys, (c_tile, c_blk, te[c_grp], c_first, c_valid))


def _rope_tables(pos):
    half = ROPE_DIM // 2
    inv_freq = ROPE_THETA ** (-2.0 * jnp.arange(half, dtype=F32) / ROPE_DIM)
    ang = pos.astype(F32).reshape(-1, 1) * inv_freq
    cos, sin = jnp.cos(ang), jnp.sin(ang)
    n = ang.shape[0]
    rest = HEAD_DIM - ROPE_DIM
    cos_h = jnp.concatenate([cos, cos, jnp.ones((n, rest), F32)], axis=-1)
    sin_h = jnp.concatenate([-sin, sin, jnp.zeros((n, rest), F32)], axis=-1)
    return jnp.tile(cos_h, (1, HEADS_PER_VREG)), jnp.tile(sin_h, (1, HEADS_PER_VREG))


def _permute_w_in(w, conv_w):
    d = w.shape[0]
    kv_end = Q_W + 6 * KV_W
    g = w[:, kv_end:kv_end + N_HEADS * N_BRANCH]
    pad = jnp.zeros((d, LANES - GATES_PER_KV), w.dtype)
    gate_cols = []
    for kh in range(N_KV_HEADS):
        gate_cols += [g[:, kh * GATES_PER_KV:(kh + 1) * GATES_PER_KV], pad]
    u = w[:, kv_end + N_HEADS * N_BRANCH:]
    return jnp.concatenate([w[:, :kv_end]] + gate_cols + [u], axis=1).astype(BF16)


def _compress_weights(pos_emb, w1, w2):
    hidden = w1.shape[1]
    eye = jnp.eye(N_KV_HEADS, dtype=w1.dtype)
    w1r = w1.reshape(CMP_LEN, HEAD_DIM, hidden)
    halves = []
    for part in (w1r[:CMP_STRIDE], w1r[CMP_STRIDE:]):
        full = jnp.einsum('ldj,hg->lhdgj', part, eye)
        halves.append(full.reshape(CMP_STRIDE * N_KV_HEADS * HEAD_DIM, N_KV_HEADS * hidden).astype(BF16))
    w2p = jnp.einsum('jd,hg->hjgd', w2, eye).reshape(N_KV_HEADS * hidden, N_KV_HEADS * HEAD_DIM).astype(BF16)
    pos = []
    for part in (pos_emb[:CMP_STRIDE], pos_emb[CMP_STRIDE:]):
        pos.append(jnp.broadcast_to(part[:, None, :], (CMP_STRIDE, N_KV_HEADS, HEAD_DIM)).reshape(1, -1))
    return [halves[0], halves[1], w2p, pos[0], pos[1]]


def _selection_map_t(seq):
    ncp = seq // CMP_STRIDE
    n_cmp = (seq - CMP_LEN) // CMP_STRIDE + 1
    c0 = np.arange(ncp) * CMP_STRIDE
    s0 = np.arange(seq // SEL_LEN) * SEL_LEN
    ov = np.minimum(c0[None, :] + CMP_LEN, s0[:, None] + SEL_LEN) - np.maximum(c0[None, :], s0[:, None])
    m = np.clip(ov, 0, None) / CMP_LEN
    m[:, n_cmp:] = 0.0
    return jnp.asarray(m, dtype=BF16)


def kernel(x, positions, attn_norm_g, ffn_norm_g, w_in, w_out, q_norm_g, k_norm_g, cmp_pos_k, cmp_w1_k, cmp_w2_k, cmp_pos_v, cmp_w1_v, cmp_w2_v, conv_w, conv_b, conv_ln_g, conv_ln_b, ffn_w_gate, ffn_w_up, ffn_w_down, moe_router, moe_w_gate, moe_w_up, moe_w_down):
    b, seq, d = x.shape
    t = b * seq
    depth = w_in.shape[0]
    cw = conv_w.shape[2]
    ncp = seq // CMP_STRIDE
    n_cmp = (seq - CMP_LEN) // CMP_STRIDE + 1
    assert seq % max(TQ, KC, TS_CONV) == 0 and seq >= WINDOW + TQ

    cos_t, sin_t = _rope_tables(positions)
    cmp_end = np.minimum(np.arange(ncp) * CMP_STRIDE + CMP_LEN - 1, seq - 1)
    cos_c, sin_c = _rope_tables(positions[:, cmp_end])
    selmap_t = _selection_map_t(seq)
    tile2 = lambda v: jnp.tile(v.reshape(1, HEAD_DIM), (1, HEADS_PER_VREG))

    x2 = x.reshape(t, d)
    for layer in range(depth):
        w_perm = _permute_w_in(w_in[layer], cw)
        q, kc, vc, ks, vs, kw, vw, gates, glu = _in_proj(
            x2, attn_norm_g[layer].reshape(1, d), w_perm, cos_t, sin_t,
            tile2(q_norm_g[layer]), tile2(k_norm_g[layer, 1]), tile2(k_norm_g[layer, 2]), seq)
        kcmp, vcmp = _compress(
            kc, vc,
            _compress_weights(cmp_pos_k[layer], cmp_w1_k[layer], cmp_w2_k[layer]),
            _compress_weights(cmp_pos_v[layer], cmp_w1_v[layer], cmp_w2_v[layer]),
            tile2(k_norm_g[layer, 0]), cos_c, sin_c, b, ncp)
        score_bound = (HEAD_DIM ** 0.5 * LOG2_E * SCORE_BOUND_MARGIN * jnp.max(jnp.abs(q_norm_g[layer]))
                       * jnp.max(jnp.abs(k_norm_g[layer]))).astype(F32).reshape(1)
        attn = _attention(score_bound, q, kcmp, vcmp, ks, vs, kw, vw, gates, selmap_t, b, seq)
        conv = _conv(glu, conv_w[layer], conv_b[layer].reshape(1, cw), conv_ln_g[layer].reshape(1, cw),
                     conv_ln_b[layer].reshape(1, cw), b, seq)
        wo = w_out[layer].astype(BF16)
        g2 = ffn_norm_g[layer].reshape(1, d)
        i = layer // 2
        if layer % 2 == 0:
            x2, h = _out_proj(attn, conv, wo, x2, g2)
            x2 = _ffn(h, x2, ffn_w_gate[i].astype(BF16), ffn_w_up[i].astype(BF16), ffn_w_down[i].astype(BF16))
        else:
            n_e = moe_router.shape[2]
            r = jnp.pad(moe_router[i], ((0, 0), (0, LANES - n_e)))
            r_hi = r.astype(BF16)
            r_lo = (r - r_hi.astype(F32)).astype(BF16)
            x2, h, route = _out_proj(attn, conv, wo, x2, g2, jnp.concatenate([r_hi, r_lo], axis=1), n_e)
            x2 = _moe_routed(h, x2, route, moe_w_gate[i].astype(BF16), moe_w_up[i].astype(BF16),
                             moe_w_down[i].astype(BF16))
    return x2.reshape(b, seq, d)
```

```python
import functools
import math

import jax
import jax.numpy as jnp
import numpy as np
from jax import lax
from jax.experimental import pallas as pl
from jax.experimental.pallas import tpu as pltpu

F32 = jnp.float32
BF16 = jnp.bfloat16

N_HEADS = 8
N_KV_HEADS = 2
Q_PER_KV = N_HEADS // N_KV_HEADS
HEAD_DIM = 64
N_BRANCH = 3
CMP_LEN = 32
CMP_STRIDE = 16
SEL_LEN = 64
SEL_TOPK = 16
N_LOCAL_SEL = 2
WINDOW = 512
CONV_KERNEL = 31
ROPE_THETA = 500000.0
ROPE_DIM = HEAD_DIM // 4
TOP_K = 2
EPS = 1e-6

LANES = 128
SUBLANES = 8
LOG2_E = math.log2(math.e)
NEG_BIG = -(2.0 ** 100)
MAX_SCORE_BOUND = 50.0
SCORE_BOUND_MARGIN = 1.02
HEADS_PER_VREG = LANES // HEAD_DIM
VMEM_LIMIT = 56 * 1024 * 1024

TM_PROJ = 512
TQ = 256
KC = 512
CH_CONV = 32
HALO = 32
TM_FFN = 1024
TF_FFN = 512
CT_MOE = 1024
BG_MOE = 256
BM_MOE = 512


def _cparams(sem):
    return pltpu.CompilerParams(dimension_semantics=sem, vmem_limit_bytes=VMEM_LIMIT)


def _dot(a, b):
    return jnp.dot(a, b, preferred_element_type=F32)


def _dot_nt(a, b):
    return lax.dot_general(a, b, (((1,), (1,)), ((), ())), preferred_element_type=F32)


def _split_bf16(x):
    hi = x.astype(BF16)
    lo = (x - hi.astype(F32)).astype(BF16)
    return hi, lo


def _rms_rows(x, g):
    ms = jnp.mean(x * x, axis=-1, keepdims=True)
    return x * lax.rsqrt(ms + EPS) * g


def _head_block_ones(width):
    r = lax.broadcasted_iota(jnp.int32, (width, width), 0) // HEAD_DIM
    c = lax.broadcasted_iota(jnp.int32, (width, width), 1) // HEAD_DIM
    return jnp.where(r == c, 1.0, 0.0).astype(BF16)


def _norm_rope(xg, ms, gain, cos, sin):
    y = xg * lax.rsqrt(ms + EPS) * gain
    lane = lax.broadcasted_iota(jnp.int32, y.shape, 1) % HEAD_DIM
    half = ROPE_DIM // 2
    partner = jnp.where(lane < half, pltpu.roll(y, LANES - half, 1), pltpu.roll(y, half, 1))
    return y * cos + partner * sin


def _head_norm_rope(xg, gain, cos, sin, ones_bd):
    ms = _dot((xg * xg).astype(BF16), ones_bd) * (1.0 / HEAD_DIM)
    return _norm_rope(xg, ms, gain, cos, sin)


def _head_norm_rope_pair(xa, xb, gain_a, gain_b, cos, sin, ones_bd2):
    sq = jnp.concatenate([xa * xa, xb * xb], axis=-1).astype(BF16)
    ms = _dot(sq, ones_bd2) * (1.0 / HEAD_DIM)
    return (_norm_rope(xa, ms[:, :LANES], gain_a, cos, sin),
            _norm_rope(xb, ms[:, LANES:], gain_b, cos, sin))


Q_W = N_HEADS * HEAD_DIM
KV_W = N_KV_HEADS * HEAD_DIM
SEG_Q = 0
SEG_KV = Q_W
SEG_GATE = SEG_KV + 6 * KV_W
SEG_UA = SEG_GATE + N_KV_HEADS * LANES
GATES_PER_KV = Q_PER_KV * N_BRANCH


def _causal_conv_tile(glu, seq_start, w_ref, b_ref, lg_ref, lb_ref, o_ref, ext_ref, shift_ref):
    ts = glu.shape[0]
    ext_ref[0:HALO, :] = jnp.where(seq_start, 0.0, ext_ref[ts:ts + HALO, :])
    ext_ref[HALO:HALO + ts, :] = glu
    n_shift = shift_ref.shape[1]
    for r in range(1, SUBLANES):
        shift_ref[r - 1] = ext_ref[r:r + n_shift, :]

    def rows_from(o):
        r = o % SUBLANES
        if r == 0:
            return ext_ref[o:o + CH_CONV, :]
        return shift_ref[r - 1, o - r:o - r + CH_CONV, :]

    w = w_ref[...]
    first_tap = HALO - (CONV_KERNEL - 1)
    for c in range(ts // CH_CONV):
        base = c * CH_CONV + first_tap
        acc = w[0:1, :] * rows_from(base)
        for k in range(1, CONV_KERNEL):
            acc = acc + w[k:k + 1, :] * rows_from(base + k)
        y = acc + b_ref[...]
        yc = y - jnp.mean(y, axis=-1, keepdims=True)
        yn = yc * lax.rsqrt(jnp.mean(yc * yc, axis=-1, keepdims=True) + EPS)
        z = yn * lg_ref[...] + lb_ref[...]
        o_ref[c * CH_CONV:(c + 1) * CH_CONV, :] = (z * jax.nn.sigmoid(z)).astype(o_ref.dtype)


def _in_proj_kernel(x_ref, g_ref, w_ref, cos_ref, sin_ref, qg_ref, ksg_ref, kwg_ref,
                    cw_ref, cb_ref, clg_ref, clb_ref,
                    q_ref, kc_ref, vc_ref, ks_ref, vs_ref, kw_ref, vw_ref, gate_ref, conv_ref,
                    glu_ref, ext_ref, shift_ref, *, conv_w, seq, n_tiles):
    i = pl.program_id(0)
    tm = x_ref.shape[0]

    @pl.when(i == 0)
    def _():
        glu_ref[...] = jnp.zeros_like(glu_ref)
        ext_ref[...] = jnp.zeros_like(ext_ref)

    _causal_conv_tile(glu_ref[...], lax.rem((i - 1) * tm, seq) == 0, cw_ref, cb_ref, clg_ref, clb_ref,
                      conv_ref, ext_ref, shift_ref)

    h = _rms_rows(x_ref[...], g_ref[...]).astype(BF16)
    cos = cos_ref[...]
    sin = sin_ref[...]
    ones_bd = _head_block_ones(2 * LANES)
    lane = lax.broadcasted_iota(jnp.int32, (tm, LANES), 1)
    tile_start = lax.rem(jnp.minimum(i, n_tiles - 1) * tm, seq)
    tok = tile_start + lax.broadcasted_iota(jnp.int32, (tm, 1), 0)
    block_aug = jnp.where(lane - HEAD_DIM == tok // SEL_LEN, NEG_BIG, 0.0)
    ones_aug = jnp.where(lane == HEAD_DIM, 1.0, 0.0)

    def put_heads(ref, first, val, aug):
        for j in range(HEADS_PER_VREG):
            head = val if j == 0 else pltpu.roll(val, LANES - j * HEAD_DIM, 1)
            ref[first + j] = jnp.where(lane < HEAD_DIM, head, aug).astype(ref.dtype)

    qkv = _dot(h, w_ref[:, SEG_Q:SEG_GATE])
    group = lambda j: qkv[:, j * LANES:(j + 1) * LANES]
    first_kv = Q_W // LANES
    scale = HEAD_DIM ** -0.5 * LOG2_E
    qg = qg_ref[...]
    for c in range(0, first_kv, 2):
        pair = _head_norm_rope_pair(group(c), group(c + 1), qg, qg, cos, sin, ones_bd)
        for j, y in enumerate(pair):
            put_heads(q_ref, (c + j) * HEADS_PER_VREG, y * scale, 0.0)
    kc_ref[...] = group(first_kv)
    vc_ref[...] = group(first_kv + 1)
    k_sel, k_win = _head_norm_rope_pair(group(first_kv + 2), group(first_kv + 4), ksg_ref[...], kwg_ref[...],
                                        cos, sin, ones_bd)
    put_heads(ks_ref, 0, k_sel, block_aug)
    put_heads(vs_ref, 0, group(first_kv + 3), ones_aug)
    put_heads(kw_ref, 0, k_win, 0.0)
    put_heads(vw_ref, 0, group(first_kv + 5), ones_aug)
    gate_ref[...] = jax.nn.sigmoid(_dot(h, w_ref[:, SEG_GATE:SEG_UA]))
    u = _dot(h, w_ref[:, SEG_UA:])
    glu_ref[...] = u[:, :conv_w] * jax.nn.sigmoid(u[:, conv_w:])


def _in_proj(x2, g, w_perm, cos_t, sin_t, qg, ksg, kwg, conv_params, seq):
    t, d = x2.shape
    conv_w = (w_perm.shape[1] - SEG_UA) // 2
    tm = min(TM_PROJ, seq)
    n_tiles = t // tm
    row = lambda i: (jnp.minimum(i, n_tiles - 1), 0)
    const = lambda i: (0, 0)
    head_row = lambda i: (0, jnp.minimum(i, n_tiles - 1), 0)
    out_shape = [
        jax.ShapeDtypeStruct((N_HEADS, t, LANES), BF16),
        jax.ShapeDtypeStruct((t, KV_W), F32),
        jax.ShapeDtypeStruct((t, KV_W), F32),
        jax.ShapeDtypeStruct((N_KV_HEADS, t, LANES), BF16),
        jax.ShapeDtypeStruct((N_KV_HEADS, t, LANES), BF16),
        jax.ShapeDtypeStruct((N_KV_HEADS, t, LANES), BF16),
        jax.ShapeDtypeStruct((N_KV_HEADS, t, LANES), BF16),
        jax.ShapeDtypeStruct((t, N_KV_HEADS * LANES), F32),
        jax.ShapeDtypeStruct((t, conv_w), BF16),
    ]
    kv_spec = pl.BlockSpec((N_KV_HEADS, tm, LANES), head_row)
    out_specs = [
        pl.BlockSpec((N_HEADS, tm, LANES), head_row),
        pl.BlockSpec((tm, KV_W), row), pl.BlockSpec((tm, KV_W), row),
        kv_spec, kv_spec, kv_spec, kv_spec,
        pl.BlockSpec((tm, N_KV_HEADS * LANES), row),
        pl.BlockSpec((tm, conv_w), lambda i: (jnp.maximum(i - 1, 0), 0)),
    ]
    in_specs = [
        pl.BlockSpec((tm, d), row), pl.BlockSpec((1, d), const),
        pl.BlockSpec(w_perm.shape, const),
        pl.BlockSpec((tm, LANES), row), pl.BlockSpec((tm, LANES), row),
        pl.BlockSpec((1, LANES), const), pl.BlockSpec((1, LANES), const), pl.BlockSpec((1, LANES), const),
    ] + [pl.BlockSpec(p.shape, const) for p in conv_params]
    return pl.pallas_call(
        functools.partial(_in_proj_kernel, conv_w=conv_w, seq=seq, n_tiles=n_tiles),
        grid=(n_tiles + 1,), in_specs=in_specs, out_specs=out_specs, out_shape=out_shape,
        scratch_shapes=[pltpu.VMEM((tm, conv_w), F32),
                        pltpu.VMEM((tm + HALO, conv_w), F32),
                        pltpu.VMEM((SUBLANES - 1, tm + HALO - SUBLANES, conv_w), F32)],
        compiler_params=_cparams(("arbitrary",)), name="in_proj",
    )(x2, g, w_perm, cos_t, sin_t, qg, ksg, kwg, *conv_params)


def _gelu_tanh(x):
    c = math.sqrt(2.0 / math.pi)
    return 0.5 * x * (1.0 + jnp.tanh(c * (x + 0.044715 * (x * x * x))))


def _compress_kernel(k_ref, v_ref, w1ak_ref, w1bk_ref, w2k_ref, pak_ref, pbk_ref,
                     w1av_ref, w1bv_ref, w2v_ref, pav_ref, pbv_ref,
                     kg_ref, cos_ref, sin_ref, ko_ref, vo_ref):
    def mlp(x_ref, w1a_ref, w1b_ref, w2_ref, pa_ref, pb_ref):
        n = x_ref.shape[0] // CMP_STRIDE
        first = second = None
        for l in range(CMP_STRIDE):
            x = x_ref[pl.ds(l, n, stride=CMP_STRIDE), :]
            cols = slice(l * KV_W, (l + 1) * KV_W)
            fa = _dot((x + pa_ref[:, cols]).astype(BF16), w1a_ref[cols, :])
            fb = _dot((x + pb_ref[:, cols]).astype(BF16), w1b_ref[cols, :])
            first = fa if first is None else first + fa
            second = fb if second is None else second + fb
        hid = first + pltpu.roll(second, n - 1, 0)
        return _dot(_gelu_tanh(hid).astype(BF16), w2_ref[...])

    kc = mlp(k_ref, w1ak_ref, w1bk_ref, w2k_ref, pak_ref, pbk_ref)
    kc = _head_norm_rope(kc, kg_ref[...], cos_ref[...], sin_ref[...], _head_block_ones(LANES))
    vc = mlp(v_ref, w1av_ref, w1bv_ref, w2v_ref, pav_ref, pbv_ref)
    lane = lax.broadcasted_iota(jnp.int32, kc.shape, 1)
    for j in range(N_KV_HEADS):
        for val, ref in ((kc, ko_ref), (vc, vo_ref)):
            head = val if j == 0 else pltpu.roll(val, LANES - j * HEAD_DIM, 1)
            ref[0, j] = jnp.where(lane < HEAD_DIM, head, 0.0).astype(ref.dtype)


def _compress(kc, vc, wk, wv, kg, cosc, sinc, b, ncp):
    seq = kc.shape[0] // b
    const = lambda i: (0, 0)
    row = lambda i: (i, 0)

    def wspecs(ws):
        return [pl.BlockSpec(w.shape, const) for w in ws]

    out = jax.ShapeDtypeStruct((b, N_KV_HEADS, ncp, LANES), BF16)
    ospec = pl.BlockSpec((1, N_KV_HEADS, ncp, LANES), lambda i: (i, 0, 0, 0))
    return pl.pallas_call(
        _compress_kernel, grid=(b,),
        in_specs=[pl.BlockSpec((seq, KV_W), row), pl.BlockSpec((seq, KV_W), row)]
        + wspecs(wk) + wspecs(wv)
        + [pl.BlockSpec((1, LANES), const), pl.BlockSpec((ncp, LANES), row), pl.BlockSpec((ncp, LANES), row)],
        out_specs=[ospec, ospec], out_shape=[out, out],
        compiler_params=_cparams(("parallel",)), name="compress",
    )(kc, vc, *wk, *wv, kg, cosc, sinc)


def _attn_kernel(off_ref, q_ref, kc_ref, vc_ref, ks_ref, vs_ref, kw_ref, vw_ref, gate_ref, selmap_ref,
                 o_ref, *, seq, tq, kc_len, top_n, bounded):
    i = pl.program_id(2)
    t0 = i * tq
    rows = Q_PER_KV * tq
    n_sel = seq // SEL_LEN
    q2 = q_ref[...].reshape(rows, LANES)
    t_row = t0 + (lax.broadcasted_iota(jnp.int32, (rows, 1), 0) & (tq - 1))
    t_tok = t0 + lax.broadcasted_iota(jnp.int32, (tq, 1), 0)
    neg_offset = -off_ref[0] if bounded else 0.0

    def add_bias(s, bias):
        return (s.reshape(s.shape[0] // tq, tq, s.shape[1]) + bias[None]).reshape(s.shape)

    kcmp = kc_ref[0, 0]
    ncp = kcmp.shape[0]
    s_c = _dot_nt(q2, kcmp)
    cmp_end = lax.broadcasted_iota(jnp.int32, (1, ncp), 1) * CMP_STRIDE + (CMP_LEN - 1)
    if bounded:
        e_c = jnp.exp2(add_bias(s_c, jnp.where(cmp_end <= t_tok, neg_offset, NEG_BIG)))
    else:
        s_c = jnp.where(cmp_end <= t_row, s_c, -jnp.inf)
        m_c = jnp.max(s_c, axis=-1, keepdims=True)
        e_c = jnp.exp2(s_c - jnp.where(m_c == -jnp.inf, 0.0, m_c))
    p_c = e_c * (1.0 / jnp.maximum(jnp.sum(e_c, axis=-1, keepdims=True), jnp.finfo(F32).tiny))
    o_c = _dot(p_c.astype(BF16), vc_ref[0, 0])

    def denominator(acc):
        return acc[:, HEAD_DIM:HEAD_DIM + 1]

    span = min(WINDOW + tq, seq)
    w0 = pl.multiple_of(jnp.maximum(t0 - WINDOW, 0), tq)
    key_w = w0 + lax.broadcasted_iota(jnp.int32, (1, span), 1)
    bias_w = jnp.where((key_w <= t_tok) & (key_w > t_tok - WINDOW), neg_offset, NEG_BIG)
    s_w = add_bias(_dot_nt(q2, kw_ref[0, pl.ds(w0, span), :]), bias_w)
    if bounded:
        p_w = jnp.exp2(s_w).astype(BF16)
    else:
        p_w = jnp.exp2((s_w - jnp.max(s_w, axis=-1, keepdims=True)).astype(BF16))
    acc_w = _dot(p_w, vw_ref[0, pl.ds(w0, span), :])
    o_w = acc_w * (1.0 / denominator(acc_w))

    p_hi, p_lo = _split_bf16(jnp.sum(p_c.reshape(Q_PER_KV, tq, ncp), axis=0))
    selmap = selmap_ref[...]
    imp = _dot_nt(selmap, p_hi) + _dot_nt(selmap, p_lo)
    blk = lax.broadcasted_iota(jnp.int32, (n_sel, tq), 0)
    cur = (t0 + lax.broadcasted_iota(jnp.int32, (n_sel, tq), 1)) // SEL_LEN
    causal_blk = blk <= cur
    forced = (blk == 0) | (causal_blk & (blk > cur - N_LOCAL_SEL))
    score = jnp.where(forced, jnp.inf, jnp.where(causal_blk, imp, -jnp.inf))
    sub = lax.broadcasted_iota(jnp.int32, (SUBLANES, tq), 0)
    groups = [score[g * SUBLANES:(g + 1) * SUBLANES, :] for g in range(n_sel // SUBLANES)]
    ranks = [jnp.zeros((SUBLANES, tq), F32) for _ in groups]
    for jp in range(n_sel):
        other = jnp.broadcast_to(score[jp:jp + 1, :], (SUBLANES, tq))
        for g, sg in enumerate(groups):
            first = g * SUBLANES
            if first > jp:
                inc = jnp.where(other >= sg, 1.0, 0.0)
            elif first + SUBLANES - 1 <= jp:
                inc = jnp.where(other > sg, 1.0, 0.0)
            else:
                inc = jnp.where(other > sg, 1.0, jnp.where((other == sg) & (sub > jp - first), 1.0, 0.0))
            ranks[g] = ranks[g] + inc
    chosen_flag = neg_offset * (1.0 / NEG_BIG)
    block_flags = jnp.where(jnp.concatenate(ranks, axis=0) < top_n, chosen_flag, 1.0)
    flag_rows = [jnp.zeros((HEAD_DIM, tq), F32), block_flags]
    if HEAD_DIM + n_sel < LANES:
        flag_rows.append(jnp.zeros((LANES - HEAD_DIM - n_sel, tq), F32))
    flags = jnp.transpose(jnp.concatenate(flag_rows, axis=0))
    q_sel = (q2.reshape(Q_PER_KV, tq, LANES) + flags.astype(BF16)[None]).reshape(rows, LANES)

    def sel_chunk(k0, carry, bias):
        s = _dot_nt(q_sel, ks_ref[0, pl.ds(k0, kc_len), :])
        if bias is not None:
            s = add_bias(s, bias)
        v = vs_ref[0, pl.ds(k0, kc_len), :]
        if bounded:
            (acc,) = carry
            return (acc + _dot(jnp.exp2(s).astype(BF16), v),)
        m, acc = carry
        m_new = jnp.maximum(m, jnp.max(s, axis=-1, keepdims=True))
        p = jnp.exp2((s - m_new).astype(BF16))
        return m_new, jnp.exp2(m - m_new) * acc + _dot(p, v)

    n_full = t0 // kc_len
    init = (jnp.zeros((rows, LANES), F32),)
    if not bounded:
        init = (jnp.full((rows, 1), -jnp.inf, F32),) + init
    carry = lax.fori_loop(
        0, n_full, lambda c, cr: sel_chunk(pl.multiple_of(c * kc_len, kc_len), cr, None), init)
    kd = pl.multiple_of(n_full * kc_len, kc_len)
    key_d = kd + lax.broadcasted_iota(jnp.int32, (1, kc_len), 1)
    acc_s = sel_chunk(kd, carry, jnp.where(key_d <= t_tok, 0.0, NEG_BIG))[-1]
    o_s = acc_s * (1.0 / denominator(acc_s))

    gates = gate_ref[...]

    def gate_col(br):
        cols = [gates[:, g * N_BRANCH + br:g * N_BRANCH + br + 1] for g in range(Q_PER_KV)]
        return jnp.concatenate(cols, axis=0)

    o = gate_col(0) * o_c + gate_col(1) * o_s + gate_col(2) * o_w
    o3 = o.reshape(Q_PER_KV, tq, LANES)
    lane = lax.broadcasted_iota(jnp.int32, (tq, LANES), 1)
    pairs = [jnp.where(lane < HEAD_DIM, o3[g], pltpu.roll(o3[g + 1], HEAD_DIM, 1))
             for g in range(0, Q_PER_KV, HEADS_PER_VREG)]
    o_ref[...] = jnp.concatenate(pairs, axis=-1).astype(o_ref.dtype)


def _attention(score_bound, q, kcmp, vcmp, ks, vs, kw, vw, gates, selmap_t, b, seq):
    t = b * seq
    tq = min(TQ, seq)
    kc_len = min(KC, seq)
    nq = seq // tq
    ncp = kcmp.shape[2]
    n_sel = seq // SEL_LEN
    assert HEAD_DIM + n_sel <= LANES, "selection-block flags must fit beside the head dims"
    top_n = min(SEL_TOPK, n_sel)
    cmp_spec = pl.BlockSpec((1, 1, ncp, LANES), lambda bi, kh, i: (bi, kh, 0, 0))
    seq_spec = pl.BlockSpec((1, seq, LANES), lambda bi, kh, i: (kh, bi, 0))

    def run(bounded):
        return pl.pallas_call(
            functools.partial(_attn_kernel, seq=seq, tq=tq, kc_len=kc_len, top_n=top_n, bounded=bounded),
            grid=(b, N_KV_HEADS, nq),
            in_specs=[
                pl.BlockSpec(memory_space=pltpu.SMEM),
                pl.BlockSpec((Q_PER_KV, tq, LANES), lambda bi, kh, i: (kh, bi * nq + i, 0)),
                cmp_spec, cmp_spec, seq_spec, seq_spec, seq_spec, seq_spec,
                pl.BlockSpec((tq, LANES), lambda bi, kh, i: (bi * nq + i, kh)),
                pl.BlockSpec(selmap_t.shape, lambda bi, kh, i: (0, 0)),
            ],
            out_specs=pl.BlockSpec((tq, Q_PER_KV * HEAD_DIM), lambda bi, kh, i: (bi * nq + i, kh)),
            out_shape=jax.ShapeDtypeStruct((t, N_HEADS * HEAD_DIM), BF16),
            compiler_params=_cparams(("parallel", "parallel", "arbitrary")),
            name="nsa_attention" if bounded else "nsa_attention_running_max",
        )(score_bound, q, kcmp, vcmp, ks, vs, kw, vw, gates, selmap_t)

    return lax.cond(score_bound[0] < MAX_SCORE_BOUND, lambda: run(True), lambda: run(False))


def _top2_gates(logits, n_experts):
    lane = lax.broadcasted_iota(jnp.int32, logits.shape, 1)
    x = jnp.where(lane < n_experts, logits, -jnp.inf)
    m1 = jnp.max(x, axis=-1, keepdims=True)
    i1 = jnp.min(jnp.where(x == m1, lane, LANES), axis=-1, keepdims=True)
    x2 = jnp.where(lane == i1, -jnp.inf, x)
    m2 = jnp.max(x2, axis=-1, keepdims=True)
    i2 = jnp.min(jnp.where(x2 == m2, lane, LANES), axis=-1, keepdims=True)
    e2 = jnp.exp(m2 - m1)
    inv = 1.0 / (1.0 + e2)
    return jnp.where(lane == i1, inv, jnp.where(lane == i2, e2 * inv, 0.0))


def _out_proj_kernel(*refs, n_experts):
    if n_experts:
        attn_ref, conv_ref, wo_ref, x_ref, g_ref, rt_ref, xo_ref, h_ref, gate_ref = refs
    else:
        attn_ref, conv_ref, wo_ref, x_ref, g_ref, xo_ref, h_ref = refs
    aw = attn_ref.shape[1]
    x = x_ref[...] + _dot(attn_ref[...], wo_ref[0:aw, :]) + _dot(conv_ref[...], wo_ref[aw:, :])
    xo_ref[...] = x
    h = _rms_rows(x, g_ref[...])
    h_ref[...] = h.astype(h_ref.dtype)
    if n_experts:
        h_hi, h_lo = _split_bf16(h)
        router = rt_ref[...]
        by_hi = _dot(h_hi, router)
        logits = by_hi[:, :LANES] + (by_hi[:, LANES:] + _dot(h_lo, router[:, :LANES]))
        gate_ref[...] = _top2_gates(logits, n_experts)


def _out_proj(attn, conv, wo, x2, g, router_split=None, n_experts=0):
    t, d = x2.shape
    tm = min(TM_PROJ, t)
    row = lambda i: (i, 0)
    const = lambda i: (0, 0)
    in_specs = [pl.BlockSpec((tm, attn.shape[1]), row), pl.BlockSpec((tm, conv.shape[1]), row),
                pl.BlockSpec(wo.shape, const), pl.BlockSpec((tm, d), row), pl.BlockSpec((1, d), const)]
    out_shape = [jax.ShapeDtypeStruct((t, d), F32), jax.ShapeDtypeStruct((t, d), BF16)]
    out_specs = [pl.BlockSpec((tm, d), row), pl.BlockSpec((tm, d), row)]
    args = [attn, conv, wo, x2, g]
    if n_experts:
        in_specs.append(pl.BlockSpec(router_split.shape, const))
        out_shape.append(jax.ShapeDtypeStruct((t, LANES), F32))
        out_specs.append(pl.BlockSpec((tm, LANES), row))
        args.append(router_split)
    return pl.pallas_call(
        functools.partial(_out_proj_kernel, n_experts=n_experts),
        grid=(t // tm,), in_specs=in_specs, out_specs=out_specs, out_shape=out_shape,
        compiler_params=_cparams(("parallel",)), name="out_proj",
    )(*args)


def _swiglu_tiles(h, acc, wg, wu, wd, dff, tf):
    for f in range(dff // tf):
        cols = slice(f * tf, (f + 1) * tf)
        a = _dot(h, wg(cols))
        u = _dot(h, wu(cols))
        acc = acc + _dot(((a * jax.nn.sigmoid(a)) * u).astype(BF16), wd(cols))
    return acc


def _ffn_kernel(h_ref, x_ref, wg_ref, wu_ref, wd_ref, o_ref, *, tf):
    o_ref[...] = _swiglu_tiles(h_ref[...], x_ref[...], lambda c: wg_ref[:, c], lambda c: wu_ref[:, c],
                               lambda c: wd_ref[c, :], wg_ref.shape[1], tf)


def _ffn(h, x2, wg, wu, wd):
    t, d = x2.shape
    tm = min(TM_FFN, t)
    row = lambda i: (i, 0)
    resident = lambda w: pl.BlockSpec(w.shape, lambda i: (0, 0), pipeline_mode=pl.Buffered(1))
    return pl.pallas_call(
        functools.partial(_ffn_kernel, tf=TF_FFN), grid=(t // tm,),
        in_specs=[pl.BlockSpec((tm, d), row), pl.BlockSpec((tm, d), row),
                  resident(wg), resident(wu), resident(wd)],
        out_specs=pl.BlockSpec((tm, d), row),
        out_shape=jax.ShapeDtypeStruct((t, d), F32),
        compiler_params=_cparams(("parallel",)), name="ffn",
    )(h, x2, wg, wu, wd)


def _route_scan_kernel(g_ref, pos_t_ref, pos_ref, cnt_ref, tot_ref, carry_ref, *, n_experts):
    c = pl.program_id(0)

    @pl.when(c == 0)
    def _():
        carry_ref[...] = jnp.zeros_like(carry_ref)

    ct = g_ref.shape[0]
    routed = g_ref[...] > 0.0
    a = jnp.where(routed, 1.0, 0.0)
    earlier = lax.broadcasted_iota(jnp.int32, (ct, ct), 1) < lax.broadcasted_iota(jnp.int32, (ct, ct), 0)
    base = carry_ref[...]
    pos = jnp.where(routed, _dot(jnp.where(earlier, 1.0, 0.0).astype(BF16), a.astype(BF16)) + base, -1.0)
    pos_ref[...] = pos
    pos_t_ref[0] = jnp.transpose(pos)[0:n_experts, :]
    cnt_ref[0] = base
    total = base + jnp.sum(a, axis=0, keepdims=True)
    carry_ref[...] = total
    tot_ref[...] = total


def _route_scan(gates, n_experts):
    t = gates.shape[0]
    ct = CT_MOE
    nch = t // ct
    return pl.pallas_call(
        functools.partial(_route_scan_kernel, n_experts=n_experts), grid=(nch,),
        in_specs=[pl.BlockSpec((ct, LANES), lambda c: (c, 0))],
        out_specs=[pl.BlockSpec((1, n_experts, ct), lambda c: (c, 0, 0)),
                   pl.BlockSpec((ct, LANES), lambda c: (c, 0)),
                   pl.BlockSpec((1, 1, LANES), lambda c: (c, 0, 0)),
                   pl.BlockSpec((1, LANES), lambda c: (0, 0))],
        out_shape=[jax.ShapeDtypeStruct((nch, n_experts, ct), F32),
                   jax.ShapeDtypeStruct((t, LANES), F32),
                   jax.ShapeDtypeStruct((nch, 1, LANES), F32),
                   jax.ShapeDtypeStruct((1, LANES), F32)],
        scratch_shapes=[pltpu.VMEM((1, LANES), F32)],
        compiler_params=_cparams(("arbitrary",)), name="route_scan",
    )(gates)


def _moe_gather_kernel(blk_ref, chk_ref, exp_ref, first_ref, valid_ref,
                       h_ref, pos_t_ref, pstart_ref, xs_ref):
    w = pl.program_id(0)

    @pl.when(first_ref[w] == 1)
    def _():
        xs_ref[...] = jnp.zeros_like(xs_ref)

    @pl.when(valid_ref[w] == 1)
    def _():
        pos = pos_t_ref[0]
        slot = jnp.where(pos >= 0.0, pos + pstart_ref[...], -1.0)
        sub = lax.broadcasted_iota(jnp.int32, pos.shape, 0)
        slot_e = jnp.sum(jnp.where(sub == exp_ref[w], slot, 0.0), axis=0, keepdims=True)
        bg = xs_ref.shape[0]
        target = (blk_ref[w] * bg + lax.broadcasted_iota(jnp.int32, (bg, 1), 0)).astype(F32)
        onehot = jnp.where(slot_e == target, 1.0, 0.0).astype(BF16)
        xs_ref[...] += _dot(onehot, h_ref[...]).astype(xs_ref.dtype)


def _moe_gather(h, pos_t, pstart_col, items, n_slots):
    t, d = h.shape
    n_items = items[0].shape[0]
    n_e = pos_t.shape[1]
    grid_spec = pltpu.PrefetchScalarGridSpec(
        num_scalar_prefetch=5, grid=(n_items,),
        in_specs=[pl.BlockSpec((CT_MOE, d), lambda w, blk, chk, *_: (chk[w], 0)),
                  pl.BlockSpec((1, n_e, CT_MOE), lambda w, blk, chk, *_: (chk[w], 0, 0)),
                  pl.BlockSpec((n_e, 1), lambda w, *_: (0, 0))],
        out_specs=pl.BlockSpec((BG_MOE, d), lambda w, blk, *_: (blk[w], 0)))
    return pl.pallas_call(
        _moe_gather_kernel, grid_spec=grid_spec,
        out_shape=jax.ShapeDtypeStruct((n_slots, d), BF16),
        compiler_params=_cparams(("arbitrary",)), name="moe_gather",
    )(*items, h, pos_t, pstart_col)


def _moe_ffn_kernel(exp_ref, valid_ref, xs_ref, wg_ref, wu_ref, wd_ref, ys_ref, *, tf):
    j = pl.program_id(0)

    @pl.when(valid_ref[j] == 1)
    def _():
        zero = jnp.zeros(ys_ref.shape, F32)
        ys = _swiglu_tiles(xs_ref[...], zero, lambda c: wg_ref[0, :, c], lambda c: wu_ref[0, :, c],
                           lambda c: wd_ref[0, c, :], wg_ref.shape[2], tf)
        ys_ref[...] = ys.astype(ys_ref.dtype)

    @pl.when(valid_ref[j] == 0)
    def _():
        ys_ref[...] = jnp.zeros_like(ys_ref)


def _moe_ffn(xs, blk_expert, blk_valid, wg, wu, wd):
    n_slots, d = xs.shape
    expert = lambda w: pl.BlockSpec((1,) + w.shape[1:], lambda j, e, v: (e[j], 0, 0),
                                    pipeline_mode=pl.Buffered(1))
    grid_spec = pltpu.PrefetchScalarGridSpec(
        num_scalar_prefetch=2, grid=(n_slots // BM_MOE,),
        in_specs=[pl.BlockSpec((BM_MOE, d), lambda j, e, v: (j, 0)), expert(wg), expert(wu), expert(wd)],
        out_specs=pl.BlockSpec((BM_MOE, d), lambda j, e, v: (j, 0)))
    return pl.pallas_call(
        functools.partial(_moe_ffn_kernel, tf=TF_FFN), grid_spec=grid_spec,
        out_shape=jax.ShapeDtypeStruct((n_slots, d), BF16),
        compiler_params=_cparams(("arbitrary",)), name="moe_ffn",
    )(blk_expert, blk_valid, xs, wg, wu, wd)


def _moe_combine_kernel(tile_ref, blk_ref, exp_ref, first_ref, valid_ref,
                        x_ref, pos_ref, g_ref, pstart_ref, ys_ref, o_ref):
    w = pl.program_id(0)

    @pl.when(first_ref[w] == 1)
    def _():
        o_ref[...] = x_ref[...]

    @pl.when(valid_ref[w] == 1)
    def _():
        pos = pos_ref[...]
        slot = jnp.where(pos >= 0.0, pos + pstart_ref[...], -1.0)
        mine = lax.broadcasted_iota(jnp.int32, pos.shape, 1) == exp_ref[w]
        slot_e = jnp.sum(jnp.where(mine, slot, 0.0), axis=-1, keepdims=True)
        gate_e = jnp.sum(jnp.where(mine, g_ref[...], 0.0), axis=-1, keepdims=True)
        bs = ys_ref.shape[0]
        target = (blk_ref[w] * bs + lax.broadcasted_iota(jnp.int32, (1, bs), 1)).astype(F32)
        onehot = jnp.where(slot_e == target, 1.0, 0.0).astype(BF16)
        o_ref[...] += gate_e * _dot(onehot, ys_ref[...])


def _moe_combine(x2, pos, gates, pstart_row, ys, items):
    t, d = x2.shape
    n_items = items[0].shape[0]
    tok = lambda w, tile, *_: (tile[w], 0)
    grid_spec = pltpu.PrefetchScalarGridSpec(
        num_scalar_prefetch=5, grid=(n_items,),
        in_specs=[pl.BlockSpec((CT_MOE, d), tok), pl.BlockSpec((CT_MOE, LANES), tok),
                  pl.BlockSpec((CT_MOE, LANES), tok), pl.BlockSpec((1, LANES), lambda w, *_: (0, 0)),
                  pl.BlockSpec((BG_MOE, d), lambda w, tile, blk, *_: (blk[w], 0))],
        out_specs=pl.BlockSpec((CT_MOE, d), tok))
    return pl.pallas_call(
        _moe_combine_kernel, grid_spec=grid_spec,
        out_shape=jax.ShapeDtypeStruct((t, d), F32),
        compiler_params=_cparams(("arbitrary",)), name="moe_combine",
    )(*items, x2, pos, gates, pstart_row, ys)


def _count_le(ascending, x):
    return jnp.sum(ascending[None, :] <= x[:, None], axis=1).astype(jnp.int32)


def _work_items(group_id, lo, hi, n_items_max):
    n = jnp.maximum(hi - lo + 1, 0)
    ends = jnp.cumsum(n)
    total = ends[-1]
    w = jnp.arange(n_items_max, dtype=jnp.int32)
    wc = jnp.minimum(w, total - 1)
    g = _count_le(ends, wc)
    k = wc - (ends[g] - n[g])
    valid = (w < total).astype(jnp.int32)
    return group_id[g], lo[g] + k, g, ((k == 0) & (w < total)).astype(jnp.int32), valid


def _moe_routed(h, x2, gates, wg, wu, wd):
    t, d = x2.shape
    n_e = wg.shape[0]
    nch = t // CT_MOE
    n_slots = t * TOP_K + n_e * BM_MOE
    nbg = n_slots // BG_MOE
    pos_t, pos, cnt, tot = _route_scan(gates, n_e)

    counts = tot[0, :n_e].astype(jnp.int32)
    padded = (counts + BM_MOE - 1) // BM_MOE * BM_MOE
    pend = jnp.cumsum(padded)
    pstart = pend - padded
    cum = cnt[:, 0, :n_e].astype(jnp.int32)
    cum_end = jnp.concatenate([cum[1:], counts[None]], axis=0)

    sb = jnp.arange(nbg, dtype=jnp.int32)
    sb_e = jnp.minimum(_count_le(pend, sb * BG_MOE), n_e - 1)
    p_lo = sb * BG_MOE - pstart[sb_e]
    has_rows = (sb * BG_MOE < pend[-1]) & (p_lo < counts[sb_e])
    c_lo = jnp.sum(cum_end.T[sb_e] <= p_lo[:, None], axis=1).astype(jnp.int32)
    c_hi = jnp.sum(cum.T[sb_e] <= (p_lo + BG_MOE - 1)[:, None], axis=1).astype(jnp.int32) - 1
    c_lo = jnp.where(has_rows, c_lo, 0)
    c_hi = jnp.where(has_rows, c_hi, 0)
    g_blk, g_chk, g_grp, g_first, g_valid = _work_items(sb, c_lo, c_hi, nbg + n_e * nch)
    xs = _moe_gather(h, pos_t, pstart.astype(F32).reshape(n_e, 1),
                     (g_blk, g_chk, sb_e[g_grp], g_first, g_valid), n_slots)

    mb = jnp.arange(n_slots // BM_MOE, dtype=jnp.int32) * BM_MOE
    mb_e = jnp.minimum(_count_le(pend, mb), n_e - 1)
    mb_valid = ((mb < pend[-1]) & (mb - pstart[mb_e] < counts[mb_e])).astype(jnp.int32)
    ys = _moe_ffn(xs, mb_e, mb_valid, wg, wu, wd)

    tile = jnp.repeat(jnp.arange(nch, dtype=jnp.int32), n_e)
    te = jnp.tile(jnp.arange(n_e, dtype=jnp.int32), nch)
    s_lo = (pstart[None, :] + cum).reshape(-1)
    s_hi = (pstart[None, :] + cum_end).reshape(-1) - 1
    b_lo = s_lo // BG_MOE
    b_hi = jnp.where(s_hi >= s_lo, s_hi // BG_MOE, b_lo - 1)
    grp = jnp.arange(nch * n_e, dtype=jnp.int32)
    c_grp, c_blk, _, _, c_valid = _work_items(grp, b_lo, b_hi, nbg + n_e * nch)
    c_tile = tile[c_grp]
    c_first = jnp.concatenate([jnp.ones((1,), jnp.int32), (c_tile[1:] != c_tile[:-1]).astype(jnp.int32)])
    pstart_row = jnp.zeros((1, LANES), F32).at[0, :n_e].set(pstart.astype(F32))
    return _moe_combine(x2, pos, gates, pstart_row, ys, (c_tile, c_blk, te[c_grp], c_first, c_valid))


def _rope_tables(pos):
    half = ROPE_DIM // 2
    inv_freq = ROPE_THETA ** (-2.0 * jnp.arange(half, dtype=F32) / ROPE_DIM)
    ang = pos.astype(F32).reshape(-1, 1) * inv_freq
    cos, sin = jnp.cos(ang), jnp.sin(ang)
    n = ang.shape[0]
    rest = HEAD_DIM - ROPE_DIM
    cos_h = jnp.concatenate([cos, cos, jnp.ones((n, rest), F32)], axis=-1)
    sin_h = jnp.concatenate([-sin, sin, jnp.zeros((n, rest), F32)], axis=-1)
    return jnp.tile(cos_h, (1, HEADS_PER_VREG)), jnp.tile(sin_h, (1, HEADS_PER_VREG))


def _permute_w_in(w, conv_w):
    d = w.shape[0]
    kv_end = Q_W + 6 * KV_W
    g = w[:, kv_end:kv_end + N_HEADS * N_BRANCH]
    pad = jnp.zeros((d, LANES - GATES_PER_KV), w.dtype)
    gate_cols = []
    for kh in range(N_KV_HEADS):
        gate_cols += [g[:, kh * GATES_PER_KV:(kh + 1) * GATES_PER_KV], pad]
    u = w[:, kv_end + N_HEADS * N_BRANCH:]
    return jnp.concatenate([w[:, :kv_end]] + gate_cols + [u], axis=1).astype(BF16)


def _compress_weights(pos_emb, w1, w2):
    hidden = w1.shape[1]
    eye = jnp.eye(N_KV_HEADS, dtype=w1.dtype)
    w1r = w1.reshape(CMP_LEN, HEAD_DIM, hidden)
    halves = []
    for part in (w1r[:CMP_STRIDE], w1r[CMP_STRIDE:]):
        full = jnp.einsum('ldj,hg->lhdgj', part, eye)
        halves.append(full.reshape(CMP_STRIDE * N_KV_HEADS * HEAD_DIM, N_KV_HEADS * hidden).astype(BF16))
    w2p = jnp.einsum('jd,hg->hjgd', w2, eye).reshape(N_KV_HEADS * hidden, N_KV_HEADS * HEAD_DIM).astype(BF16)
    pos = []
    for part in (pos_emb[:CMP_STRIDE], pos_emb[CMP_STRIDE:]):
        pos.append(jnp.broadcast_to(part[:, None, :], (CMP_STRIDE, N_KV_HEADS, HEAD_DIM)).reshape(1, -1))
    return [halves[0], halves[1], w2p, pos[0], pos[1]]


def _selection_map_t(seq):
    ncp = seq // CMP_STRIDE
    n_cmp = (seq - CMP_LEN) // CMP_STRIDE + 1
    c0 = np.arange(ncp) * CMP_STRIDE
    s0 = np.arange(seq // SEL_LEN) * SEL_LEN
    ov = np.minimum(c0[None, :] + CMP_LEN, s0[:, None] + SEL_LEN) - np.maximum(c0[None, :], s0[:, None])
    m = np.clip(ov, 0, None) / CMP_LEN
    m[:, n_cmp:] = 0.0
    return jnp.asarray(m, dtype=BF16)


def kernel(x, positions, attn_norm_g, ffn_norm_g, w_in, w_out, q_norm_g, k_norm_g, cmp_pos_k, cmp_w1_k, cmp_w2_k, cmp_pos_v, cmp_w1_v, cmp_w2_v, conv_w, conv_b, conv_ln_g, conv_ln_b, ffn_w_gate, ffn_w_up, ffn_w_down, moe_router, moe_w_gate, moe_w_up, moe_w_down):
    b, seq, d = x.shape
    t = b * seq
    depth = w_in.shape[0]
    cw = conv_w.shape[2]
    ncp = seq // CMP_STRIDE
    n_cmp = (seq - CMP_LEN) // CMP_STRIDE + 1
    assert seq % max(TQ, KC, TM_PROJ) == 0 and seq >= WINDOW + TQ

    cos_t, sin_t = _rope_tables(positions)
    cmp_end = np.minimum(np.arange(ncp) * CMP_STRIDE + CMP_LEN - 1, seq - 1)
    cos_c, sin_c = _rope_tables(positions[:, cmp_end])
    selmap_t = _selection_map_t(seq)
    tile2 = lambda v: jnp.tile(v.reshape(1, HEAD_DIM), (1, HEADS_PER_VREG))

    x2 = x.reshape(t, d)
    for layer in range(depth):
        w_perm = _permute_w_in(w_in[layer], cw)
        conv_params = (conv_w[layer], conv_b[layer].reshape(1, cw), conv_ln_g[layer].reshape(1, cw),
                       conv_ln_b[layer].reshape(1, cw))
        q, kc, vc, ks, vs, kw, vw, gates, conv = _in_proj(
            x2, attn_norm_g[layer].reshape(1, d), w_perm, cos_t, sin_t,
            tile2(q_norm_g[layer]), tile2(k_norm_g[layer, 1]), tile2(k_norm_g[layer, 2]), conv_params, seq)
        kcmp, vcmp = _compress(
            kc, vc,
            _compress_weights(cmp_pos_k[layer], cmp_w1_k[layer], cmp_w2_k[layer]),
            _compress_weights(cmp_pos_v[layer], cmp_w1_v[layer], cmp_w2_v[layer]),
            tile2(k_norm_g[layer, 0]), cos_c, sin_c, b, ncp)
        score_bound = (HEAD_DIM ** 0.5 * LOG2_E * SCORE_BOUND_MARGIN * jnp.max(jnp.abs(q_norm_g[layer]))
                       * jnp.max(jnp.abs(k_norm_g[layer]))).astype(F32).reshape(1)
        attn = _attention(score_bound, q, kcmp, vcmp, ks, vs, kw, vw, gates, selmap_t, b, seq)
        wo = w_out[layer].astype(BF16)
        g2 = ffn_norm_g[layer].reshape(1, d)
        i = layer // 2
        if layer % 2 == 0:
            x2, h = _out_proj(attn, conv, wo, x2, g2)
            x2 = _ffn(h, x2, ffn_w_gate[i].astype(BF16), ffn_w_up[i].astype(BF16), ffn_w_down[i].astype(BF16))
        else:
            n_e = moe_router.shape[2]
            r = jnp.pad(moe_router[i], ((0, 0), (0, LANES - n_e)))
            r_hi = r.astype(BF16)
            r_lo = (r - r_hi.astype(F32)).astype(BF16)
            x2, h, route = _out_proj(attn, conv, wo, x2, g2, jnp.concatenate([r_hi, r_lo], axis=1), n_e)
            x2 = _moe_routed(h, x2, route, moe_w_gate[i].astype(BF16), moe_w_up[i].astype(BF16),
                             moe_w_down[i].astype(BF16))
    return x2.reshape(b, seq, d)
```

```python
import functools
import math

import jax
import jax.numpy as jnp
import numpy as np
from jax import lax
from jax.experimental import pallas as pl
from jax.experimental.pallas import tpu as pltpu
from jax.experimental.pallas import tpu_sc as plsc

F32 = jnp.float32
BF16 = jnp.bfloat16

N_HEADS = 8
N_KV_HEADS = 2
Q_PER_KV = N_HEADS // N_KV_HEADS
HEAD_DIM = 64
N_BRANCH = 3
CMP_LEN = 32
CMP_STRIDE = 16
SEL_LEN = 64
SEL_TOPK = 16
N_LOCAL_SEL = 2
WINDOW = 512
CONV_KERNEL = 31
ROPE_THETA = 500000.0
ROPE_DIM = HEAD_DIM // 4
TOP_K = 2
EPS = 1e-6

LANES = 128
SUBLANES = 8
LOG2_E = math.log2(math.e)
NEG_BIG = -(2.0 ** 100)
MAX_SCORE_BOUND = 50.0
SCORE_BOUND_MARGIN = 1.02
HEADS_PER_VREG = LANES // HEAD_DIM
VMEM_LIMIT = 56 * 1024 * 1024

TM_PROJ = 512
TQ = 256
KC = 512
CH_CONV = 32
HALO = 32
TM_FFN = 1024
TF_FFN = 512
CT_MOE = 1024
BG_MOE = 256
BM_MOE = 512
SC_WINDOW = 64


def _cparams(sem):
    return pltpu.CompilerParams(dimension_semantics=sem, vmem_limit_bytes=VMEM_LIMIT)


def _dot(a, b):
    return jnp.dot(a, b, preferred_element_type=F32)


def _dot_nt(a, b):
    return lax.dot_general(a, b, (((1,), (1,)), ((), ())), preferred_element_type=F32)


def _split_bf16(x):
    hi = x.astype(BF16)
    lo = (x - hi.astype(F32)).astype(BF16)
    return hi, lo


def _rms_rows(x, g):
    ms = jnp.mean(x * x, axis=-1, keepdims=True)
    return x * lax.rsqrt(ms + EPS) * g


def _head_block_ones(width):
    r = lax.broadcasted_iota(jnp.int32, (width, width), 0) // HEAD_DIM
    c = lax.broadcasted_iota(jnp.int32, (width, width), 1) // HEAD_DIM
    return jnp.where(r == c, 1.0, 0.0).astype(BF16)


def _norm_rope(xg, ms, gain, cos, sin):
    y = xg * lax.rsqrt(ms + EPS) * gain
    lane = lax.broadcasted_iota(jnp.int32, y.shape, 1) % HEAD_DIM
    half = ROPE_DIM // 2
    partner = jnp.where(lane < half, pltpu.roll(y, LANES - half, 1), pltpu.roll(y, half, 1))
    return y * cos + partner * sin


def _head_norm_rope(xg, gain, cos, sin, ones_bd):
    ms = _dot((xg * xg).astype(BF16), ones_bd) * (1.0 / HEAD_DIM)
    return _norm_rope(xg, ms, gain, cos, sin)


def _head_norm_rope_pair(xa, xb, gain_a, gain_b, cos, sin, ones_bd2):
    sq = jnp.concatenate([xa * xa, xb * xb], axis=-1).astype(BF16)
    ms = _dot(sq, ones_bd2) * (1.0 / HEAD_DIM)
    return (_norm_rope(xa, ms[:, :LANES], gain_a, cos, sin),
            _norm_rope(xb, ms[:, LANES:], gain_b, cos, sin))


Q_W = N_HEADS * HEAD_DIM
KV_W = N_KV_HEADS * HEAD_DIM
SEG_Q = 0
SEG_KV = Q_W
SEG_GATE = SEG_KV + 6 * KV_W
SEG_UA = SEG_GATE + N_KV_HEADS * LANES
GATES_PER_KV = Q_PER_KV * N_BRANCH


def _causal_conv_tile(glu, seq_start, w_ref, b_ref, lg_ref, lb_ref, o_ref, ext_ref, shift_ref):
    ts = glu.shape[0]
    ext_ref[0:HALO, :] = jnp.where(seq_start, 0.0, ext_ref[ts:ts + HALO, :])
    ext_ref[HALO:HALO + ts, :] = glu
    n_shift = shift_ref.shape[1]
    for r in range(1, SUBLANES):
        shift_ref[r - 1] = ext_ref[r:r + n_shift, :]

    def rows_from(o):
        r = o % SUBLANES
        if r == 0:
            return ext_ref[o:o + CH_CONV, :]
        return shift_ref[r - 1, o - r:o - r + CH_CONV, :]

    w = w_ref[...]
    first_tap = HALO - (CONV_KERNEL - 1)
    for c in range(ts // CH_CONV):
        base = c * CH_CONV + first_tap
        acc = w[0:1, :] * rows_from(base)
        for k in range(1, CONV_KERNEL):
            acc = acc + w[k:k + 1, :] * rows_from(base + k)
        y = acc + b_ref[...]
        yc = y - jnp.mean(y, axis=-1, keepdims=True)
        yn = yc * lax.rsqrt(jnp.mean(yc * yc, axis=-1, keepdims=True) + EPS)
        z = yn * lg_ref[...] + lb_ref[...]
        o_ref[c * CH_CONV:(c + 1) * CH_CONV, :] = (z * jax.nn.sigmoid(z)).astype(o_ref.dtype)


def _in_proj_kernel(x_ref, g_ref, w_ref, cos_ref, sin_ref, qg_ref, ksg_ref, kwg_ref,
                    cw_ref, cb_ref, clg_ref, clb_ref,
                    q_ref, kc_ref, vc_ref, ks_ref, vs_ref, kw_ref, vw_ref, gate_ref, conv_ref,
                    glu_ref, ext_ref, shift_ref, *, conv_w, seq, n_tiles):
    i = pl.program_id(0)
    tm = x_ref.shape[0]

    @pl.when(i == 0)
    def _():
        glu_ref[...] = jnp.zeros_like(glu_ref)
        ext_ref[...] = jnp.zeros_like(ext_ref)

    _causal_conv_tile(glu_ref[...], lax.rem((i - 1) * tm, seq) == 0, cw_ref, cb_ref, clg_ref, clb_ref,
                      conv_ref, ext_ref, shift_ref)

    h = _rms_rows(x_ref[...], g_ref[...]).astype(BF16)
    cos = cos_ref[...]
    sin = sin_ref[...]
    ones_bd = _head_block_ones(2 * LANES)
    lane = lax.broadcasted_iota(jnp.int32, (tm, LANES), 1)
    tile_start = lax.rem(jnp.minimum(i, n_tiles - 1) * tm, seq)
    tok = tile_start + lax.broadcasted_iota(jnp.int32, (tm, 1), 0)
    block_aug = jnp.where(lane - HEAD_DIM == tok // SEL_LEN, NEG_BIG, 0.0)
    ones_aug = jnp.where(lane == HEAD_DIM, 1.0, 0.0)

    def put_heads(ref, first, val, aug):
        for j in range(HEADS_PER_VREG):
            head = val if j == 0 else pltpu.roll(val, LANES - j * HEAD_DIM, 1)
            ref[first + j] = jnp.where(lane < HEAD_DIM, head, aug).astype(ref.dtype)

    qkv = _dot(h, w_ref[:, SEG_Q:SEG_GATE])
    group = lambda j: qkv[:, j * LANES:(j + 1) * LANES]
    first_kv = Q_W // LANES
    scale = HEAD_DIM ** -0.5 * LOG2_E
    qg = qg_ref[...]
    for c in range(0, first_kv, 2):
        pair = _head_norm_rope_pair(group(c), group(c + 1), qg, qg, cos, sin, ones_bd)
        for j, y in enumerate(pair):
            put_heads(q_ref, (c + j) * HEADS_PER_VREG, y * scale, 0.0)
    kc_ref[...] = group(first_kv)
    vc_ref[...] = group(first_kv + 1)
    k_sel, k_win = _head_norm_rope_pair(group(first_kv + 2), group(first_kv + 4), ksg_ref[...], kwg_ref[...],
                                        cos, sin, ones_bd)
    put_heads(ks_ref, 0, k_sel, block_aug)
    put_heads(vs_ref, 0, group(first_kv + 3), ones_aug)
    put_heads(kw_ref, 0, k_win, 0.0)
    put_heads(vw_ref, 0, group(first_kv + 5), ones_aug)
    gate_ref[...] = jax.nn.sigmoid(_dot(h, w_ref[:, SEG_GATE:SEG_UA]))
    u = _dot(h, w_ref[:, SEG_UA:])
    glu_ref[...] = u[:, :conv_w] * jax.nn.sigmoid(u[:, conv_w:])


def _in_proj(x2, g, w_perm, cos_t, sin_t, qg, ksg, kwg, conv_params, seq):
    t, d = x2.shape
    conv_w = (w_perm.shape[1] - SEG_UA) // 2
    tm = min(TM_PROJ, seq)
    n_tiles = t // tm
    row = lambda i: (jnp.minimum(i, n_tiles - 1), 0)
    const = lambda i: (0, 0)
    head_row = lambda i: (0, jnp.minimum(i, n_tiles - 1), 0)
    out_shape = [
        jax.ShapeDtypeStruct((N_HEADS, t, LANES), BF16),
        jax.ShapeDtypeStruct((t, KV_W), F32),
        jax.ShapeDtypeStruct((t, KV_W), F32),
        jax.ShapeDtypeStruct((N_KV_HEADS, t, LANES), BF16),
        jax.ShapeDtypeStruct((N_KV_HEADS, t, LANES), BF16),
        jax.ShapeDtypeStruct((N_KV_HEADS, t, LANES), BF16),
        jax.ShapeDtypeStruct((N_KV_HEADS, t, LANES), BF16),
        jax.ShapeDtypeStruct((t, N_KV_HEADS * LANES), F32),
        jax.ShapeDtypeStruct((t, conv_w), BF16),
    ]
    kv_spec = pl.BlockSpec((N_KV_HEADS, tm, LANES), head_row)
    out_specs = [
        pl.BlockSpec((N_HEADS, tm, LANES), head_row),
        pl.BlockSpec((tm, KV_W), row), pl.BlockSpec((tm, KV_W), row),
        kv_spec, kv_spec, kv_spec, kv_spec,
        pl.BlockSpec((tm, N_KV_HEADS * LANES), row),
        pl.BlockSpec((tm, conv_w), lambda i: (jnp.maximum(i - 1, 0), 0)),
    ]
    in_specs = [
        pl.BlockSpec((tm, d), row), pl.BlockSpec((1, d), const),
        pl.BlockSpec(w_perm.shape, const),
        pl.BlockSpec((tm, LANES), row), pl.BlockSpec((tm, LANES), row),
        pl.BlockSpec((1, LANES), const), pl.BlockSpec((1, LANES), const), pl.BlockSpec((1, LANES), const),
    ] + [pl.BlockSpec(p.shape, const) for p in conv_params]
    return pl.pallas_call(
        functools.partial(_in_proj_kernel, conv_w=conv_w, seq=seq, n_tiles=n_tiles),
        grid=(n_tiles + 1,), in_specs=in_specs, out_specs=out_specs, out_shape=out_shape,
        scratch_shapes=[pltpu.VMEM((tm, conv_w), F32),
                        pltpu.VMEM((tm + HALO, conv_w), F32),
                        pltpu.VMEM((SUBLANES - 1, tm + HALO - SUBLANES, conv_w), F32)],
        compiler_params=_cparams(("arbitrary",)), name="in_proj",
    )(x2, g, w_perm, cos_t, sin_t, qg, ksg, kwg, *conv_params)


def _gelu_tanh(x):
    c = math.sqrt(2.0 / math.pi)
    return 0.5 * x * (1.0 + jnp.tanh(c * (x + 0.044715 * (x * x * x))))


def _compress_kernel(k_ref, v_ref, w1ak_ref, w1bk_ref, w2k_ref, pak_ref, pbk_ref,
                     w1av_ref, w1bv_ref, w2v_ref, pav_ref, pbv_ref,
                     kg_ref, cos_ref, sin_ref, ko_ref, vo_ref):
    def mlp(x_ref, w1a_ref, w1b_ref, w2_ref, pa_ref, pb_ref):
        n = x_ref.shape[0] // CMP_STRIDE
        first = second = None
        for l in range(CMP_STRIDE):
            x = x_ref[pl.ds(l, n, stride=CMP_STRIDE), :]
            cols = slice(l * KV_W, (l + 1) * KV_W)
            fa = _dot((x + pa_ref[:, cols]).astype(BF16), w1a_ref[cols, :])
            fb = _dot((x + pb_ref[:, cols]).astype(BF16), w1b_ref[cols, :])
            first = fa if first is None else first + fa
            second = fb if second is None else second + fb
        hid = first + pltpu.roll(second, n - 1, 0)
        return _dot(_gelu_tanh(hid).astype(BF16), w2_ref[...])

    kc = mlp(k_ref, w1ak_ref, w1bk_ref, w2k_ref, pak_ref, pbk_ref)
    kc = _head_norm_rope(kc, kg_ref[...], cos_ref[...], sin_ref[...], _head_block_ones(LANES))
    vc = mlp(v_ref, w1av_ref, w1bv_ref, w2v_ref, pav_ref, pbv_ref)
    lane = lax.broadcasted_iota(jnp.int32, kc.shape, 1)
    for j in range(N_KV_HEADS):
        for val, ref in ((kc, ko_ref), (vc, vo_ref)):
            head = val if j == 0 else pltpu.roll(val, LANES - j * HEAD_DIM, 1)
            ref[0, j] = jnp.where(lane < HEAD_DIM, head, 0.0).astype(ref.dtype)


def _compress(kc, vc, wk, wv, kg, cosc, sinc, b, ncp):
    seq = kc.shape[0] // b
    const = lambda i: (0, 0)
    row = lambda i: (i, 0)

    def wspecs(ws):
        return [pl.BlockSpec(w.shape, const) for w in ws]

    out = jax.ShapeDtypeStruct((b, N_KV_HEADS, ncp, LANES), BF16)
    ospec = pl.BlockSpec((1, N_KV_HEADS, ncp, LANES), lambda i: (i, 0, 0, 0))
    return pl.pallas_call(
        _compress_kernel, grid=(b,),
        in_specs=[pl.BlockSpec((seq, KV_W), row), pl.BlockSpec((seq, KV_W), row)]
        + wspecs(wk) + wspecs(wv)
        + [pl.BlockSpec((1, LANES), const), pl.BlockSpec((ncp, LANES), row), pl.BlockSpec((ncp, LANES), row)],
        out_specs=[ospec, ospec], out_shape=[out, out],
        compiler_params=_cparams(("parallel",)), name="compress",
    )(kc, vc, *wk, *wv, kg, cosc, sinc)


def _attn_kernel(off_ref, q_ref, kc_ref, vc_ref, ks_ref, vs_ref, kw_ref, vw_ref, gate_ref, selmap_ref,
                 o_ref, *, seq, tq, kc_len, top_n, bounded):
    i = pl.program_id(2)
    t0 = i * tq
    rows = Q_PER_KV * tq
    n_sel = seq // SEL_LEN
    q2 = q_ref[...].reshape(rows, LANES)
    t_row = t0 + (lax.broadcasted_iota(jnp.int32, (rows, 1), 0) & (tq - 1))
    t_tok = t0 + lax.broadcasted_iota(jnp.int32, (tq, 1), 0)
    neg_offset = -off_ref[0] if bounded else 0.0

    def add_bias(s, bias):
        return (s.reshape(s.shape[0] // tq, tq, s.shape[1]) + bias[None]).reshape(s.shape)

    kcmp = kc_ref[0, 0]
    ncp = kcmp.shape[0]
    s_c = _dot_nt(q2, kcmp)
    cmp_end = lax.broadcasted_iota(jnp.int32, (1, ncp), 1) * CMP_STRIDE + (CMP_LEN - 1)
    if bounded:
        e_c = jnp.exp2(add_bias(s_c, jnp.where(cmp_end <= t_tok, neg_offset, NEG_BIG)))
    else:
        s_c = jnp.where(cmp_end <= t_row, s_c, -jnp.inf)
        m_c = jnp.max(s_c, axis=-1, keepdims=True)
        e_c = jnp.exp2(s_c - jnp.where(m_c == -jnp.inf, 0.0, m_c))
    p_c = e_c * (1.0 / jnp.maximum(jnp.sum(e_c, axis=-1, keepdims=True), jnp.finfo(F32).tiny))
    o_c = _dot(p_c.astype(BF16), vc_ref[0, 0])

    def denominator(acc):
        return acc[:, HEAD_DIM:HEAD_DIM + 1]

    span = min(WINDOW + tq, seq)
    w0 = pl.multiple_of(jnp.maximum(t0 - WINDOW, 0), tq)
    key_w = w0 + lax.broadcasted_iota(jnp.int32, (1, span), 1)
    bias_w = jnp.where((key_w <= t_tok) & (key_w > t_tok - WINDOW), neg_offset, NEG_BIG)
    s_w = add_bias(_dot_nt(q2, kw_ref[0, pl.ds(w0, span), :]), bias_w)
    if bounded:
        p_w = jnp.exp2(s_w).astype(BF16)
    else:
        p_w = jnp.exp2((s_w - jnp.max(s_w, axis=-1, keepdims=True)).astype(BF16))
    acc_w = _dot(p_w, vw_ref[0, pl.ds(w0, span), :])
    o_w = acc_w * (1.0 / denominator(acc_w))

    p_hi, p_lo = _split_bf16(jnp.sum(p_c.reshape(Q_PER_KV, tq, ncp), axis=0))
    selmap = selmap_ref[...]
    imp = _dot_nt(selmap, p_hi) + _dot_nt(selmap, p_lo)
    blk = lax.broadcasted_iota(jnp.int32, (n_sel, tq), 0)
    cur = (t0 + lax.broadcasted_iota(jnp.int32, (n_sel, tq), 1)) // SEL_LEN
    causal_blk = blk <= cur
    forced = (blk == 0) | (causal_blk & (blk > cur - N_LOCAL_SEL))
    score = jnp.where(forced, jnp.inf, jnp.where(causal_blk, imp, -jnp.inf))
    sub = lax.broadcasted_iota(jnp.int32, (SUBLANES, tq), 0)
    groups = [score[g * SUBLANES:(g + 1) * SUBLANES, :] for g in range(n_sel // SUBLANES)]
    ranks = [jnp.zeros((SUBLANES, tq), F32) for _ in groups]
    for jp in range(n_sel):
        other = jnp.broadcast_to(score[jp:jp + 1, :], (SUBLANES, tq))
        for g, sg in enumerate(groups):
            first = g * SUBLANES
            if first > jp:
                inc = jnp.where(other >= sg, 1.0, 0.0)
            elif first + SUBLANES - 1 <= jp:
                inc = jnp.where(other > sg, 1.0, 0.0)
            else:
                inc = jnp.where(other > sg, 1.0, jnp.where((other == sg) & (sub > jp - first), 1.0, 0.0))
            ranks[g] = ranks[g] + inc
    chosen_flag = neg_offset * (1.0 / NEG_BIG)
    block_flags = jnp.where(jnp.concatenate(ranks, axis=0) < top_n, chosen_flag, 1.0)
    flag_rows = [jnp.zeros((HEAD_DIM, tq), F32), block_flags]
    if HEAD_DIM + n_sel < LANES:
        flag_rows.append(jnp.zeros((LANES - HEAD_DIM - n_sel, tq), F32))
    flags = jnp.transpose(jnp.concatenate(flag_rows, axis=0))
    q_sel = (q2.reshape(Q_PER_KV, tq, LANES) + flags.astype(BF16)[None]).reshape(rows, LANES)

    def sel_chunk(k0, carry, bias):
        s = _dot_nt(q_sel, ks_ref[0, pl.ds(k0, kc_len), :])
        if bias is not None:
            s = add_bias(s, bias)
        v = vs_ref[0, pl.ds(k0, kc_len), :]
        if bounded:
            (acc,) = carry
            return (acc + _dot(jnp.exp2(s).astype(BF16), v),)
        m, acc = carry
        m_new = jnp.maximum(m, jnp.max(s, axis=-1, keepdims=True))
        p = jnp.exp2((s - m_new).astype(BF16))
        return m_new, jnp.exp2(m - m_new) * acc + _dot(p, v)

    n_full = t0 // kc_len
    init = (jnp.zeros((rows, LANES), F32),)
    if not bounded:
        init = (jnp.full((rows, 1), -jnp.inf, F32),) + init
    carry = lax.fori_loop(
        0, n_full, lambda c, cr: sel_chunk(pl.multiple_of(c * kc_len, kc_len), cr, None), init)
    kd = pl.multiple_of(n_full * kc_len, kc_len)
    key_d = kd + lax.broadcasted_iota(jnp.int32, (1, kc_len), 1)
    acc_s = sel_chunk(kd, carry, jnp.where(key_d <= t_tok, 0.0, NEG_BIG))[-1]
    o_s = acc_s * (1.0 / denominator(acc_s))

    gates = gate_ref[...]

    def gate_col(br):
        cols = [gates[:, g * N_BRANCH + br:g * N_BRANCH + br + 1] for g in range(Q_PER_KV)]
        return jnp.concatenate(cols, axis=0)

    o = gate_col(0) * o_c + gate_col(1) * o_s + gate_col(2) * o_w
    o3 = o.reshape(Q_PER_KV, tq, LANES)
    lane = lax.broadcasted_iota(jnp.int32, (tq, LANES), 1)
    pairs = [jnp.where(lane < HEAD_DIM, o3[g], pltpu.roll(o3[g + 1], HEAD_DIM, 1))
             for g in range(0, Q_PER_KV, HEADS_PER_VREG)]
    o_ref[...] = jnp.concatenate(pairs, axis=-1).astype(o_ref.dtype)


def _attention(score_bound, q, kcmp, vcmp, ks, vs, kw, vw, gates, selmap_t, b, seq):
    t = b * seq
    tq = min(TQ, seq)
    kc_len = min(KC, seq)
    nq = seq // tq
    ncp = kcmp.shape[2]
    n_sel = seq // SEL_LEN
    assert HEAD_DIM + n_sel <= LANES, "selection-block flags must fit beside the head dims"
    top_n = min(SEL_TOPK, n_sel)
    cmp_spec = pl.BlockSpec((1, 1, ncp, LANES), lambda bi, kh, i: (bi, kh, 0, 0))
    seq_spec = pl.BlockSpec((1, seq, LANES), lambda bi, kh, i: (kh, bi, 0))

    def run(bounded):
        return pl.pallas_call(
            functools.partial(_attn_kernel, seq=seq, tq=tq, kc_len=kc_len, top_n=top_n, bounded=bounded),
            grid=(b, N_KV_HEADS, nq),
            in_specs=[
                pl.BlockSpec(memory_space=pltpu.SMEM),
                pl.BlockSpec((Q_PER_KV, tq, LANES), lambda bi, kh, i: (kh, bi * nq + i, 0)),
                cmp_spec, cmp_spec, seq_spec, seq_spec, seq_spec, seq_spec,
                pl.BlockSpec((tq, LANES), lambda bi, kh, i: (bi * nq + i, kh)),
                pl.BlockSpec(selmap_t.shape, lambda bi, kh, i: (0, 0)),
            ],
            out_specs=pl.BlockSpec((tq, Q_PER_KV * HEAD_DIM), lambda bi, kh, i: (bi * nq + i, kh)),
            out_shape=jax.ShapeDtypeStruct((t, N_HEADS * HEAD_DIM), BF16),
            compiler_params=_cparams(("parallel", "parallel", "arbitrary")),
            name="nsa_attention" if bounded else "nsa_attention_running_max",
        )(score_bound, q, kcmp, vcmp, ks, vs, kw, vw, gates, selmap_t)

    return lax.cond(score_bound[0] < MAX_SCORE_BOUND, lambda: run(True), lambda: run(False))


def _top2_gates(logits, n_experts):
    lane = lax.broadcasted_iota(jnp.int32, logits.shape, 1)
    x = jnp.where(lane < n_experts, logits, -jnp.inf)
    m1 = jnp.max(x, axis=-1, keepdims=True)
    i1 = jnp.min(jnp.where(x == m1, lane, LANES), axis=-1, keepdims=True)
    x2 = jnp.where(lane == i1, -jnp.inf, x)
    m2 = jnp.max(x2, axis=-1, keepdims=True)
    i2 = jnp.min(jnp.where(x2 == m2, lane, LANES), axis=-1, keepdims=True)
    e2 = jnp.exp(m2 - m1)
    inv = 1.0 / (1.0 + e2)
    return jnp.where(lane == i1, inv, jnp.where(lane == i2, e2 * inv, 0.0))


def _pack_bf16_halves(x):
    w = x.shape[1] // 2
    bits = lax.bitcast_convert_type(x.astype(BF16).astype(F32), jnp.uint32)
    return (bits[:, :w] >> 16) | (bits[:, w:] & jnp.uint32(0xFFFF0000))


def _unpack_bf16_halves(p):
    lo = lax.bitcast_convert_type(p << 16, F32)
    hi = lax.bitcast_convert_type(p & jnp.uint32(0xFFFF0000), F32)
    return jnp.concatenate([lo, hi], axis=-1).astype(BF16)


def _out_proj_kernel(*refs, n_experts):
    if n_experts:
        attn_ref, conv_ref, wo_ref, x_ref, g_ref, rt_ref, xo_ref, h_ref, gate_ref = refs
    else:
        attn_ref, conv_ref, wo_ref, x_ref, g_ref, xo_ref, h_ref = refs
    aw = attn_ref.shape[1]
    x = x_ref[...] + _dot(attn_ref[...], wo_ref[0:aw, :]) + _dot(conv_ref[...], wo_ref[aw:, :])
    xo_ref[...] = x
    h = _rms_rows(x, g_ref[...])
    if not n_experts:
        h_ref[...] = h.astype(h_ref.dtype)
    else:
        h_ref[...] = _pack_bf16_halves(h)
        h_hi, h_lo = _split_bf16(h)
        router = rt_ref[...]
        by_hi = _dot(h_hi, router)
        logits = by_hi[:, :LANES] + (by_hi[:, LANES:] + _dot(h_lo, router[:, :LANES]))
        gate_ref[...] = _top2_gates(logits, n_experts)


def _out_proj(attn, conv, wo, x2, g, router_split=None, n_experts=0):
    t, d = x2.shape
    tm = min(TM_PROJ, t)
    row = lambda i: (i, 0)
    const = lambda i: (0, 0)
    in_specs = [pl.BlockSpec((tm, attn.shape[1]), row), pl.BlockSpec((tm, conv.shape[1]), row),
                pl.BlockSpec(wo.shape, const), pl.BlockSpec((tm, d), row), pl.BlockSpec((1, d), const)]
    out_shape = [jax.ShapeDtypeStruct((t, d), F32), jax.ShapeDtypeStruct((t, d), BF16)]
    out_specs = [pl.BlockSpec((tm, d), row), pl.BlockSpec((tm, d), row)]
    args = [attn, conv, wo, x2, g]
    if n_experts:
        out_shape[1] = jax.ShapeDtypeStruct((t, d // 2), jnp.uint32)
        out_specs[1] = pl.BlockSpec((tm, d // 2), row)
        in_specs.append(pl.BlockSpec(router_split.shape, const))
        out_shape.append(jax.ShapeDtypeStruct((t, LANES), F32))
        out_specs.append(pl.BlockSpec((tm, LANES), row))
        args.append(router_split)
    return pl.pallas_call(
        functools.partial(_out_proj_kernel, n_experts=n_experts),
        grid=(t // tm,), in_specs=in_specs, out_specs=out_specs, out_shape=out_shape,
        compiler_params=_cparams(("parallel",)), name="out_proj",
    )(*args)


def _swiglu_tiles(h, acc, wg, wu, wd, dff, tf):
    for f in range(dff // tf):
        cols = slice(f * tf, (f + 1) * tf)
        a = _dot(h, wg(cols))
        u = _dot(h, wu(cols))
        acc = acc + _dot(((a * jax.nn.sigmoid(a)) * u).astype(BF16), wd(cols))
    return acc


def _ffn_kernel(h_ref, x_ref, wg_ref, wu_ref, wd_ref, o_ref, *, tf):
    o_ref[...] = _swiglu_tiles(h_ref[...], x_ref[...], lambda c: wg_ref[:, c], lambda c: wu_ref[:, c],
                               lambda c: wd_ref[c, :], wg_ref.shape[1], tf)


def _ffn(h, x2, wg, wu, wd):
    t, d = x2.shape
    tm = min(TM_FFN, t)
    row = lambda i: (i, 0)
    resident = lambda w: pl.BlockSpec(w.shape, lambda i: (0, 0), pipeline_mode=pl.Buffered(1))
    return pl.pallas_call(
        functools.partial(_ffn_kernel, tf=TF_FFN), grid=(t // tm,),
        in_specs=[pl.BlockSpec((tm, d), row), pl.BlockSpec((tm, d), row),
                  resident(wg), resident(wu), resident(wd)],
        out_specs=pl.BlockSpec((tm, d), row),
        out_shape=jax.ShapeDtypeStruct((t, d), F32),
        compiler_params=_cparams(("parallel",)), name="ffn",
    )(h, x2, wg, wu, wd)


def _route_scan_kernel(g_ref, pos_ref, cnt_ref, tot_ref, carry_ref):
    c = pl.program_id(0)

    @pl.when(c == 0)
    def _():
        carry_ref[...] = jnp.zeros_like(carry_ref)

    ct = g_ref.shape[0]
    routed = g_ref[...] > 0.0
    a = jnp.where(routed, 1.0, 0.0)
    earlier = lax.broadcasted_iota(jnp.int32, (ct, ct), 1) < lax.broadcasted_iota(jnp.int32, (ct, ct), 0)
    base = carry_ref[...]
    pos = jnp.where(routed, _dot(jnp.where(earlier, 1.0, 0.0).astype(BF16), a.astype(BF16)) + base, -1.0)
    pos_ref[...] = pos
    cnt_ref[0] = base
    total = base + jnp.sum(a, axis=0, keepdims=True)
    carry_ref[...] = total
    tot_ref[...] = total


def _route_scan(gates):
    t = gates.shape[0]
    ct = CT_MOE
    nch = t // ct
    return pl.pallas_call(
        _route_scan_kernel, grid=(nch,),
        in_specs=[pl.BlockSpec((ct, LANES), lambda c: (c, 0))],
        out_specs=[pl.BlockSpec((ct, LANES), lambda c: (c, 0)),
                   pl.BlockSpec((1, 1, LANES), lambda c: (c, 0, 0)),
                   pl.BlockSpec((1, LANES), lambda c: (0, 0))],
        out_shape=[jax.ShapeDtypeStruct((t, LANES), F32),
                   jax.ShapeDtypeStruct((nch, 1, LANES), F32),
                   jax.ShapeDtypeStruct((1, LANES), F32)],
        scratch_shapes=[pltpu.VMEM((1, LANES), F32)],
        compiler_params=_cparams(("arbitrary",)), name="route_scan",
    )(gates)


def _sc_scatter_rows(rows, idx_a, idx_b, n_slots):
    t, w = rows.shape
    mesh = plsc.VectorSubcoreMesh(core_axis_name="core", subcore_axis_name="subcore")

    @pl.kernel(out_type=jax.ShapeDtypeStruct((n_slots, w), rows.dtype), mesh=mesh, scratch_types=[])
    def scatter(x_hbm, ia_hbm, ib_hbm, o_hbm):
        def body(x_vmem, ia_vmem, ib_vmem):
            pltpu.sync_copy(x_vmem, o_hbm.at[ia_vmem.at[0]])
            pltpu.sync_copy(x_vmem, o_hbm.at[ib_vmem.at[0]])

        pltpu.emit_pipeline(
            body, grid=(t // SC_WINDOW,),
            in_specs=[pl.BlockSpec((SC_WINDOW, w), lambda i: (i, 0)),
                      pl.BlockSpec((1, SC_WINDOW), lambda i: (i, 0)),
                      pl.BlockSpec((1, SC_WINDOW), lambda i: (i, 0))],
            out_specs=[], core_axis_name=("core", "subcore"),
            dimension_semantics=(pltpu.PARALLEL,))(x_hbm, ia_hbm, ib_hbm)

    return scatter(rows, idx_a, idx_b)


def _moe_ffn_kernel(exp_ref, rows_ref, xs_ref, wg_ref, wu_ref, wd_ref, ys_ref, *, tf):
    j = pl.program_id(0)
    n_rows = rows_ref[j]

    @pl.when(n_rows > 0)
    def _():
        xs = _unpack_bf16_halves(xs_ref[...])
        row = lax.broadcasted_iota(jnp.int32, xs.shape, 0)
        xs = jnp.where(row < n_rows, xs, jnp.zeros_like(xs))
        zero = jnp.zeros(ys_ref.shape, F32)
        ys = _swiglu_tiles(xs, zero, lambda c: wg_ref[0, :, c], lambda c: wu_ref[0, :, c],
                           lambda c: wd_ref[0, c, :], wg_ref.shape[2], tf)
        ys_ref[...] = ys.astype(ys_ref.dtype)

    @pl.when(n_rows == 0)
    def _():
        ys_ref[...] = jnp.zeros_like(ys_ref)


def _moe_ffn(xs, blk_expert, blk_rows, wg, wu, wd):
    n_slots = xs.shape[0]
    d = wg.shape[1]
    expert = lambda w: pl.BlockSpec((1,) + w.shape[1:], lambda j, e, v: (e[j], 0, 0),
                                    pipeline_mode=pl.Buffered(1))
    grid_spec = pltpu.PrefetchScalarGridSpec(
        num_scalar_prefetch=2, grid=(n_slots // BM_MOE,),
        in_specs=[pl.BlockSpec((BM_MOE, d // 2), lambda j, e, v: (j, 0)), expert(wg), expert(wu), expert(wd)],
        out_specs=pl.BlockSpec((BM_MOE, d), lambda j, e, v: (j, 0)))
    return pl.pallas_call(
        functools.partial(_moe_ffn_kernel, tf=TF_FFN), grid_spec=grid_spec,
        out_shape=jax.ShapeDtypeStruct((n_slots, d), BF16),
        compiler_params=_cparams(("arbitrary",)), name="moe_ffn",
    )(blk_expert, blk_rows, xs, wg, wu, wd)


def _moe_combine_kernel(tile_ref, blk_ref, exp_ref, first_ref, valid_ref,
                        x_ref, pos_ref, g_ref, pstart_ref, ys_ref, o_ref):
    w = pl.program_id(0)

    @pl.when(first_ref[w] == 1)
    def _():
        o_ref[...] = x_ref[...]

    @pl.when(valid_ref[w] == 1)
    def _():
        pos = pos_ref[...]
        slot = jnp.where(pos >= 0.0, pos + pstart_ref[...], -1.0)
        mine = lax.broadcasted_iota(jnp.int32, pos.shape, 1) == exp_ref[w]
        slot_e = jnp.sum(jnp.where(mine, slot, 0.0), axis=-1, keepdims=True)
        gate_e = jnp.sum(jnp.where(mine, g_ref[...], 0.0), axis=-1, keepdims=True)
        bs = ys_ref.shape[0]
        target = (blk_ref[w] * bs + lax.broadcasted_iota(jnp.int32, (1, bs), 1)).astype(F32)
        onehot = jnp.where(slot_e == target, 1.0, 0.0).astype(BF16)
        o_ref[...] += gate_e * _dot(onehot, ys_ref[...])


def _moe_combine(x2, pos, gates, pstart_row, ys, items):
    t, d = x2.shape
    n_items = items[0].shape[0]
    tok = lambda w, tile, *_: (tile[w], 0)
    grid_spec = pltpu.PrefetchScalarGridSpec(
        num_scalar_prefetch=5, grid=(n_items,),
        in_specs=[pl.BlockSpec((CT_MOE, d), tok), pl.BlockSpec((CT_MOE, LANES), tok),
                  pl.BlockSpec((CT_MOE, LANES), tok), pl.BlockSpec((1, LANES), lambda w, *_: (0, 0)),
                  pl.BlockSpec((BG_MOE, d), lambda w, tile, blk, *_: (blk[w], 0))],
        out_specs=pl.BlockSpec((CT_MOE, d), tok))
    return pl.pallas_call(
        _moe_combine_kernel, grid_spec=grid_spec,
        out_shape=jax.ShapeDtypeStruct((t, d), F32),
        compiler_params=_cparams(("arbitrary",)), name="moe_combine",
    )(*items, x2, pos, gates, pstart_row, ys)


def _count_le(ascending, x):
    return jnp.sum(ascending[None, :] <= x[:, None], axis=1).astype(jnp.int32)


def _work_items(group_id, lo, hi, n_items_max):
    n = jnp.maximum(hi - lo + 1, 0)
    ends = jnp.cumsum(n)
    total = ends[-1]
    w = jnp.arange(n_items_max, dtype=jnp.int32)
    wc = jnp.minimum(w, total - 1)
    g = _count_le(ends, wc)
    k = wc - (ends[g] - n[g])
    valid = (w < total).astype(jnp.int32)
    return group_id[g], lo[g] + k, g, ((k == 0) & (w < total)).astype(jnp.int32), valid


def _moe_routed(h, x2, gates, wg, wu, wd):
    t, d = x2.shape
    n_e = wg.shape[0]
    nch = t // CT_MOE
    n_slots = t * TOP_K + n_e * BM_MOE
    nbg = n_slots // BG_MOE
    pos, cnt, tot = _route_scan(gates)

    counts = tot[0, :n_e].astype(jnp.int32)
    padded = (counts + BM_MOE - 1) // BM_MOE * BM_MOE
    pend = jnp.cumsum(padded)
    pstart = pend - padded
    cum = cnt[:, 0, :n_e].astype(jnp.int32)
    cum_end = jnp.concatenate([cum[1:], counts[None]], axis=0)

    pstart_row = jnp.zeros((1, LANES), F32).at[0, :n_e].set(pstart.astype(F32))
    routed = pos >= 0.0
    slot = pos + pstart_row
    slot_a = jnp.min(jnp.where(routed, slot, float(n_slots)), axis=1).astype(jnp.int32)
    slot_b = jnp.max(jnp.where(routed, slot, -1.0), axis=1).astype(jnp.int32)
    xs = _sc_scatter_rows(h, slot_a.reshape(-1, SC_WINDOW), slot_b.reshape(-1, SC_WINDOW), n_slots)

    mb = jnp.arange(n_slots // BM_MOE, dtype=jnp.int32) * BM_MOE
    mb_e = jnp.minimum(_count_le(pend, mb), n_e - 1)
    mb_rows = jnp.where(mb < pend[-1], jnp.clip(counts[mb_e] - (mb - pstart[mb_e]), 0, BM_MOE), 0)
    ys = _moe_ffn(xs, mb_e, mb_rows.astype(jnp.int32), wg, wu, wd)

    tile = jnp.repeat(jnp.arange(nch, dtype=jnp.int32), n_e)
    te = jnp.tile(jnp.arange(n_e, dtype=jnp.int32), nch)
    s_lo = (pstart[None, :] + cum).reshape(-1)
    s_hi = (pstart[None, :] + cum_end).reshape(-1) - 1
    b_lo = s_lo // BG_MOE
    b_hi = jnp.where(s_hi >= s_lo, s_hi // BG_MOE, b_lo - 1)
    grp = jnp.arange(nch * n_e, dtype=jnp.int32)
    c_grp, c_blk, _, _, c_valid = _work_items(grp, b_lo, b_hi, nbg + n_e * nch)
    c_tile = tile[c_grp]
    c_first = jnp.concatenate([jnp.ones((1,), jnp.int32), (c_tile[1:] != c_tile[:-1]).astype(jnp.int32)])
    return _moe_combine(x2, pos, gates, pstart_row, ys, (c_tile, c_blk, te[c_grp], c_first, c_valid))


def _rope_tables(pos):
    half = ROPE_DIM // 2
    inv_freq = ROPE_THETA ** (-2.0 * jnp.arange(half, dtype=F32) / ROPE_DIM)
    ang = pos.astype(F32).reshape(-1, 1) * inv_freq
    cos, sin = jnp.cos(ang), jnp.sin(ang)
    n = ang.shape[0]
    rest = HEAD_DIM - ROPE_DIM
    cos_h = jnp.concatenate([cos, cos, jnp.ones((n, rest), F32)], axis=-1)
    sin_h = jnp.concatenate([-sin, sin, jnp.zeros((n, rest), F32)], axis=-1)
    return jnp.tile(cos_h, (1, HEADS_PER_VREG)), jnp.tile(sin_h, (1, HEADS_PER_VREG))


def _permute_w_in(w, conv_w):
    d = w.shape[0]
    kv_end = Q_W + 6 * KV_W
    g = w[:, kv_end:kv_end + N_HEADS * N_BRANCH]
    pad = jnp.zeros((d, LANES - GATES_PER_KV), w.dtype)
    gate_cols = []
    for kh in range(N_KV_HEADS):
        gate_cols += [g[:, kh * GATES_PER_KV:(kh + 1) * GATES_PER_KV], pad]
    u = w[:, kv_end + N_HEADS * N_BRANCH:]
    return jnp.concatenate([w[:, :kv_end]] + gate_cols + [u], axis=1).astype(BF16)


def _compress_weights(pos_emb, w1, w2):
    hidden = w1.shape[1]
    eye = jnp.eye(N_KV_HEADS, dtype=w1.dtype)
    w1r = w1.reshape(CMP_LEN, HEAD_DIM, hidden)
    halves = []
    for part in (w1r[:CMP_STRIDE], w1r[CMP_STRIDE:]):
        full = jnp.einsum('ldj,hg->lhdgj', part, eye)
        halves.append(full.reshape(CMP_STRIDE * N_KV_HEADS * HEAD_DIM, N_KV_HEADS * hidden).astype(BF16))
    w2p = jnp.einsum('jd,hg->hjgd', w2, eye).reshape(N_KV_HEADS * hidden, N_KV_HEADS * HEAD_DIM).astype(BF16)
    pos = []
    for part in (pos_emb[:CMP_STRIDE], pos_emb[CMP_STRIDE:]):
        pos.append(jnp.broadcast_to(part[:, None, :], (CMP_STRIDE, N_KV_HEADS, HEAD_DIM)).reshape(1, -1))
    return [halves[0], halves[1], w2p, pos[0], pos[1]]


def _selection_map_t(seq):
    ncp = seq // CMP_STRIDE
    n_cmp = (seq - CMP_LEN) // CMP_STRIDE + 1
    c0 = np.arange(ncp) * CMP_STRIDE
    s0 = np.arange(seq // SEL_LEN) * SEL_LEN
    ov = np.minimum(c0[None, :] + CMP_LEN, s0[:, None] + SEL_LEN) - np.maximum(c0[None, :], s0[:, None])
    m = np.clip(ov, 0, None) / CMP_LEN
    m[:, n_cmp:] = 0.0
    return jnp.asarray(m, dtype=BF16)


def kernel(x, positions, attn_norm_g, ffn_norm_g, w_in, w_out, q_norm_g, k_norm_g, cmp_pos_k, cmp_w1_k, cmp_w2_k, cmp_pos_v, cmp_w1_v, cmp_w2_v, conv_w, conv_b, conv_ln_g, conv_ln_b, ffn_w_gate, ffn_w_up, ffn_w_down, moe_router, moe_w_gate, moe_w_up, moe_w_down):
    b, seq, d = x.shape
    t = b * seq
    depth = w_in.shape[0]
    cw = conv_w.shape[2]
    ncp = seq // CMP_STRIDE
    n_cmp = (seq - CMP_LEN) // CMP_STRIDE + 1
    assert seq % max(TQ, KC, TM_PROJ) == 0 and seq >= WINDOW + TQ

    cos_t, sin_t = _rope_tables(positions)
    cmp_end = np.minimum(np.arange(ncp) * CMP_STRIDE + CMP_LEN - 1, seq - 1)
    cos_c, sin_c = _rope_tables(positions[:, cmp_end])
    selmap_t = _selection_map_t(seq)
    tile2 = lambda v: jnp.tile(v.reshape(1, HEAD_DIM), (1, HEADS_PER_VREG))

    x2 = x.reshape(t, d)
    for layer in range(depth):
        w_perm = _permute_w_in(w_in[layer], cw)
        conv_params = (conv_w[layer], conv_b[layer].reshape(1, cw), conv_ln_g[layer].reshape(1, cw),
                       conv_ln_b[layer].reshape(1, cw))
        q, kc, vc, ks, vs, kw, vw, gates, conv = _in_proj(
            x2, attn_norm_g[layer].reshape(1, d), w_perm, cos_t, sin_t,
            tile2(q_norm_g[layer]), tile2(k_norm_g[layer, 1]), tile2(k_norm_g[layer, 2]), conv_params, seq)
        kcmp, vcmp = _compress(
            kc, vc,
            _compress_weights(cmp_pos_k[layer], cmp_w1_k[layer], cmp_w2_k[layer]),
            _compress_weights(cmp_pos_v[layer], cmp_w1_v[layer], cmp_w2_v[layer]),
            tile2(k_norm_g[layer, 0]), cos_c, sin_c, b, ncp)
        score_bound = (HEAD_DIM ** 0.5 * LOG2_E * SCORE_BOUND_MARGIN * jnp.max(jnp.abs(q_norm_g[layer]))
                       * jnp.max(jnp.abs(k_norm_g[layer]))).astype(F32).reshape(1)
        attn = _attention(score_bound, q, kcmp, vcmp, ks, vs, kw, vw, gates, selmap_t, b, seq)
        wo = w_out[layer].astype(BF16)
        g2 = ffn_norm_g[layer].reshape(1, d)
        i = layer // 2
        if layer % 2 == 0:
            x2, h = _out_proj(attn, conv, wo, x2, g2)
            x2 = _ffn(h, x2, ffn_w_gate[i].astype(BF16), ffn_w_up[i].astype(BF16), ffn_w_down[i].astype(BF16))
        else:
            n_e = moe_router.shape[2]
            r = jnp.pad(moe_router[i], ((0, 0), (0, LANES - n_e)))
            r_hi = r.astype(BF16)
            r_lo = (r - r_hi.astype(F32)).astype(BF16)
            x2, h, route = _out_proj(attn, conv, wo, x2, g2, jnp.concatenate([r_hi, r_lo], axis=1), n_e)
            x2 = _moe_routed(h, x2, route, moe_w_gate[i].astype(BF16), moe_w_up[i].astype(BF16),
                             moe_w_down[i].astype(BF16))
    return x2.reshape(b, seq, d)
```

```python
import functools
import math

import jax
import jax.numpy as jnp
import numpy as np
from jax import lax
from jax.experimental import pallas as pl
from jax.experimental.pallas import tpu as pltpu
from jax.experimental.pallas import tpu_sc as plsc

F32 = jnp.float32
BF16 = jnp.bfloat16

N_HEADS = 8
N_KV_HEADS = 2
Q_PER_KV = N_HEADS // N_KV_HEADS
HEAD_DIM = 64
N_BRANCH = 3
CMP_LEN = 32
CMP_STRIDE = 16
SEL_LEN = 64
SEL_TOPK = 16
N_LOCAL_SEL = 2
WINDOW = 512
CONV_KERNEL = 31
ROPE_THETA = 500000.0
ROPE_DIM = HEAD_DIM // 4
TOP_K = 2
EPS = 1e-6

LANES = 128
SUBLANES = 8
LOG2_E = math.log2(math.e)
NEG_BIG = -(2.0 ** 100)
MAX_SCORE_BOUND = 50.0
SCORE_BOUND_MARGIN = 1.02
HEADS_PER_VREG = LANES // HEAD_DIM
VMEM_LIMIT = 56 * 1024 * 1024

TM_PROJ = 512
TQ = 256
KC = 512
CH_CONV = 32
HALO = 32
TM_FFN = 1024
TF_FFN = 512
CT_MOE = 1024
BM_MOE = 512
SC_WINDOW = 64


def _cparams(sem):
    return pltpu.CompilerParams(dimension_semantics=sem, vmem_limit_bytes=VMEM_LIMIT)


def _dot(a, b):
    return jnp.dot(a, b, preferred_element_type=F32)


def _dot_nt(a, b):
    return lax.dot_general(a, b, (((1,), (1,)), ((), ())), preferred_element_type=F32)


def _split_bf16(x):
    hi = x.astype(BF16)
    lo = (x - hi.astype(F32)).astype(BF16)
    return hi, lo


def _rms_rows(x, g):
    ms = jnp.mean(x * x, axis=-1, keepdims=True)
    return x * lax.rsqrt(ms + EPS) * g


def _head_block_ones(width):
    r = lax.broadcasted_iota(jnp.int32, (width, width), 0) // HEAD_DIM
    c = lax.broadcasted_iota(jnp.int32, (width, width), 1) // HEAD_DIM
    return jnp.where(r == c, 1.0, 0.0).astype(BF16)


def _norm_rope(xg, ms, gain, cos, sin):
    y = xg * lax.rsqrt(ms + EPS) * gain
    lane = lax.broadcasted_iota(jnp.int32, y.shape, 1) % HEAD_DIM
    half = ROPE_DIM // 2
    partner = jnp.where(lane < half, pltpu.roll(y, LANES - half, 1), pltpu.roll(y, half, 1))
    return y * cos + partner * sin


def _head_norm_rope(xg, gain, cos, sin, ones_bd):
    ms = _dot((xg * xg).astype(BF16), ones_bd) * (1.0 / HEAD_DIM)
    return _norm_rope(xg, ms, gain, cos, sin)


def _head_norm_rope_pair(xa, xb, gain_a, gain_b, cos, sin, ones_bd2):
    sq = jnp.concatenate([xa * xa, xb * xb], axis=-1).astype(BF16)
    ms = _dot(sq, ones_bd2) * (1.0 / HEAD_DIM)
    return (_norm_rope(xa, ms[:, :LANES], gain_a, cos, sin),
            _norm_rope(xb, ms[:, LANES:], gain_b, cos, sin))


Q_W = N_HEADS * HEAD_DIM
KV_W = N_KV_HEADS * HEAD_DIM
SEG_Q = 0
SEG_KV = Q_W
SEG_GATE = SEG_KV + 6 * KV_W
SEG_UA = SEG_GATE + N_KV_HEADS * LANES
GATES_PER_KV = Q_PER_KV * N_BRANCH


def _causal_conv_tile(glu, seq_start, w_ref, b_ref, lg_ref, lb_ref, o_ref, ext_ref, shift_ref):
    ts = glu.shape[0]
    ext_ref[0:HALO, :] = jnp.where(seq_start, 0.0, ext_ref[ts:ts + HALO, :])
    ext_ref[HALO:HALO + ts, :] = glu
    n_shift = shift_ref.shape[1]
    for r in range(1, SUBLANES):
        shift_ref[r - 1] = ext_ref[r:r + n_shift, :]

    def rows_from(o):
        r = o % SUBLANES
        if r == 0:
            return ext_ref[o:o + CH_CONV, :]
        return shift_ref[r - 1, o - r:o - r + CH_CONV, :]

    w = w_ref[...]
    first_tap = HALO - (CONV_KERNEL - 1)
    for c in range(ts // CH_CONV):
        base = c * CH_CONV + first_tap
        acc = w[0:1, :] * rows_from(base)
        for k in range(1, CONV_KERNEL):
            acc = acc + w[k:k + 1, :] * rows_from(base + k)
        y = acc + b_ref[...]
        yc = y - jnp.mean(y, axis=-1, keepdims=True)
        yn = yc * lax.rsqrt(jnp.mean(yc * yc, axis=-1, keepdims=True) + EPS)
        z = yn * lg_ref[...] + lb_ref[...]
        o_ref[c * CH_CONV:(c + 1) * CH_CONV, :] = (z * jax.nn.sigmoid(z)).astype(o_ref.dtype)


def _in_proj_kernel(x_ref, g_ref, w_ref, cos_ref, sin_ref, qg_ref, ksg_ref, kwg_ref,
                    cw_ref, cb_ref, clg_ref, clb_ref,
                    q_ref, kc_ref, vc_ref, ks_ref, vs_ref, kw_ref, vw_ref, gate_ref, conv_ref,
                    glu_ref, ext_ref, shift_ref, *, conv_w, seq, n_tiles):
    i = pl.program_id(0)
    tm = x_ref.shape[0]

    @pl.when(i == 0)
    def _():
        glu_ref[...] = jnp.zeros_like(glu_ref)
        ext_ref[...] = jnp.zeros_like(ext_ref)

    _causal_conv_tile(glu_ref[...], lax.rem((i - 1) * tm, seq) == 0, cw_ref, cb_ref, clg_ref, clb_ref,
                      conv_ref, ext_ref, shift_ref)

    h = _rms_rows(x_ref[...], g_ref[...]).astype(BF16)
    cos = cos_ref[...]
    sin = sin_ref[...]
    ones_bd = _head_block_ones(2 * LANES)
    lane = lax.broadcasted_iota(jnp.int32, (tm, LANES), 1)
    tile_start = lax.rem(jnp.minimum(i, n_tiles - 1) * tm, seq)
    tok = tile_start + lax.broadcasted_iota(jnp.int32, (tm, 1), 0)
    block_aug = jnp.where(lane - HEAD_DIM == tok // SEL_LEN, NEG_BIG, 0.0)
    ones_aug = jnp.where(lane == HEAD_DIM, 1.0, 0.0)

    def put_heads(ref, first, val, aug):
        for j in range(HEADS_PER_VREG):
            head = val if j == 0 else pltpu.roll(val, LANES - j * HEAD_DIM, 1)
            ref[first + j] = jnp.where(lane < HEAD_DIM, head, aug).astype(ref.dtype)

    qkv = _dot(h, w_ref[:, SEG_Q:SEG_GATE])
    group = lambda j: qkv[:, j * LANES:(j + 1) * LANES]
    first_kv = Q_W // LANES
    scale = HEAD_DIM ** -0.5 * LOG2_E
    qg = qg_ref[...]
    for c in range(0, first_kv, 2):
        pair = _head_norm_rope_pair(group(c), group(c + 1), qg, qg, cos, sin, ones_bd)
        for j, y in enumerate(pair):
            put_heads(q_ref, (c + j) * HEADS_PER_VREG, y * scale, 0.0)
    kc_ref[...] = group(first_kv)
    vc_ref[...] = group(first_kv + 1)
    k_sel, k_win = _head_norm_rope_pair(group(first_kv + 2), group(first_kv + 4), ksg_ref[...], kwg_ref[...],
                                        cos, sin, ones_bd)
    put_heads(ks_ref, 0, k_sel, block_aug)
    put_heads(vs_ref, 0, group(first_kv + 3), ones_aug)
    put_heads(kw_ref, 0, k_win, 0.0)
    put_heads(vw_ref, 0, group(first_kv + 5), ones_aug)
    gate_ref[...] = jax.nn.sigmoid(_dot(h, w_ref[:, SEG_GATE:SEG_UA]))
    u = _dot(h, w_ref[:, SEG_UA:])
    glu_ref[...] = u[:, :conv_w] * jax.nn.sigmoid(u[:, conv_w:])


def _in_proj(x2, g, w_perm, cos_t, sin_t, qg, ksg, kwg, conv_params, seq):
    t, d = x2.shape
    conv_w = (w_perm.shape[1] - SEG_UA) // 2
    tm = min(TM_PROJ, seq)
    n_tiles = t // tm
    row = lambda i: (jnp.minimum(i, n_tiles - 1), 0)
    const = lambda i: (0, 0)
    head_row = lambda i: (0, jnp.minimum(i, n_tiles - 1), 0)
    out_shape = [
        jax.ShapeDtypeStruct((N_HEADS, t, LANES), BF16),
        jax.ShapeDtypeStruct((t, KV_W), F32),
        jax.ShapeDtypeStruct((t, KV_W), F32),
        jax.ShapeDtypeStruct((N_KV_HEADS, t, LANES), BF16),
        jax.ShapeDtypeStruct((N_KV_HEADS, t, LANES), BF16),
        jax.ShapeDtypeStruct((N_KV_HEADS, t, LANES), BF16),
        jax.ShapeDtypeStruct((N_KV_HEADS, t, LANES), BF16),
        jax.ShapeDtypeStruct((t, N_KV_HEADS * LANES), F32),
        jax.ShapeDtypeStruct((t, conv_w), BF16),
    ]
    kv_spec = pl.BlockSpec((N_KV_HEADS, tm, LANES), head_row)
    out_specs = [
        pl.BlockSpec((N_HEADS, tm, LANES), head_row),
        pl.BlockSpec((tm, KV_W), row), pl.BlockSpec((tm, KV_W), row),
        kv_spec, kv_spec, kv_spec, kv_spec,
        pl.BlockSpec((tm, N_KV_HEADS * LANES), row),
        pl.BlockSpec((tm, conv_w), lambda i: (jnp.maximum(i - 1, 0), 0)),
    ]
    in_specs = [
        pl.BlockSpec((tm, d), row), pl.BlockSpec((1, d), const),
        pl.BlockSpec(w_perm.shape, const),
        pl.BlockSpec((tm, LANES), row), pl.BlockSpec((tm, LANES), row),
        pl.BlockSpec((1, LANES), const), pl.BlockSpec((1, LANES), const), pl.BlockSpec((1, LANES), const),
    ] + [pl.BlockSpec(p.shape, const) for p in conv_params]
    return pl.pallas_call(
        functools.partial(_in_proj_kernel, conv_w=conv_w, seq=seq, n_tiles=n_tiles),
        grid=(n_tiles + 1,), in_specs=in_specs, out_specs=out_specs, out_shape=out_shape,
        scratch_shapes=[pltpu.VMEM((tm, conv_w), F32),
                        pltpu.VMEM((tm + HALO, conv_w), F32),
                        pltpu.VMEM((SUBLANES - 1, tm + HALO - SUBLANES, conv_w), F32)],
        compiler_params=_cparams(("arbitrary",)), name="in_proj",
    )(x2, g, w_perm, cos_t, sin_t, qg, ksg, kwg, *conv_params)


def _gelu_tanh(x):
    c = math.sqrt(2.0 / math.pi)
    return 0.5 * x * (1.0 + jnp.tanh(c * (x + 0.044715 * (x * x * x))))


def _compress_kernel(k_ref, v_ref, w1ak_ref, w1bk_ref, w2k_ref, pak_ref, pbk_ref,
                     w1av_ref, w1bv_ref, w2v_ref, pav_ref, pbv_ref,
                     kg_ref, cos_ref, sin_ref, ko_ref, vo_ref):
    def mlp(x_ref, w1a_ref, w1b_ref, w2_ref, pa_ref, pb_ref):
        n = x_ref.shape[0] // CMP_STRIDE
        first = second = None
        for l in range(CMP_STRIDE):
            x = x_ref[pl.ds(l, n, stride=CMP_STRIDE), :]
            cols = slice(l * KV_W, (l + 1) * KV_W)
            fa = _dot((x + pa_ref[:, cols]).astype(BF16), w1a_ref[cols, :])
            fb = _dot((x + pb_ref[:, cols]).astype(BF16), w1b_ref[cols, :])
            first = fa if first is None else first + fa
            second = fb if second is None else second + fb
        hid = first + pltpu.roll(second, n - 1, 0)
        return _dot(_gelu_tanh(hid).astype(BF16), w2_ref[...])

    kc = mlp(k_ref, w1ak_ref, w1bk_ref, w2k_ref, pak_ref, pbk_ref)
    kc = _head_norm_rope(kc, kg_ref[...], cos_ref[...], sin_ref[...], _head_block_ones(LANES))
    vc = mlp(v_ref, w1av_ref, w1bv_ref, w2v_ref, pav_ref, pbv_ref)
    lane = lax.broadcasted_iota(jnp.int32, kc.shape, 1)
    for j in range(N_KV_HEADS):
        for val, ref in ((kc, ko_ref), (vc, vo_ref)):
            head = val if j == 0 else pltpu.roll(val, LANES - j * HEAD_DIM, 1)
            ref[0, j] = jnp.where(lane < HEAD_DIM, head, 0.0).astype(ref.dtype)


def _compress(kc, vc, wk, wv, kg, cosc, sinc, b, ncp):
    seq = kc.shape[0] // b
    const = lambda i: (0, 0)
    row = lambda i: (i, 0)

    def wspecs(ws):
        return [pl.BlockSpec(w.shape, const) for w in ws]

    out = jax.ShapeDtypeStruct((b, N_KV_HEADS, ncp, LANES), BF16)
    ospec = pl.BlockSpec((1, N_KV_HEADS, ncp, LANES), lambda i: (i, 0, 0, 0))
    return pl.pallas_call(
        _compress_kernel, grid=(b,),
        in_specs=[pl.BlockSpec((seq, KV_W), row), pl.BlockSpec((seq, KV_W), row)]
        + wspecs(wk) + wspecs(wv)
        + [pl.BlockSpec((1, LANES), const), pl.BlockSpec((ncp, LANES), row), pl.BlockSpec((ncp, LANES), row)],
        out_specs=[ospec, ospec], out_shape=[out, out],
        compiler_params=_cparams(("parallel",)), name="compress",
    )(kc, vc, *wk, *wv, kg, cosc, sinc)


def _attn_kernel(off_ref, q_ref, kc_ref, vc_ref, ks_ref, vs_ref, kw_ref, vw_ref, gate_ref, selmap_ref,
                 o_ref, *, seq, tq, kc_len, top_n, bounded):
    i = pl.program_id(2)
    t0 = i * tq
    rows = Q_PER_KV * tq
    n_sel = seq // SEL_LEN
    q2 = q_ref[...].reshape(rows, LANES)
    t_row = t0 + (lax.broadcasted_iota(jnp.int32, (rows, 1), 0) & (tq - 1))
    t_tok = t0 + lax.broadcasted_iota(jnp.int32, (tq, 1), 0)
    neg_offset = -off_ref[0] if bounded else 0.0

    def add_bias(s, bias):
        return (s.reshape(s.shape[0] // tq, tq, s.shape[1]) + bias[None]).reshape(s.shape)

    kcmp = kc_ref[0, 0]
    ncp = kcmp.shape[0]
    s_c = _dot_nt(q2, kcmp)
    cmp_end = lax.broadcasted_iota(jnp.int32, (1, ncp), 1) * CMP_STRIDE + (CMP_LEN - 1)
    if bounded:
        e_c = jnp.exp2(add_bias(s_c, jnp.where(cmp_end <= t_tok, neg_offset, NEG_BIG)))
    else:
        s_c = jnp.where(cmp_end <= t_row, s_c, -jnp.inf)
        m_c = jnp.max(s_c, axis=-1, keepdims=True)
        e_c = jnp.exp2(s_c - jnp.where(m_c == -jnp.inf, 0.0, m_c))
    p_c = e_c * (1.0 / jnp.maximum(jnp.sum(e_c, axis=-1, keepdims=True), jnp.finfo(F32).tiny))
    o_c = _dot(p_c.astype(BF16), vc_ref[0, 0])

    def denominator(acc):
        return acc[:, HEAD_DIM:HEAD_DIM + 1]

    span = min(WINDOW + tq, seq)
    w0 = pl.multiple_of(jnp.maximum(t0 - WINDOW, 0), tq)
    key_w = w0 + lax.broadcasted_iota(jnp.int32, (1, span), 1)
    bias_w = jnp.where((key_w <= t_tok) & (key_w > t_tok - WINDOW), neg_offset, NEG_BIG)
    s_w = add_bias(_dot_nt(q2, kw_ref[0, pl.ds(w0, span), :]), bias_w)
    if bounded:
        p_w = jnp.exp2(s_w).astype(BF16)
    else:
        p_w = jnp.exp2((s_w - jnp.max(s_w, axis=-1, keepdims=True)).astype(BF16))
    acc_w = _dot(p_w, vw_ref[0, pl.ds(w0, span), :])
    o_w = acc_w * (1.0 / denominator(acc_w))

    p_hi, p_lo = _split_bf16(jnp.sum(p_c.reshape(Q_PER_KV, tq, ncp), axis=0))
    selmap = selmap_ref[...]
    imp = _dot_nt(selmap, p_hi) + _dot_nt(selmap, p_lo)
    blk = lax.broadcasted_iota(jnp.int32, (n_sel, tq), 0)
    cur = (t0 + lax.broadcasted_iota(jnp.int32, (n_sel, tq), 1)) // SEL_LEN
    causal_blk = blk <= cur
    forced = (blk == 0) | (causal_blk & (blk > cur - N_LOCAL_SEL))
    score = jnp.where(forced, jnp.inf, jnp.where(causal_blk, imp, -jnp.inf))
    sub = lax.broadcasted_iota(jnp.int32, (SUBLANES, tq), 0)
    groups = [score[g * SUBLANES:(g + 1) * SUBLANES, :] for g in range(n_sel // SUBLANES)]
    ranks = [jnp.zeros((SUBLANES, tq), F32) for _ in groups]
    for jp in range(n_sel):
        other = jnp.broadcast_to(score[jp:jp + 1, :], (SUBLANES, tq))
        for g, sg in enumerate(groups):
            first = g * SUBLANES
            if first > jp:
                inc = jnp.where(other >= sg, 1.0, 0.0)
            elif first + SUBLANES - 1 <= jp:
                inc = jnp.where(other > sg, 1.0, 0.0)
            else:
                inc = jnp.where(other > sg, 1.0, jnp.where((other == sg) & (sub > jp - first), 1.0, 0.0))
            ranks[g] = ranks[g] + inc
    chosen_flag = neg_offset * (1.0 / NEG_BIG)
    block_flags = jnp.where(jnp.concatenate(ranks, axis=0) < top_n, chosen_flag, 1.0)
    flag_rows = [jnp.zeros((HEAD_DIM, tq), F32), block_flags]
    if HEAD_DIM + n_sel < LANES:
        flag_rows.append(jnp.zeros((LANES - HEAD_DIM - n_sel, tq), F32))
    flags = jnp.transpose(jnp.concatenate(flag_rows, axis=0))
    q_sel = (q2.reshape(Q_PER_KV, tq, LANES) + flags.astype(BF16)[None]).reshape(rows, LANES)

    def sel_chunk(k0, carry, bias):
        s = _dot_nt(q_sel, ks_ref[0, pl.ds(k0, kc_len), :])
        if bias is not None:
            s = add_bias(s, bias)
        v = vs_ref[0, pl.ds(k0, kc_len), :]
        if bounded:
            (acc,) = carry
            return (acc + _dot(jnp.exp2(s).astype(BF16), v),)
        m, acc = carry
        m_new = jnp.maximum(m, jnp.max(s, axis=-1, keepdims=True))
        p = jnp.exp2((s - m_new).astype(BF16))
        return m_new, jnp.exp2(m - m_new) * acc + _dot(p, v)

    n_full = t0 // kc_len
    init = (jnp.zeros((rows, LANES), F32),)
    if not bounded:
        init = (jnp.full((rows, 1), -jnp.inf, F32),) + init
    carry = lax.fori_loop(
        0, n_full, lambda c, cr: sel_chunk(pl.multiple_of(c * kc_len, kc_len), cr, None), init)
    kd = pl.multiple_of(n_full * kc_len, kc_len)
    key_d = kd + lax.broadcasted_iota(jnp.int32, (1, kc_len), 1)
    acc_s = sel_chunk(kd, carry, jnp.where(key_d <= t_tok, 0.0, NEG_BIG))[-1]
    o_s = acc_s * (1.0 / denominator(acc_s))

    gates = gate_ref[...]

    def gate_col(br):
        cols = [gates[:, g * N_BRANCH + br:g * N_BRANCH + br + 1] for g in range(Q_PER_KV)]
        return jnp.concatenate(cols, axis=0)

    o = gate_col(0) * o_c + gate_col(1) * o_s + gate_col(2) * o_w
    o3 = o.reshape(Q_PER_KV, tq, LANES)
    lane = lax.broadcasted_iota(jnp.int32, (tq, LANES), 1)
    pairs = [jnp.where(lane < HEAD_DIM, o3[g], pltpu.roll(o3[g + 1], HEAD_DIM, 1))
             for g in range(0, Q_PER_KV, HEADS_PER_VREG)]
    o_ref[...] = jnp.concatenate(pairs, axis=-1).astype(o_ref.dtype)


def _attention(score_bound, q, kcmp, vcmp, ks, vs, kw, vw, gates, selmap_t, b, seq):
    t = b * seq
    tq = min(TQ, seq)
    kc_len = min(KC, seq)
    nq = seq // tq
    ncp = kcmp.shape[2]
    n_sel = seq // SEL_LEN
    assert HEAD_DIM + n_sel <= LANES, "selection-block flags must fit beside the head dims"
    top_n = min(SEL_TOPK, n_sel)
    cmp_spec = pl.BlockSpec((1, 1, ncp, LANES), lambda bi, kh, i: (bi, kh, 0, 0))
    seq_spec = pl.BlockSpec((1, seq, LANES), lambda bi, kh, i: (kh, bi, 0))

    def run(bounded):
        return pl.pallas_call(
            functools.partial(_attn_kernel, seq=seq, tq=tq, kc_len=kc_len, top_n=top_n, bounded=bounded),
            grid=(b, N_KV_HEADS, nq),
            in_specs=[
                pl.BlockSpec(memory_space=pltpu.SMEM),
                pl.BlockSpec((Q_PER_KV, tq, LANES), lambda bi, kh, i: (kh, bi * nq + i, 0)),
                cmp_spec, cmp_spec, seq_spec, seq_spec, seq_spec, seq_spec,
                pl.BlockSpec((tq, LANES), lambda bi, kh, i: (bi * nq + i, kh)),
                pl.BlockSpec(selmap_t.shape, lambda bi, kh, i: (0, 0)),
            ],
            out_specs=pl.BlockSpec((tq, Q_PER_KV * HEAD_DIM), lambda bi, kh, i: (bi * nq + i, kh)),
            out_shape=jax.ShapeDtypeStruct((t, N_HEADS * HEAD_DIM), BF16),
            compiler_params=_cparams(("parallel", "parallel", "arbitrary")),
            name="nsa_attention" if bounded else "nsa_attention_running_max",
        )(score_bound, q, kcmp, vcmp, ks, vs, kw, vw, gates, selmap_t)

    return lax.cond(score_bound[0] < MAX_SCORE_BOUND, lambda: run(True), lambda: run(False))


def _top2_gates(logits, n_experts):
    lane = lax.broadcasted_iota(jnp.int32, logits.shape, 1)
    x = jnp.where(lane < n_experts, logits, -jnp.inf)
    m1 = jnp.max(x, axis=-1, keepdims=True)
    i1 = jnp.min(jnp.where(x == m1, lane, LANES), axis=-1, keepdims=True)
    x2 = jnp.where(lane == i1, -jnp.inf, x)
    m2 = jnp.max(x2, axis=-1, keepdims=True)
    i2 = jnp.min(jnp.where(x2 == m2, lane, LANES), axis=-1, keepdims=True)
    e2 = jnp.exp(m2 - m1)
    inv = 1.0 / (1.0 + e2)
    return jnp.where(lane == i1, inv, jnp.where(lane == i2, e2 * inv, 0.0))


def _pack_bf16_halves(x):
    w = x.shape[1] // 2
    bits = lax.bitcast_convert_type(x.astype(BF16).astype(F32), jnp.uint32)
    return (bits[:, :w] >> 16) | (bits[:, w:] & jnp.uint32(0xFFFF0000))


def _unpack_bf16_halves(p):
    lo = lax.bitcast_convert_type(p << 16, F32)
    hi = lax.bitcast_convert_type(p & jnp.uint32(0xFFFF0000), F32)
    return jnp.concatenate([lo, hi], axis=-1).astype(BF16)


def _out_proj_kernel(*refs, n_experts):
    if n_experts:
        attn_ref, conv_ref, wo_ref, x_ref, g_ref, rt_ref, xo_ref, h_ref, gate_ref = refs
    else:
        attn_ref, conv_ref, wo_ref, x_ref, g_ref, xo_ref, h_ref = refs
    aw = attn_ref.shape[1]
    x = x_ref[...] + _dot(attn_ref[...], wo_ref[0:aw, :]) + _dot(conv_ref[...], wo_ref[aw:, :])
    xo_ref[...] = x
    h = _rms_rows(x, g_ref[...])
    if not n_experts:
        h_ref[...] = h.astype(h_ref.dtype)
    else:
        h_ref[...] = _pack_bf16_halves(h)
        h_hi, h_lo = _split_bf16(h)
        router = rt_ref[...]
        by_hi = _dot(h_hi, router)
        logits = by_hi[:, :LANES] + (by_hi[:, LANES:] + _dot(h_lo, router[:, :LANES]))
        gate_ref[...] = _top2_gates(logits, n_experts)


def _out_proj(attn, conv, wo, x2, g, router_split=None, n_experts=0):
    t, d = x2.shape
    tm = min(TM_PROJ, t)
    row = lambda i: (i, 0)
    const = lambda i: (0, 0)
    in_specs = [pl.BlockSpec((tm, attn.shape[1]), row), pl.BlockSpec((tm, conv.shape[1]), row),
                pl.BlockSpec(wo.shape, const), pl.BlockSpec((tm, d), row), pl.BlockSpec((1, d), const)]
    out_shape = [jax.ShapeDtypeStruct((t, d), F32), jax.ShapeDtypeStruct((t, d), BF16)]
    out_specs = [pl.BlockSpec((tm, d), row), pl.BlockSpec((tm, d), row)]
    args = [attn, conv, wo, x2, g]
    if n_experts:
        out_shape[1] = jax.ShapeDtypeStruct((t, d // 2), jnp.uint32)
        out_specs[1] = pl.BlockSpec((tm, d // 2), row)
        in_specs.append(pl.BlockSpec(router_split.shape, const))
        out_shape.append(jax.ShapeDtypeStruct((t, LANES), F32))
        out_specs.append(pl.BlockSpec((tm, LANES), row))
        args.append(router_split)
    return pl.pallas_call(
        functools.partial(_out_proj_kernel, n_experts=n_experts),
        grid=(t // tm,), in_specs=in_specs, out_specs=out_specs, out_shape=out_shape,
        compiler_params=_cparams(("parallel",)), name="out_proj",
    )(*args)


def _swiglu_tiles(h, acc, wg, wu, wd, dff, tf):
    for f in range(dff // tf):
        cols = slice(f * tf, (f + 1) * tf)
        a = _dot(h, wg(cols))
        u = _dot(h, wu(cols))
        acc = acc + _dot(((a * jax.nn.sigmoid(a)) * u).astype(BF16), wd(cols))
    return acc


def _ffn_kernel(h_ref, x_ref, wg_ref, wu_ref, wd_ref, o_ref, *, tf):
    o_ref[...] = _swiglu_tiles(h_ref[...], x_ref[...], lambda c: wg_ref[:, c], lambda c: wu_ref[:, c],
                               lambda c: wd_ref[c, :], wg_ref.shape[1], tf)


def _ffn(h, x2, wg, wu, wd):
    t, d = x2.shape
    tm = min(TM_FFN, t)
    row = lambda i: (i, 0)
    resident = lambda w: pl.BlockSpec(w.shape, lambda i: (0, 0), pipeline_mode=pl.Buffered(1))
    return pl.pallas_call(
        functools.partial(_ffn_kernel, tf=TF_FFN), grid=(t // tm,),
        in_specs=[pl.BlockSpec((tm, d), row), pl.BlockSpec((tm, d), row),
                  resident(wg), resident(wu), resident(wd)],
        out_specs=pl.BlockSpec((tm, d), row),
        out_shape=jax.ShapeDtypeStruct((t, d), F32),
        compiler_params=_cparams(("parallel",)), name="ffn",
    )(h, x2, wg, wu, wd)


def _route_scan_kernel(g_ref, pos_ref, tot_ref, carry_ref):
    c = pl.program_id(0)

    @pl.when(c == 0)
    def _():
        carry_ref[...] = jnp.zeros_like(carry_ref)

    ct = g_ref.shape[0]
    routed = g_ref[...] > 0.0
    a = jnp.where(routed, 1.0, 0.0)
    earlier = lax.broadcasted_iota(jnp.int32, (ct, ct), 1) < lax.broadcasted_iota(jnp.int32, (ct, ct), 0)
    base = carry_ref[...]
    pos = jnp.where(routed, _dot(jnp.where(earlier, 1.0, 0.0).astype(BF16), a.astype(BF16)) + base, -1.0)
    pos_ref[...] = pos
    total = base + jnp.sum(a, axis=0, keepdims=True)
    carry_ref[...] = total
    tot_ref[...] = total


def _route_scan(gates):
    t = gates.shape[0]
    ct = CT_MOE
    nch = t // ct
    return pl.pallas_call(
        _route_scan_kernel, grid=(nch,),
        in_specs=[pl.BlockSpec((ct, LANES), lambda c: (c, 0))],
        out_specs=[pl.BlockSpec((ct, LANES), lambda c: (c, 0)),
                   pl.BlockSpec((1, LANES), lambda c: (0, 0))],
        out_shape=[jax.ShapeDtypeStruct((t, LANES), F32),
                   jax.ShapeDtypeStruct((1, LANES), F32)],
        scratch_shapes=[pltpu.VMEM((1, LANES), F32)],
        compiler_params=_cparams(("arbitrary",)), name="route_scan",
    )(gates)


def _sc_scatter_rows(rows, idx_a, idx_b, n_slots):
    t, w = rows.shape
    mesh = plsc.VectorSubcoreMesh(core_axis_name="core", subcore_axis_name="subcore")

    @pl.kernel(out_type=jax.ShapeDtypeStruct((n_slots, w), rows.dtype), mesh=mesh, scratch_types=[])
    def scatter(x_hbm, ia_hbm, ib_hbm, o_hbm):
        def body(x_vmem, ia_vmem, ib_vmem):
            pltpu.sync_copy(x_vmem, o_hbm.at[ia_vmem.at[0]])
            pltpu.sync_copy(x_vmem, o_hbm.at[ib_vmem.at[0]])

        pltpu.emit_pipeline(
            body, grid=(t // SC_WINDOW,),
            in_specs=[pl.BlockSpec((SC_WINDOW, w), lambda i: (i, 0)),
                      pl.BlockSpec((1, SC_WINDOW), lambda i: (i, 0)),
                      pl.BlockSpec((1, SC_WINDOW), lambda i: (i, 0))],
            out_specs=[], core_axis_name=("core", "subcore"),
            dimension_semantics=(pltpu.PARALLEL,))(x_hbm, ia_hbm, ib_hbm)

    return scatter(rows, idx_a, idx_b)


def _moe_ffn_kernel(exp_ref, rows_ref, xs_ref, wg_ref, wu_ref, wd_ref, ys_ref, *, tf):
    j = pl.program_id(0)
    n_rows = rows_ref[j]

    @pl.when(n_rows > 0)
    def _():
        xs = _unpack_bf16_halves(xs_ref[...])
        row = lax.broadcasted_iota(jnp.int32, xs.shape, 0)
        xs = jnp.where(row < n_rows, xs, jnp.zeros_like(xs))
        zero = jnp.zeros(xs.shape, F32)
        ys = _swiglu_tiles(xs, zero, lambda c: wg_ref[0, :, c], lambda c: wu_ref[0, :, c],
                           lambda c: wd_ref[0, c, :], wg_ref.shape[2], tf)
        ys_ref[...] = _pack_bf16_halves(ys)

    @pl.when(n_rows == 0)
    def _():
        ys_ref[...] = jnp.zeros_like(ys_ref)


def _moe_ffn(xs, blk_expert, blk_rows, wg, wu, wd):
    n_slots = xs.shape[0]
    d = wg.shape[1]
    expert = lambda w: pl.BlockSpec((1,) + w.shape[1:], lambda j, e, v: (e[j], 0, 0),
                                    pipeline_mode=pl.Buffered(1))
    grid_spec = pltpu.PrefetchScalarGridSpec(
        num_scalar_prefetch=2, grid=(n_slots // BM_MOE,),
        in_specs=[pl.BlockSpec((BM_MOE, d // 2), lambda j, e, v: (j, 0)), expert(wg), expert(wu), expert(wd)],
        out_specs=pl.BlockSpec((BM_MOE, d // 2), lambda j, e, v: (j, 0)))
    return pl.pallas_call(
        functools.partial(_moe_ffn_kernel, tf=TF_FFN), grid_spec=grid_spec,
        out_shape=jax.ShapeDtypeStruct((n_slots, d // 2), jnp.uint32),
        compiler_params=_cparams(("arbitrary",)), name="moe_ffn",
    )(blk_expert, blk_rows, xs, wg, wu, wd)


def _sc_gather_rows(table, idx):
    w = table.shape[1]
    t = idx.shape[0] * SC_WINDOW
    mesh = plsc.VectorSubcoreMesh(core_axis_name="core", subcore_axis_name="subcore")

    @pl.kernel(out_type=jax.ShapeDtypeStruct((t, w), table.dtype), mesh=mesh, scratch_types=[])
    def gather(x_hbm, i_hbm, o_hbm):
        def body(i_vmem, o_vmem):
            pltpu.sync_copy(x_hbm.at[i_vmem.at[0]], o_vmem)

        pltpu.emit_pipeline(
            body, grid=(t // SC_WINDOW,),
            in_specs=[pl.BlockSpec((1, SC_WINDOW), lambda i: (i, 0))],
            out_specs=[pl.BlockSpec((SC_WINDOW, w), lambda i: (i, 0))],
            core_axis_name=("core", "subcore"),
            dimension_semantics=(pltpu.PARALLEL,))(i_hbm, o_hbm)

    return gather(table, idx)


def _token_slots(pos, pstart_row, unrouted):
    routed = pos >= 0.0
    slot = pos + pstart_row
    return (jnp.min(jnp.where(routed, slot, unrouted), axis=-1, keepdims=True),
            jnp.max(jnp.where(routed, slot, -1.0), axis=-1, keepdims=True), slot, routed)


def _moe_combine_kernel(x_ref, pos_ref, g_ref, pstart_ref, ya_ref, yb_ref, o_ref, *, n_slots):
    slot_a, slot_b, slot, routed = _token_slots(pos_ref[...], pstart_ref[...], float(n_slots))
    gates = g_ref[...]
    gate_a = jnp.sum(jnp.where(routed & (slot == slot_a), gates, 0.0), axis=-1, keepdims=True)
    gate_b = jnp.sum(jnp.where(routed & (slot == slot_b), gates, 0.0), axis=-1, keepdims=True)
    gate_b = jnp.where(slot_b == slot_a, 0.0, gate_b)

    def rows_f32(ref):
        p = ref[...]
        return jnp.concatenate([lax.bitcast_convert_type(p << 16, F32),
                                lax.bitcast_convert_type(p & jnp.uint32(0xFFFF0000), F32)], axis=-1)

    o_ref[...] = x_ref[...] + gate_a * rows_f32(ya_ref) + gate_b * rows_f32(yb_ref)


def _moe_combine(x2, pos, gates, pstart_row, ya, yb, n_slots):
    t, d = x2.shape
    tm = min(CT_MOE, t)
    row = lambda i: (i, 0)
    return pl.pallas_call(
        functools.partial(_moe_combine_kernel, n_slots=n_slots), grid=(t // tm,),
        in_specs=[pl.BlockSpec((tm, d), row), pl.BlockSpec((tm, LANES), row), pl.BlockSpec((tm, LANES), row),
                  pl.BlockSpec((1, LANES), lambda i: (0, 0)),
                  pl.BlockSpec((tm, d // 2), row), pl.BlockSpec((tm, d // 2), row)],
        out_specs=pl.BlockSpec((tm, d), row),
        out_shape=jax.ShapeDtypeStruct((t, d), F32),
        compiler_params=_cparams(("parallel",)), name="moe_combine",
    )(x2, pos, gates, pstart_row, ya, yb)


def _count_le(ascending, x):
    return jnp.sum(ascending[None, :] <= x[:, None], axis=1).astype(jnp.int32)


def _moe_routed(h, x2, gates, wg, wu, wd):
    t, d = x2.shape
    n_e = wg.shape[0]
    n_slots = t * TOP_K + n_e * BM_MOE
    pos, tot = _route_scan(gates)

    counts = tot[0, :n_e].astype(jnp.int32)
    padded = (counts + BM_MOE - 1) // BM_MOE * BM_MOE
    pend = jnp.cumsum(padded)
    pstart = pend - padded

    pstart_row = jnp.zeros((1, LANES), F32).at[0, :n_e].set(pstart.astype(F32))
    slot_a, slot_b, _, _ = _token_slots(pos, pstart_row, float(n_slots))
    slot_a = slot_a.astype(jnp.int32).reshape(-1, SC_WINDOW)
    slot_b = slot_b.astype(jnp.int32).reshape(-1, SC_WINDOW)
    xs = _sc_scatter_rows(h, slot_a, slot_b, n_slots)

    mb = jnp.arange(n_slots // BM_MOE, dtype=jnp.int32) * BM_MOE
    mb_e = jnp.minimum(_count_le(pend, mb), n_e - 1)
    mb_rows = jnp.where(mb < pend[-1], jnp.clip(counts[mb_e] - (mb - pstart[mb_e]), 0, BM_MOE), 0)
    ys = _moe_ffn(xs, mb_e, mb_rows.astype(jnp.int32), wg, wu, wd)

    return _moe_combine(x2, pos, gates, pstart_row, _sc_gather_rows(ys, slot_a), _sc_gather_rows(ys, slot_b),
                        n_slots)


def _rope_tables(pos):
    half = ROPE_DIM // 2
    inv_freq = ROPE_THETA ** (-2.0 * jnp.arange(half, dtype=F32) / ROPE_DIM)
    ang = pos.astype(F32).reshape(-1, 1) * inv_freq
    cos, sin = jnp.cos(ang), jnp.sin(ang)
    n = ang.shape[0]
    rest = HEAD_DIM - ROPE_DIM
    cos_h = jnp.concatenate([cos, cos, jnp.ones((n, rest), F32)], axis=-1)
    sin_h = jnp.concatenate([-sin, sin, jnp.zeros((n, rest), F32)], axis=-1)
    return jnp.tile(cos_h, (1, HEADS_PER_VREG)), jnp.tile(sin_h, (1, HEADS_PER_VREG))


def _permute_w_in(w, conv_w):
    d = w.shape[0]
    kv_end = Q_W + 6 * KV_W
    g = w[:, kv_end:kv_end + N_HEADS * N_BRANCH]
    pad = jnp.zeros((d, LANES - GATES_PER_KV), w.dtype)
    gate_cols = []
    for kh in range(N_KV_HEADS):
        gate_cols += [g[:, kh * GATES_PER_KV:(kh + 1) * GATES_PER_KV], pad]
    u = w[:, kv_end + N_HEADS * N_BRANCH:]
    return jnp.concatenate([w[:, :kv_end]] + gate_cols + [u], axis=1).astype(BF16)


def _compress_weights(pos_emb, w1, w2):
    hidden = w1.shape[1]
    eye = jnp.eye(N_KV_HEADS, dtype=w1.dtype)
    w1r = w1.reshape(CMP_LEN, HEAD_DIM, hidden)
    halves = []
    for part in (w1r[:CMP_STRIDE], w1r[CMP_STRIDE:]):
        full = jnp.einsum('ldj,hg->lhdgj', part, eye)
        halves.append(full.reshape(CMP_STRIDE * N_KV_HEADS * HEAD_DIM, N_KV_HEADS * hidden).astype(BF16))
    w2p = jnp.einsum('jd,hg->hjgd', w2, eye).reshape(N_KV_HEADS * hidden, N_KV_HEADS * HEAD_DIM).astype(BF16)
    pos = []
    for part in (pos_emb[:CMP_STRIDE], pos_emb[CMP_STRIDE:]):
        pos.append(jnp.broadcast_to(part[:, None, :], (CMP_STRIDE, N_KV_HEADS, HEAD_DIM)).reshape(1, -1))
    return [halves[0], halves[1], w2p, pos[0], pos[1]]


def _selection_map_t(seq):
    ncp = seq // CMP_STRIDE
    n_cmp = (seq - CMP_LEN) // CMP_STRIDE + 1
    c0 = np.arange(ncp) * CMP_STRIDE
    s0 = np.arange(seq // SEL_LEN) * SEL_LEN
    ov = np.minimum(c0[None, :] + CMP_LEN, s0[:, None] + SEL_LEN) - np.maximum(c0[None, :], s0[:, None])
    m = np.clip(ov, 0, None) / CMP_LEN
    m[:, n_cmp:] = 0.0
    return jnp.asarray(m, dtype=BF16)


def kernel(x, positions, attn_norm_g, ffn_norm_g, w_in, w_out, q_norm_g, k_norm_g, cmp_pos_k, cmp_w1_k, cmp_w2_k, cmp_pos_v, cmp_w1_v, cmp_w2_v, conv_w, conv_b, conv_ln_g, conv_ln_b, ffn_w_gate, ffn_w_up, ffn_w_down, moe_router, moe_w_gate, moe_w_up, moe_w_down):
    b, seq, d = x.shape
    t = b * seq
    depth = w_in.shape[0]
    cw = conv_w.shape[2]
    ncp = seq // CMP_STRIDE
    n_cmp = (seq - CMP_LEN) // CMP_STRIDE + 1
    assert seq % max(TQ, KC, TM_PROJ) == 0 and seq >= WINDOW + TQ

    cos_t, sin_t = _rope_tables(positions)
    cmp_end = np.minimum(np.arange(ncp) * CMP_STRIDE + CMP_LEN - 1, seq - 1)
    cos_c, sin_c = _rope_tables(positions[:, cmp_end])
    selmap_t = _selection_map_t(seq)
    tile2 = lambda v: jnp.tile(v.reshape(1, HEAD_DIM), (1, HEADS_PER_VREG))

    x2 = x.reshape(t, d)
    for layer in range(depth):
        w_perm = _permute_w_in(w_in[layer], cw)
        conv_params = (conv_w[layer], conv_b[layer].reshape(1, cw), conv_ln_g[layer].reshape(1, cw),
                       conv_ln_b[layer].reshape(1, cw))
        q, kc, vc, ks, vs, kw, vw, gates, conv = _in_proj(
            x2, attn_norm_g[layer].reshape(1, d), w_perm, cos_t, sin_t,
            tile2(q_norm_g[layer]), tile2(k_norm_g[layer, 1]), tile2(k_norm_g[layer, 2]), conv_params, seq)
        kcmp, vcmp = _compress(
            kc, vc,
            _compress_weights(cmp_pos_k[layer], cmp_w1_k[layer], cmp_w2_k[layer]),
            _compress_weights(cmp_pos_v[layer], cmp_w1_v[layer], cmp_w2_v[layer]),
            tile2(k_norm_g[layer, 0]), cos_c, sin_c, b, ncp)
        score_bound = (HEAD_DIM ** 0.5 * LOG2_E * SCORE_BOUND_MARGIN * jnp.max(jnp.abs(q_norm_g[layer]))
                       * jnp.max(jnp.abs(k_norm_g[layer]))).astype(F32).reshape(1)
        attn = _attention(score_bound, q, kcmp, vcmp, ks, vs, kw, vw, gates, selmap_t, b, seq)
        wo = w_out[layer].astype(BF16)
        g2 = ffn_norm_g[layer].reshape(1, d)
        i = layer // 2
        if layer % 2 == 0:
            x2, h = _out_proj(attn, conv, wo, x2, g2)
            x2 = _ffn(h, x2, ffn_w_gate[i].astype(BF16), ffn_w_up[i].astype(BF16), ffn_w_down[i].astype(BF16))
        else:
            n_e = moe_router.shape[2]
            r = jnp.pad(moe_router[i], ((0, 0), (0, LANES - n_e)))
            r_hi = r.astype(BF16)
            r_lo = (r - r_hi.astype(F32)).astype(BF16)
            x2, h, route = _out_proj(attn, conv, wo, x2, g2, jnp.concatenate([r_hi, r_lo], axis=1), n_e)
            x2 = _moe_routed(h, x2, route, moe_w_gate[i].astype(BF16), moe_w_up[i].astype(BF16),
                             moe_w_down[i].astype(BF16))
    return x2.reshape(b, seq, d)
```

```python
import functools
import math

import jax
import jax.numpy as jnp
import numpy as np
from jax import lax
from jax.experimental import pallas as pl
from jax.experimental.pallas import tpu as pltpu
from jax.experimental.pallas import tpu_sc as plsc

F32 = jnp.float32
BF16 = jnp.bfloat16

N_HEADS = 8
N_KV_HEADS = 2
Q_PER_KV = N_HEADS // N_KV_HEADS
HEAD_DIM = 64
N_BRANCH = 3
CMP_LEN = 32
CMP_STRIDE = 16
SEL_LEN = 64
SEL_TOPK = 16
N_LOCAL_SEL = 2
WINDOW = 512
CONV_KERNEL = 31
ROPE_THETA = 500000.0
ROPE_DIM = HEAD_DIM // 4
TOP_K = 2
EPS = 1e-6

LANES = 128
SUBLANES = 8
LOG2_E = math.log2(math.e)
NEG_BIG = -(2.0 ** 100)
MAX_SCORE_BOUND = 50.0
SCORE_BOUND_MARGIN = 1.02
HEADS_PER_VREG = LANES // HEAD_DIM
VMEM_LIMIT = 56 * 1024 * 1024

TM_PROJ = 512
TQ = 256
KC = 512
CH_CONV = 32
HALO = 32
TM_FFN = 1024
TF_FFN = 512
CT_MOE = 1024
BM_MOE = 512
SC_WINDOW = 64


def _cparams(sem):
    return pltpu.CompilerParams(dimension_semantics=sem, vmem_limit_bytes=VMEM_LIMIT)


def _dot(a, b):
    return jnp.dot(a, b, preferred_element_type=F32)


def _dot_nt(a, b):
    return lax.dot_general(a, b, (((1,), (1,)), ((), ())), preferred_element_type=F32)


def _split_bf16(x):
    hi = x.astype(BF16)
    lo = (x - hi.astype(F32)).astype(BF16)
    return hi, lo


def _rms_rows(x, g):
    ms = jnp.mean(x * x, axis=-1, keepdims=True)
    return x * lax.rsqrt(ms + EPS) * g


def _head_block_ones(width):
    r = lax.broadcasted_iota(jnp.int32, (width, width), 0) // HEAD_DIM
    c = lax.broadcasted_iota(jnp.int32, (width, width), 1) // HEAD_DIM
    return jnp.where(r == c, 1.0, 0.0).astype(BF16)


def _norm_rope(xg, ms, gain, cos, sin):
    y = xg * lax.rsqrt(ms + EPS) * gain
    lane = lax.broadcasted_iota(jnp.int32, y.shape, 1) % HEAD_DIM
    half = ROPE_DIM // 2
    partner = jnp.where(lane < half, pltpu.roll(y, LANES - half, 1), pltpu.roll(y, half, 1))
    return y * cos + partner * sin


def _head_norm_rope(xg, gain, cos, sin, ones_bd):
    ms = _dot((xg * xg).astype(BF16), ones_bd) * (1.0 / HEAD_DIM)
    return _norm_rope(xg, ms, gain, cos, sin)


def _head_norm_rope_pair(xa, xb, gain_a, gain_b, cos, sin, ones_bd2):
    sq = jnp.concatenate([xa * xa, xb * xb], axis=-1).astype(BF16)
    ms = _dot(sq, ones_bd2) * (1.0 / HEAD_DIM)
    return (_norm_rope(xa, ms[:, :LANES], gain_a, cos, sin),
            _norm_rope(xb, ms[:, LANES:], gain_b, cos, sin))


Q_W = N_HEADS * HEAD_DIM
KV_W = N_KV_HEADS * HEAD_DIM
SEG_Q = 0
SEG_KV = Q_W
SEG_GATE = SEG_KV + 6 * KV_W
SEG_UA = SEG_GATE + N_KV_HEADS * LANES
GATES_PER_KV = Q_PER_KV * N_BRANCH


def _causal_conv_tile(glu, seq_start, w_ref, b_ref, lg_ref, lb_ref, o_ref, ext_ref, shift_ref):
    ts = glu.shape[0]
    ext_ref[0:HALO, :] = jnp.where(seq_start, 0.0, ext_ref[ts:ts + HALO, :])
    ext_ref[HALO:HALO + ts, :] = glu
    n_shift = shift_ref.shape[1]
    for r in range(1, SUBLANES):
        shift_ref[r - 1] = ext_ref[r:r + n_shift, :]

    def rows_from(o):
        r = o % SUBLANES
        if r == 0:
            return ext_ref[o:o + CH_CONV, :]
        return shift_ref[r - 1, o - r:o - r + CH_CONV, :]

    w = w_ref[...]
    first_tap = HALO - (CONV_KERNEL - 1)
    for c in range(ts // CH_CONV):
        base = c * CH_CONV + first_tap
        acc = w[0:1, :] * rows_from(base)
        for k in range(1, CONV_KERNEL):
            acc = acc + w[k:k + 1, :] * rows_from(base + k)
        y = acc + b_ref[...]
        yc = y - jnp.mean(y, axis=-1, keepdims=True)
        yn = yc * lax.rsqrt(jnp.mean(yc * yc, axis=-1, keepdims=True) + EPS)
        z = yn * lg_ref[...] + lb_ref[...]
        o_ref[c * CH_CONV:(c + 1) * CH_CONV, :] = (z * jax.nn.sigmoid(z)).astype(o_ref.dtype)


def _in_proj_kernel(x_ref, g_ref, w_ref, cos_ref, sin_ref, qg_ref, ksg_ref, kwg_ref,
                    cw_ref, cb_ref, clg_ref, clb_ref,
                    q_ref, kc_ref, vc_ref, ks_ref, vs_ref, kw_ref, vw_ref, gate_ref, conv_ref,
                    glu_ref, ext_ref, shift_ref, *, conv_w, seq, n_tiles):
    i = pl.program_id(0)
    tm = x_ref.shape[0]

    @pl.when(i == 0)
    def _():
        glu_ref[...] = jnp.zeros_like(glu_ref)
        ext_ref[...] = jnp.zeros_like(ext_ref)

    _causal_conv_tile(glu_ref[...], lax.rem((i - 1) * tm, seq) == 0, cw_ref, cb_ref, clg_ref, clb_ref,
                      conv_ref, ext_ref, shift_ref)

    h = _rms_rows(x_ref[...], g_ref[...]).astype(BF16)
    cos = cos_ref[...]
    sin = sin_ref[...]
    ones_bd = _head_block_ones(2 * LANES)
    lane = lax.broadcasted_iota(jnp.int32, (tm, LANES), 1)
    tile_start = lax.rem(jnp.minimum(i, n_tiles - 1) * tm, seq)
    tok = tile_start + lax.broadcasted_iota(jnp.int32, (tm, 1), 0)
    block_aug = jnp.where(lane - HEAD_DIM == tok // SEL_LEN, NEG_BIG, 0.0)
    ones_aug = jnp.where(lane == HEAD_DIM, 1.0, 0.0)

    def put_heads(ref, first, val, aug):
        for j in range(HEADS_PER_VREG):
            head = val if j == 0 else pltpu.roll(val, LANES - j * HEAD_DIM, 1)
            ref[first + j] = jnp.where(lane < HEAD_DIM, head, aug).astype(ref.dtype)

    qkv = _dot(h, w_ref[:, SEG_Q:SEG_GATE])
    group = lambda j: qkv[:, j * LANES:(j + 1) * LANES]
    first_kv = Q_W // LANES
    scale = HEAD_DIM ** -0.5 * LOG2_E
    qg = qg_ref[...]
    for c in range(0, first_kv, 2):
        pair = _head_norm_rope_pair(group(c), group(c + 1), qg, qg, cos, sin, ones_bd)
        for j, y in enumerate(pair):
            put_heads(q_ref, (c + j) * HEADS_PER_VREG, y * scale, 0.0)
    kc_ref[...] = group(first_kv)
    vc_ref[...] = group(first_kv + 1)
    k_sel, k_win = _head_norm_rope_pair(group(first_kv + 2), group(first_kv + 4), ksg_ref[...], kwg_ref[...],
                                        cos, sin, ones_bd)
    put_heads(ks_ref, 0, k_sel, block_aug)
    put_heads(vs_ref, 0, group(first_kv + 3), ones_aug)
    put_heads(kw_ref, 0, k_win, 0.0)
    put_heads(vw_ref, 0, group(first_kv + 5), ones_aug)
    gate_ref[...] = jax.nn.sigmoid(_dot(h, w_ref[:, SEG_GATE:SEG_UA]))
    u = _dot(h, w_ref[:, SEG_UA:])
    glu_ref[...] = u[:, :conv_w] * jax.nn.sigmoid(u[:, conv_w:])


def _in_proj(x2, g, w_perm, cos_t, sin_t, qg, ksg, kwg, conv_params, seq):
    t, d = x2.shape
    conv_w = (w_perm.shape[1] - SEG_UA) // 2
    tm = min(TM_PROJ, seq)
    n_tiles = t // tm
    row = lambda i: (jnp.minimum(i, n_tiles - 1), 0)
    const = lambda i: (0, 0)
    head_row = lambda i: (0, jnp.minimum(i, n_tiles - 1), 0)
    out_shape = [
        jax.ShapeDtypeStruct((N_HEADS, t, LANES), BF16),
        jax.ShapeDtypeStruct((t, KV_W), F32),
        jax.ShapeDtypeStruct((t, KV_W), F32),
        jax.ShapeDtypeStruct((N_KV_HEADS, t, LANES), BF16),
        jax.ShapeDtypeStruct((N_KV_HEADS, t, LANES), BF16),
        jax.ShapeDtypeStruct((N_KV_HEADS, t, LANES), BF16),
        jax.ShapeDtypeStruct((N_KV_HEADS, t, LANES), BF16),
        jax.ShapeDtypeStruct((t, N_KV_HEADS * LANES), F32),
        jax.ShapeDtypeStruct((t, conv_w), BF16),
    ]
    kv_spec = pl.BlockSpec((N_KV_HEADS, tm, LANES), head_row)
    out_specs = [
        pl.BlockSpec((N_HEADS, tm, LANES), head_row),
        pl.BlockSpec((tm, KV_W), row), pl.BlockSpec((tm, KV_W), row),
        kv_spec, kv_spec, kv_spec, kv_spec,
        pl.BlockSpec((tm, N_KV_HEADS * LANES), row),
        pl.BlockSpec((tm, conv_w), lambda i: (jnp.maximum(i - 1, 0), 0)),
    ]
    in_specs = [
        pl.BlockSpec((tm, d), row), pl.BlockSpec((1, d), const),
        pl.BlockSpec(w_perm.shape, const),
        pl.BlockSpec((tm, LANES), row), pl.BlockSpec((tm, LANES), row),
        pl.BlockSpec((1, LANES), const), pl.BlockSpec((1, LANES), const), pl.BlockSpec((1, LANES), const),
    ] + [pl.BlockSpec(p.shape, const) for p in conv_params]
    return pl.pallas_call(
        functools.partial(_in_proj_kernel, conv_w=conv_w, seq=seq, n_tiles=n_tiles),
        grid=(n_tiles + 1,), in_specs=in_specs, out_specs=out_specs, out_shape=out_shape,
        scratch_shapes=[pltpu.VMEM((tm, conv_w), F32),
                        pltpu.VMEM((tm + HALO, conv_w), F32),
                        pltpu.VMEM((SUBLANES - 1, tm + HALO - SUBLANES, conv_w), F32)],
        compiler_params=_cparams(("arbitrary",)), name="in_proj",
    )(x2, g, w_perm, cos_t, sin_t, qg, ksg, kwg, *conv_params)


def _gelu_tanh(x):
    c = math.sqrt(2.0 / math.pi)
    return 0.5 * x * (1.0 + jnp.tanh(c * (x + 0.044715 * (x * x * x))))


def _compress_kernel(k_ref, v_ref, w1ak_ref, w1bk_ref, w2k_ref, pak_ref, pbk_ref,
                     w1av_ref, w1bv_ref, w2v_ref, pav_ref, pbv_ref,
                     kg_ref, cos_ref, sin_ref, ko_ref, vo_ref):
    def mlp(x_ref, w1a_ref, w1b_ref, w2_ref, pa_ref, pb_ref):
        n = x_ref.shape[0] // CMP_STRIDE
        first = second = None
        for l in range(CMP_STRIDE):
            x = x_ref[pl.ds(l, n, stride=CMP_STRIDE), :]
            cols = slice(l * KV_W, (l + 1) * KV_W)
            fa = _dot((x + pa_ref[:, cols]).astype(BF16), w1a_ref[cols, :])
            fb = _dot((x + pb_ref[:, cols]).astype(BF16), w1b_ref[cols, :])
            first = fa if first is None else first + fa
            second = fb if second is None else second + fb
        hid = first + pltpu.roll(second, n - 1, 0)
        return _dot(_gelu_tanh(hid).astype(BF16), w2_ref[...])

    kc = mlp(k_ref, w1ak_ref, w1bk_ref, w2k_ref, pak_ref, pbk_ref)
    kc = _head_norm_rope(kc, kg_ref[...], cos_ref[...], sin_ref[...], _head_block_ones(LANES))
    vc = mlp(v_ref, w1av_ref, w1bv_ref, w2v_ref, pav_ref, pbv_ref)
    lane = lax.broadcasted_iota(jnp.int32, kc.shape, 1)
    for j in range(N_KV_HEADS):
        for val, ref in ((kc, ko_ref), (vc, vo_ref)):
            head = val if j == 0 else pltpu.roll(val, LANES - j * HEAD_DIM, 1)
            ref[0, j] = jnp.where(lane < HEAD_DIM, head, 0.0).astype(ref.dtype)


def _compress(kc, vc, wk, wv, kg, cosc, sinc, b, ncp):
    seq = kc.shape[0] // b
    const = lambda i: (0, 0)
    row = lambda i: (i, 0)

    def wspecs(ws):
        return [pl.BlockSpec(w.shape, const) for w in ws]

    out = jax.ShapeDtypeStruct((b, N_KV_HEADS, ncp, LANES), BF16)
    ospec = pl.BlockSpec((1, N_KV_HEADS, ncp, LANES), lambda i: (i, 0, 0, 0))
    return pl.pallas_call(
        _compress_kernel, grid=(b,),
        in_specs=[pl.BlockSpec((seq, KV_W), row), pl.BlockSpec((seq, KV_W), row)]
        + wspecs(wk) + wspecs(wv)
        + [pl.BlockSpec((1, LANES), const), pl.BlockSpec((ncp, LANES), row), pl.BlockSpec((ncp, LANES), row)],
        out_specs=[ospec, ospec], out_shape=[out, out],
        compiler_params=_cparams(("parallel",)), name="compress",
    )(kc, vc, *wk, *wv, kg, cosc, sinc)


def _attn_kernel(off_ref, q_ref, kc_ref, vc_ref, ks_ref, vs_ref, kw_ref, vw_ref, gate_ref, selmap_ref,
                 o_ref, *, seq, tq, kc_len, top_n, bounded):
    i = pl.program_id(2)
    t0 = i * tq
    rows = Q_PER_KV * tq
    n_sel = seq // SEL_LEN
    q2 = q_ref[...].reshape(rows, LANES)
    t_row = t0 + (lax.broadcasted_iota(jnp.int32, (rows, 1), 0) & (tq - 1))
    t_tok = t0 + lax.broadcasted_iota(jnp.int32, (tq, 1), 0)
    neg_offset = -off_ref[0] if bounded else 0.0

    def add_bias(s, bias):
        return (s.reshape(s.shape[0] // tq, tq, s.shape[1]) + bias[None]).reshape(s.shape)

    kcmp = kc_ref[0, 0]
    ncp = kcmp.shape[0]
    s_c = _dot_nt(q2, kcmp)
    cmp_end = lax.broadcasted_iota(jnp.int32, (1, ncp), 1) * CMP_STRIDE + (CMP_LEN - 1)
    if bounded:
        e_c = jnp.exp2(add_bias(s_c, jnp.where(cmp_end <= t_tok, neg_offset, NEG_BIG)))
    else:
        s_c = jnp.where(cmp_end <= t_row, s_c, -jnp.inf)
        m_c = jnp.max(s_c, axis=-1, keepdims=True)
        e_c = jnp.exp2(s_c - jnp.where(m_c == -jnp.inf, 0.0, m_c))
    p_c = e_c * (1.0 / jnp.maximum(jnp.sum(e_c, axis=-1, keepdims=True), jnp.finfo(F32).tiny))
    o_c = _dot(p_c.astype(BF16), vc_ref[0, 0])

    def denominator(acc):
        return acc[:, HEAD_DIM:HEAD_DIM + 1]

    span = min(WINDOW + tq, seq)
    w0 = pl.multiple_of(jnp.maximum(t0 - WINDOW, 0), tq)
    key_w = w0 + lax.broadcasted_iota(jnp.int32, (1, span), 1)
    bias_w = jnp.where((key_w <= t_tok) & (key_w > t_tok - WINDOW), neg_offset, NEG_BIG)
    s_w = add_bias(_dot_nt(q2, kw_ref[0, pl.ds(w0, span), :]), bias_w)
    if bounded:
        p_w = jnp.exp2(s_w).astype(BF16)
    else:
        p_w = jnp.exp2((s_w - jnp.max(s_w, axis=-1, keepdims=True)).astype(BF16))
    acc_w = _dot(p_w, vw_ref[0, pl.ds(w0, span), :])
    o_w = acc_w * (1.0 / denominator(acc_w))

    p_hi, p_lo = _split_bf16(jnp.sum(p_c.reshape(Q_PER_KV, tq, ncp), axis=0))
    selmap = selmap_ref[...]
    imp = _dot_nt(selmap, p_hi) + _dot_nt(selmap, p_lo)
    blk = lax.broadcasted_iota(jnp.int32, (n_sel, tq), 0)
    cur = (t0 + lax.broadcasted_iota(jnp.int32, (n_sel, tq), 1)) // SEL_LEN
    causal_blk = blk <= cur
    forced = (blk == 0) | (causal_blk & (blk > cur - N_LOCAL_SEL))
    score = jnp.where(forced, jnp.inf, jnp.where(causal_blk, imp, -jnp.inf))
    sub = lax.broadcasted_iota(jnp.int32, (SUBLANES, tq), 0)
    groups = [score[g * SUBLANES:(g + 1) * SUBLANES, :] for g in range(n_sel // SUBLANES)]
    ranks = [jnp.zeros((SUBLANES, tq), F32) for _ in groups]
    for jp in range(n_sel):
        other = jnp.broadcast_to(score[jp:jp + 1, :], (SUBLANES, tq))
        for g, sg in enumerate(groups):
            first = g * SUBLANES
            if first > jp:
                inc = jnp.where(other >= sg, 1.0, 0.0)
            elif first + SUBLANES - 1 <= jp:
                inc = jnp.where(other > sg, 1.0, 0.0)
            else:
                inc = jnp.where(other > sg, 1.0, jnp.where((other == sg) & (sub > jp - first), 1.0, 0.0))
            ranks[g] = ranks[g] + inc
    chosen_flag = neg_offset * (1.0 / NEG_BIG)
    block_flags = jnp.where(jnp.concatenate(ranks, axis=0) < top_n, chosen_flag, 1.0)
    flag_rows = [jnp.zeros((HEAD_DIM, tq), F32), block_flags]
    if HEAD_DIM + n_sel < LANES:
        flag_rows.append(jnp.zeros((LANES - HEAD_DIM - n_sel, tq), F32))
    flags = jnp.transpose(jnp.concatenate(flag_rows, axis=0))
    q_sel = (q2.reshape(Q_PER_KV, tq, LANES) + flags.astype(BF16)[None]).reshape(rows, LANES)

    def sel_chunk(k0, carry, bias):
        s = _dot_nt(q_sel, ks_ref[0, pl.ds(k0, kc_len), :])
        if bias is not None:
            s = add_bias(s, bias)
        v = vs_ref[0, pl.ds(k0, kc_len), :]
        if bounded:
            (acc,) = carry
            return (acc + _dot(jnp.exp2(s).astype(BF16), v),)
        m, acc = carry
        m_new = jnp.maximum(m, jnp.max(s, axis=-1, keepdims=True))
        p = jnp.exp2((s - m_new).astype(BF16))
        return m_new, jnp.exp2(m - m_new) * acc + _dot(p, v)

    n_full = t0 // kc_len
    init = (jnp.zeros((rows, LANES), F32),)
    if not bounded:
        init = (jnp.full((rows, 1), -jnp.inf, F32),) + init
    carry = lax.fori_loop(
        0, n_full, lambda c, cr: sel_chunk(pl.multiple_of(c * kc_len, kc_len), cr, None), init)
    kd = pl.multiple_of(n_full * kc_len, kc_len)
    key_d = kd + lax.broadcasted_iota(jnp.int32, (1, kc_len), 1)
    acc_s = sel_chunk(kd, carry, jnp.where(key_d <= t_tok, 0.0, NEG_BIG))[-1]
    o_s = acc_s * (1.0 / denominator(acc_s))

    gates = gate_ref[...]

    def gate_col(br):
        cols = [gates[:, g * N_BRANCH + br:g * N_BRANCH + br + 1] for g in range(Q_PER_KV)]
        return jnp.concatenate(cols, axis=0)

    o = gate_col(0) * o_c + gate_col(1) * o_s + gate_col(2) * o_w
    o3 = o.reshape(Q_PER_KV, tq, LANES)
    lane = lax.broadcasted_iota(jnp.int32, (tq, LANES), 1)
    pairs = [jnp.where(lane < HEAD_DIM, o3[g], pltpu.roll(o3[g + 1], HEAD_DIM, 1))
             for g in range(0, Q_PER_KV, HEADS_PER_VREG)]
    o_ref[...] = jnp.concatenate(pairs, axis=-1).astype(o_ref.dtype)


def _attention(score_bound, q, kcmp, vcmp, ks, vs, kw, vw, gates, selmap_t, b, seq):
    t = b * seq
    tq = min(TQ, seq)
    kc_len = min(KC, seq)
    nq = seq // tq
    ncp = kcmp.shape[2]
    n_sel = seq // SEL_LEN
    assert HEAD_DIM + n_sel <= LANES, "selection-block flags must fit beside the head dims"
    top_n = min(SEL_TOPK, n_sel)
    cmp_spec = pl.BlockSpec((1, 1, ncp, LANES), lambda bi, kh, i: (bi, kh, 0, 0))
    seq_spec = pl.BlockSpec((1, seq, LANES), lambda bi, kh, i: (kh, bi, 0))

    def run(bounded):
        return pl.pallas_call(
            functools.partial(_attn_kernel, seq=seq, tq=tq, kc_len=kc_len, top_n=top_n, bounded=bounded),
            grid=(b, N_KV_HEADS, nq),
            in_specs=[
                pl.BlockSpec(memory_space=pltpu.SMEM),
                pl.BlockSpec((Q_PER_KV, tq, LANES), lambda bi, kh, i: (kh, bi * nq + i, 0)),
                cmp_spec, cmp_spec, seq_spec, seq_spec, seq_spec, seq_spec,
                pl.BlockSpec((tq, LANES), lambda bi, kh, i: (bi * nq + i, kh)),
                pl.BlockSpec(selmap_t.shape, lambda bi, kh, i: (0, 0)),
            ],
            out_specs=pl.BlockSpec((tq, Q_PER_KV * HEAD_DIM), lambda bi, kh, i: (bi * nq + i, kh)),
            out_shape=jax.ShapeDtypeStruct((t, N_HEADS * HEAD_DIM), BF16),
            compiler_params=_cparams(("parallel", "parallel", "arbitrary")),
            name="nsa_attention" if bounded else "nsa_attention_running_max",
        )(score_bound, q, kcmp, vcmp, ks, vs, kw, vw, gates, selmap_t)

    return lax.cond(score_bound[0] < MAX_SCORE_BOUND, lambda: run(True), lambda: run(False))


def _top2_gates(logits, n_experts):
    lane = lax.broadcasted_iota(jnp.int32, logits.shape, 1)
    x = jnp.where(lane < n_experts, logits, -jnp.inf)
    m1 = jnp.max(x, axis=-1, keepdims=True)
    i1 = jnp.min(jnp.where(x == m1, lane, LANES), axis=-1, keepdims=True)
    x2 = jnp.where(lane == i1, -jnp.inf, x)
    m2 = jnp.max(x2, axis=-1, keepdims=True)
    i2 = jnp.min(jnp.where(x2 == m2, lane, LANES), axis=-1, keepdims=True)
    e2 = jnp.exp(m2 - m1)
    inv = 1.0 / (1.0 + e2)
    return jnp.where(lane == i1, inv, jnp.where(lane == i2, e2 * inv, 0.0))


def _pack_bf16_halves(x):
    w = x.shape[1] // 2
    bits = lax.bitcast_convert_type(x.astype(BF16).astype(F32), jnp.uint32)
    return (bits[:, :w] >> 16) | (bits[:, w:] & jnp.uint32(0xFFFF0000))


def _unpack_bf16_halves(p):
    lo = lax.bitcast_convert_type(p << 16, F32)
    hi = lax.bitcast_convert_type(p & jnp.uint32(0xFFFF0000), F32)
    return jnp.concatenate([lo, hi], axis=-1).astype(BF16)


def _out_proj_kernel(*refs, n_experts):
    if n_experts:
        attn_ref, conv_ref, wo_ref, x_ref, g_ref, rt_ref, xo_ref, h_ref, gate_ref = refs
    else:
        attn_ref, conv_ref, wo_ref, x_ref, g_ref, xo_ref, h_ref = refs
    aw = attn_ref.shape[1]
    x = x_ref[...] + _dot(attn_ref[...], wo_ref[0:aw, :]) + _dot(conv_ref[...], wo_ref[aw:, :])
    xo_ref[...] = x
    h = _rms_rows(x, g_ref[...])
    if not n_experts:
        h_ref[...] = h.astype(h_ref.dtype)
    else:
        h_ref[...] = _pack_bf16_halves(h)
        h_hi, h_lo = _split_bf16(h)
        router = rt_ref[...]
        by_hi = _dot(h_hi, router)
        logits = by_hi[:, :LANES] + (by_hi[:, LANES:] + _dot(h_lo, router[:, :LANES]))
        gate_ref[...] = _top2_gates(logits, n_experts)


def _out_proj(attn, conv, wo, x2, g, router_split=None, n_experts=0):
    t, d = x2.shape
    tm = min(TM_PROJ, t)
    row = lambda i: (i, 0)
    const = lambda i: (0, 0)
    in_specs = [pl.BlockSpec((tm, attn.shape[1]), row), pl.BlockSpec((tm, conv.shape[1]), row),
                pl.BlockSpec(wo.shape, const), pl.BlockSpec((tm, d), row), pl.BlockSpec((1, d), const)]
    out_shape = [jax.ShapeDtypeStruct((t, d), F32), jax.ShapeDtypeStruct((t, d), BF16)]
    out_specs = [pl.BlockSpec((tm, d), row), pl.BlockSpec((tm, d), row)]
    args = [attn, conv, wo, x2, g]
    if n_experts:
        out_shape[1] = jax.ShapeDtypeStruct((t, d // 2), jnp.uint32)
        out_specs[1] = pl.BlockSpec((tm, d // 2), row)
        in_specs.append(pl.BlockSpec(router_split.shape, const))
        out_shape.append(jax.ShapeDtypeStruct((t, LANES), F32))
        out_specs.append(pl.BlockSpec((tm, LANES), row))
        args.append(router_split)
    return pl.pallas_call(
        functools.partial(_out_proj_kernel, n_experts=n_experts),
        grid=(t // tm,), in_specs=in_specs, out_specs=out_specs, out_shape=out_shape,
        compiler_params=_cparams(("parallel",)), name="out_proj",
    )(*args)


def _swiglu_tiles(h, acc, wg, wu, wd, dff, tf):
    for f in range(dff // tf):
        cols = slice(f * tf, (f + 1) * tf)
        a = _dot(h, wg(cols))
        u = _dot(h, wu(cols))
        acc = acc + _dot(((a * jax.nn.sigmoid(a)) * u).astype(BF16), wd(cols))
    return acc


def _ffn_kernel(h_ref, x_ref, wg_ref, wu_ref, wd_ref, o_ref, *, tf):
    o_ref[...] = _swiglu_tiles(h_ref[...], x_ref[...], lambda c: wg_ref[:, c], lambda c: wu_ref[:, c],
                               lambda c: wd_ref[c, :], wg_ref.shape[1], tf)


def _ffn(h, x2, wg, wu, wd):
    t, d = x2.shape
    tm = min(TM_FFN, t)
    row = lambda i: (i, 0)
    resident = lambda w: pl.BlockSpec(w.shape, lambda i: (0, 0), pipeline_mode=pl.Buffered(1))
    return pl.pallas_call(
        functools.partial(_ffn_kernel, tf=TF_FFN), grid=(t // tm,),
        in_specs=[pl.BlockSpec((tm, d), row), pl.BlockSpec((tm, d), row),
                  resident(wg), resident(wu), resident(wd)],
        out_specs=pl.BlockSpec((tm, d), row),
        out_shape=jax.ShapeDtypeStruct((t, d), F32),
        compiler_params=_cparams(("parallel",)), name="ffn",
    )(h, x2, wg, wu, wd)


def _route_scan_kernel(g_ref, pos_ref, tot_ref, carry_ref):
    c = pl.program_id(0)

    @pl.when(c == 0)
    def _():
        carry_ref[...] = jnp.zeros_like(carry_ref)

    ct = g_ref.shape[0]
    routed = g_ref[...] > 0.0
    a = jnp.where(routed, 1.0, 0.0)
    earlier = lax.broadcasted_iota(jnp.int32, (ct, ct), 1) < lax.broadcasted_iota(jnp.int32, (ct, ct), 0)
    base = carry_ref[...]
    pos = jnp.where(routed, _dot(jnp.where(earlier, 1.0, 0.0).astype(BF16), a.astype(BF16)) + base, -1.0)
    pos_ref[...] = pos
    total = base + jnp.sum(a, axis=0, keepdims=True)
    carry_ref[...] = total
    tot_ref[...] = total


def _route_scan(gates):
    t = gates.shape[0]
    ct = CT_MOE
    nch = t // ct
    return pl.pallas_call(
        _route_scan_kernel, grid=(nch,),
        in_specs=[pl.BlockSpec((ct, LANES), lambda c: (c, 0))],
        out_specs=[pl.BlockSpec((ct, LANES), lambda c: (c, 0)),
                   pl.BlockSpec((1, LANES), lambda c: (0, 0))],
        out_shape=[jax.ShapeDtypeStruct((t, LANES), F32),
                   jax.ShapeDtypeStruct((1, LANES), F32)],
        scratch_shapes=[pltpu.VMEM((1, LANES), F32)],
        compiler_params=_cparams(("arbitrary",)), name="route_scan",
    )(gates)


def _sc_scatter_rows(rows, idx_a, idx_b, n_slots):
    t, w = rows.shape
    mesh = plsc.VectorSubcoreMesh(core_axis_name="core", subcore_axis_name="subcore")

    @pl.kernel(out_type=jax.ShapeDtypeStruct((n_slots, w), rows.dtype), mesh=mesh, scratch_types=[])
    def scatter(x_hbm, ia_hbm, ib_hbm, o_hbm):
        def body(x_vmem, ia_vmem, ib_vmem):
            pltpu.sync_copy(x_vmem, o_hbm.at[ia_vmem.at[0]])
            pltpu.sync_copy(x_vmem, o_hbm.at[ib_vmem.at[0]])

        pltpu.emit_pipeline(
            body, grid=(t // SC_WINDOW,),
            in_specs=[pl.BlockSpec((SC_WINDOW, w), lambda i: (i, 0)),
                      pl.BlockSpec((1, SC_WINDOW), lambda i: (i, 0)),
                      pl.BlockSpec((1, SC_WINDOW), lambda i: (i, 0))],
            out_specs=[], core_axis_name=("core", "subcore"),
            dimension_semantics=(pltpu.PARALLEL,))(x_hbm, ia_hbm, ib_hbm)

    return scatter(rows, idx_a, idx_b)


def _moe_ffn_kernel(exp_ref, rows_ref, xs_ref, wg_ref, wu_ref, wd_ref, ys_ref, *, tf):
    j = pl.program_id(0)
    n_rows = rows_ref[j]

    @pl.when(n_rows > 0)
    def _():
        xs = _unpack_bf16_halves(xs_ref[...])
        row = lax.broadcasted_iota(jnp.int32, xs.shape, 0)
        xs = jnp.where(row < n_rows, xs, jnp.zeros_like(xs))
        zero = jnp.zeros(xs.shape, F32)
        ys = _swiglu_tiles(xs, zero, lambda c: wg_ref[0, :, c], lambda c: wu_ref[0, :, c],
                           lambda c: wd_ref[0, c, :], wg_ref.shape[2], tf)
        ys_ref[...] = _pack_bf16_halves(ys)

    @pl.when(n_rows == 0)
    def _():
        ys_ref[...] = jnp.zeros_like(ys_ref)


def _moe_ffn(xs, blk_expert, blk_rows, wg, wu, wd):
    n_slots = xs.shape[0]
    d = wg.shape[1]
    expert = lambda w: pl.BlockSpec((1,) + w.shape[1:], lambda j, e, v: (e[j], 0, 0))
    grid_spec = pltpu.PrefetchScalarGridSpec(
        num_scalar_prefetch=2, grid=(n_slots // BM_MOE,),
        in_specs=[pl.BlockSpec((BM_MOE, d // 2), lambda j, e, v: (j, 0)), expert(wg), expert(wu), expert(wd)],
        out_specs=pl.BlockSpec((BM_MOE, d // 2), lambda j, e, v: (j, 0)))
    return pl.pallas_call(
        functools.partial(_moe_ffn_kernel, tf=TF_FFN), grid_spec=grid_spec,
        out_shape=jax.ShapeDtypeStruct((n_slots, d // 2), jnp.uint32),
        compiler_params=_cparams(("arbitrary",)), name="moe_ffn",
    )(blk_expert, blk_rows, xs, wg, wu, wd)


def _sc_gather_row_pairs(table, idx_a, idx_b):
    w = table.shape[1]
    t = idx_a.shape[0] * SC_WINDOW
    mesh = plsc.VectorSubcoreMesh(core_axis_name="core", subcore_axis_name="subcore")
    out = jax.ShapeDtypeStruct((t, w), table.dtype)

    @pl.kernel(out_type=(out, out), mesh=mesh, scratch_types=[])
    def gather(x_hbm, ia_hbm, ib_hbm, oa_hbm, ob_hbm):
        def body(i_vmem, o_vmem):
            pltpu.sync_copy(x_hbm.at[i_vmem.at[0]], o_vmem)

        for i_hbm, o_hbm in ((ia_hbm, oa_hbm), (ib_hbm, ob_hbm)):
            pltpu.emit_pipeline(
                body, grid=(t // SC_WINDOW,),
                in_specs=[pl.BlockSpec((1, SC_WINDOW), lambda i: (i, 0))],
                out_specs=[pl.BlockSpec((SC_WINDOW, w), lambda i: (i, 0))],
                core_axis_name=("core", "subcore"),
                dimension_semantics=(pltpu.PARALLEL,))(i_hbm, o_hbm)

    return gather(table, idx_a, idx_b)


def _token_slots(pos, pstart_row, unrouted):
    routed = pos >= 0.0
    slot = pos + pstart_row
    return (jnp.min(jnp.where(routed, slot, unrouted), axis=-1, keepdims=True),
            jnp.max(jnp.where(routed, slot, -1.0), axis=-1, keepdims=True), slot, routed)


def _moe_combine_kernel(x_ref, pos_ref, g_ref, pstart_ref, ya_ref, yb_ref, o_ref, *, n_slots):
    slot_a, slot_b, slot, routed = _token_slots(pos_ref[...], pstart_ref[...], float(n_slots))
    gates = g_ref[...]
    gate_a = jnp.sum(jnp.where(routed & (slot == slot_a), gates, 0.0), axis=-1, keepdims=True)
    gate_b = jnp.sum(jnp.where(routed & (slot == slot_b), gates, 0.0), axis=-1, keepdims=True)
    gate_b = jnp.where(slot_b == slot_a, 0.0, gate_b)

    def rows_f32(ref):
        p = ref[...]
        return jnp.concatenate([lax.bitcast_convert_type(p << 16, F32),
                                lax.bitcast_convert_type(p & jnp.uint32(0xFFFF0000), F32)], axis=-1)

    o_ref[...] = x_ref[...] + gate_a * rows_f32(ya_ref) + gate_b * rows_f32(yb_ref)


def _moe_combine(x2, pos, gates, pstart_row, ya, yb, n_slots):
    t, d = x2.shape
    tm = min(CT_MOE, t)
    row = lambda i: (i, 0)
    return pl.pallas_call(
        functools.partial(_moe_combine_kernel, n_slots=n_slots), grid=(t // tm,),
        in_specs=[pl.BlockSpec((tm, d), row), pl.BlockSpec((tm, LANES), row), pl.BlockSpec((tm, LANES), row),
                  pl.BlockSpec((1, LANES), lambda i: (0, 0)),
                  pl.BlockSpec((tm, d // 2), row), pl.BlockSpec((tm, d // 2), row)],
        out_specs=pl.BlockSpec((tm, d), row),
        out_shape=jax.ShapeDtypeStruct((t, d), F32),
        compiler_params=_cparams(("parallel",)), name="moe_combine",
    )(x2, pos, gates, pstart_row, ya, yb)


def _count_le(ascending, x):
    return jnp.sum(ascending[None, :] <= x[:, None], axis=1).astype(jnp.int32)


def _moe_routed(h, x2, gates, wg, wu, wd):
    t, d = x2.shape
    n_e = wg.shape[0]
    n_slots = t * TOP_K + n_e * BM_MOE
    pos, tot = _route_scan(gates)

    counts = tot[0, :n_e].astype(jnp.int32)
    padded = (counts + BM_MOE - 1) // BM_MOE * BM_MOE
    pend = jnp.cumsum(padded)
    pstart = pend - padded

    pstart_row = jnp.zeros((1, LANES), F32).at[0, :n_e].set(pstart.astype(F32))
    slot_a, slot_b, _, _ = _token_slots(pos, pstart_row, float(n_slots))
    slot_a = slot_a.astype(jnp.int32).reshape(-1, SC_WINDOW)
    slot_b = slot_b.astype(jnp.int32).reshape(-1, SC_WINDOW)
    xs = _sc_scatter_rows(h, slot_a, slot_b, n_slots)

    mb = jnp.arange(n_slots // BM_MOE, dtype=jnp.int32) * BM_MOE
    mb_e = jnp.minimum(_count_le(pend, mb), n_e - 1)
    mb_rows = jnp.where(mb < pend[-1], jnp.clip(counts[mb_e] - (mb - pstart[mb_e]), 0, BM_MOE), 0)
    ys = _moe_ffn(xs, mb_e, mb_rows.astype(jnp.int32), wg, wu, wd)

    ya, yb = _sc_gather_row_pairs(ys, slot_a, slot_b)
    return _moe_combine(x2, pos, gates, pstart_row, ya, yb, n_slots)


def _rope_tables(pos):
    half = ROPE_DIM // 2
    inv_freq = ROPE_THETA ** (-2.0 * jnp.arange(half, dtype=F32) / ROPE_DIM)
    ang = pos.astype(F32).reshape(-1, 1) * inv_freq
    cos, sin = jnp.cos(ang), jnp.sin(ang)
    n = ang.shape[0]
    rest = HEAD_DIM - ROPE_DIM
    cos_h = jnp.concatenate([cos, cos, jnp.ones((n, rest), F32)], axis=-1)
    sin_h = jnp.concatenate([-sin, sin, jnp.zeros((n, rest), F32)], axis=-1)
    return jnp.tile(cos_h, (1, HEADS_PER_VREG)), jnp.tile(sin_h, (1, HEADS_PER_VREG))


def _permute_w_in(w, conv_w):
    d = w.shape[0]
    kv_end = Q_W + 6 * KV_W
    g = w[:, kv_end:kv_end + N_HEADS * N_BRANCH]
    pad = jnp.zeros((d, LANES - GATES_PER_KV), w.dtype)
    gate_cols = []
    for kh in range(N_KV_HEADS):
        gate_cols += [g[:, kh * GATES_PER_KV:(kh + 1) * GATES_PER_KV], pad]
    u = w[:, kv_end + N_HEADS * N_BRANCH:]
    return jnp.concatenate([w[:, :kv_end]] + gate_cols + [u], axis=1).astype(BF16)


def _compress_weights(pos_emb, w1, w2):
    hidden = w1.shape[1]
    eye = jnp.eye(N_KV_HEADS, dtype=w1.dtype)
    w1r = w1.reshape(CMP_LEN, HEAD_DIM, hidden)
    halves = []
    for part in (w1r[:CMP_STRIDE], w1r[CMP_STRIDE:]):
        full = jnp.einsum('ldj,hg->lhdgj', part, eye)
        halves.append(full.reshape(CMP_STRIDE * N_KV_HEADS * HEAD_DIM, N_KV_HEADS * hidden).astype(BF16))
    w2p = jnp.einsum('jd,hg->hjgd', w2, eye).reshape(N_KV_HEADS * hidden, N_KV_HEADS * HEAD_DIM).astype(BF16)
    pos = []
    for part in (pos_emb[:CMP_STRIDE], pos_emb[CMP_STRIDE:]):
        pos.append(jnp.broadcast_to(part[:, None, :], (CMP_STRIDE, N_KV_HEADS, HEAD_DIM)).reshape(1, -1))
    return [halves[0], halves[1], w2p, pos[0], pos[1]]


def _selection_map_t(seq):
    ncp = seq // CMP_STRIDE
    n_cmp = (seq - CMP_LEN) // CMP_STRIDE + 1
    c0 = np.arange(ncp) * CMP_STRIDE
    s0 = np.arange(seq // SEL_LEN) * SEL_LEN
    ov = np.minimum(c0[None, :] + CMP_LEN, s0[:, None] + SEL_LEN) - np.maximum(c0[None, :], s0[:, None])
    m = np.clip(ov, 0, None) / CMP_LEN
    m[:, n_cmp:] = 0.0
    return jnp.asarray(m, dtype=BF16)


def kernel(x, positions, attn_norm_g, ffn_norm_g, w_in, w_out, q_norm_g, k_norm_g, cmp_pos_k, cmp_w1_k, cmp_w2_k, cmp_pos_v, cmp_w1_v, cmp_w2_v, conv_w, conv_b, conv_ln_g, conv_ln_b, ffn_w_gate, ffn_w_up, ffn_w_down, moe_router, moe_w_gate, moe_w_up, moe_w_down):
    b, seq, d = x.shape
    t = b * seq
    depth = w_in.shape[0]
    cw = conv_w.shape[2]
    ncp = seq // CMP_STRIDE
    n_cmp = (seq - CMP_LEN) // CMP_STRIDE + 1
    assert seq % max(TQ, KC, TM_PROJ) == 0 and seq >= WINDOW + TQ

    cos_t, sin_t = _rope_tables(positions)
    cmp_end = np.minimum(np.arange(ncp) * CMP_STRIDE + CMP_LEN - 1, seq - 1)
    cos_c, sin_c = _rope_tables(positions[:, cmp_end])
    selmap_t = _selection_map_t(seq)
    tile2 = lambda v: jnp.tile(v.reshape(1, HEAD_DIM), (1, HEADS_PER_VREG))

    x2 = x.reshape(t, d)
    for layer in range(depth):
        w_perm = _permute_w_in(w_in[layer], cw)
        conv_params = (conv_w[layer], conv_b[layer].reshape(1, cw), conv_ln_g[layer].reshape(1, cw),
                       conv_ln_b[layer].reshape(1, cw))
        q, kc, vc, ks, vs, kw, vw, gates, conv = _in_proj(
            x2, attn_norm_g[layer].reshape(1, d), w_perm, cos_t, sin_t,
            tile2(q_norm_g[layer]), tile2(k_norm_g[layer, 1]), tile2(k_norm_g[layer, 2]), conv_params, seq)
        kcmp, vcmp = _compress(
            kc, vc,
            _compress_weights(cmp_pos_k[layer], cmp_w1_k[layer], cmp_w2_k[layer]),
            _compress_weights(cmp_pos_v[layer], cmp_w1_v[layer], cmp_w2_v[layer]),
            tile2(k_norm_g[layer, 0]), cos_c, sin_c, b, ncp)
        score_bound = (HEAD_DIM ** 0.5 * LOG2_E * SCORE_BOUND_MARGIN * jnp.max(jnp.abs(q_norm_g[layer]))
                       * jnp.max(jnp.abs(k_norm_g[layer]))).astype(F32).reshape(1)
        attn = _attention(score_bound, q, kcmp, vcmp, ks, vs, kw, vw, gates, selmap_t, b, seq)
        wo = w_out[layer].astype(BF16)
        g2 = ffn_norm_g[layer].reshape(1, d)
        i = layer // 2
        if layer % 2 == 0:
            x2, h = _out_proj(attn, conv, wo, x2, g2)
            x2 = _ffn(h, x2, ffn_w_gate[i].astype(BF16), ffn_w_up[i].astype(BF16), ffn_w_down[i].astype(BF16))
        else:
            n_e = moe_router.shape[2]
            r = jnp.pad(moe_router[i], ((0, 0), (0, LANES - n_e)))
            r_hi = r.astype(BF16)
            r_lo = (r - r_hi.astype(F32)).astype(BF16)
            x2, h, route = _out_proj(attn, conv, wo, x2, g2, jnp.concatenate([r_hi, r_lo], axis=1), n_e)
            x2 = _moe_routed(h, x2, route, moe_w_gate[i].astype(BF16), moe_w_up[i].astype(BF16),
                             moe_w_down[i].astype(BF16))
    return x2.reshape(b, seq, d)
```

```python
import functools
import math

import jax
import jax.numpy as jnp
import numpy as np
from jax import lax
from jax.experimental import pallas as pl
from jax.experimental.pallas import tpu as pltpu
from jax.experimental.pallas import tpu_sc as plsc

F32 = jnp.float32
BF16 = jnp.bfloat16

N_HEADS = 8
N_KV_HEADS = 2
Q_PER_KV = N_HEADS // N_KV_HEADS
HEAD_DIM = 64
N_BRANCH = 3
CMP_LEN = 32
CMP_STRIDE = 16
SEL_LEN = 64
SEL_TOPK = 16
N_LOCAL_SEL = 2
WINDOW = 512
CONV_KERNEL = 31
ROPE_THETA = 500000.0
ROPE_DIM = HEAD_DIM // 4
TOP_K = 2
EPS = 1e-6

LANES = 128
SUBLANES = 8
LOG2_E = math.log2(math.e)
NEG_BIG = -(2.0 ** 100)
MAX_SCORE_BOUND = 50.0
SCORE_BOUND_MARGIN = 1.02
HEADS_PER_VREG = LANES // HEAD_DIM
VMEM_LIMIT = 56 * 1024 * 1024

TM_PROJ = 512
TQ = 512
KC = 512
CH_CONV = 32
HALO = 32
TM_FFN = 1024
TF_FFN = 512
CT_MOE = 1024
BM_MOE = 512
SC_WINDOW = 64


def _cparams(sem):
    return pltpu.CompilerParams(dimension_semantics=sem, vmem_limit_bytes=VMEM_LIMIT)


def _dot(a, b):
    return jnp.dot(a, b, preferred_element_type=F32)


def _dot_nt(a, b):
    return lax.dot_general(a, b, (((1,), (1,)), ((), ())), preferred_element_type=F32)


def _split_bf16(x):
    hi = x.astype(BF16)
    lo = (x - hi.astype(F32)).astype(BF16)
    return hi, lo


def _rms_rows(x, g):
    ms = jnp.mean(x * x, axis=-1, keepdims=True)
    return x * lax.rsqrt(ms + EPS) * g


def _head_block_ones(width):
    r = lax.broadcasted_iota(jnp.int32, (width, width), 0) // HEAD_DIM
    c = lax.broadcasted_iota(jnp.int32, (width, width), 1) // HEAD_DIM
    return jnp.where(r == c, 1.0, 0.0).astype(BF16)


def _norm_rope(xg, ms, gain, cos, sin):
    y = xg * lax.rsqrt(ms + EPS) * gain
    lane = lax.broadcasted_iota(jnp.int32, y.shape, 1) % HEAD_DIM
    half = ROPE_DIM // 2
    partner = jnp.where(lane < half, pltpu.roll(y, LANES - half, 1), pltpu.roll(y, half, 1))
    return y * cos + partner * sin


def _head_norm_rope(xg, gain, cos, sin, ones_bd):
    ms = _dot((xg * xg).astype(BF16), ones_bd) * (1.0 / HEAD_DIM)
    return _norm_rope(xg, ms, gain, cos, sin)


def _head_norm_rope_pair(xa, xb, gain_a, gain_b, cos, sin, ones_bd2):
    sq = jnp.concatenate([xa * xa, xb * xb], axis=-1).astype(BF16)
    ms = _dot(sq, ones_bd2) * (1.0 / HEAD_DIM)
    return (_norm_rope(xa, ms[:, :LANES], gain_a, cos, sin),
            _norm_rope(xb, ms[:, LANES:], gain_b, cos, sin))


Q_W = N_HEADS * HEAD_DIM
KV_W = N_KV_HEADS * HEAD_DIM
SEG_Q = 0
SEG_KV = Q_W
SEG_GATE = SEG_KV + 6 * KV_W
SEG_UA = SEG_GATE + N_KV_HEADS * LANES
GATES_PER_KV = Q_PER_KV * N_BRANCH


def _causal_conv_tile(glu, seq_start, w_ref, b_ref, lg_ref, lb_ref, o_ref, ext_ref, shift_ref):
    ts = glu.shape[0]
    ext_ref[0:HALO, :] = jnp.where(seq_start, 0.0, ext_ref[ts:ts + HALO, :])
    ext_ref[HALO:HALO + ts, :] = glu
    n_shift = shift_ref.shape[1]
    for r in range(1, SUBLANES):
        shift_ref[r - 1] = ext_ref[r:r + n_shift, :]

    def rows_from(o):
        r = o % SUBLANES
        if r == 0:
            return ext_ref[o:o + CH_CONV, :]
        return shift_ref[r - 1, o - r:o - r + CH_CONV, :]

    w = w_ref[...]
    first_tap = HALO - (CONV_KERNEL - 1)
    for c in range(ts // CH_CONV):
        base = c * CH_CONV + first_tap
        acc = w[0:1, :] * rows_from(base)
        for k in range(1, CONV_KERNEL):
            acc = acc + w[k:k + 1, :] * rows_from(base + k)
        y = acc + b_ref[...]
        yc = y - jnp.mean(y, axis=-1, keepdims=True)
        yn = yc * lax.rsqrt(jnp.mean(yc * yc, axis=-1, keepdims=True) + EPS)
        z = yn * lg_ref[...] + lb_ref[...]
        o_ref[c * CH_CONV:(c + 1) * CH_CONV, :] = (z * jax.nn.sigmoid(z)).astype(o_ref.dtype)


def _in_proj_kernel(x_ref, g_ref, w_ref, cos_ref, sin_ref, qg_ref, ksg_ref, kwg_ref,
                    cw_ref, cb_ref, clg_ref, clb_ref,
                    q_ref, kc_ref, vc_ref, ks_ref, vs_ref, kw_ref, vw_ref, gate_ref, conv_ref,
                    glu_ref, ext_ref, shift_ref, *, conv_w, seq, n_tiles):
    i = pl.program_id(0)
    tm = x_ref.shape[0]

    @pl.when(i == 0)
    def _():
        glu_ref[...] = jnp.zeros_like(glu_ref)
        ext_ref[...] = jnp.zeros_like(ext_ref)

    _causal_conv_tile(glu_ref[...], lax.rem((i - 1) * tm, seq) == 0, cw_ref, cb_ref, clg_ref, clb_ref,
                      conv_ref, ext_ref, shift_ref)

    h = _rms_rows(x_ref[...], g_ref[...]).astype(BF16)
    cos = cos_ref[...]
    sin = sin_ref[...]
    ones_bd = _head_block_ones(2 * LANES)
    lane = lax.broadcasted_iota(jnp.int32, (tm, LANES), 1)
    tile_start = lax.rem(jnp.minimum(i, n_tiles - 1) * tm, seq)
    tok = tile_start + lax.broadcasted_iota(jnp.int32, (tm, 1), 0)
    block_aug = jnp.where(lane - HEAD_DIM == tok // SEL_LEN, NEG_BIG, 0.0)
    ones_aug = jnp.where(lane == HEAD_DIM, 1.0, 0.0)

    def put_heads(ref, first, val, aug):
        for j in range(HEADS_PER_VREG):
            head = val if j == 0 else pltpu.roll(val, LANES - j * HEAD_DIM, 1)
            ref[first + j] = jnp.where(lane < HEAD_DIM, head, aug).astype(ref.dtype)

    qkv = _dot(h, w_ref[:, SEG_Q:SEG_GATE])
    group = lambda j: qkv[:, j * LANES:(j + 1) * LANES]
    first_kv = Q_W // LANES
    scale = HEAD_DIM ** -0.5 * LOG2_E
    qg = qg_ref[...]
    for c in range(0, first_kv, 2):
        pair = _head_norm_rope_pair(group(c), group(c + 1), qg, qg, cos, sin, ones_bd)
        for j, y in enumerate(pair):
            put_heads(q_ref, (c + j) * HEADS_PER_VREG, y * scale, 0.0)
    kc_ref[...] = group(first_kv)
    vc_ref[...] = group(first_kv + 1)
    k_sel, k_win = _head_norm_rope_pair(group(first_kv + 2), group(first_kv + 4), ksg_ref[...], kwg_ref[...],
                                        cos, sin, ones_bd)
    put_heads(ks_ref, 0, k_sel, block_aug)
    put_heads(vs_ref, 0, group(first_kv + 3), ones_aug)
    put_heads(kw_ref, 0, k_win, 0.0)
    put_heads(vw_ref, 0, group(first_kv + 5), ones_aug)
    gate_ref[...] = jax.nn.sigmoid(_dot(h, w_ref[:, SEG_GATE:SEG_UA]))
    u = _dot(h, w_ref[:, SEG_UA:])
    glu_ref[...] = u[:, :conv_w] * jax.nn.sigmoid(u[:, conv_w:])


def _in_proj(x2, g, w_perm, cos_t, sin_t, qg, ksg, kwg, conv_params, seq):
    t, d = x2.shape
    conv_w = (w_perm.shape[1] - SEG_UA) // 2
    tm = min(TM_PROJ, seq)
    n_tiles = t // tm
    row = lambda i: (jnp.minimum(i, n_tiles - 1), 0)
    const = lambda i: (0, 0)
    head_row = lambda i: (0, jnp.minimum(i, n_tiles - 1), 0)
    out_shape = [
        jax.ShapeDtypeStruct((N_HEADS, t, LANES), BF16),
        jax.ShapeDtypeStruct((t, KV_W), F32),
        jax.ShapeDtypeStruct((t, KV_W), F32),
        jax.ShapeDtypeStruct((N_KV_HEADS, t, LANES), BF16),
        jax.ShapeDtypeStruct((N_KV_HEADS, t, LANES), BF16),
        jax.ShapeDtypeStruct((N_KV_HEADS, t, LANES), BF16),
        jax.ShapeDtypeStruct((N_KV_HEADS, t, LANES), BF16),
        jax.ShapeDtypeStruct((t, N_KV_HEADS * LANES), F32),
        jax.ShapeDtypeStruct((t, conv_w), BF16),
    ]
    kv_spec = pl.BlockSpec((N_KV_HEADS, tm, LANES), head_row)
    out_specs = [
        pl.BlockSpec((N_HEADS, tm, LANES), head_row),
        pl.BlockSpec((tm, KV_W), row), pl.BlockSpec((tm, KV_W), row),
        kv_spec, kv_spec, kv_spec, kv_spec,
        pl.BlockSpec((tm, N_KV_HEADS * LANES), row),
        pl.BlockSpec((tm, conv_w), lambda i: (jnp.maximum(i - 1, 0), 0)),
    ]
    in_specs = [
        pl.BlockSpec((tm, d), row), pl.BlockSpec((1, d), const),
        pl.BlockSpec(w_perm.shape, const),
        pl.BlockSpec((tm, LANES), row), pl.BlockSpec((tm, LANES), row),
        pl.BlockSpec((1, LANES), const), pl.BlockSpec((1, LANES), const), pl.BlockSpec((1, LANES), const),
    ] + [pl.BlockSpec(p.shape, const) for p in conv_params]
    return pl.pallas_call(
        functools.partial(_in_proj_kernel, conv_w=conv_w, seq=seq, n_tiles=n_tiles),
        grid=(n_tiles + 1,), in_specs=in_specs, out_specs=out_specs, out_shape=out_shape,
        scratch_shapes=[pltpu.VMEM((tm, conv_w), F32),
                        pltpu.VMEM((tm + HALO, conv_w), F32),
                        pltpu.VMEM((SUBLANES - 1, tm + HALO - SUBLANES, conv_w), F32)],
        compiler_params=_cparams(("arbitrary",)), name="in_proj",
    )(x2, g, w_perm, cos_t, sin_t, qg, ksg, kwg, *conv_params)


def _gelu_tanh(x):
    c = math.sqrt(2.0 / math.pi)
    return 0.5 * x * (1.0 + jnp.tanh(c * (x + 0.044715 * (x * x * x))))


def _compress_kernel(k_ref, v_ref, w1ak_ref, w1bk_ref, w2k_ref, pak_ref, pbk_ref,
                     w1av_ref, w1bv_ref, w2v_ref, pav_ref, pbv_ref,
                     kg_ref, cos_ref, sin_ref, ko_ref, vo_ref):
    def mlp(x_ref, w1a_ref, w1b_ref, w2_ref, pa_ref, pb_ref):
        n = x_ref.shape[0] // CMP_STRIDE
        first = second = None
        for l in range(CMP_STRIDE):
            x = x_ref[pl.ds(l, n, stride=CMP_STRIDE), :]
            cols = slice(l * KV_W, (l + 1) * KV_W)
            fa = _dot((x + pa_ref[:, cols]).astype(BF16), w1a_ref[cols, :])
            fb = _dot((x + pb_ref[:, cols]).astype(BF16), w1b_ref[cols, :])
            first = fa if first is None else first + fa
            second = fb if second is None else second + fb
        hid = first + pltpu.roll(second, n - 1, 0)
        return _dot(_gelu_tanh(hid).astype(BF16), w2_ref[...])

    kc = mlp(k_ref, w1ak_ref, w1bk_ref, w2k_ref, pak_ref, pbk_ref)
    kc = _head_norm_rope(kc, kg_ref[...], cos_ref[...], sin_ref[...], _head_block_ones(LANES))
    vc = mlp(v_ref, w1av_ref, w1bv_ref, w2v_ref, pav_ref, pbv_ref)
    lane = lax.broadcasted_iota(jnp.int32, kc.shape, 1)
    for j in range(N_KV_HEADS):
        for val, ref in ((kc, ko_ref), (vc, vo_ref)):
            head = val if j == 0 else pltpu.roll(val, LANES - j * HEAD_DIM, 1)
            ref[0, j] = jnp.where(lane < HEAD_DIM, head, 0.0).astype(ref.dtype)


def _compress(kc, vc, wk, wv, kg, cosc, sinc, b, ncp):
    seq = kc.shape[0] // b
    const = lambda i: (0, 0)
    row = lambda i: (i, 0)

    def wspecs(ws):
        return [pl.BlockSpec(w.shape, const) for w in ws]

    out = jax.ShapeDtypeStruct((b, N_KV_HEADS, ncp, LANES), BF16)
    ospec = pl.BlockSpec((1, N_KV_HEADS, ncp, LANES), lambda i: (i, 0, 0, 0))
    return pl.pallas_call(
        _compress_kernel, grid=(b,),
        in_specs=[pl.BlockSpec((seq, KV_W), row), pl.BlockSpec((seq, KV_W), row)]
        + wspecs(wk) + wspecs(wv)
        + [pl.BlockSpec((1, LANES), const), pl.BlockSpec((ncp, LANES), row), pl.BlockSpec((ncp, LANES), row)],
        out_specs=[ospec, ospec], out_shape=[out, out],
        compiler_params=_cparams(("parallel",)), name="compress",
    )(kc, vc, *wk, *wv, kg, cosc, sinc)


def _attn_kernel(off_ref, q_ref, kc_ref, vc_ref, ks_ref, vs_ref, kw_ref, vw_ref, gate_ref, selmap_ref,
                 o_ref, *, seq, tq, kc_len, top_n, bounded):
    i = pl.program_id(2)
    t0 = i * tq
    rows = Q_PER_KV * tq
    n_sel = seq // SEL_LEN
    q2 = q_ref[...].reshape(rows, LANES)
    t_row = t0 + (lax.broadcasted_iota(jnp.int32, (rows, 1), 0) & (tq - 1))
    t_tok = t0 + lax.broadcasted_iota(jnp.int32, (tq, 1), 0)
    neg_offset = -off_ref[0] if bounded else 0.0

    def add_bias(s, bias):
        return (s.reshape(s.shape[0] // tq, tq, s.shape[1]) + bias[None]).reshape(s.shape)

    kcmp = kc_ref[0, 0]
    ncp = kcmp.shape[0]
    s_c = _dot_nt(q2, kcmp)
    cmp_end = lax.broadcasted_iota(jnp.int32, (1, ncp), 1) * CMP_STRIDE + (CMP_LEN - 1)
    if bounded:
        e_c = jnp.exp2(add_bias(s_c, jnp.where(cmp_end <= t_tok, neg_offset, NEG_BIG)))
    else:
        s_c = jnp.where(cmp_end <= t_row, s_c, -jnp.inf)
        m_c = jnp.max(s_c, axis=-1, keepdims=True)
        e_c = jnp.exp2(s_c - jnp.where(m_c == -jnp.inf, 0.0, m_c))
    p_c = e_c * (1.0 / jnp.maximum(jnp.sum(e_c, axis=-1, keepdims=True), jnp.finfo(F32).tiny))
    o_c = _dot(p_c.astype(BF16), vc_ref[0, 0])

    def denominator(acc):
        return acc[:, HEAD_DIM:HEAD_DIM + 1]

    span = min(WINDOW + tq, seq)
    w0 = pl.multiple_of(jnp.maximum(t0 - WINDOW, 0), tq)
    key_w = w0 + lax.broadcasted_iota(jnp.int32, (1, span), 1)
    bias_w = jnp.where((key_w <= t_tok) & (key_w > t_tok - WINDOW), neg_offset, NEG_BIG)
    s_w = add_bias(_dot_nt(q2, kw_ref[0, pl.ds(w0, span), :]), bias_w)
    if bounded:
        p_w = jnp.exp2(s_w).astype(BF16)
    else:
        p_w = jnp.exp2((s_w - jnp.max(s_w, axis=-1, keepdims=True)).astype(BF16))
    acc_w = _dot(p_w, vw_ref[0, pl.ds(w0, span), :])
    o_w = acc_w * (1.0 / denominator(acc_w))

    p_hi, p_lo = _split_bf16(jnp.sum(p_c.reshape(Q_PER_KV, tq, ncp), axis=0))
    selmap = selmap_ref[...]
    imp = _dot_nt(selmap, p_hi) + _dot_nt(selmap, p_lo)
    blk = lax.broadcasted_iota(jnp.int32, (n_sel, tq), 0)
    cur = (t0 + lax.broadcasted_iota(jnp.int32, (n_sel, tq), 1)) // SEL_LEN
    causal_blk = blk <= cur
    forced = (blk == 0) | (causal_blk & (blk > cur - N_LOCAL_SEL))
    score = jnp.where(forced, jnp.inf, jnp.where(causal_blk, imp, -jnp.inf))
    sub = lax.broadcasted_iota(jnp.int32, (SUBLANES, tq), 0)
    groups = [score[g * SUBLANES:(g + 1) * SUBLANES, :] for g in range(n_sel // SUBLANES)]
    ranks = [jnp.zeros((SUBLANES, tq), F32) for _ in groups]
    for jp in range(n_sel):
        other = jnp.broadcast_to(score[jp:jp + 1, :], (SUBLANES, tq))
        for g, sg in enumerate(groups):
            first = g * SUBLANES
            if first > jp:
                inc = jnp.where(other >= sg, 1.0, 0.0)
            elif first + SUBLANES - 1 <= jp:
                inc = jnp.where(other > sg, 1.0, 0.0)
            else:
                inc = jnp.where(other > sg, 1.0, jnp.where((other == sg) & (sub > jp - first), 1.0, 0.0))
            ranks[g] = ranks[g] + inc
    chosen_flag = neg_offset * (1.0 / NEG_BIG)
    block_flags = jnp.where(jnp.concatenate(ranks, axis=0) < top_n, chosen_flag, 1.0)
    flag_rows = [jnp.zeros((HEAD_DIM, tq), F32), block_flags]
    if HEAD_DIM + n_sel < LANES:
        flag_rows.append(jnp.zeros((LANES - HEAD_DIM - n_sel, tq), F32))
    flags = jnp.transpose(jnp.concatenate(flag_rows, axis=0))
    q_sel = (q2.reshape(Q_PER_KV, tq, LANES) + flags.astype(BF16)[None]).reshape(rows, LANES)

    def sel_chunk(k0, carry, bias):
        s = _dot_nt(q_sel, ks_ref[0, pl.ds(k0, kc_len), :])
        if bias is not None:
            s = add_bias(s, bias)
        v = vs_ref[0, pl.ds(k0, kc_len), :]
        if bounded:
            (acc,) = carry
            return (acc + _dot(jnp.exp2(s).astype(BF16), v),)
        m, acc = carry
        m_new = jnp.maximum(m, jnp.max(s, axis=-1, keepdims=True))
        p = jnp.exp2((s - m_new).astype(BF16))
        return m_new, jnp.exp2(m - m_new) * acc + _dot(p, v)

    n_full = t0 // kc_len
    init = (jnp.zeros((rows, LANES), F32),)
    if not bounded:
        init = (jnp.full((rows, 1), -jnp.inf, F32),) + init
    carry = lax.fori_loop(
        0, n_full, lambda c, cr: sel_chunk(pl.multiple_of(c * kc_len, kc_len), cr, None), init)
    kd = pl.multiple_of(n_full * kc_len, kc_len)
    key_d = kd + lax.broadcasted_iota(jnp.int32, (1, kc_len), 1)
    acc_s = sel_chunk(kd, carry, jnp.where(key_d <= t_tok, 0.0, NEG_BIG))[-1]
    o_s = acc_s * (1.0 / denominator(acc_s))

    gates = gate_ref[...]

    def gate_col(br):
        cols = [gates[:, g * N_BRANCH + br:g * N_BRANCH + br + 1] for g in range(Q_PER_KV)]
        return jnp.concatenate(cols, axis=0)

    o = gate_col(0) * o_c + gate_col(1) * o_s + gate_col(2) * o_w
    o3 = o.reshape(Q_PER_KV, tq, LANES)
    lane = lax.broadcasted_iota(jnp.int32, (tq, LANES), 1)
    pairs = [jnp.where(lane < HEAD_DIM, o3[g], pltpu.roll(o3[g + 1], HEAD_DIM, 1))
             for g in range(0, Q_PER_KV, HEADS_PER_VREG)]
    o_ref[...] = jnp.concatenate(pairs, axis=-1).astype(o_ref.dtype)


def _attention(score_bound, q, kcmp, vcmp, ks, vs, kw, vw, gates, selmap_t, b, seq):
    t = b * seq
    tq = min(TQ, seq)
    kc_len = min(KC, seq)
    nq = seq // tq
    ncp = kcmp.shape[2]
    n_sel = seq // SEL_LEN
    assert HEAD_DIM + n_sel <= LANES, "selection-block flags must fit beside the head dims"
    top_n = min(SEL_TOPK, n_sel)
    cmp_spec = pl.BlockSpec((1, 1, ncp, LANES), lambda bi, kh, i: (bi, kh, 0, 0))
    seq_spec = pl.BlockSpec((1, seq, LANES), lambda bi, kh, i: (kh, bi, 0))

    def run(bounded):
        return pl.pallas_call(
            functools.partial(_attn_kernel, seq=seq, tq=tq, kc_len=kc_len, top_n=top_n, bounded=bounded),
            grid=(b, N_KV_HEADS, nq),
            in_specs=[
                pl.BlockSpec(memory_space=pltpu.SMEM),
                pl.BlockSpec((Q_PER_KV, tq, LANES), lambda bi, kh, i: (kh, bi * nq + i, 0)),
                cmp_spec, cmp_spec, seq_spec, seq_spec, seq_spec, seq_spec,
                pl.BlockSpec((tq, LANES), lambda bi, kh, i: (bi * nq + i, kh)),
                pl.BlockSpec(selmap_t.shape, lambda bi, kh, i: (0, 0)),
            ],
            out_specs=pl.BlockSpec((tq, Q_PER_KV * HEAD_DIM), lambda bi, kh, i: (bi * nq + i, kh)),
            out_shape=jax.ShapeDtypeStruct((t, N_HEADS * HEAD_DIM), BF16),
            compiler_params=_cparams(("parallel", "parallel", "arbitrary")),
            name="nsa_attention" if bounded else "nsa_attention_running_max",
        )(score_bound, q, kcmp, vcmp, ks, vs, kw, vw, gates, selmap_t)

    return lax.cond(score_bound[0] < MAX_SCORE_BOUND, lambda: run(True), lambda: run(False))


def _top2_gates(logits, n_experts):
    lane = lax.broadcasted_iota(jnp.int32, logits.shape, 1)
    x = jnp.where(lane < n_experts, logits, -jnp.inf)
    m1 = jnp.max(x, axis=-1, keepdims=True)
    i1 = jnp.min(jnp.where(x == m1, lane, LANES), axis=-1, keepdims=True)
    x2 = jnp.where(lane == i1, -jnp.inf, x)
    m2 = jnp.max(x2, axis=-1, keepdims=True)
    i2 = jnp.min(jnp.where(x2 == m2, lane, LANES), axis=-1, keepdims=True)
    e2 = jnp.exp(m2 - m1)
    inv = 1.0 / (1.0 + e2)
    return jnp.where(lane == i1, inv, jnp.where(lane == i2, e2 * inv, 0.0))


def _pack_bf16_halves(x):
    w = x.shape[1] // 2
    bits = lax.bitcast_convert_type(x.astype(BF16).astype(F32), jnp.uint32)
    return (bits[:, :w] >> 16) | (bits[:, w:] & jnp.uint32(0xFFFF0000))


def _unpack_bf16_halves(p):
    lo = lax.bitcast_convert_type(p << 16, F32)
    hi = lax.bitcast_convert_type(p & jnp.uint32(0xFFFF0000), F32)
    return jnp.concatenate([lo, hi], axis=-1).astype(BF16)


def _out_proj_kernel(*refs, n_experts):
    if n_experts:
        attn_ref, conv_ref, wo_ref, x_ref, g_ref, rt_ref, xo_ref, h_ref, gate_ref = refs
    else:
        attn_ref, conv_ref, wo_ref, x_ref, g_ref, xo_ref, h_ref = refs
    aw = attn_ref.shape[1]
    x = x_ref[...] + _dot(attn_ref[...], wo_ref[0:aw, :]) + _dot(conv_ref[...], wo_ref[aw:, :])
    xo_ref[...] = x
    h = _rms_rows(x, g_ref[...])
    if not n_experts:
        h_ref[...] = h.astype(h_ref.dtype)
    else:
        h_ref[...] = _pack_bf16_halves(h)
        h_hi, h_lo = _split_bf16(h)
        router = rt_ref[...]
        by_hi = _dot(h_hi, router)
        logits = by_hi[:, :LANES] + (by_hi[:, LANES:] + _dot(h_lo, router[:, :LANES]))
        gate_ref[...] = _top2_gates(logits, n_experts)


def _out_proj(attn, conv, wo, x2, g, router_split=None, n_experts=0):
    t, d = x2.shape
    tm = min(TM_PROJ, t)
    row = lambda i: (i, 0)
    const = lambda i: (0, 0)
    in_specs = [pl.BlockSpec((tm, attn.shape[1]), row), pl.BlockSpec((tm, conv.shape[1]), row),
                pl.BlockSpec(wo.shape, const), pl.BlockSpec((tm, d), row), pl.BlockSpec((1, d), const)]
    out_shape = [jax.ShapeDtypeStruct((t, d), F32), jax.ShapeDtypeStruct((t, d), BF16)]
    out_specs = [pl.BlockSpec((tm, d), row), pl.BlockSpec((tm, d), row)]
    args = [attn, conv, wo, x2, g]
    if n_experts:
        out_shape[1] = jax.ShapeDtypeStruct((t, d // 2), jnp.uint32)
        out_specs[1] = pl.BlockSpec((tm, d // 2), row)
        in_specs.append(pl.BlockSpec(router_split.shape, const))
        out_shape.append(jax.ShapeDtypeStruct((t, LANES), F32))
        out_specs.append(pl.BlockSpec((tm, LANES), row))
        args.append(router_split)
    return pl.pallas_call(
        functools.partial(_out_proj_kernel, n_experts=n_experts),
        grid=(t // tm,), in_specs=in_specs, out_specs=out_specs, out_shape=out_shape,
        compiler_params=_cparams(("parallel",)), name="out_proj",
    )(*args)


def _swiglu_tiles(h, acc, wg, wu, wd, dff, tf):
    for f in range(dff // tf):
        cols = slice(f * tf, (f + 1) * tf)
        a = _dot(h, wg(cols))
        u = _dot(h, wu(cols))
        acc = acc + _dot(((a * jax.nn.sigmoid(a)) * u).astype(BF16), wd(cols))
    return acc


def _ffn_kernel(h_ref, x_ref, wg_ref, wu_ref, wd_ref, o_ref, *, tf):
    o_ref[...] = _swiglu_tiles(h_ref[...], x_ref[...], lambda c: wg_ref[:, c], lambda c: wu_ref[:, c],
                               lambda c: wd_ref[c, :], wg_ref.shape[1], tf)


def _ffn(h, x2, wg, wu, wd):
    t, d = x2.shape
    tm = min(TM_FFN, t)
    row = lambda i: (i, 0)
    resident = lambda w: pl.BlockSpec(w.shape, lambda i: (0, 0), pipeline_mode=pl.Buffered(1))
    return pl.pallas_call(
        functools.partial(_ffn_kernel, tf=TF_FFN), grid=(t // tm,),
        in_specs=[pl.BlockSpec((tm, d), row), pl.BlockSpec((tm, d), row),
                  resident(wg), resident(wu), resident(wd)],
        out_specs=pl.BlockSpec((tm, d), row),
        out_shape=jax.ShapeDtypeStruct((t, d), F32),
        compiler_params=_cparams(("parallel",)), name="ffn",
    )(h, x2, wg, wu, wd)


def _route_scan_kernel(g_ref, pos_ref, tot_ref, carry_ref):
    c = pl.program_id(0)

    @pl.when(c == 0)
    def _():
        carry_ref[...] = jnp.zeros_like(carry_ref)

    ct = g_ref.shape[0]
    routed = g_ref[...] > 0.0
    a = jnp.where(routed, 1.0, 0.0)
    earlier = lax.broadcasted_iota(jnp.int32, (ct, ct), 1) < lax.broadcasted_iota(jnp.int32, (ct, ct), 0)
    base = carry_ref[...]
    pos = jnp.where(routed, _dot(jnp.where(earlier, 1.0, 0.0).astype(BF16), a.astype(BF16)) + base, -1.0)
    pos_ref[...] = pos
    total = base + jnp.sum(a, axis=0, keepdims=True)
    carry_ref[...] = total
    tot_ref[...] = total


def _route_scan(gates):
    t = gates.shape[0]
    ct = CT_MOE
    nch = t // ct
    return pl.pallas_call(
        _route_scan_kernel, grid=(nch,),
        in_specs=[pl.BlockSpec((ct, LANES), lambda c: (c, 0))],
        out_specs=[pl.BlockSpec((ct, LANES), lambda c: (c, 0)),
                   pl.BlockSpec((1, LANES), lambda c: (0, 0))],
        out_shape=[jax.ShapeDtypeStruct((t, LANES), F32),
                   jax.ShapeDtypeStruct((1, LANES), F32)],
        scratch_shapes=[pltpu.VMEM((1, LANES), F32)],
        compiler_params=_cparams(("arbitrary",)), name="route_scan",
    )(gates)


def _sc_scatter_rows(rows, idx_a, idx_b, n_slots):
    t, w = rows.shape
    mesh = plsc.VectorSubcoreMesh(core_axis_name="core", subcore_axis_name="subcore")

    @pl.kernel(out_type=jax.ShapeDtypeStruct((n_slots, w), rows.dtype), mesh=mesh, scratch_types=[])
    def scatter(x_hbm, ia_hbm, ib_hbm, o_hbm):
        def body(x_vmem, ia_vmem, ib_vmem):
            pltpu.sync_copy(x_vmem, o_hbm.at[ia_vmem.at[0]])
            pltpu.sync_copy(x_vmem, o_hbm.at[ib_vmem.at[0]])

        pltpu.emit_pipeline(
            body, grid=(t // SC_WINDOW,),
            in_specs=[pl.BlockSpec((SC_WINDOW, w), lambda i: (i, 0)),
                      pl.BlockSpec((1, SC_WINDOW), lambda i: (i, 0)),
                      pl.BlockSpec((1, SC_WINDOW), lambda i: (i, 0))],
            out_specs=[], core_axis_name=("core", "subcore"),
            dimension_semantics=(pltpu.PARALLEL,))(x_hbm, ia_hbm, ib_hbm)

    return scatter(rows, idx_a, idx_b)


def _moe_ffn_kernel(exp_ref, rows_ref, xs_ref, wg_ref, wu_ref, wd_ref, ys_ref, *, tf):
    j = pl.program_id(0)
    n_rows = rows_ref[j]

    @pl.when(n_rows > 0)
    def _():
        xs = _unpack_bf16_halves(xs_ref[...])
        row = lax.broadcasted_iota(jnp.int32, xs.shape, 0)
        xs = jnp.where(row < n_rows, xs, jnp.zeros_like(xs))
        zero = jnp.zeros(xs.shape, F32)
        ys = _swiglu_tiles(xs, zero, lambda c: wg_ref[0, :, c], lambda c: wu_ref[0, :, c],
                           lambda c: wd_ref[0, c, :], wg_ref.shape[2], tf)
        ys_ref[...] = _pack_bf16_halves(ys)

    @pl.when(n_rows == 0)
    def _():
        ys_ref[...] = jnp.zeros_like(ys_ref)


def _moe_ffn(xs, blk_expert, blk_rows, wg, wu, wd):
    n_slots = xs.shape[0]
    d = wg.shape[1]
    expert = lambda w: pl.BlockSpec((1,) + w.shape[1:], lambda j, e, v: (e[j], 0, 0))
    grid_spec = pltpu.PrefetchScalarGridSpec(
        num_scalar_prefetch=2, grid=(n_slots // BM_MOE,),
        in_specs=[pl.BlockSpec((BM_MOE, d // 2), lambda j, e, v: (j, 0)), expert(wg), expert(wu), expert(wd)],
        out_specs=pl.BlockSpec((BM_MOE, d // 2), lambda j, e, v: (j, 0)))
    return pl.pallas_call(
        functools.partial(_moe_ffn_kernel, tf=TF_FFN), grid_spec=grid_spec,
        out_shape=jax.ShapeDtypeStruct((n_slots, d // 2), jnp.uint32),
        compiler_params=_cparams(("arbitrary",)), name="moe_ffn",
    )(blk_expert, blk_rows, xs, wg, wu, wd)


def _sc_gather_row_pairs(table, idx_a, idx_b):
    w = table.shape[1]
    t = idx_a.shape[0] * SC_WINDOW
    mesh = plsc.VectorSubcoreMesh(core_axis_name="core", subcore_axis_name="subcore")
    out = jax.ShapeDtypeStruct((t, w), table.dtype)

    @pl.kernel(out_type=(out, out), mesh=mesh, scratch_types=[])
    def gather(x_hbm, ia_hbm, ib_hbm, oa_hbm, ob_hbm):
        def body(i_vmem, o_vmem):
            pltpu.sync_copy(x_hbm.at[i_vmem.at[0]], o_vmem)

        for i_hbm, o_hbm in ((ia_hbm, oa_hbm), (ib_hbm, ob_hbm)):
            pltpu.emit_pipeline(
                body, grid=(t // SC_WINDOW,),
                in_specs=[pl.BlockSpec((1, SC_WINDOW), lambda i: (i, 0))],
                out_specs=[pl.BlockSpec((SC_WINDOW, w), lambda i: (i, 0))],
                core_axis_name=("core", "subcore"),
                dimension_semantics=(pltpu.PARALLEL,))(i_hbm, o_hbm)

    return gather(table, idx_a, idx_b)


def _token_slots(pos, pstart_row, unrouted):
    routed = pos >= 0.0
    slot = pos + pstart_row
    return (jnp.min(jnp.where(routed, slot, unrouted), axis=-1, keepdims=True),
            jnp.max(jnp.where(routed, slot, -1.0), axis=-1, keepdims=True), slot, routed)


def _moe_combine_kernel(x_ref, pos_ref, g_ref, pstart_ref, ya_ref, yb_ref, o_ref, *, n_slots):
    slot_a, slot_b, slot, routed = _token_slots(pos_ref[...], pstart_ref[...], float(n_slots))
    gates = g_ref[...]
    gate_a = jnp.sum(jnp.where(routed & (slot == slot_a), gates, 0.0), axis=-1, keepdims=True)
    gate_b = jnp.sum(jnp.where(routed & (slot == slot_b), gates, 0.0), axis=-1, keepdims=True)
    gate_b = jnp.where(slot_b == slot_a, 0.0, gate_b)

    def rows_f32(ref):
        p = ref[...]
        return jnp.concatenate([lax.bitcast_convert_type(p << 16, F32),
                                lax.bitcast_convert_type(p & jnp.uint32(0xFFFF0000), F32)], axis=-1)

    o_ref[...] = x_ref[...] + gate_a * rows_f32(ya_ref) + gate_b * rows_f32(yb_ref)


def _moe_combine(x2, pos, gates, pstart_row, ya, yb, n_slots):
    t, d = x2.shape
    tm = min(CT_MOE, t)
    row = lambda i: (i, 0)
    return pl.pallas_call(
        functools.partial(_moe_combine_kernel, n_slots=n_slots), grid=(t // tm,),
        in_specs=[pl.BlockSpec((tm, d), row), pl.BlockSpec((tm, LANES), row), pl.BlockSpec((tm, LANES), row),
                  pl.BlockSpec((1, LANES), lambda i: (0, 0)),
                  pl.BlockSpec((tm, d // 2), row), pl.BlockSpec((tm, d // 2), row)],
        out_specs=pl.BlockSpec((tm, d), row),
        out_shape=jax.ShapeDtypeStruct((t, d), F32),
        compiler_params=_cparams(("parallel",)), name="moe_combine",
    )(x2, pos, gates, pstart_row, ya, yb)


def _count_le(ascending, x):
    return jnp.sum(ascending[None, :] <= x[:, None], axis=1).astype(jnp.int32)


def _moe_routed(h, x2, gates, wg, wu, wd):
    t, d = x2.shape
    n_e = wg.shape[0]
    n_slots = t * TOP_K + n_e * BM_MOE
    pos, tot = _route_scan(gates)

    counts = tot[0, :n_e].astype(jnp.int32)
    padded = (counts + BM_MOE - 1) // BM_MOE * BM_MOE
    pend = jnp.cumsum(padded)
    pstart = pend - padded

    pstart_row = jnp.zeros((1, LANES), F32).at[0, :n_e].set(pstart.astype(F32))
    slot_a, slot_b, _, _ = _token_slots(pos, pstart_row, float(n_slots))
    slot_a = slot_a.astype(jnp.int32).reshape(-1, SC_WINDOW)
    slot_b = slot_b.astype(jnp.int32).reshape(-1, SC_WINDOW)
    xs = _sc_scatter_rows(h, slot_a, slot_b, n_slots)

    mb = jnp.arange(n_slots // BM_MOE, dtype=jnp.int32) * BM_MOE
    mb_e = jnp.minimum(_count_le(pend, mb), n_e - 1)
    mb_rows = jnp.where(mb < pend[-1], jnp.clip(counts[mb_e] - (mb - pstart[mb_e]), 0, BM_MOE), 0)
    ys = _moe_ffn(xs, mb_e, mb_rows.astype(jnp.int32), wg, wu, wd)

    ya, yb = _sc_gather_row_pairs(ys, slot_a, slot_b)
    return _moe_combine(x2, pos, gates, pstart_row, ya, yb, n_slots)


def _rope_tables(pos):
    half = ROPE_DIM // 2
    inv_freq = ROPE_THETA ** (-2.0 * jnp.arange(half, dtype=F32) / ROPE_DIM)
    ang = pos.astype(F32).reshape(-1, 1) * inv_freq
    cos, sin = jnp.cos(ang), jnp.sin(ang)
    n = ang.shape[0]
    rest = HEAD_DIM - ROPE_DIM
    cos_h = jnp.concatenate([cos, cos, jnp.ones((n, rest), F32)], axis=-1)
    sin_h = jnp.concatenate([-sin, sin, jnp.zeros((n, rest), F32)], axis=-1)
    return jnp.tile(cos_h, (1, HEADS_PER_VREG)), jnp.tile(sin_h, (1, HEADS_PER_VREG))


def _permute_w_in(w, conv_w):
    d = w.shape[0]
    kv_end = Q_W + 6 * KV_W
    g = w[:, kv_end:kv_end + N_HEADS * N_BRANCH]
    pad = jnp.zeros((d, LANES - GATES_PER_KV), w.dtype)
    gate_cols = []
    for kh in range(N_KV_HEADS):
        gate_cols += [g[:, kh * GATES_PER_KV:(kh + 1) * GATES_PER_KV], pad]
    u = w[:, kv_end + N_HEADS * N_BRANCH:]
    return jnp.concatenate([w[:, :kv_end]] + gate_cols + [u], axis=1).astype(BF16)


def _compress_weights(pos_emb, w1, w2):
    hidden = w1.shape[1]
    eye = jnp.eye(N_KV_HEADS, dtype=w1.dtype)
    w1r = w1.reshape(CMP_LEN, HEAD_DIM, hidden)
    halves = []
    for part in (w1r[:CMP_STRIDE], w1r[CMP_STRIDE:]):
        full = jnp.einsum('ldj,hg->lhdgj', part, eye)
        halves.append(full.reshape(CMP_STRIDE * N_KV_HEADS * HEAD_DIM, N_KV_HEADS * hidden).astype(BF16))
    w2p = jnp.einsum('jd,hg->hjgd', w2, eye).reshape(N_KV_HEADS * hidden, N_KV_HEADS * HEAD_DIM).astype(BF16)
    pos = []
    for part in (pos_emb[:CMP_STRIDE], pos_emb[CMP_STRIDE:]):
        pos.append(jnp.broadcast_to(part[:, None, :], (CMP_STRIDE, N_KV_HEADS, HEAD_DIM)).reshape(1, -1))
    return [halves[0], halves[1], w2p, pos[0], pos[1]]


def _selection_map_t(seq):
    ncp = seq // CMP_STRIDE
    n_cmp = (seq - CMP_LEN) // CMP_STRIDE + 1
    c0 = np.arange(ncp) * CMP_STRIDE
    s0 = np.arange(seq // SEL_LEN) * SEL_LEN
    ov = np.minimum(c0[None, :] + CMP_LEN, s0[:, None] + SEL_LEN) - np.maximum(c0[None, :], s0[:, None])
    m = np.clip(ov, 0, None) / CMP_LEN
    m[:, n_cmp:] = 0.0
    return jnp.asarray(m, dtype=BF16)


def kernel(x, positions, attn_norm_g, ffn_norm_g, w_in, w_out, q_norm_g, k_norm_g, cmp_pos_k, cmp_w1_k, cmp_w2_k, cmp_pos_v, cmp_w1_v, cmp_w2_v, conv_w, conv_b, conv_ln_g, conv_ln_b, ffn_w_gate, ffn_w_up, ffn_w_down, moe_router, moe_w_gate, moe_w_up, moe_w_down):
    b, seq, d = x.shape
    t = b * seq
    depth = w_in.shape[0]
    cw = conv_w.shape[2]
    ncp = seq // CMP_STRIDE
    n_cmp = (seq - CMP_LEN) // CMP_STRIDE + 1
    assert seq % max(TQ, KC, TM_PROJ) == 0 and seq >= WINDOW + TQ

    cos_t, sin_t = _rope_tables(positions)
    cmp_end = np.minimum(np.arange(ncp) * CMP_STRIDE + CMP_LEN - 1, seq - 1)
    cos_c, sin_c = _rope_tables(positions[:, cmp_end])
    selmap_t = _selection_map_t(seq)
    tile2 = lambda v: jnp.tile(v.reshape(1, HEAD_DIM), (1, HEADS_PER_VREG))

    x2 = x.reshape(t, d)
    for layer in range(depth):
        w_perm = _permute_w_in(w_in[layer], cw)
        conv_params = (conv_w[layer], conv_b[layer].reshape(1, cw), conv_ln_g[layer].reshape(1, cw),
                       conv_ln_b[layer].reshape(1, cw))
        q, kc, vc, ks, vs, kw, vw, gates, conv = _in_proj(
            x2, attn_norm_g[layer].reshape(1, d), w_perm, cos_t, sin_t,
            tile2(q_norm_g[layer]), tile2(k_norm_g[layer, 1]), tile2(k_norm_g[layer, 2]), conv_params, seq)
        kcmp, vcmp = _compress(
            kc, vc,
            _compress_weights(cmp_pos_k[layer], cmp_w1_k[layer], cmp_w2_k[layer]),
            _compress_weights(cmp_pos_v[layer], cmp_w1_v[layer], cmp_w2_v[layer]),
            tile2(k_norm_g[layer, 0]), cos_c, sin_c, b, ncp)
        score_bound = (HEAD_DIM ** 0.5 * LOG2_E * SCORE_BOUND_MARGIN * jnp.max(jnp.abs(q_norm_g[layer]))
                       * jnp.max(jnp.abs(k_norm_g[layer]))).astype(F32).reshape(1)
        attn = _attention(score_bound, q, kcmp, vcmp, ks, vs, kw, vw, gates, selmap_t, b, seq)
        wo = w_out[layer].astype(BF16)
        g2 = ffn_norm_g[layer].reshape(1, d)
        i = layer // 2
        if layer % 2 == 0:
            x2, h = _out_proj(attn, conv, wo, x2, g2)
            x2 = _ffn(h, x2, ffn_w_gate[i].astype(BF16), ffn_w_up[i].astype(BF16), ffn_w_down[i].astype(BF16))
        else:
            n_e = moe_router.shape[2]
            r = jnp.pad(moe_router[i], ((0, 0), (0, LANES - n_e)))
            r_hi = r.astype(BF16)
            r_lo = (r - r_hi.astype(F32)).astype(BF16)
            x2, h, route = _out_proj(attn, conv, wo, x2, g2, jnp.concatenate([r_hi, r_lo], axis=1), n_e)
            x2 = _moe_routed(h, x2, route, moe_w_gate[i].astype(BF16), moe_w_up[i].astype(BF16),
                             moe_w_down[i].astype(BF16))
    return x2.reshape(b, seq, d)
```

```python
import functools
import math

import jax
import jax.numpy as jnp
import numpy as np
from jax import lax
from jax.experimental import pallas as pl
from jax.experimental.pallas import tpu as pltpu
from jax.experimental.pallas import tpu_sc as plsc

F32 = jnp.float32
BF16 = jnp.bfloat16

N_HEADS = 8
N_KV_HEADS = 2
Q_PER_KV = N_HEADS // N_KV_HEADS
HEAD_DIM = 64
N_BRANCH = 3
CMP_LEN = 32
CMP_STRIDE = 16
SEL_LEN = 64
SEL_TOPK = 16
N_LOCAL_SEL = 2
WINDOW = 512
CONV_KERNEL = 31
ROPE_THETA = 500000.0
ROPE_DIM = HEAD_DIM // 4
TOP_K = 2
EPS = 1e-6

LANES = 128
SUBLANES = 8
LOG2_E = math.log2(math.e)
NEG_BIG = -(2.0 ** 100)
MAX_SCORE_BOUND = 50.0
SCORE_BOUND_MARGIN = 1.02
HEADS_PER_VREG = LANES // HEAD_DIM
VMEM_LIMIT = 56 * 1024 * 1024

TM_PROJ = 512
TQ = 512
KC = 512
CH_CONV = 32
HALO = 32
TM_FFN = 1024
TF_FFN = 512
CT_MOE = 1024
BM_MOE = 512
SC_WINDOW = 64


def _cparams(sem):
    return pltpu.CompilerParams(dimension_semantics=sem, vmem_limit_bytes=VMEM_LIMIT)


def _dot(a, b):
    return jnp.dot(a, b, preferred_element_type=F32)


def _dot_nt(a, b):
    return lax.dot_general(a, b, (((1,), (1,)), ((), ())), preferred_element_type=F32)


def _split_bf16(x):
    hi = x.astype(BF16)
    lo = (x - hi.astype(F32)).astype(BF16)
    return hi, lo


def _rms_rows(x, g):
    ms = jnp.mean(x * x, axis=-1, keepdims=True)
    return x * lax.rsqrt(ms + EPS) * g


def _head_block_ones(width):
    r = lax.broadcasted_iota(jnp.int32, (width, width), 0) // HEAD_DIM
    c = lax.broadcasted_iota(jnp.int32, (width, width), 1) // HEAD_DIM
    return jnp.where(r == c, 1.0, 0.0).astype(BF16)


def _norm_rope(xg, ms, gain, cos, sin):
    y = xg * lax.rsqrt(ms + EPS) * gain
    lane = lax.broadcasted_iota(jnp.int32, y.shape, 1) % HEAD_DIM
    half = ROPE_DIM // 2
    partner = jnp.where(lane < half, pltpu.roll(y, LANES - half, 1), pltpu.roll(y, half, 1))
    return y * cos + partner * sin


def _head_norm_rope(xg, gain, cos, sin, ones_bd):
    ms = _dot((xg * xg).astype(BF16), ones_bd) * (1.0 / HEAD_DIM)
    return _norm_rope(xg, ms, gain, cos, sin)


def _head_norm_rope_pair(xa, xb, gain_a, gain_b, cos, sin, ones_bd2):
    sq = jnp.concatenate([xa * xa, xb * xb], axis=-1).astype(BF16)
    ms = _dot(sq, ones_bd2) * (1.0 / HEAD_DIM)
    return (_norm_rope(xa, ms[:, :LANES], gain_a, cos, sin),
            _norm_rope(xb, ms[:, LANES:], gain_b, cos, sin))


Q_W = N_HEADS * HEAD_DIM
KV_W = N_KV_HEADS * HEAD_DIM
SEG_Q = 0
SEG_KV = Q_W
SEG_GATE = SEG_KV + 6 * KV_W
SEG_UA = SEG_GATE + N_KV_HEADS * LANES
GATES_PER_KV = Q_PER_KV * N_BRANCH


def _causal_conv_tile(glu, seq_start, w_ref, b_ref, lg_ref, lb_ref, o_ref, ext_ref, shift_ref):
    ts = glu.shape[0]
    ext_ref[0:HALO, :] = jnp.where(seq_start, 0.0, ext_ref[ts:ts + HALO, :])
    ext_ref[HALO:HALO + ts, :] = glu
    n_shift = shift_ref.shape[1]
    for r in range(1, SUBLANES):
        shift_ref[r - 1] = ext_ref[r:r + n_shift, :]

    def rows_from(o):
        r = o % SUBLANES
        if r == 0:
            return ext_ref[o:o + CH_CONV, :]
        return shift_ref[r - 1, o - r:o - r + CH_CONV, :]

    w = w_ref[...]
    first_tap = HALO - (CONV_KERNEL - 1)
    for c in range(ts // CH_CONV):
        base = c * CH_CONV + first_tap
        acc = w[0:1, :] * rows_from(base)
        for k in range(1, CONV_KERNEL):
            acc = acc + w[k:k + 1, :] * rows_from(base + k)
        y = acc + b_ref[...]
        yc = y - jnp.mean(y, axis=-1, keepdims=True)
        yn = yc * lax.rsqrt(jnp.mean(yc * yc, axis=-1, keepdims=True) + EPS)
        z = yn * lg_ref[...] + lb_ref[...]
        o_ref[c * CH_CONV:(c + 1) * CH_CONV, :] = (z * jax.nn.sigmoid(z)).astype(o_ref.dtype)


def _in_proj_kernel(x_ref, g_ref, w_ref, cos_ref, sin_ref, qg_ref, ksg_ref, kwg_ref,
                    cw_ref, cb_ref, clg_ref, clb_ref,
                    q_ref, kc_ref, vc_ref, ks_ref, vs_ref, kw_ref, vw_ref, gate_ref, conv_ref,
                    glu_ref, ext_ref, shift_ref, *, conv_w, seq, n_tiles):
    i = pl.program_id(0)
    tm = x_ref.shape[0]

    @pl.when(i == 0)
    def _():
        glu_ref[...] = jnp.zeros_like(glu_ref)
        ext_ref[...] = jnp.zeros_like(ext_ref)

    _causal_conv_tile(glu_ref[...], lax.rem((i - 1) * tm, seq) == 0, cw_ref, cb_ref, clg_ref, clb_ref,
                      conv_ref, ext_ref, shift_ref)

    h = _rms_rows(x_ref[...], g_ref[...]).astype(BF16)
    cos = cos_ref[...]
    sin = sin_ref[...]
    ones_bd = _head_block_ones(2 * LANES)
    lane = lax.broadcasted_iota(jnp.int32, (tm, LANES), 1)
    tile_start = lax.rem(jnp.minimum(i, n_tiles - 1) * tm, seq)
    tok = tile_start + lax.broadcasted_iota(jnp.int32, (tm, 1), 0)
    block_aug = jnp.where(lane - HEAD_DIM == tok // SEL_LEN, NEG_BIG, 0.0)
    ones_aug = jnp.where(lane == HEAD_DIM, 1.0, 0.0)

    def put_heads(ref, first, val, aug):
        for j in range(HEADS_PER_VREG):
            head = val if j == 0 else pltpu.roll(val, LANES - j * HEAD_DIM, 1)
            ref[first + j] = jnp.where(lane < HEAD_DIM, head, aug).astype(ref.dtype)

    qkv = _dot(h, w_ref[:, SEG_Q:SEG_GATE])
    group = lambda j: qkv[:, j * LANES:(j + 1) * LANES]
    first_kv = Q_W // LANES
    scale = HEAD_DIM ** -0.5 * LOG2_E
    qg = qg_ref[...]
    for c in range(0, first_kv, 2):
        pair = _head_norm_rope_pair(group(c), group(c + 1), qg, qg, cos, sin, ones_bd)
        for j, y in enumerate(pair):
            put_heads(q_ref, (c + j) * HEADS_PER_VREG, y * scale, 0.0)
    kc_ref[...] = group(first_kv)
    vc_ref[...] = group(first_kv + 1)
    k_sel, k_win = _head_norm_rope_pair(group(first_kv + 2), group(first_kv + 4), ksg_ref[...], kwg_ref[...],
                                        cos, sin, ones_bd)
    put_heads(ks_ref, 0, k_sel, block_aug)
    put_heads(vs_ref, 0, group(first_kv + 3), ones_aug)
    put_heads(kw_ref, 0, k_win, 0.0)
    put_heads(vw_ref, 0, group(first_kv + 5), ones_aug)
    gate_ref[...] = jax.nn.sigmoid(_dot(h, w_ref[:, SEG_GATE:SEG_UA]))
    u = _dot(h, w_ref[:, SEG_UA:])
    glu_ref[...] = u[:, :conv_w] * jax.nn.sigmoid(u[:, conv_w:])


def _in_proj(x2, g, w_perm, cos_t, sin_t, qg, ksg, kwg, conv_params, seq):
    t, d = x2.shape
    conv_w = (w_perm.shape[1] - SEG_UA) // 2
    tm = min(TM_PROJ, seq)
    n_tiles = t // tm
    row = lambda i: (jnp.minimum(i, n_tiles - 1), 0)
    const = lambda i: (0, 0)
    head_row = lambda i: (0, jnp.minimum(i, n_tiles - 1), 0)
    out_shape = [
        jax.ShapeDtypeStruct((N_HEADS, t, LANES), BF16),
        jax.ShapeDtypeStruct((t, KV_W), F32),
        jax.ShapeDtypeStruct((t, KV_W), F32),
        jax.ShapeDtypeStruct((N_KV_HEADS, t, LANES), BF16),
        jax.ShapeDtypeStruct((N_KV_HEADS, t, LANES), BF16),
        jax.ShapeDtypeStruct((N_KV_HEADS, t, LANES), BF16),
        jax.ShapeDtypeStruct((N_KV_HEADS, t, LANES), BF16),
        jax.ShapeDtypeStruct((t, N_KV_HEADS * LANES), F32),
        jax.ShapeDtypeStruct((t, conv_w), BF16),
    ]
    kv_spec = pl.BlockSpec((N_KV_HEADS, tm, LANES), head_row)
    out_specs = [
        pl.BlockSpec((N_HEADS, tm, LANES), head_row),
        pl.BlockSpec((tm, KV_W), row), pl.BlockSpec((tm, KV_W), row),
        kv_spec, kv_spec, kv_spec, kv_spec,
        pl.BlockSpec((tm, N_KV_HEADS * LANES), row),
        pl.BlockSpec((tm, conv_w), lambda i: (jnp.maximum(i - 1, 0), 0)),
    ]
    in_specs = [
        pl.BlockSpec((tm, d), row), pl.BlockSpec((1, d), const),
        pl.BlockSpec(w_perm.shape, const),
        pl.BlockSpec((tm, LANES), row), pl.BlockSpec((tm, LANES), row),
        pl.BlockSpec((1, LANES), const), pl.BlockSpec((1, LANES), const), pl.BlockSpec((1, LANES), const),
    ] + [pl.BlockSpec(p.shape, const) for p in conv_params]
    return pl.pallas_call(
        functools.partial(_in_proj_kernel, conv_w=conv_w, seq=seq, n_tiles=n_tiles),
        grid=(n_tiles + 1,), in_specs=in_specs, out_specs=out_specs, out_shape=out_shape,
        scratch_shapes=[pltpu.VMEM((tm, conv_w), F32),
                        pltpu.VMEM((tm + HALO, conv_w), F32),
                        pltpu.VMEM((SUBLANES - 1, tm + HALO - SUBLANES, conv_w), F32)],
        compiler_params=_cparams(("arbitrary",)), name="in_proj",
    )(x2, g, w_perm, cos_t, sin_t, qg, ksg, kwg, *conv_params)


def _gelu_tanh(x):
    c = math.sqrt(2.0 / math.pi)
    return 0.5 * x * (1.0 + jnp.tanh(c * (x + 0.044715 * (x * x * x))))


def _compress_kernel(k_ref, v_ref, w1ak_ref, w1bk_ref, w2k_ref, pak_ref, pbk_ref,
                     w1av_ref, w1bv_ref, w2v_ref, pav_ref, pbv_ref,
                     kg_ref, cos_ref, sin_ref, ko_ref, vo_ref):
    def mlp(x_ref, w1a_ref, w1b_ref, w2_ref, pa_ref, pb_ref):
        n = x_ref.shape[0] // CMP_STRIDE
        first = second = None
        for l in range(CMP_STRIDE):
            x = x_ref[pl.ds(l, n, stride=CMP_STRIDE), :]
            cols = slice(l * KV_W, (l + 1) * KV_W)
            fa = _dot((x + pa_ref[:, cols]).astype(BF16), w1a_ref[cols, :])
            fb = _dot((x + pb_ref[:, cols]).astype(BF16), w1b_ref[cols, :])
            first = fa if first is None else first + fa
            second = fb if second is None else second + fb
        hid = first + pltpu.roll(second, n - 1, 0)
        return _dot(_gelu_tanh(hid).astype(BF16), w2_ref[...])

    kc = mlp(k_ref, w1ak_ref, w1bk_ref, w2k_ref, pak_ref, pbk_ref)
    kc = _head_norm_rope(kc, kg_ref[...], cos_ref[...], sin_ref[...], _head_block_ones(LANES))
    vc = mlp(v_ref, w1av_ref, w1bv_ref, w2v_ref, pav_ref, pbv_ref)
    lane = lax.broadcasted_iota(jnp.int32, kc.shape, 1)
    for j in range(N_KV_HEADS):
        for val, ref in ((kc, ko_ref), (vc, vo_ref)):
            head = val if j == 0 else pltpu.roll(val, LANES - j * HEAD_DIM, 1)
            ref[0, j] = jnp.where(lane < HEAD_DIM, head, 0.0).astype(ref.dtype)


def _compress(kc, vc, wk, wv, kg, cosc, sinc, b, ncp):
    seq = kc.shape[0] // b
    const = lambda i: (0, 0)
    row = lambda i: (i, 0)

    def wspecs(ws):
        return [pl.BlockSpec(w.shape, const) for w in ws]

    out = jax.ShapeDtypeStruct((b, N_KV_HEADS, ncp, LANES), BF16)
    ospec = pl.BlockSpec((1, N_KV_HEADS, ncp, LANES), lambda i: (i, 0, 0, 0))
    return pl.pallas_call(
        _compress_kernel, grid=(b,),
        in_specs=[pl.BlockSpec((seq, KV_W), row), pl.BlockSpec((seq, KV_W), row)]
        + wspecs(wk) + wspecs(wv)
        + [pl.BlockSpec((1, LANES), const), pl.BlockSpec((ncp, LANES), row), pl.BlockSpec((ncp, LANES), row)],
        out_specs=[ospec, ospec], out_shape=[out, out],
        compiler_params=_cparams(("parallel",)), name="compress",
    )(kc, vc, *wk, *wv, kg, cosc, sinc)


def _attn_kernel(off_ref, q_ref, kc_ref, vc_ref, ks_ref, vs_ref, kw_ref, vw_ref, gate_ref, selmap_ref,
                 o_ref, *, seq, tq, kc_len, top_n, bounded):
    i = pl.program_id(2)
    t0 = i * tq
    rows = Q_PER_KV * tq
    n_sel = seq // SEL_LEN
    q2 = q_ref[...].reshape(rows, LANES)
    t_row = t0 + (lax.broadcasted_iota(jnp.int32, (rows, 1), 0) & (tq - 1))
    t_tok = t0 + lax.broadcasted_iota(jnp.int32, (tq, 1), 0)
    neg_offset = -off_ref[0] if bounded else 0.0

    def add_bias(s, bias):
        return (s.reshape(s.shape[0] // tq, tq, s.shape[1]) + bias[None]).reshape(s.shape)

    kcmp = kc_ref[0, 0]
    ncp = kcmp.shape[0]
    s_c = _dot_nt(q2, kcmp)
    cmp_end = lax.broadcasted_iota(jnp.int32, (1, ncp), 1) * CMP_STRIDE + (CMP_LEN - 1)
    if bounded:
        e_c = jnp.exp2(add_bias(s_c, jnp.where(cmp_end <= t_tok, neg_offset, NEG_BIG)))
    else:
        s_c = jnp.where(cmp_end <= t_row, s_c, -jnp.inf)
        m_c = jnp.max(s_c, axis=-1, keepdims=True)
        e_c = jnp.exp2(s_c - jnp.where(m_c == -jnp.inf, 0.0, m_c))
    p_c = e_c * (1.0 / jnp.maximum(jnp.sum(e_c, axis=-1, keepdims=True), jnp.finfo(F32).tiny))
    o_c = _dot(p_c.astype(BF16), vc_ref[0, 0])

    def denominator(acc):
        return acc[:, HEAD_DIM:HEAD_DIM + 1]

    p_hi, p_lo = _split_bf16(jnp.sum(p_c.reshape(Q_PER_KV, tq, ncp), axis=0))
    selmap = selmap_ref[...]
    imp = _dot_nt(selmap, p_hi) + _dot_nt(selmap, p_lo)
    blk = lax.broadcasted_iota(jnp.int32, (n_sel, tq), 0)
    cur = (t0 + lax.broadcasted_iota(jnp.int32, (n_sel, tq), 1)) // SEL_LEN
    causal_blk = blk <= cur
    forced = (blk == 0) | (causal_blk & (blk > cur - N_LOCAL_SEL))
    score = jnp.where(forced, jnp.inf, jnp.where(causal_blk, imp, -jnp.inf))
    sub = lax.broadcasted_iota(jnp.int32, (SUBLANES, tq), 0)
    groups = [score[g * SUBLANES:(g + 1) * SUBLANES, :] for g in range(n_sel // SUBLANES)]
    ranks = [jnp.zeros((SUBLANES, tq), F32) for _ in groups]
    for jp in range(n_sel):
        other = jnp.broadcast_to(score[jp:jp + 1, :], (SUBLANES, tq))
        for g, sg in enumerate(groups):
            first = g * SUBLANES
            if first > jp:
                inc = jnp.where(other >= sg, 1.0, 0.0)
            elif first + SUBLANES - 1 <= jp:
                inc = jnp.where(other > sg, 1.0, 0.0)
            else:
                inc = jnp.where(other > sg, 1.0, jnp.where((other == sg) & (sub > jp - first), 1.0, 0.0))
            ranks[g] = ranks[g] + inc
    chosen_flag = neg_offset * (1.0 / NEG_BIG)
    block_flags = jnp.where(jnp.concatenate(ranks, axis=0) < top_n, chosen_flag, 1.0)
    flag_rows = [jnp.zeros((HEAD_DIM, tq), F32), block_flags]
    if HEAD_DIM + n_sel < LANES:
        flag_rows.append(jnp.zeros((LANES - HEAD_DIM - n_sel, tq), F32))
    flags = jnp.transpose(jnp.concatenate(flag_rows, axis=0))
    q_sel = (q2.reshape(Q_PER_KV, tq, LANES) + flags.astype(BF16)[None]).reshape(rows, LANES)

    span = min(WINDOW + tq, seq)
    w0 = pl.multiple_of(jnp.maximum(t0 - WINDOW, 0), tq)
    key_w = w0 + lax.broadcasted_iota(jnp.int32, (1, span), 1)
    bias_w = jnp.where((key_w <= t_tok) & (key_w > t_tok - WINDOW), neg_offset, NEG_BIG)
    s_w = add_bias(_dot_nt(q2, kw_ref[0, pl.ds(w0, span), :]), bias_w)
    if bounded:
        p_w = jnp.exp2(s_w).astype(BF16)
    else:
        p_w = jnp.exp2((s_w - jnp.max(s_w, axis=-1, keepdims=True)).astype(BF16))
    acc_w = _dot(p_w, vw_ref[0, pl.ds(w0, span), :])
    o_w = acc_w * (1.0 / denominator(acc_w))

    def sel_chunk(k0, carry, bias):
        s = _dot_nt(q_sel, ks_ref[0, pl.ds(k0, kc_len), :])
        if bias is not None:
            s = add_bias(s, bias)
        v = vs_ref[0, pl.ds(k0, kc_len), :]
        if bounded:
            (acc,) = carry
            return (acc + _dot(jnp.exp2(s).astype(BF16), v),)
        m, acc = carry
        m_new = jnp.maximum(m, jnp.max(s, axis=-1, keepdims=True))
        p = jnp.exp2((s - m_new).astype(BF16))
        return m_new, jnp.exp2(m - m_new) * acc + _dot(p, v)

    n_full = t0 // kc_len
    init = (jnp.zeros((rows, LANES), F32),)
    if not bounded:
        init = (jnp.full((rows, 1), -jnp.inf, F32),) + init
    carry = lax.fori_loop(
        0, n_full, lambda c, cr: sel_chunk(pl.multiple_of(c * kc_len, kc_len), cr, None), init)
    kd = pl.multiple_of(n_full * kc_len, kc_len)
    key_d = kd + lax.broadcasted_iota(jnp.int32, (1, kc_len), 1)
    acc_s = sel_chunk(kd, carry, jnp.where(key_d <= t_tok, 0.0, NEG_BIG))[-1]
    o_s = acc_s * (1.0 / denominator(acc_s))

    gates = gate_ref[...]

    def gate_col(br):
        cols = [gates[:, g * N_BRANCH + br:g * N_BRANCH + br + 1] for g in range(Q_PER_KV)]
        return jnp.concatenate(cols, axis=0)

    o = gate_col(0) * o_c + gate_col(1) * o_s + gate_col(2) * o_w
    o3 = o.reshape(Q_PER_KV, tq, LANES)
    lane = lax.broadcasted_iota(jnp.int32, (tq, LANES), 1)
    pairs = [jnp.where(lane < HEAD_DIM, o3[g], pltpu.roll(o3[g + 1], HEAD_DIM, 1))
             for g in range(0, Q_PER_KV, HEADS_PER_VREG)]
    o_ref[...] = jnp.concatenate(pairs, axis=-1).astype(o_ref.dtype)


def _attention(score_bound, q, kcmp, vcmp, ks, vs, kw, vw, gates, selmap_t, b, seq):
    t = b * seq
    tq = min(TQ, seq)
    kc_len = min(KC, seq)
    nq = seq // tq
    ncp = kcmp.shape[2]
    n_sel = seq // SEL_LEN
    assert HEAD_DIM + n_sel <= LANES, "selection-block flags must fit beside the head dims"
    top_n = min(SEL_TOPK, n_sel)
    cmp_spec = pl.BlockSpec((1, 1, ncp, LANES), lambda bi, kh, i: (bi, kh, 0, 0))
    seq_spec = pl.BlockSpec((1, seq, LANES), lambda bi, kh, i: (kh, bi, 0))

    def run(bounded):
        return pl.pallas_call(
            functools.partial(_attn_kernel, seq=seq, tq=tq, kc_len=kc_len, top_n=top_n, bounded=bounded),
            grid=(b, N_KV_HEADS, nq),
            in_specs=[
                pl.BlockSpec(memory_space=pltpu.SMEM),
                pl.BlockSpec((Q_PER_KV, tq, LANES), lambda bi, kh, i: (kh, bi * nq + i, 0)),
                cmp_spec, cmp_spec, seq_spec, seq_spec, seq_spec, seq_spec,
                pl.BlockSpec((tq, LANES), lambda bi, kh, i: (bi * nq + i, kh)),
                pl.BlockSpec(selmap_t.shape, lambda bi, kh, i: (0, 0)),
            ],
            out_specs=pl.BlockSpec((tq, Q_PER_KV * HEAD_DIM), lambda bi, kh, i: (bi * nq + i, kh)),
            out_shape=jax.ShapeDtypeStruct((t, N_HEADS * HEAD_DIM), BF16),
            compiler_params=_cparams(("parallel", "parallel", "arbitrary")),
            name="nsa_attention" if bounded else "nsa_attention_running_max",
        )(score_bound, q, kcmp, vcmp, ks, vs, kw, vw, gates, selmap_t)

    return lax.cond(score_bound[0] < MAX_SCORE_BOUND, lambda: run(True), lambda: run(False))


def _top2_gates(logits, n_experts):
    lane = lax.broadcasted_iota(jnp.int32, logits.shape, 1)
    x = jnp.where(lane < n_experts, logits, -jnp.inf)
    m1 = jnp.max(x, axis=-1, keepdims=True)
    i1 = jnp.min(jnp.where(x == m1, lane, LANES), axis=-1, keepdims=True)
    x2 = jnp.where(lane == i1, -jnp.inf, x)
    m2 = jnp.max(x2, axis=-1, keepdims=True)
    i2 = jnp.min(jnp.where(x2 == m2, lane, LANES), axis=-1, keepdims=True)
    e2 = jnp.exp(m2 - m1)
    inv = 1.0 / (1.0 + e2)
    return jnp.where(lane == i1, inv, jnp.where(lane == i2, e2 * inv, 0.0))


def _pack_bf16_halves(x):
    w = x.shape[1] // 2
    bits = lax.bitcast_convert_type(x.astype(BF16).astype(F32), jnp.uint32)
    return (bits[:, :w] >> 16) | (bits[:, w:] & jnp.uint32(0xFFFF0000))


def _unpack_bf16_halves(p):
    lo = lax.bitcast_convert_type(p << 16, F32)
    hi = lax.bitcast_convert_type(p & jnp.uint32(0xFFFF0000), F32)
    return jnp.concatenate([lo, hi], axis=-1).astype(BF16)


def _out_proj_kernel(*refs, n_experts):
    if n_experts:
        attn_ref, conv_ref, wo_ref, x_ref, g_ref, rt_ref, xo_ref, h_ref, gate_ref = refs
    else:
        attn_ref, conv_ref, wo_ref, x_ref, g_ref, xo_ref, h_ref = refs
    aw = attn_ref.shape[1]
    x = x_ref[...] + _dot(attn_ref[...], wo_ref[0:aw, :]) + _dot(conv_ref[...], wo_ref[aw:, :])
    xo_ref[...] = x
    h = _rms_rows(x, g_ref[...])
    if not n_experts:
        h_ref[...] = h.astype(h_ref.dtype)
    else:
        h_ref[...] = _pack_bf16_halves(h)
        h_hi, h_lo = _split_bf16(h)
        router = rt_ref[...]
        by_hi = _dot(h_hi, router)
        logits = by_hi[:, :LANES] + (by_hi[:, LANES:] + _dot(h_lo, router[:, :LANES]))
        gate_ref[...] = _top2_gates(logits, n_experts)


def _out_proj(attn, conv, wo, x2, g, router_split=None, n_experts=0):
    t, d = x2.shape
    tm = min(TM_PROJ, t)
    row = lambda i: (i, 0)
    const = lambda i: (0, 0)
    in_specs = [pl.BlockSpec((tm, attn.shape[1]), row), pl.BlockSpec((tm, conv.shape[1]), row),
                pl.BlockSpec(wo.shape, const), pl.BlockSpec((tm, d), row), pl.BlockSpec((1, d), const)]
    out_shape = [jax.ShapeDtypeStruct((t, d), F32), jax.ShapeDtypeStruct((t, d), BF16)]
    out_specs = [pl.BlockSpec((tm, d), row), pl.BlockSpec((tm, d), row)]
    args = [attn, conv, wo, x2, g]
    if n_experts:
        out_shape[1] = jax.ShapeDtypeStruct((t, d // 2), jnp.uint32)
        out_specs[1] = pl.BlockSpec((tm, d // 2), row)
        in_specs.append(pl.BlockSpec(router_split.shape, const))
        out_shape.append(jax.ShapeDtypeStruct((t, LANES), F32))
        out_specs.append(pl.BlockSpec((tm, LANES), row))
        args.append(router_split)
    return pl.pallas_call(
        functools.partial(_out_proj_kernel, n_experts=n_experts),
        grid=(t // tm,), in_specs=in_specs, out_specs=out_specs, out_shape=out_shape,
        compiler_params=_cparams(("parallel",)), name="out_proj",
    )(*args)


def _swiglu_tiles(h, acc, wg, wu, wd, dff, tf):
    for f in range(dff // tf):
        cols = slice(f * tf, (f + 1) * tf)
        a = _dot(h, wg(cols))
        u = _dot(h, wu(cols))
        acc = acc + _dot(((a * jax.nn.sigmoid(a)) * u).astype(BF16), wd(cols))
    return acc


def _ffn_kernel(h_ref, x_ref, wg_ref, wu_ref, wd_ref, o_ref, *, tf):
    o_ref[...] = _swiglu_tiles(h_ref[...], x_ref[...], lambda c: wg_ref[:, c], lambda c: wu_ref[:, c],
                               lambda c: wd_ref[c, :], wg_ref.shape[1], tf)


def _ffn(h, x2, wg, wu, wd):
    t, d = x2.shape
    tm = min(TM_FFN, t)
    row = lambda i: (i, 0)
    resident = lambda w: pl.BlockSpec(w.shape, lambda i: (0, 0), pipeline_mode=pl.Buffered(1))
    return pl.pallas_call(
        functools.partial(_ffn_kernel, tf=TF_FFN), grid=(t // tm,),
        in_specs=[pl.BlockSpec((tm, d), row), pl.BlockSpec((tm, d), row),
                  resident(wg), resident(wu), resident(wd)],
        out_specs=pl.BlockSpec((tm, d), row),
        out_shape=jax.ShapeDtypeStruct((t, d), F32),
        compiler_params=_cparams(("parallel",)), name="ffn",
    )(h, x2, wg, wu, wd)


def _route_scan_kernel(g_ref, pos_ref, tot_ref, carry_ref):
    c = pl.program_id(0)

    @pl.when(c == 0)
    def _():
        carry_ref[...] = jnp.zeros_like(carry_ref)

    ct = g_ref.shape[0]
    routed = g_ref[...] > 0.0
    a = jnp.where(routed, 1.0, 0.0)
    earlier = lax.broadcasted_iota(jnp.int32, (ct, ct), 1) < lax.broadcasted_iota(jnp.int32, (ct, ct), 0)
    base = carry_ref[...]
    pos = jnp.where(routed, _dot(jnp.where(earlier, 1.0, 0.0).astype(BF16), a.astype(BF16)) + base, -1.0)
    pos_ref[...] = pos
    total = base + jnp.sum(a, axis=0, keepdims=True)
    carry_ref[...] = total
    tot_ref[...] = total


def _route_scan(gates):
    t = gates.shape[0]
    ct = CT_MOE
    nch = t // ct
    return pl.pallas_call(
        _route_scan_kernel, grid=(nch,),
        in_specs=[pl.BlockSpec((ct, LANES), lambda c: (c, 0))],
        out_specs=[pl.BlockSpec((ct, LANES), lambda c: (c, 0)),
                   pl.BlockSpec((1, LANES), lambda c: (0, 0))],
        out_shape=[jax.ShapeDtypeStruct((t, LANES), F32),
                   jax.ShapeDtypeStruct((1, LANES), F32)],
        scratch_shapes=[pltpu.VMEM((1, LANES), F32)],
        compiler_params=_cparams(("arbitrary",)), name="route_scan",
    )(gates)


def _sc_scatter_rows(rows, idx_a, idx_b, n_slots):
    t, w = rows.shape
    mesh = plsc.VectorSubcoreMesh(core_axis_name="core", subcore_axis_name="subcore")

    @pl.kernel(out_type=jax.ShapeDtypeStruct((n_slots, w), rows.dtype), mesh=mesh, scratch_types=[])
    def scatter(x_hbm, ia_hbm, ib_hbm, o_hbm):
        def body(x_vmem, ia_vmem, ib_vmem):
            pltpu.sync_copy(x_vmem, o_hbm.at[ia_vmem.at[0]])
            pltpu.sync_copy(x_vmem, o_hbm.at[ib_vmem.at[0]])

        pltpu.emit_pipeline(
            body, grid=(t // SC_WINDOW,),
            in_specs=[pl.BlockSpec((SC_WINDOW, w), lambda i: (i, 0)),
                      pl.BlockSpec((1, SC_WINDOW), lambda i: (i, 0)),
                      pl.BlockSpec((1, SC_WINDOW), lambda i: (i, 0))],
            out_specs=[], core_axis_name=("core", "subcore"),
            dimension_semantics=(pltpu.PARALLEL,))(x_hbm, ia_hbm, ib_hbm)

    return scatter(rows, idx_a, idx_b)


def _moe_ffn_kernel(exp_ref, rows_ref, xs_ref, wg_ref, wu_ref, wd_ref, ys_ref, *, tf):
    j = pl.program_id(0)
    n_rows = rows_ref[j]

    @pl.when(n_rows > 0)
    def _():
        xs = _unpack_bf16_halves(xs_ref[...])
        row = lax.broadcasted_iota(jnp.int32, xs.shape, 0)
        xs = jnp.where(row < n_rows, xs, jnp.zeros_like(xs))
        zero = jnp.zeros(xs.shape, F32)
        ys = _swiglu_tiles(xs, zero, lambda c: wg_ref[0, :, c], lambda c: wu_ref[0, :, c],
                           lambda c: wd_ref[0, c, :], wg_ref.shape[2], tf)
        ys_ref[...] = _pack_bf16_halves(ys)

    @pl.when(n_rows == 0)
    def _():
        ys_ref[...] = jnp.zeros_like(ys_ref)


def _moe_ffn(xs, blk_expert, blk_rows, wg, wu, wd):
    n_slots = xs.shape[0]
    d = wg.shape[1]
    expert = lambda w: pl.BlockSpec((1,) + w.shape[1:], lambda j, e, v: (e[j], 0, 0))
    grid_spec = pltpu.PrefetchScalarGridSpec(
        num_scalar_prefetch=2, grid=(n_slots // BM_MOE,),
        in_specs=[pl.BlockSpec((BM_MOE, d // 2), lambda j, e, v: (j, 0)), expert(wg), expert(wu), expert(wd)],
        out_specs=pl.BlockSpec((BM_MOE, d // 2), lambda j, e, v: (j, 0)))
    return pl.pallas_call(
        functools.partial(_moe_ffn_kernel, tf=TF_FFN), grid_spec=grid_spec,
        out_shape=jax.ShapeDtypeStruct((n_slots, d // 2), jnp.uint32),
        compiler_params=_cparams(("arbitrary",)), name="moe_ffn",
    )(blk_expert, blk_rows, xs, wg, wu, wd)


def _sc_gather_row_pairs(table, idx_a, idx_b):
    w = table.shape[1]
    t = idx_a.shape[0] * SC_WINDOW
    mesh = plsc.VectorSubcoreMesh(core_axis_name="core", subcore_axis_name="subcore")
    out = jax.ShapeDtypeStruct((t, w), table.dtype)

    @pl.kernel(out_type=(out, out), mesh=mesh, scratch_types=[])
    def gather(x_hbm, ia_hbm, ib_hbm, oa_hbm, ob_hbm):
        def body(i_vmem, o_vmem):
            pltpu.sync_copy(x_hbm.at[i_vmem.at[0]], o_vmem)

        for i_hbm, o_hbm in ((ia_hbm, oa_hbm), (ib_hbm, ob_hbm)):
            pltpu.emit_pipeline(
                body, grid=(t // SC_WINDOW,),
                in_specs=[pl.BlockSpec((1, SC_WINDOW), lambda i: (i, 0))],
                out_specs=[pl.BlockSpec((SC_WINDOW, w), lambda i: (i, 0))],
                core_axis_name=("core", "subcore"),
                dimension_semantics=(pltpu.PARALLEL,))(i_hbm, o_hbm)

    return gather(table, idx_a, idx_b)


def _token_slots(pos, pstart_row, unrouted):
    routed = pos >= 0.0
    slot = pos + pstart_row
    return (jnp.min(jnp.where(routed, slot, unrouted), axis=-1, keepdims=True),
            jnp.max(jnp.where(routed, slot, -1.0), axis=-1, keepdims=True), slot, routed)


def _moe_combine_kernel(x_ref, pos_ref, g_ref, pstart_ref, ya_ref, yb_ref, o_ref, *, n_slots):
    slot_a, slot_b, slot, routed = _token_slots(pos_ref[...], pstart_ref[...], float(n_slots))
    gates = g_ref[...]
    gate_a = jnp.sum(jnp.where(routed & (slot == slot_a), gates, 0.0), axis=-1, keepdims=True)
    gate_b = jnp.sum(jnp.where(routed & (slot == slot_b), gates, 0.0), axis=-1, keepdims=True)
    gate_b = jnp.where(slot_b == slot_a, 0.0, gate_b)

    def rows_f32(ref):
        p = ref[...]
        return jnp.concatenate([lax.bitcast_convert_type(p << 16, F32),
                                lax.bitcast_convert_type(p & jnp.uint32(0xFFFF0000), F32)], axis=-1)

    o_ref[...] = x_ref[...] + gate_a * rows_f32(ya_ref) + gate_b * rows_f32(yb_ref)


def _moe_combine(x2, pos, gates, pstart_row, ya, yb, n_slots):
    t, d = x2.shape
    tm = min(CT_MOE, t)
    row = lambda i: (i, 0)
    return pl.pallas_call(
        functools.partial(_moe_combine_kernel, n_slots=n_slots), grid=(t // tm,),
        in_specs=[pl.BlockSpec((tm, d), row), pl.BlockSpec((tm, LANES), row), pl.BlockSpec((tm, LANES), row),
                  pl.BlockSpec((1, LANES), lambda i: (0, 0)),
                  pl.BlockSpec((tm, d // 2), row), pl.BlockSpec((tm, d // 2), row)],
        out_specs=pl.BlockSpec((tm, d), row),
        out_shape=jax.ShapeDtypeStruct((t, d), F32),
        compiler_params=_cparams(("parallel",)), name="moe_combine",
    )(x2, pos, gates, pstart_row, ya, yb)


def _count_le(ascending, x):
    return jnp.sum(ascending[None, :] <= x[:, None], axis=1).astype(jnp.int32)


def _moe_routed(h, x2, gates, wg, wu, wd):
    t, d = x2.shape
    n_e = wg.shape[0]
    n_slots = t * TOP_K + n_e * BM_MOE
    pos, tot = _route_scan(gates)

    counts = tot[0, :n_e].astype(jnp.int32)
    padded = (counts + BM_MOE - 1) // BM_MOE * BM_MOE
    pend = jnp.cumsum(padded)
    pstart = pend - padded

    pstart_row = jnp.zeros((1, LANES), F32).at[0, :n_e].set(pstart.astype(F32))
    slot_a, slot_b, _, _ = _token_slots(pos, pstart_row, float(n_slots))
    slot_a = slot_a.astype(jnp.int32).reshape(-1, SC_WINDOW)
    slot_b = slot_b.astype(jnp.int32).reshape(-1, SC_WINDOW)
    xs = _sc_scatter_rows(h, slot_a, slot_b, n_slots)

    mb = jnp.arange(n_slots // BM_MOE, dtype=jnp.int32) * BM_MOE
    mb_e = jnp.minimum(_count_le(pend, mb), n_e - 1)
    mb_rows = jnp.where(mb < pend[-1], jnp.clip(counts[mb_e] - (mb - pstart[mb_e]), 0, BM_MOE), 0)
    ys = _moe_ffn(xs, mb_e, mb_rows.astype(jnp.int32), wg, wu, wd)

    ya, yb = _sc_gather_row_pairs(ys, slot_a, slot_b)
    return _moe_combine(x2, pos, gates, pstart_row, ya, yb, n_slots)


def _rope_tables(pos):
    half = ROPE_DIM // 2
    inv_freq = ROPE_THETA ** (-2.0 * jnp.arange(half, dtype=F32) / ROPE_DIM)
    ang = pos.astype(F32).reshape(-1, 1) * inv_freq
    cos, sin = jnp.cos(ang), jnp.sin(ang)
    n = ang.shape[0]
    rest = HEAD_DIM - ROPE_DIM
    cos_h = jnp.concatenate([cos, cos, jnp.ones((n, rest), F32)], axis=-1)
    sin_h = jnp.concatenate([-sin, sin, jnp.zeros((n, rest), F32)], axis=-1)
    return jnp.tile(cos_h, (1, HEADS_PER_VREG)), jnp.tile(sin_h, (1, HEADS_PER_VREG))


def _permute_w_in(w, conv_w):
    d = w.shape[0]
    kv_end = Q_W + 6 * KV_W
    g = w[:, kv_end:kv_end + N_HEADS * N_BRANCH]
    pad = jnp.zeros((d, LANES - GATES_PER_KV), w.dtype)
    gate_cols = []
    for kh in range(N_KV_HEADS):
        gate_cols += [g[:, kh * GATES_PER_KV:(kh + 1) * GATES_PER_KV], pad]
    u = w[:, kv_end + N_HEADS * N_BRANCH:]
    return jnp.concatenate([w[:, :kv_end]] + gate_cols + [u], axis=1).astype(BF16)


def _compress_weights(pos_emb, w1, w2):
    hidden = w1.shape[1]
    eye = jnp.eye(N_KV_HEADS, dtype=w1.dtype)
    w1r = w1.reshape(CMP_LEN, HEAD_DIM, hidden)
    halves = []
    for part in (w1r[:CMP_STRIDE], w1r[CMP_STRIDE:]):
        full = jnp.einsum('ldj,hg->lhdgj', part, eye)
        halves.append(full.reshape(CMP_STRIDE * N_KV_HEADS * HEAD_DIM, N_KV_HEADS * hidden).astype(BF16))
    w2p = jnp.einsum('jd,hg->hjgd', w2, eye).reshape(N_KV_HEADS * hidden, N_KV_HEADS * HEAD_DIM).astype(BF16)
    pos = []
    for part in (pos_emb[:CMP_STRIDE], pos_emb[CMP_STRIDE:]):
        pos.append(jnp.broadcast_to(part[:, None, :], (CMP_STRIDE, N_KV_HEADS, HEAD_DIM)).reshape(1, -1))
    return [halves[0], halves[1], w2p, pos[0], pos[1]]


def _selection_map_t(seq):
    ncp = seq // CMP_STRIDE
    n_cmp = (seq - CMP_LEN) // CMP_STRIDE + 1
    c0 = np.arange(ncp) * CMP_STRIDE
    s0 = np.arange(seq // SEL_LEN) * SEL_LEN
    ov = np.minimum(c0[None, :] + CMP_LEN, s0[:, None] + SEL_LEN) - np.maximum(c0[None, :], s0[:, None])
    m = np.clip(ov, 0, None) / CMP_LEN
    m[:, n_cmp:] = 0.0
    return jnp.asarray(m, dtype=BF16)


def kernel(x, positions, attn_norm_g, ffn_norm_g, w_in, w_out, q_norm_g, k_norm_g, cmp_pos_k, cmp_w1_k, cmp_w2_k, cmp_pos_v, cmp_w1_v, cmp_w2_v, conv_w, conv_b, conv_ln_g, conv_ln_b, ffn_w_gate, ffn_w_up, ffn_w_down, moe_router, moe_w_gate, moe_w_up, moe_w_down):
    b, seq, d = x.shape
    t = b * seq
    depth = w_in.shape[0]
    cw = conv_w.shape[2]
    ncp = seq // CMP_STRIDE
    n_cmp = (seq - CMP_LEN) // CMP_STRIDE + 1
    assert seq % max(TQ, KC, TM_PROJ) == 0 and seq >= WINDOW + TQ

    cos_t, sin_t = _rope_tables(positions)
    cmp_end = np.minimum(np.arange(ncp) * CMP_STRIDE + CMP_LEN - 1, seq - 1)
    cos_c, sin_c = _rope_tables(positions[:, cmp_end])
    selmap_t = _selection_map_t(seq)
    tile2 = lambda v: jnp.tile(v.reshape(1, HEAD_DIM), (1, HEADS_PER_VREG))

    x2 = x.reshape(t, d)
    for layer in range(depth):
        w_perm = _permute_w_in(w_in[layer], cw)
        conv_params = (conv_w[layer], conv_b[layer].reshape(1, cw), conv_ln_g[layer].reshape(1, cw),
                       conv_ln_b[layer].reshape(1, cw))
        q, kc, vc, ks, vs, kw, vw, gates, conv = _in_proj(
            x2, attn_norm_g[layer].reshape(1, d), w_perm, cos_t, sin_t,
            tile2(q_norm_g[layer]), tile2(k_norm_g[layer, 1]), tile2(k_norm_g[layer, 2]), conv_params, seq)
        kcmp, vcmp = _compress(
            kc, vc,
            _compress_weights(cmp_pos_k[layer], cmp_w1_k[layer], cmp_w2_k[layer]),
            _compress_weights(cmp_pos_v[layer], cmp_w1_v[layer], cmp_w2_v[layer]),
            tile2(k_norm_g[layer, 0]), cos_c, sin_c, b, ncp)
        score_bound = (HEAD_DIM ** 0.5 * LOG2_E * SCORE_BOUND_MARGIN * jnp.max(jnp.abs(q_norm_g[layer]))
                       * jnp.max(jnp.abs(k_norm_g[layer]))).astype(F32).reshape(1)
        attn = _attention(score_bound, q, kcmp, vcmp, ks, vs, kw, vw, gates, selmap_t, b, seq)
        wo = w_out[layer].astype(BF16)
        g2 = ffn_norm_g[layer].reshape(1, d)
        i = layer // 2
        if layer % 2 == 0:
            x2, h = _out_proj(attn, conv, wo, x2, g2)
            x2 = _ffn(h, x2, ffn_w_gate[i].astype(BF16), ffn_w_up[i].astype(BF16), ffn_w_down[i].astype(BF16))
        else:
            n_e = moe_router.shape[2]
            r = jnp.pad(moe_router[i], ((0, 0), (0, LANES - n_e)))
            r_hi = r.astype(BF16)
            r_lo = (r - r_hi.astype(F32)).astype(BF16)
            x2, h, route = _out_proj(attn, conv, wo, x2, g2, jnp.concatenate([r_hi, r_lo], axis=1), n_e)
            x2 = _moe_routed(h, x2, route, moe_w_gate[i].astype(BF16), moe_w_up[i].astype(BF16),
                             moe_w_down[i].astype(BF16))
    return x2.reshape(b, seq, d)
```

```python
import functools
import math

import jax
import jax.numpy as jnp
import numpy as np
from jax import lax
from jax.experimental import pallas as pl
from jax.experimental.pallas import tpu as pltpu
from jax.experimental.pallas import tpu_sc as plsc

F32 = jnp.float32
BF16 = jnp.bfloat16

N_HEADS = 8
N_KV_HEADS = 2
Q_PER_KV = N_HEADS // N_KV_HEADS
HEAD_DIM = 64
N_BRANCH = 3
CMP_LEN = 32
CMP_STRIDE = 16
SEL_LEN = 64
SEL_TOPK = 16
N_LOCAL_SEL = 2
WINDOW = 512
CONV_KERNEL = 31
ROPE_THETA = 500000.0
ROPE_DIM = HEAD_DIM // 4
TOP_K = 2
EPS = 1e-6

LANES = 128
SUBLANES = 8
LOG2_E = math.log2(math.e)
NEG_BIG = -(2.0 ** 100)
MAX_SCORE_BOUND = 50.0
SCORE_BOUND_MARGIN = 1.02
HEADS_PER_VREG = LANES // HEAD_DIM
VMEM_LIMIT = 56 * 1024 * 1024

TM_PROJ = 512
TM_OUT = 1024
TQ = 512
KC = 512
CH_CONV = 16
HALO = 32
TM_FFN = 1024
TF_FFN = 512
CT_MOE = 1024
BM_MOE = 512
SC_WINDOW = 64


def _cparams(sem):
    return pltpu.CompilerParams(dimension_semantics=sem, vmem_limit_bytes=VMEM_LIMIT)


def _dot(a, b):
    return jnp.dot(a, b, preferred_element_type=F32)


def _dot_nt(a, b):
    return lax.dot_general(a, b, (((1,), (1,)), ((), ())), preferred_element_type=F32)


def _split_bf16(x):
    hi = x.astype(BF16)
    lo = (x - hi.astype(F32)).astype(BF16)
    return hi, lo


def _rms_rows(x, g):
    ms = jnp.mean(x * x, axis=-1, keepdims=True)
    return x * lax.rsqrt(ms + EPS) * g


def _head_block_ones(width):
    r = lax.broadcasted_iota(jnp.int32, (width, width), 0) // HEAD_DIM
    c = lax.broadcasted_iota(jnp.int32, (width, width), 1) // HEAD_DIM
    return jnp.where(r == c, 1.0 / HEAD_DIM, 0.0).astype(BF16)


def _norm_rope(xg, ms, gain, cos, sin):
    y = xg * lax.rsqrt(ms + EPS) * gain
    lane = lax.broadcasted_iota(jnp.int32, y.shape, 1) % HEAD_DIM
    half = ROPE_DIM // 2
    partner = jnp.where(lane < half, pltpu.roll(y, LANES - half, 1), pltpu.roll(y, half, 1))
    return y * cos + partner * sin


def _head_norm_rope(xg, gain, cos, sin, ones_bd):
    ms = _dot((xg * xg).astype(BF16), ones_bd)
    return _norm_rope(xg, ms, gain, cos, sin)


def _head_norm_rope_pair(xa, xb, gain_a, gain_b, cos, sin, ones_bd2):
    sq = jnp.concatenate([xa * xa, xb * xb], axis=-1).astype(BF16)
    ms = _dot(sq, ones_bd2)
    return (_norm_rope(xa, ms[:, :LANES], gain_a, cos, sin),
            _norm_rope(xb, ms[:, LANES:], gain_b, cos, sin))


Q_W = N_HEADS * HEAD_DIM
KV_W = N_KV_HEADS * HEAD_DIM
SEG_Q = 0
SEG_KV = Q_W
SEG_GATE = SEG_KV + 6 * KV_W
SEG_UA = SEG_GATE + N_KV_HEADS * LANES
GATES_PER_KV = Q_PER_KV * N_BRANCH


def _causal_conv_tile(glu, seq_start, w_ref, b_ref, lg_ref, lb_ref, o_ref, ext_ref, shift_ref):
    ts = glu.shape[0]
    ext_ref[0:HALO, :] = jnp.where(seq_start, 0.0, ext_ref[ts:ts + HALO, :])
    ext_ref[HALO:HALO + ts, :] = glu
    n_shift = shift_ref.shape[1]
    for r in range(1, SUBLANES):
        shift_ref[r - 1] = ext_ref[r:r + n_shift, :]

    def rows_from(o):
        r = o % SUBLANES
        if r == 0:
            return ext_ref[o:o + CH_CONV, :]
        return shift_ref[r - 1, o - r:o - r + CH_CONV, :]

    w = w_ref[...]
    first_tap = HALO - (CONV_KERNEL - 1)
    for c in range(ts // CH_CONV):
        base = c * CH_CONV + first_tap
        acc = w[0:1, :] * rows_from(base)
        for k in range(1, CONV_KERNEL):
            acc = acc + w[k:k + 1, :] * rows_from(base + k)
        y = acc + b_ref[...]
        yc = y - jnp.mean(y, axis=-1, keepdims=True)
        yn = yc * lax.rsqrt(jnp.mean(yc * yc, axis=-1, keepdims=True) + EPS)
        z = yn * lg_ref[...] + lb_ref[...]
        o_ref[c * CH_CONV:(c + 1) * CH_CONV, :] = (z * jax.nn.sigmoid(z)).astype(o_ref.dtype)


def _in_proj_kernel(x_ref, g_ref, w_ref, cos_ref, sin_ref, qg_ref, ksg_ref, kwg_ref,
                    cw_ref, cb_ref, clg_ref, clb_ref,
                    q_ref, kc_ref, vc_ref, ks_ref, vs_ref, kw_ref, vw_ref, gate_ref, conv_ref,
                    glu_ref, ext_ref, shift_ref, *, conv_w, seq, n_tiles):
    i = pl.program_id(0)
    tm = x_ref.shape[0]

    @pl.when(i == 0)
    def _():
        glu_ref[...] = jnp.zeros_like(glu_ref)
        ext_ref[...] = jnp.zeros_like(ext_ref)

    _causal_conv_tile(glu_ref[...], lax.rem((i - 1) * tm, seq) == 0, cw_ref, cb_ref, clg_ref, clb_ref,
                      conv_ref, ext_ref, shift_ref)

    h = _rms_rows(x_ref[...], g_ref[...]).astype(BF16)
    cos = cos_ref[...]
    sin = sin_ref[...]
    ones_bd = _head_block_ones(2 * LANES)
    lane = lax.broadcasted_iota(jnp.int32, (tm, LANES), 1)
    tile_start = lax.rem(jnp.minimum(i, n_tiles - 1) * tm, seq)
    tok = tile_start + lax.broadcasted_iota(jnp.int32, (tm, 1), 0)
    block_aug = jnp.where(lane - HEAD_DIM == tok // SEL_LEN, NEG_BIG, 0.0)
    ones_aug = jnp.where(lane == HEAD_DIM, 1.0, 0.0)

    def put_heads(ref, first, val, aug):
        for j in range(HEADS_PER_VREG):
            head = val if j == 0 else pltpu.roll(val, LANES - j * HEAD_DIM, 1)
            ref[first + j] = jnp.where(lane < HEAD_DIM, head, aug).astype(ref.dtype)

    qkv = _dot(h, w_ref[:, SEG_Q:SEG_GATE])
    group = lambda j: qkv[:, j * LANES:(j + 1) * LANES]
    first_kv = Q_W // LANES
    qg = qg_ref[...]
    for c in range(0, first_kv, 2):
        pair = _head_norm_rope_pair(group(c), group(c + 1), qg, qg, cos, sin, ones_bd)
        for j, y in enumerate(pair):
            put_heads(q_ref, (c + j) * HEADS_PER_VREG, y, 0.0)
    kc_ref[...] = group(first_kv)
    vc_ref[...] = group(first_kv + 1)
    k_sel, k_win = _head_norm_rope_pair(group(first_kv + 2), group(first_kv + 4), ksg_ref[...], kwg_ref[...],
                                        cos, sin, ones_bd)
    put_heads(ks_ref, 0, k_sel, block_aug)
    put_heads(vs_ref, 0, group(first_kv + 3), ones_aug)
    put_heads(kw_ref, 0, k_win, 0.0)
    put_heads(vw_ref, 0, group(first_kv + 5), ones_aug)
    gate_ref[...] = jax.nn.sigmoid(_dot(h, w_ref[:, SEG_GATE:SEG_UA]))
    u = _dot(h, w_ref[:, SEG_UA:])
    glu_ref[...] = u[:, :conv_w] * jax.nn.sigmoid(u[:, conv_w:])


def _in_proj(x2, g, w_perm, cos_t, sin_t, qg, ksg, kwg, conv_params, seq):
    t, d = x2.shape
    conv_w = (w_perm.shape[1] - SEG_UA) // 2
    tm = min(TM_PROJ, seq)
    n_tiles = t // tm
    row = lambda i: (jnp.minimum(i, n_tiles - 1), 0)
    const = lambda i: (0, 0)
    head_row = lambda i: (0, jnp.minimum(i, n_tiles - 1), 0)
    out_shape = [
        jax.ShapeDtypeStruct((N_HEADS, t, LANES), BF16),
        jax.ShapeDtypeStruct((t, KV_W), F32),
        jax.ShapeDtypeStruct((t, KV_W), F32),
        jax.ShapeDtypeStruct((N_KV_HEADS, t, LANES), BF16),
        jax.ShapeDtypeStruct((N_KV_HEADS, t, LANES), BF16),
        jax.ShapeDtypeStruct((N_KV_HEADS, t, LANES), BF16),
        jax.ShapeDtypeStruct((N_KV_HEADS, t, LANES), BF16),
        jax.ShapeDtypeStruct((t, N_KV_HEADS * LANES), F32),
        jax.ShapeDtypeStruct((t, conv_w), BF16),
    ]
    kv_spec = pl.BlockSpec((N_KV_HEADS, tm, LANES), head_row)
    out_specs = [
        pl.BlockSpec((N_HEADS, tm, LANES), head_row),
        pl.BlockSpec((tm, KV_W), row), pl.BlockSpec((tm, KV_W), row),
        kv_spec, kv_spec, kv_spec, kv_spec,
        pl.BlockSpec((tm, N_KV_HEADS * LANES), row),
        pl.BlockSpec((tm, conv_w), lambda i: (jnp.maximum(i - 1, 0), 0)),
    ]
    in_specs = [
        pl.BlockSpec((tm, d), row), pl.BlockSpec((1, d), const),
        pl.BlockSpec(w_perm.shape, const),
        pl.BlockSpec((tm, LANES), row), pl.BlockSpec((tm, LANES), row),
        pl.BlockSpec((1, LANES), const), pl.BlockSpec((1, LANES), const), pl.BlockSpec((1, LANES), const),
    ] + [pl.BlockSpec(p.shape, const) for p in conv_params]
    return pl.pallas_call(
        functools.partial(_in_proj_kernel, conv_w=conv_w, seq=seq, n_tiles=n_tiles),
        grid=(n_tiles + 1,), in_specs=in_specs, out_specs=out_specs, out_shape=out_shape,
        scratch_shapes=[pltpu.VMEM((tm, conv_w), F32),
                        pltpu.VMEM((tm + HALO, conv_w), F32),
                        pltpu.VMEM((SUBLANES - 1, tm + HALO - SUBLANES, conv_w), F32)],
        compiler_params=_cparams(("arbitrary",)), name="in_proj",
    )(x2, g, w_perm, cos_t, sin_t, qg, ksg, kwg, *conv_params)


def _gelu_tanh(x):
    c = math.sqrt(2.0 / math.pi)
    return 0.5 * x * (1.0 + jnp.tanh(c * (x + 0.044715 * (x * x * x))))


def _compress_kernel(k_ref, v_ref, w1ak_ref, w1bk_ref, w2k_ref, pak_ref, pbk_ref,
                     w1av_ref, w1bv_ref, w2v_ref, pav_ref, pbv_ref,
                     kg_ref, cos_ref, sin_ref, ko_ref, vo_ref):
    def mlp(x_ref, w1a_ref, w1b_ref, w2_ref, pa_ref, pb_ref):
        n = x_ref.shape[0] // CMP_STRIDE
        first = second = None
        for l in range(CMP_STRIDE):
            x = x_ref[pl.ds(l, n, stride=CMP_STRIDE), :]
            cols = slice(l * KV_W, (l + 1) * KV_W)
            fa = _dot((x + pa_ref[:, cols]).astype(BF16), w1a_ref[cols, :])
            fb = _dot((x + pb_ref[:, cols]).astype(BF16), w1b_ref[cols, :])
            first = fa if first is None else first + fa
            second = fb if second is None else second + fb
        hid = first + pltpu.roll(second, n - 1, 0)
        return _dot(_gelu_tanh(hid).astype(BF16), w2_ref[...])

    kc = mlp(k_ref, w1ak_ref, w1bk_ref, w2k_ref, pak_ref, pbk_ref)
    kc = _head_norm_rope(kc, kg_ref[...], cos_ref[...], sin_ref[...], _head_block_ones(LANES))
    vc = mlp(v_ref, w1av_ref, w1bv_ref, w2v_ref, pav_ref, pbv_ref)
    lane = lax.broadcasted_iota(jnp.int32, kc.shape, 1)
    for j in range(N_KV_HEADS):
        for val, ref in ((kc, ko_ref), (vc, vo_ref)):
            head = val if j == 0 else pltpu.roll(val, LANES - j * HEAD_DIM, 1)
            ref[0, j] = jnp.where(lane < HEAD_DIM, head, 0.0).astype(ref.dtype)


def _compress(kc, vc, wk, wv, kg, cosc, sinc, b, ncp):
    seq = kc.shape[0] // b
    const = lambda i: (0, 0)
    row = lambda i: (i, 0)

    def wspecs(ws):
        return [pl.BlockSpec(w.shape, const) for w in ws]

    out = jax.ShapeDtypeStruct((b, N_KV_HEADS, ncp, LANES), BF16)
    ospec = pl.BlockSpec((1, N_KV_HEADS, ncp, LANES), lambda i: (i, 0, 0, 0))
    return pl.pallas_call(
        _compress_kernel, grid=(b,),
        in_specs=[pl.BlockSpec((seq, KV_W), row), pl.BlockSpec((seq, KV_W), row)]
        + wspecs(wk) + wspecs(wv)
        + [pl.BlockSpec((1, LANES), const), pl.BlockSpec((ncp, LANES), row), pl.BlockSpec((ncp, LANES), row)],
        out_specs=[ospec, ospec], out_shape=[out, out],
        compiler_params=_cparams(("parallel",)), name="compress",
    )(kc, vc, *wk, *wv, kg, cosc, sinc)


def _attn_kernel(off_ref, q_ref, kc_ref, vc_ref, ks_ref, vs_ref, kw_ref, vw_ref, gate_ref, selmap_ref,
                 o_ref, *, seq, tq, kc_len, top_n, bounded):
    i = pl.program_id(2)
    t0 = i * tq
    rows = Q_PER_KV * tq
    n_sel = seq // SEL_LEN
    q2 = q_ref[...].reshape(rows, LANES)
    t_row = t0 + (lax.broadcasted_iota(jnp.int32, (rows, 1), 0) & (tq - 1))
    t_tok = t0 + lax.broadcasted_iota(jnp.int32, (tq, 1), 0)
    neg_offset = -off_ref[0] if bounded else 0.0

    def add_bias(s, bias):
        return (s.reshape(s.shape[0] // tq, tq, s.shape[1]) + bias[None]).reshape(s.shape)

    kcmp = kc_ref[0, 0]
    ncp = kcmp.shape[0]
    s_c = _dot_nt(q2, kcmp)
    cmp_end = lax.broadcasted_iota(jnp.int32, (1, ncp), 1) * CMP_STRIDE + (CMP_LEN - 1)
    if bounded:
        e_c = jnp.exp2(add_bias(s_c, jnp.where(cmp_end <= t_tok, neg_offset, NEG_BIG)))
    else:
        s_c = jnp.where(cmp_end <= t_row, s_c, -jnp.inf)
        m_c = jnp.max(s_c, axis=-1, keepdims=True)
        e_c = jnp.exp2(s_c - jnp.where(m_c == -jnp.inf, 0.0, m_c))
    p_c = e_c * (1.0 / jnp.maximum(jnp.sum(e_c, axis=-1, keepdims=True), jnp.finfo(F32).tiny))
    o_c = _dot(p_c.astype(BF16), vc_ref[0, 0])

    def denominator(acc):
        return acc[:, HEAD_DIM:HEAD_DIM + 1]

    p_hi, p_lo = _split_bf16(jnp.sum(p_c.reshape(Q_PER_KV, tq, ncp), axis=0))
    selmap = selmap_ref[...]
    imp = _dot_nt(selmap, p_hi) + _dot_nt(selmap, p_lo)
    blk = lax.broadcasted_iota(jnp.int32, (n_sel, tq), 0)
    cur = (t0 + lax.broadcasted_iota(jnp.int32, (n_sel, tq), 1)) // SEL_LEN
    causal_blk = blk <= cur
    forced = (blk == 0) | (causal_blk & (blk > cur - N_LOCAL_SEL))
    score = jnp.where(forced, jnp.inf, jnp.where(causal_blk, imp, -jnp.inf))
    sub = lax.broadcasted_iota(jnp.int32, (SUBLANES, tq), 0)
    groups = [score[g * SUBLANES:(g + 1) * SUBLANES, :] for g in range(n_sel // SUBLANES)]
    ranks = [jnp.zeros((SUBLANES, tq), F32) for _ in groups]
    for jp in range(n_sel):
        other = jnp.broadcast_to(score[jp:jp + 1, :], (SUBLANES, tq))
        for g, sg in enumerate(groups):
            first = g * SUBLANES
            if first > jp:
                inc = jnp.where(other >= sg, 1.0, 0.0)
            elif first + SUBLANES - 1 <= jp:
                inc = jnp.where(other > sg, 1.0, 0.0)
            else:
                inc = jnp.where(other > sg, 1.0, jnp.where((other == sg) & (sub > jp - first), 1.0, 0.0))
            ranks[g] = ranks[g] + inc
    chosen_flag = neg_offset * (1.0 / NEG_BIG)
    block_flags = jnp.where(jnp.concatenate(ranks, axis=0) < top_n, chosen_flag, 1.0)
    flag_rows = [jnp.zeros((HEAD_DIM, tq), F32), block_flags]
    if HEAD_DIM + n_sel < LANES:
        flag_rows.append(jnp.zeros((LANES - HEAD_DIM - n_sel, tq), F32))
    flags = jnp.transpose(jnp.concatenate(flag_rows, axis=0))
    q_sel = (q2.reshape(Q_PER_KV, tq, LANES) + flags.astype(BF16)[None]).reshape(rows, LANES)

    span = min(WINDOW + tq, seq)
    w0 = pl.multiple_of(jnp.maximum(t0 - WINDOW, 0), tq)
    key_w = w0 + lax.broadcasted_iota(jnp.int32, (1, span), 1)
    bias_w = jnp.where((key_w <= t_tok) & (key_w > t_tok - WINDOW), neg_offset, NEG_BIG)
    s_w = add_bias(_dot_nt(q2, kw_ref[0, pl.ds(w0, span), :]), bias_w)
    if bounded:
        p_w = jnp.exp2(s_w).astype(BF16)
    else:
        p_w = jnp.exp2((s_w - jnp.max(s_w, axis=-1, keepdims=True)).astype(BF16))
    acc_w = _dot(p_w, vw_ref[0, pl.ds(w0, span), :])
    o_w = acc_w * (1.0 / denominator(acc_w))

    def sel_chunk(k0, carry, bias):
        s = _dot_nt(q_sel, ks_ref[0, pl.ds(k0, kc_len), :])
        if bias is not None:
            s = add_bias(s, bias)
        v = vs_ref[0, pl.ds(k0, kc_len), :]
        if bounded:
            (acc,) = carry
            return (acc + _dot(jnp.exp2(s).astype(BF16), v),)
        m, acc = carry
        m_new = jnp.maximum(m, jnp.max(s, axis=-1, keepdims=True))
        p = jnp.exp2((s - m_new).astype(BF16))
        return m_new, jnp.exp2(m - m_new) * acc + _dot(p, v)

    n_full = t0 // kc_len
    init = (jnp.zeros((rows, LANES), F32),)
    if not bounded:
        init = (jnp.full((rows, 1), -jnp.inf, F32),) + init
    carry = lax.fori_loop(
        0, n_full, lambda c, cr: sel_chunk(pl.multiple_of(c * kc_len, kc_len), cr, None), init)
    kd = pl.multiple_of(n_full * kc_len, kc_len)
    key_d = kd + lax.broadcasted_iota(jnp.int32, (1, kc_len), 1)
    acc_s = sel_chunk(kd, carry, jnp.where(key_d <= t_tok, 0.0, NEG_BIG))[-1]
    o_s = acc_s * (1.0 / denominator(acc_s))

    gates = gate_ref[...]

    def gate_col(br):
        cols = [gates[:, g * N_BRANCH + br:g * N_BRANCH + br + 1] for g in range(Q_PER_KV)]
        return jnp.concatenate(cols, axis=0)

    o = gate_col(0) * o_c + gate_col(1) * o_s + gate_col(2) * o_w
    o3 = o.reshape(Q_PER_KV, tq, LANES)
    lane = lax.broadcasted_iota(jnp.int32, (tq, LANES), 1)
    pairs = [jnp.where(lane < HEAD_DIM, o3[g], pltpu.roll(o3[g + 1], HEAD_DIM, 1))
             for g in range(0, Q_PER_KV, HEADS_PER_VREG)]
    o_ref[...] = jnp.concatenate(pairs, axis=-1).astype(o_ref.dtype)


def _attention(score_bound, q, kcmp, vcmp, ks, vs, kw, vw, gates, selmap_t, b, seq):
    t = b * seq
    tq = min(TQ, seq)
    kc_len = min(KC, seq)
    nq = seq // tq
    ncp = kcmp.shape[2]
    n_sel = seq // SEL_LEN
    assert HEAD_DIM + n_sel <= LANES, "selection-block flags must fit beside the head dims"
    top_n = min(SEL_TOPK, n_sel)
    cmp_spec = pl.BlockSpec((1, 1, ncp, LANES), lambda bi, kh, i: (bi, kh, 0, 0))
    seq_spec = pl.BlockSpec((1, seq, LANES), lambda bi, kh, i: (kh, bi, 0))

    def run(bounded):
        return pl.pallas_call(
            functools.partial(_attn_kernel, seq=seq, tq=tq, kc_len=kc_len, top_n=top_n, bounded=bounded),
            grid=(b, N_KV_HEADS, nq),
            in_specs=[
                pl.BlockSpec(memory_space=pltpu.SMEM),
                pl.BlockSpec((Q_PER_KV, tq, LANES), lambda bi, kh, i: (kh, bi * nq + i, 0)),
                cmp_spec, cmp_spec, seq_spec, seq_spec, seq_spec, seq_spec,
                pl.BlockSpec((tq, LANES), lambda bi, kh, i: (bi * nq + i, kh)),
                pl.BlockSpec(selmap_t.shape, lambda bi, kh, i: (0, 0)),
            ],
            out_specs=pl.BlockSpec((tq, Q_PER_KV * HEAD_DIM), lambda bi, kh, i: (bi * nq + i, kh)),
            out_shape=jax.ShapeDtypeStruct((t, N_HEADS * HEAD_DIM), BF16),
            compiler_params=_cparams(("parallel", "parallel", "arbitrary")),
            name="nsa_attention" if bounded else "nsa_attention_running_max",
        )(score_bound, q, kcmp, vcmp, ks, vs, kw, vw, gates, selmap_t)

    return lax.cond(score_bound[0] < MAX_SCORE_BOUND, lambda: run(True), lambda: run(False))


def _top2_gates(logits, n_experts):
    lane = lax.broadcasted_iota(jnp.int32, logits.shape, 1)
    x = jnp.where(lane < n_experts, logits, -jnp.inf)
    m1 = jnp.max(x, axis=-1, keepdims=True)
    i1 = jnp.min(jnp.where(x == m1, lane, LANES), axis=-1, keepdims=True)
    x2 = jnp.where(lane == i1, -jnp.inf, x)
    m2 = jnp.max(x2, axis=-1, keepdims=True)
    i2 = jnp.min(jnp.where(x2 == m2, lane, LANES), axis=-1, keepdims=True)
    e2 = jnp.exp(m2 - m1)
    inv = 1.0 / (1.0 + e2)
    return jnp.where(lane == i1, inv, jnp.where(lane == i2, e2 * inv, 0.0))


def _pack_bf16_halves(x):
    w = x.shape[1] // 2
    bits = lax.bitcast_convert_type(x.astype(BF16).astype(F32), jnp.uint32)
    return (bits[:, :w] >> 16) | (bits[:, w:] & jnp.uint32(0xFFFF0000))


def _unpack_bf16_halves(p):
    lo = lax.bitcast_convert_type(p << 16, F32)
    hi = lax.bitcast_convert_type(p & jnp.uint32(0xFFFF0000), F32)
    return jnp.concatenate([lo, hi], axis=-1).astype(BF16)


def _out_proj_kernel(*refs, n_experts):
    if n_experts:
        attn_ref, conv_ref, wo_ref, x_ref, g_ref, rt_ref, xo_ref, h_ref, gate_ref = refs
    else:
        attn_ref, conv_ref, wo_ref, x_ref, g_ref, xo_ref, h_ref = refs
    aw = attn_ref.shape[1]
    x = x_ref[...] + _dot(attn_ref[...], wo_ref[0:aw, :]) + _dot(conv_ref[...], wo_ref[aw:, :])
    xo_ref[...] = x
    h = _rms_rows(x, g_ref[...])
    if not n_experts:
        h_ref[...] = h.astype(h_ref.dtype)
    else:
        h_ref[...] = _pack_bf16_halves(h)
        h_hi, h_lo = _split_bf16(h)
        router = rt_ref[...]
        by_hi = _dot(h_hi, router)
        logits = by_hi[:, :LANES] + (by_hi[:, LANES:] + _dot(h_lo, router[:, :LANES]))
        gate_ref[...] = _top2_gates(logits, n_experts)


def _out_proj(attn, conv, wo, x2, g, router_split=None, n_experts=0):
    t, d = x2.shape
    tm = min(TM_OUT, t)
    row = lambda i: (i, 0)
    const = lambda i: (0, 0)
    in_specs = [pl.BlockSpec((tm, attn.shape[1]), row), pl.BlockSpec((tm, conv.shape[1]), row),
                pl.BlockSpec(wo.shape, const), pl.BlockSpec((tm, d), row), pl.BlockSpec((1, d), const)]
    out_shape = [jax.ShapeDtypeStruct((t, d), F32), jax.ShapeDtypeStruct((t, d), BF16)]
    out_specs = [pl.BlockSpec((tm, d), row), pl.BlockSpec((tm, d), row)]
    args = [attn, conv, wo, x2, g]
    if n_experts:
        out_shape[1] = jax.ShapeDtypeStruct((t, d // 2), jnp.uint32)
        out_specs[1] = pl.BlockSpec((tm, d // 2), row)
        in_specs.append(pl.BlockSpec(router_split.shape, const))
        out_shape.append(jax.ShapeDtypeStruct((t, LANES), F32))
        out_specs.append(pl.BlockSpec((tm, LANES), row))
        args.append(router_split)
    return pl.pallas_call(
        functools.partial(_out_proj_kernel, n_experts=n_experts),
        grid=(t // tm,), in_specs=in_specs, out_specs=out_specs, out_shape=out_shape,
        compiler_params=_cparams(("parallel",)), name="out_proj",
    )(*args)


def _swiglu_tiles(h, acc, wg, wu, wd, dff, tf):
    for f in range(dff // tf):
        cols = slice(f * tf, (f + 1) * tf)
        a = _dot(h, wg(cols))
        u = _dot(h, wu(cols))
        acc = acc + _dot(((a * jax.nn.sigmoid(a)) * u).astype(BF16), wd(cols))
    return acc


def _ffn_kernel(h_ref, x_ref, wg_ref, wu_ref, wd_ref, o_ref, *, tf):
    o_ref[...] = _swiglu_tiles(h_ref[...], x_ref[...], lambda c: wg_ref[:, c], lambda c: wu_ref[:, c],
                               lambda c: wd_ref[c, :], wg_ref.shape[1], tf)


def _ffn(h, x2, wg, wu, wd):
    t, d = x2.shape
    tm = min(TM_FFN, t)
    row = lambda i: (i, 0)
    resident = lambda w: pl.BlockSpec(w.shape, lambda i: (0, 0), pipeline_mode=pl.Buffered(1))
    return pl.pallas_call(
        functools.partial(_ffn_kernel, tf=TF_FFN), grid=(t // tm,),
        in_specs=[pl.BlockSpec((tm, d), row), pl.BlockSpec((tm, d), row),
                  resident(wg), resident(wu), resident(wd)],
        out_specs=pl.BlockSpec((tm, d), row),
        out_shape=jax.ShapeDtypeStruct((t, d), F32),
        compiler_params=_cparams(("parallel",)), name="ffn",
    )(h, x2, wg, wu, wd)


def _route_scan_kernel(g_ref, pos_ref, tot_ref, carry_ref):
    c = pl.program_id(0)

    @pl.when(c == 0)
    def _():
        carry_ref[...] = jnp.zeros_like(carry_ref)

    ct = g_ref.shape[0]
    routed = g_ref[...] > 0.0
    a = jnp.where(routed, 1.0, 0.0)
    earlier = lax.broadcasted_iota(jnp.int32, (ct, ct), 1) < lax.broadcasted_iota(jnp.int32, (ct, ct), 0)
    base = carry_ref[...]
    pos = jnp.where(routed, _dot(jnp.where(earlier, 1.0, 0.0).astype(BF16), a.astype(BF16)) + base, -1.0)
    pos_ref[...] = pos
    total = base + jnp.sum(a, axis=0, keepdims=True)
    carry_ref[...] = total
    tot_ref[...] = total


def _route_scan(gates):
    t = gates.shape[0]
    ct = CT_MOE
    nch = t // ct
    return pl.pallas_call(
        _route_scan_kernel, grid=(nch,),
        in_specs=[pl.BlockSpec((ct, LANES), lambda c: (c, 0))],
        out_specs=[pl.BlockSpec((ct, LANES), lambda c: (c, 0)),
                   pl.BlockSpec((1, LANES), lambda c: (0, 0))],
        out_shape=[jax.ShapeDtypeStruct((t, LANES), F32),
                   jax.ShapeDtypeStruct((1, LANES), F32)],
        scratch_shapes=[pltpu.VMEM((1, LANES), F32)],
        compiler_params=_cparams(("arbitrary",)), name="route_scan",
    )(gates)


def _sc_scatter_rows(rows, idx_a, idx_b, n_slots):
    t, w = rows.shape
    mesh = plsc.VectorSubcoreMesh(core_axis_name="core", subcore_axis_name="subcore")

    @pl.kernel(out_type=jax.ShapeDtypeStruct((n_slots, w), rows.dtype), mesh=mesh, scratch_types=[])
    def scatter(x_hbm, ia_hbm, ib_hbm, o_hbm):
        def body(x_vmem, ia_vmem, ib_vmem):
            pltpu.sync_copy(x_vmem, o_hbm.at[ia_vmem.at[0]])
            pltpu.sync_copy(x_vmem, o_hbm.at[ib_vmem.at[0]])

        pltpu.emit_pipeline(
            body, grid=(t // SC_WINDOW,),
            in_specs=[pl.BlockSpec((SC_WINDOW, w), lambda i: (i, 0)),
                      pl.BlockSpec((1, SC_WINDOW), lambda i: (i, 0)),
                      pl.BlockSpec((1, SC_WINDOW), lambda i: (i, 0))],
            out_specs=[], core_axis_name=("core", "subcore"),
            dimension_semantics=(pltpu.PARALLEL,))(x_hbm, ia_hbm, ib_hbm)

    return scatter(rows, idx_a, idx_b)


def _moe_ffn_kernel(exp_ref, rows_ref, xs_ref, wg_ref, wu_ref, wd_ref, ys_ref, *, tf):
    j = pl.program_id(0)
    n_rows = rows_ref[j]

    @pl.when(n_rows > 0)
    def _():
        xs = _unpack_bf16_halves(xs_ref[...])
        row = lax.broadcasted_iota(jnp.int32, xs.shape, 0)
        xs = jnp.where(row < n_rows, xs, jnp.zeros_like(xs))
        zero = jnp.zeros(xs.shape, F32)
        ys = _swiglu_tiles(xs, zero, lambda c: wg_ref[0, :, c], lambda c: wu_ref[0, :, c],
                           lambda c: wd_ref[0, c, :], wg_ref.shape[2], tf)
        ys_ref[...] = _pack_bf16_halves(ys)

    @pl.when(n_rows == 0)
    def _():
        ys_ref[...] = jnp.zeros_like(ys_ref)


def _moe_ffn(xs, blk_expert, blk_rows, wg, wu, wd):
    n_slots = xs.shape[0]
    d = wg.shape[1]
    expert = lambda w: pl.BlockSpec((1,) + w.shape[1:], lambda j, e, v: (e[j], 0, 0))
    grid_spec = pltpu.PrefetchScalarGridSpec(
        num_scalar_prefetch=2, grid=(n_slots // BM_MOE,),
        in_specs=[pl.BlockSpec((BM_MOE, d // 2), lambda j, e, v: (j, 0)), expert(wg), expert(wu), expert(wd)],
        out_specs=pl.BlockSpec((BM_MOE, d // 2), lambda j, e, v: (j, 0)))
    return pl.pallas_call(
        functools.partial(_moe_ffn_kernel, tf=TF_FFN), grid_spec=grid_spec,
        out_shape=jax.ShapeDtypeStruct((n_slots, d // 2), jnp.uint32),
        compiler_params=_cparams(("arbitrary",)), name="moe_ffn",
    )(blk_expert, blk_rows, xs, wg, wu, wd)


def _sc_gather_row_pairs(table, idx_a, idx_b):
    w = table.shape[1]
    t = idx_a.shape[0] * SC_WINDOW
    mesh = plsc.VectorSubcoreMesh(core_axis_name="core", subcore_axis_name="subcore")
    out = jax.ShapeDtypeStruct((t, w), table.dtype)

    @pl.kernel(out_type=(out, out), mesh=mesh, scratch_types=[])
    def gather(x_hbm, ia_hbm, ib_hbm, oa_hbm, ob_hbm):
        def body(i_vmem, o_vmem):
            pltpu.sync_copy(x_hbm.at[i_vmem.at[0]], o_vmem)

        for i_hbm, o_hbm in ((ia_hbm, oa_hbm), (ib_hbm, ob_hbm)):
            pltpu.emit_pipeline(
                body, grid=(t // SC_WINDOW,),
                in_specs=[pl.BlockSpec((1, SC_WINDOW), lambda i: (i, 0))],
                out_specs=[pl.BlockSpec((SC_WINDOW, w), lambda i: (i, 0))],
                core_axis_name=("core", "subcore"),
                dimension_semantics=(pltpu.PARALLEL,))(i_hbm, o_hbm)

    return gather(table, idx_a, idx_b)


def _token_slots(pos, pstart_row, unrouted):
    routed = pos >= 0.0
    slot = pos + pstart_row
    return (jnp.min(jnp.where(routed, slot, unrouted), axis=-1, keepdims=True),
            jnp.max(jnp.where(routed, slot, -1.0), axis=-1, keepdims=True), slot, routed)


def _moe_combine_kernel(x_ref, pos_ref, g_ref, pstart_ref, ya_ref, yb_ref, o_ref, *, n_slots):
    slot_a, slot_b, slot, routed = _token_slots(pos_ref[...], pstart_ref[...], float(n_slots))
    gates = g_ref[...]
    gate_a = jnp.sum(jnp.where(routed & (slot == slot_a), gates, 0.0), axis=-1, keepdims=True)
    gate_b = jnp.sum(jnp.where(routed & (slot == slot_b), gates, 0.0), axis=-1, keepdims=True)
    gate_b = jnp.where(slot_b == slot_a, 0.0, gate_b)

    def rows_f32(ref):
        p = ref[...]
        return jnp.concatenate([lax.bitcast_convert_type(p << 16, F32),
                                lax.bitcast_convert_type(p & jnp.uint32(0xFFFF0000), F32)], axis=-1)

    o_ref[...] = x_ref[...] + gate_a * rows_f32(ya_ref) + gate_b * rows_f32(yb_ref)


def _moe_combine(x2, pos, gates, pstart_row, ya, yb, n_slots):
    t, d = x2.shape
    tm = min(CT_MOE, t)
    row = lambda i: (i, 0)
    return pl.pallas_call(
        functools.partial(_moe_combine_kernel, n_slots=n_slots), grid=(t // tm,),
        in_specs=[pl.BlockSpec((tm, d), row), pl.BlockSpec((tm, LANES), row), pl.BlockSpec((tm, LANES), row),
                  pl.BlockSpec((1, LANES), lambda i: (0, 0)),
                  pl.BlockSpec((tm, d // 2), row), pl.BlockSpec((tm, d // 2), row)],
        out_specs=pl.BlockSpec((tm, d), row),
        out_shape=jax.ShapeDtypeStruct((t, d), F32),
        compiler_params=_cparams(("parallel",)), name="moe_combine",
    )(x2, pos, gates, pstart_row, ya, yb)


def _count_le(ascending, x):
    return jnp.sum(ascending[None, :] <= x[:, None], axis=1).astype(jnp.int32)


def _moe_routed(h, x2, gates, wg, wu, wd):
    t, d = x2.shape
    n_e = wg.shape[0]
    n_slots = t * TOP_K + n_e * BM_MOE
    pos, tot = _route_scan(gates)

    counts = tot[0, :n_e].astype(jnp.int32)
    padded = (counts + BM_MOE - 1) // BM_MOE * BM_MOE
    pend = jnp.cumsum(padded)
    pstart = pend - padded

    pstart_row = jnp.zeros((1, LANES), F32).at[0, :n_e].set(pstart.astype(F32))
    slot_a, slot_b, _, _ = _token_slots(pos, pstart_row, float(n_slots))
    slot_a = slot_a.astype(jnp.int32).reshape(-1, SC_WINDOW)
    slot_b = slot_b.astype(jnp.int32).reshape(-1, SC_WINDOW)
    xs = _sc_scatter_rows(h, slot_a, slot_b, n_slots)

    mb = jnp.arange(n_slots // BM_MOE, dtype=jnp.int32) * BM_MOE
    mb_e = jnp.minimum(_count_le(pend, mb), n_e - 1)
    mb_rows = jnp.where(mb < pend[-1], jnp.clip(counts[mb_e] - (mb - pstart[mb_e]), 0, BM_MOE), 0)
    ys = _moe_ffn(xs, mb_e, mb_rows.astype(jnp.int32), wg, wu, wd)

    ya, yb = _sc_gather_row_pairs(ys, slot_a, slot_b)
    return _moe_combine(x2, pos, gates, pstart_row, ya, yb, n_slots)


def _rope_tables(pos):
    half = ROPE_DIM // 2
    inv_freq = ROPE_THETA ** (-2.0 * jnp.arange(half, dtype=F32) / ROPE_DIM)
    ang = pos.astype(F32).reshape(-1, 1) * inv_freq
    cos, sin = jnp.cos(ang), jnp.sin(ang)
    n = ang.shape[0]
    rest = HEAD_DIM - ROPE_DIM
    cos_h = jnp.concatenate([cos, cos, jnp.ones((n, rest), F32)], axis=-1)
    sin_h = jnp.concatenate([-sin, sin, jnp.zeros((n, rest), F32)], axis=-1)
    return jnp.tile(cos_h, (1, HEADS_PER_VREG)), jnp.tile(sin_h, (1, HEADS_PER_VREG))


def _permute_w_in(w, conv_w):
    d = w.shape[0]
    kv_end = Q_W + 6 * KV_W
    g = w[:, kv_end:kv_end + N_HEADS * N_BRANCH]
    pad = jnp.zeros((d, LANES - GATES_PER_KV), w.dtype)
    gate_cols = []
    for kh in range(N_KV_HEADS):
        gate_cols += [g[:, kh * GATES_PER_KV:(kh + 1) * GATES_PER_KV], pad]
    u = w[:, kv_end + N_HEADS * N_BRANCH:]
    return jnp.concatenate([w[:, :kv_end]] + gate_cols + [u], axis=1).astype(BF16)


def _compress_weights(pos_emb, w1, w2):
    hidden = w1.shape[1]
    eye = jnp.eye(N_KV_HEADS, dtype=w1.dtype)
    w1r = w1.reshape(CMP_LEN, HEAD_DIM, hidden)
    halves = []
    for part in (w1r[:CMP_STRIDE], w1r[CMP_STRIDE:]):
        full = jnp.einsum('ldj,hg->lhdgj', part, eye)
        halves.append(full.reshape(CMP_STRIDE * N_KV_HEADS * HEAD_DIM, N_KV_HEADS * hidden).astype(BF16))
    w2p = jnp.einsum('jd,hg->hjgd', w2, eye).reshape(N_KV_HEADS * hidden, N_KV_HEADS * HEAD_DIM).astype(BF16)
    pos = []
    for part in (pos_emb[:CMP_STRIDE], pos_emb[CMP_STRIDE:]):
        pos.append(jnp.broadcast_to(part[:, None, :], (CMP_STRIDE, N_KV_HEADS, HEAD_DIM)).reshape(1, -1))
    return [halves[0], halves[1], w2p, pos[0], pos[1]]


def _selection_map_t(seq):
    ncp = seq // CMP_STRIDE
    n_cmp = (seq - CMP_LEN) // CMP_STRIDE + 1
    c0 = np.arange(ncp) * CMP_STRIDE
    s0 = np.arange(seq // SEL_LEN) * SEL_LEN
    ov = np.minimum(c0[None, :] + CMP_LEN, s0[:, None] + SEL_LEN) - np.maximum(c0[None, :], s0[:, None])
    m = np.clip(ov, 0, None) / CMP_LEN
    m[:, n_cmp:] = 0.0
    return jnp.asarray(m, dtype=BF16)


def kernel(x, positions, attn_norm_g, ffn_norm_g, w_in, w_out, q_norm_g, k_norm_g, cmp_pos_k, cmp_w1_k, cmp_w2_k, cmp_pos_v, cmp_w1_v, cmp_w2_v, conv_w, conv_b, conv_ln_g, conv_ln_b, ffn_w_gate, ffn_w_up, ffn_w_down, moe_router, moe_w_gate, moe_w_up, moe_w_down):
    b, seq, d = x.shape
    t = b * seq
    depth = w_in.shape[0]
    cw = conv_w.shape[2]
    ncp = seq // CMP_STRIDE
    n_cmp = (seq - CMP_LEN) // CMP_STRIDE + 1
    assert seq % max(TQ, KC, TM_PROJ) == 0 and seq >= WINDOW + TQ

    cos_t, sin_t = _rope_tables(positions)
    cmp_end = np.minimum(np.arange(ncp) * CMP_STRIDE + CMP_LEN - 1, seq - 1)
    cos_c, sin_c = _rope_tables(positions[:, cmp_end])
    selmap_t = _selection_map_t(seq)
    tile2 = lambda v: jnp.tile(v.reshape(1, HEAD_DIM), (1, HEADS_PER_VREG))

    x2 = x.reshape(t, d)
    for layer in range(depth):
        w_perm = _permute_w_in(w_in[layer], cw)
        conv_params = (conv_w[layer], conv_b[layer].reshape(1, cw), conv_ln_g[layer].reshape(1, cw),
                       conv_ln_b[layer].reshape(1, cw))
        q, kc, vc, ks, vs, kw, vw, gates, conv = _in_proj(
            x2, attn_norm_g[layer].reshape(1, d), w_perm, cos_t, sin_t,
            tile2(q_norm_g[layer]) * (HEAD_DIM ** -0.5 * LOG2_E),
            tile2(k_norm_g[layer, 1]), tile2(k_norm_g[layer, 2]), conv_params, seq)
        kcmp, vcmp = _compress(
            kc, vc,
            _compress_weights(cmp_pos_k[layer], cmp_w1_k[layer], cmp_w2_k[layer]),
            _compress_weights(cmp_pos_v[layer], cmp_w1_v[layer], cmp_w2_v[layer]),
            tile2(k_norm_g[layer, 0]), cos_c, sin_c, b, ncp)
        score_bound = (HEAD_DIM ** 0.5 * LOG2_E * SCORE_BOUND_MARGIN * jnp.max(jnp.abs(q_norm_g[layer]))
                       * jnp.max(jnp.abs(k_norm_g[layer]))).astype(F32).reshape(1)
        attn = _attention(score_bound, q, kcmp, vcmp, ks, vs, kw, vw, gates, selmap_t, b, seq)
        wo = w_out[layer].astype(BF16)
        g2 = ffn_norm_g[layer].reshape(1, d)
        i = layer // 2
        if layer % 2 == 0:
            x2, h = _out_proj(attn, conv, wo, x2, g2)
            x2 = _ffn(h, x2, ffn_w_gate[i].astype(BF16), ffn_w_up[i].astype(BF16), ffn_w_down[i].astype(BF16))
        else:
            n_e = moe_router.shape[2]
            r = jnp.pad(moe_router[i], ((0, 0), (0, LANES - n_e)))
            r_hi = r.astype(BF16)
            r_lo = (r - r_hi.astype(F32)).astype(BF16)
            x2, h, route = _out_proj(attn, conv, wo, x2, g2, jnp.concatenate([r_hi, r_lo], axis=1), n_e)
            x2 = _moe_routed(h, x2, route, moe_w_gate[i].astype(BF16), moe_w_up[i].astype(BF16),
                             moe_w_down[i].astype(BF16))
    return x2.reshape(b, seq, d)
```

```python
import functools
import math

import jax
import jax.numpy as jnp
import numpy as np
from jax import lax
from jax.experimental import pallas as pl
from jax.experimental.pallas import tpu as pltpu
from jax.experimental.pallas import tpu_sc as plsc

F32 = jnp.float32
BF16 = jnp.bfloat16

N_HEADS = 8
N_KV_HEADS = 2
Q_PER_KV = N_HEADS // N_KV_HEADS
HEAD_DIM = 64
N_BRANCH = 3
CMP_LEN = 32
CMP_STRIDE = 16
SEL_LEN = 64
SEL_TOPK = 16
N_LOCAL_SEL = 2
WINDOW = 512
CONV_KERNEL = 31
ROPE_THETA = 500000.0
ROPE_DIM = HEAD_DIM // 4
TOP_K = 2
EPS = 1e-6

LANES = 128
SUBLANES = 8
LOG2_E = math.log2(math.e)
NEG_BIG = -(2.0 ** 100)
MAX_SCORE_BOUND = 50.0
SCORE_BOUND_MARGIN = 1.02
HEADS_PER_VREG = LANES // HEAD_DIM
VMEM_LIMIT = 56 * 1024 * 1024

TM_PROJ = 512
TM_OUT = 1024
TQ = 512
KC = 512
CH_CONV = 16
HALO = 32
TM_FFN = 1024
TF_FFN = 256
CT_MOE = 1024
BM_MOE = 512
SC_WINDOW = 64


def _cparams(sem):
    return pltpu.CompilerParams(dimension_semantics=sem, vmem_limit_bytes=VMEM_LIMIT)


def _dot(a, b):
    return jnp.dot(a, b, preferred_element_type=F32)


def _dot_nt(a, b):
    return lax.dot_general(a, b, (((1,), (1,)), ((), ())), preferred_element_type=F32)


def _split_bf16(x):
    hi = x.astype(BF16)
    lo = (x - hi.astype(F32)).astype(BF16)
    return hi, lo


def _rms_rows(x, g):
    ms = jnp.mean(x * x, axis=-1, keepdims=True)
    return x * lax.rsqrt(ms + EPS) * g


def _head_block_ones(width):
    r = lax.broadcasted_iota(jnp.int32, (width, width), 0) // HEAD_DIM
    c = lax.broadcasted_iota(jnp.int32, (width, width), 1) // HEAD_DIM
    return jnp.where(r == c, 1.0 / HEAD_DIM, 0.0).astype(BF16)


def _norm_rope(xg, ms, gain, cos, sin):
    y = xg * lax.rsqrt(ms + EPS) * gain
    lane = lax.broadcasted_iota(jnp.int32, y.shape, 1) % HEAD_DIM
    half = ROPE_DIM // 2
    partner = jnp.where(lane < half, pltpu.roll(y, LANES - half, 1), pltpu.roll(y, half, 1))
    return y * cos + partner * sin


def _head_norm_rope(xg, gain, cos, sin, ones_bd):
    ms = _dot((xg * xg).astype(BF16), ones_bd)
    return _norm_rope(xg, ms, gain, cos, sin)


def _head_norm_rope_pair(xa, xb, gain_a, gain_b, cos, sin, ones_bd2):
    sq = jnp.concatenate([xa * xa, xb * xb], axis=-1).astype(BF16)
    ms = _dot(sq, ones_bd2)
    return (_norm_rope(xa, ms[:, :LANES], gain_a, cos, sin),
            _norm_rope(xb, ms[:, LANES:], gain_b, cos, sin))


Q_W = N_HEADS * HEAD_DIM
KV_W = N_KV_HEADS * HEAD_DIM
SEG_Q = 0
SEG_KV = Q_W
SEG_GATE = SEG_KV + 6 * KV_W
SEG_UA = SEG_GATE + N_KV_HEADS * LANES
GATES_PER_KV = Q_PER_KV * N_BRANCH


def _causal_conv_tile(glu, seq_start, w_ref, b_ref, lg_ref, lb_ref, o_ref, ext_ref, shift_ref):
    ts = glu.shape[0]
    ext_ref[0:HALO, :] = jnp.where(seq_start, 0.0, ext_ref[ts:ts + HALO, :])
    ext_ref[HALO:HALO + ts, :] = glu
    n_shift = shift_ref.shape[1]
    for r in range(1, SUBLANES):
        shift_ref[r - 1] = ext_ref[r:r + n_shift, :]

    def rows_from(o):
        r = o % SUBLANES
        if r == 0:
            return ext_ref[o:o + CH_CONV, :]
        return shift_ref[r - 1, o - r:o - r + CH_CONV, :]

    w = w_ref[...]
    first_tap = HALO - (CONV_KERNEL - 1)
    for c in range(ts // CH_CONV):
        base = c * CH_CONV + first_tap
        acc = w[0:1, :] * rows_from(base)
        for k in range(1, CONV_KERNEL):
            acc = acc + w[k:k + 1, :] * rows_from(base + k)
        y = acc + b_ref[...]
        yc = y - jnp.mean(y, axis=-1, keepdims=True)
        yn = yc * lax.rsqrt(jnp.mean(yc * yc, axis=-1, keepdims=True) + EPS)
        z = yn * lg_ref[...] + lb_ref[...]
        o_ref[c * CH_CONV:(c + 1) * CH_CONV, :] = (z * jax.nn.sigmoid(z)).astype(o_ref.dtype)


def _in_proj_kernel(x_ref, g_ref, w_ref, cos_ref, sin_ref, qg_ref, ksg_ref, kwg_ref,
                    cw_ref, cb_ref, clg_ref, clb_ref,
                    q_ref, kc_ref, vc_ref, ks_ref, vs_ref, kw_ref, vw_ref, gate_ref, conv_ref,
                    glu_ref, ext_ref, shift_ref, *, conv_w, seq, n_tiles):
    i = pl.program_id(0)
    tm = x_ref.shape[0]

    @pl.when(i == 0)
    def _():
        glu_ref[...] = jnp.zeros_like(glu_ref)
        ext_ref[...] = jnp.zeros_like(ext_ref)

    _causal_conv_tile(glu_ref[...], lax.rem((i - 1) * tm, seq) == 0, cw_ref, cb_ref, clg_ref, clb_ref,
                      conv_ref, ext_ref, shift_ref)

    h = _rms_rows(x_ref[...], g_ref[...]).astype(BF16)
    cos = cos_ref[...]
    sin = sin_ref[...]
    ones_bd = _head_block_ones(2 * LANES)
    lane = lax.broadcasted_iota(jnp.int32, (tm, LANES), 1)
    tile_start = lax.rem(jnp.minimum(i, n_tiles - 1) * tm, seq)
    tok = tile_start + lax.broadcasted_iota(jnp.int32, (tm, 1), 0)
    block_aug = jnp.where(lane - HEAD_DIM == tok // SEL_LEN, NEG_BIG, 0.0)
    ones_aug = jnp.where(lane == HEAD_DIM, 1.0, 0.0)

    def put_heads(ref, first, val, aug):
        for j in range(HEADS_PER_VREG):
            head = val if j == 0 else pltpu.roll(val, LANES - j * HEAD_DIM, 1)
            ref[first + j] = jnp.where(lane < HEAD_DIM, head, aug).astype(ref.dtype)

    qkv = _dot(h, w_ref[:, SEG_Q:SEG_GATE])
    group = lambda j: qkv[:, j * LANES:(j + 1) * LANES]
    first_kv = Q_W // LANES
    qg = qg_ref[...]
    for c in range(0, first_kv, 2):
        pair = _head_norm_rope_pair(group(c), group(c + 1), qg, qg, cos, sin, ones_bd)
        for j, y in enumerate(pair):
            put_heads(q_ref, (c + j) * HEADS_PER_VREG, y, 0.0)
    kc_ref[...] = group(first_kv)
    vc_ref[...] = group(first_kv + 1)
    k_sel, k_win = _head_norm_rope_pair(group(first_kv + 2), group(first_kv + 4), ksg_ref[...], kwg_ref[...],
                                        cos, sin, ones_bd)
    put_heads(ks_ref, 0, k_sel, block_aug)
    put_heads(vs_ref, 0, group(first_kv + 3), ones_aug)
    put_heads(kw_ref, 0, k_win, 0.0)
    put_heads(vw_ref, 0, group(first_kv + 5), ones_aug)
    gate_ref[...] = jax.nn.sigmoid(_dot(h, w_ref[:, SEG_GATE:SEG_UA]))
    u = _dot(h, w_ref[:, SEG_UA:])
    glu_ref[...] = u[:, :conv_w] * jax.nn.sigmoid(u[:, conv_w:])


def _in_proj(x2, g, w_perm, cos_t, sin_t, qg, ksg, kwg, conv_params, seq):
    t, d = x2.shape
    conv_w = (w_perm.shape[1] - SEG_UA) // 2
    tm = min(TM_PROJ, seq)
    n_tiles = t // tm
    row = lambda i: (jnp.minimum(i, n_tiles - 1), 0)
    const = lambda i: (0, 0)
    head_row = lambda i: (0, jnp.minimum(i, n_tiles - 1), 0)
    out_shape = [
        jax.ShapeDtypeStruct((N_HEADS, t, LANES), BF16),
        jax.ShapeDtypeStruct((t, KV_W), F32),
        jax.ShapeDtypeStruct((t, KV_W), F32),
        jax.ShapeDtypeStruct((N_KV_HEADS, t, LANES), BF16),
        jax.ShapeDtypeStruct((N_KV_HEADS, t, LANES), BF16),
        jax.ShapeDtypeStruct((N_KV_HEADS, t, LANES), BF16),
        jax.ShapeDtypeStruct((N_KV_HEADS, t, LANES), BF16),
        jax.ShapeDtypeStruct((t, N_KV_HEADS * LANES), F32),
        jax.ShapeDtypeStruct((t, conv_w), BF16),
    ]
    kv_spec = pl.BlockSpec((N_KV_HEADS, tm, LANES), head_row)
    out_specs = [
        pl.BlockSpec((N_HEADS, tm, LANES), head_row),
        pl.BlockSpec((tm, KV_W), row), pl.BlockSpec((tm, KV_W), row),
        kv_spec, kv_spec, kv_spec, kv_spec,
        pl.BlockSpec((tm, N_KV_HEADS * LANES), row),
        pl.BlockSpec((tm, conv_w), lambda i: (jnp.maximum(i - 1, 0), 0)),
    ]
    in_specs = [
        pl.BlockSpec((tm, d), row), pl.BlockSpec((1, d), const),
        pl.BlockSpec(w_perm.shape, const),
        pl.BlockSpec((tm, LANES), row), pl.BlockSpec((tm, LANES), row),
        pl.BlockSpec((1, LANES), const), pl.BlockSpec((1, LANES), const), pl.BlockSpec((1, LANES), const),
    ] + [pl.BlockSpec(p.shape, const) for p in conv_params]
    return pl.pallas_call(
        functools.partial(_in_proj_kernel, conv_w=conv_w, seq=seq, n_tiles=n_tiles),
        grid=(n_tiles + 1,), in_specs=in_specs, out_specs=out_specs, out_shape=out_shape,
        scratch_shapes=[pltpu.VMEM((tm, conv_w), F32),
                        pltpu.VMEM((tm + HALO, conv_w), F32),
                        pltpu.VMEM((SUBLANES - 1, tm + HALO - SUBLANES, conv_w), F32)],
        compiler_params=_cparams(("arbitrary",)), name="in_proj",
    )(x2, g, w_perm, cos_t, sin_t, qg, ksg, kwg, *conv_params)


def _gelu_tanh(x):
    c = math.sqrt(2.0 / math.pi)
    return 0.5 * x * (1.0 + jnp.tanh(c * (x + 0.044715 * (x * x * x))))


def _compress_kernel(k_ref, v_ref, w1ak_ref, w1bk_ref, w2k_ref, pak_ref, pbk_ref,
                     w1av_ref, w1bv_ref, w2v_ref, pav_ref, pbv_ref,
                     kg_ref, cos_ref, sin_ref, ko_ref, vo_ref):
    def mlp(x_ref, w1a_ref, w1b_ref, w2_ref, pa_ref, pb_ref):
        n = x_ref.shape[0] // CMP_STRIDE
        first = second = None
        for l in range(CMP_STRIDE):
            x = x_ref[pl.ds(l, n, stride=CMP_STRIDE), :]
            cols = slice(l * KV_W, (l + 1) * KV_W)
            fa = _dot((x + pa_ref[:, cols]).astype(BF16), w1a_ref[cols, :])
            fb = _dot((x + pb_ref[:, cols]).astype(BF16), w1b_ref[cols, :])
            first = fa if first is None else first + fa
            second = fb if second is None else second + fb
        hid = first + pltpu.roll(second, n - 1, 0)
        return _dot(_gelu_tanh(hid).astype(BF16), w2_ref[...])

    kc = mlp(k_ref, w1ak_ref, w1bk_ref, w2k_ref, pak_ref, pbk_ref)
    kc = _head_norm_rope(kc, kg_ref[...], cos_ref[...], sin_ref[...], _head_block_ones(LANES))
    vc = mlp(v_ref, w1av_ref, w1bv_ref, w2v_ref, pav_ref, pbv_ref)
    lane = lax.broadcasted_iota(jnp.int32, kc.shape, 1)
    for j in range(N_KV_HEADS):
        for val, ref in ((kc, ko_ref), (vc, vo_ref)):
            head = val if j == 0 else pltpu.roll(val, LANES - j * HEAD_DIM, 1)
            ref[0, j] = jnp.where(lane < HEAD_DIM, head, 0.0).astype(ref.dtype)


def _compress(kc, vc, wk, wv, kg, cosc, sinc, b, ncp):
    seq = kc.shape[0] // b
    const = lambda i: (0, 0)
    row = lambda i: (i, 0)

    def wspecs(ws):
        return [pl.BlockSpec(w.shape, const) for w in ws]

    out = jax.ShapeDtypeStruct((b, N_KV_HEADS, ncp, LANES), BF16)
    ospec = pl.BlockSpec((1, N_KV_HEADS, ncp, LANES), lambda i: (i, 0, 0, 0))
    return pl.pallas_call(
        _compress_kernel, grid=(b,),
        in_specs=[pl.BlockSpec((seq, KV_W), row), pl.BlockSpec((seq, KV_W), row)]
        + wspecs(wk) + wspecs(wv)
        + [pl.BlockSpec((1, LANES), const), pl.BlockSpec((ncp, LANES), row), pl.BlockSpec((ncp, LANES), row)],
        out_specs=[ospec, ospec], out_shape=[out, out],
        compiler_params=_cparams(("parallel",)), name="compress",
    )(kc, vc, *wk, *wv, kg, cosc, sinc)


def _attn_kernel(off_ref, q_ref, kc_ref, vc_ref, ks_ref, vs_ref, kw_ref, vw_ref, gate_ref, selmap_ref,
                 o_ref, *, seq, tq, kc_len, top_n, bounded):
    i = pl.program_id(2)
    t0 = i * tq
    rows = Q_PER_KV * tq
    n_sel = seq // SEL_LEN
    q2 = q_ref[...].reshape(rows, LANES)
    t_row = t0 + (lax.broadcasted_iota(jnp.int32, (rows, 1), 0) & (tq - 1))
    t_tok = t0 + lax.broadcasted_iota(jnp.int32, (tq, 1), 0)
    neg_offset = -off_ref[0] if bounded else 0.0

    def add_bias(s, bias):
        return (s.reshape(s.shape[0] // tq, tq, s.shape[1]) + bias[None]).reshape(s.shape)

    kcmp = kc_ref[0, 0]
    ncp = kcmp.shape[0]
    s_c = _dot_nt(q2, kcmp)
    cmp_end = lax.broadcasted_iota(jnp.int32, (1, ncp), 1) * CMP_STRIDE + (CMP_LEN - 1)
    if bounded:
        e_c = jnp.exp2(add_bias(s_c, jnp.where(cmp_end <= t_tok, neg_offset, NEG_BIG)))
    else:
        s_c = jnp.where(cmp_end <= t_row, s_c, -jnp.inf)
        m_c = jnp.max(s_c, axis=-1, keepdims=True)
        e_c = jnp.exp2(s_c - jnp.where(m_c == -jnp.inf, 0.0, m_c))
    p_c = e_c * (1.0 / jnp.maximum(jnp.sum(e_c, axis=-1, keepdims=True), jnp.finfo(F32).tiny))
    o_c = _dot(p_c.astype(BF16), vc_ref[0, 0])

    def denominator(acc):
        return acc[:, HEAD_DIM:HEAD_DIM + 1]

    p_hi, p_lo = _split_bf16(jnp.sum(p_c.reshape(Q_PER_KV, tq, ncp), axis=0))
    selmap = selmap_ref[...]
    imp = _dot_nt(selmap, p_hi) + _dot_nt(selmap, p_lo)
    blk = lax.broadcasted_iota(jnp.int32, (n_sel, tq), 0)
    cur = (t0 + lax.broadcasted_iota(jnp.int32, (n_sel, tq), 1)) // SEL_LEN
    causal_blk = blk <= cur
    forced = (blk == 0) | (causal_blk & (blk > cur - N_LOCAL_SEL))
    score = jnp.where(forced, jnp.inf, jnp.where(causal_blk, imp, -jnp.inf))
    sub = lax.broadcasted_iota(jnp.int32, (SUBLANES, tq), 0)
    groups = [score[g * SUBLANES:(g + 1) * SUBLANES, :] for g in range(n_sel // SUBLANES)]
    ranks = [jnp.zeros((SUBLANES, tq), F32) for _ in groups]
    for jp in range(n_sel):
        other = jnp.broadcast_to(score[jp:jp + 1, :], (SUBLANES, tq))
        for g, sg in enumerate(groups):
            first = g * SUBLANES
            if first > jp:
                inc = jnp.where(other >= sg, 1.0, 0.0)
            elif first + SUBLANES - 1 <= jp:
                inc = jnp.where(other > sg, 1.0, 0.0)
            else:
                inc = jnp.where(other > sg, 1.0, jnp.where((other == sg) & (sub > jp - first), 1.0, 0.0))
            ranks[g] = ranks[g] + inc
    chosen_flag = neg_offset * (1.0 / NEG_BIG)
    block_flags = jnp.where(jnp.concatenate(ranks, axis=0) < top_n, chosen_flag, 1.0)
    flag_rows = [jnp.zeros((HEAD_DIM, tq), F32), block_flags]
    if HEAD_DIM + n_sel < LANES:
        flag_rows.append(jnp.zeros((LANES - HEAD_DIM - n_sel, tq), F32))
    flags = jnp.transpose(jnp.concatenate(flag_rows, axis=0))
    q_sel = (q2.reshape(Q_PER_KV, tq, LANES) + flags.astype(BF16)[None]).reshape(rows, LANES)

    span = min(WINDOW + tq, seq)
    w0 = pl.multiple_of(jnp.maximum(t0 - WINDOW, 0), tq)
    key_w = w0 + lax.broadcasted_iota(jnp.int32, (1, span), 1)
    bias_w = jnp.where((key_w <= t_tok) & (key_w > t_tok - WINDOW), neg_offset, NEG_BIG)
    s_w = add_bias(_dot_nt(q2, kw_ref[0, pl.ds(w0, span), :]), bias_w)
    if bounded:
        p_w = jnp.exp2(s_w).astype(BF16)
    else:
        p_w = jnp.exp2((s_w - jnp.max(s_w, axis=-1, keepdims=True)).astype(BF16))
    acc_w = _dot(p_w, vw_ref[0, pl.ds(w0, span), :])
    o_w = acc_w * (1.0 / denominator(acc_w))

    def sel_chunk(k0, carry, bias):
        s = _dot_nt(q_sel, ks_ref[0, pl.ds(k0, kc_len), :])
        if bias is not None:
            s = add_bias(s, bias)
        v = vs_ref[0, pl.ds(k0, kc_len), :]
        if bounded:
            (acc,) = carry
            return (acc + _dot(jnp.exp2(s).astype(BF16), v),)
        m, acc = carry
        m_new = jnp.maximum(m, jnp.max(s, axis=-1, keepdims=True))
        p = jnp.exp2((s - m_new).astype(BF16))
        return m_new, jnp.exp2(m - m_new) * acc + _dot(p, v)

    n_full = t0 // kc_len
    init = (jnp.zeros((rows, LANES), F32),)
    if not bounded:
        init = (jnp.full((rows, 1), -jnp.inf, F32),) + init
    carry = lax.fori_loop(
        0, n_full, lambda c, cr: sel_chunk(pl.multiple_of(c * kc_len, kc_len), cr, None), init)
    kd = pl.multiple_of(n_full * kc_len, kc_len)
    key_d = kd + lax.broadcasted_iota(jnp.int32, (1, kc_len), 1)
    acc_s = sel_chunk(kd, carry, jnp.where(key_d <= t_tok, 0.0, NEG_BIG))[-1]
    o_s = acc_s * (1.0 / denominator(acc_s))

    gates = gate_ref[...]

    def gate_col(br):
        cols = [gates[:, g * N_BRANCH + br:g * N_BRANCH + br + 1] for g in range(Q_PER_KV)]
        return jnp.concatenate(cols, axis=0)

    o = gate_col(0) * o_c + gate_col(1) * o_s + gate_col(2) * o_w
    o3 = o.reshape(Q_PER_KV, tq, LANES)
    lane = lax.broadcasted_iota(jnp.int32, (tq, LANES), 1)
    pairs = [jnp.where(lane < HEAD_DIM, o3[g], pltpu.roll(o3[g + 1], HEAD_DIM, 1))
             for g in range(0, Q_PER_KV, HEADS_PER_VREG)]
    o_ref[...] = jnp.concatenate(pairs, axis=-1).astype(o_ref.dtype)


def _attention(score_bound, q, kcmp, vcmp, ks, vs, kw, vw, gates, selmap_t, b, seq):
    t = b * seq
    tq = min(TQ, seq)
    kc_len = min(KC, seq)
    nq = seq // tq
    ncp = kcmp.shape[2]
    n_sel = seq // SEL_LEN
    assert HEAD_DIM + n_sel <= LANES, "selection-block flags must fit beside the head dims"
    top_n = min(SEL_TOPK, n_sel)
    cmp_spec = pl.BlockSpec((1, 1, ncp, LANES), lambda bi, kh, i: (bi, kh, 0, 0))
    seq_spec = pl.BlockSpec((1, seq, LANES), lambda bi, kh, i: (kh, bi, 0))

    def run(bounded):
        return pl.pallas_call(
            functools.partial(_attn_kernel, seq=seq, tq=tq, kc_len=kc_len, top_n=top_n, bounded=bounded),
            grid=(b, N_KV_HEADS, nq),
            in_specs=[
                pl.BlockSpec(memory_space=pltpu.SMEM),
                pl.BlockSpec((Q_PER_KV, tq, LANES), lambda bi, kh, i: (kh, bi * nq + i, 0)),
                cmp_spec, cmp_spec, seq_spec, seq_spec, seq_spec, seq_spec,
                pl.BlockSpec((tq, LANES), lambda bi, kh, i: (bi * nq + i, kh)),
                pl.BlockSpec(selmap_t.shape, lambda bi, kh, i: (0, 0)),
            ],
            out_specs=pl.BlockSpec((tq, Q_PER_KV * HEAD_DIM), lambda bi, kh, i: (bi * nq + i, kh)),
            out_shape=jax.ShapeDtypeStruct((t, N_HEADS * HEAD_DIM), BF16),
            compiler_params=_cparams(("parallel", "parallel", "arbitrary")),
            name="nsa_attention" if bounded else "nsa_attention_running_max",
        )(score_bound, q, kcmp, vcmp, ks, vs, kw, vw, gates, selmap_t)

    return lax.cond(score_bound[0] < MAX_SCORE_BOUND, lambda: run(True), lambda: run(False))


def _top2_gates(logits, n_experts):
    lane = lax.broadcasted_iota(jnp.int32, logits.shape, 1)
    x = jnp.where(lane < n_experts, logits, -jnp.inf)
    m1 = jnp.max(x, axis=-1, keepdims=True)
    i1 = jnp.min(jnp.where(x == m1, lane, LANES), axis=-1, keepdims=True)
    x2 = jnp.where(lane == i1, -jnp.inf, x)
    m2 = jnp.max(x2, axis=-1, keepdims=True)
    i2 = jnp.min(jnp.where(x2 == m2, lane, LANES), axis=-1, keepdims=True)
    e2 = jnp.exp(m2 - m1)
    inv = 1.0 / (1.0 + e2)
    return jnp.where(lane == i1, inv, jnp.where(lane == i2, e2 * inv, 0.0))


def _pack_bf16_halves(x):
    w = x.shape[1] // 2
    bits = lax.bitcast_convert_type(x.astype(BF16).astype(F32), jnp.uint32)
    return (bits[:, :w] >> 16) | (bits[:, w:] & jnp.uint32(0xFFFF0000))


def _unpack_bf16_halves(p):
    lo = lax.bitcast_convert_type(p << 16, F32)
    hi = lax.bitcast_convert_type(p & jnp.uint32(0xFFFF0000), F32)
    return jnp.concatenate([lo, hi], axis=-1).astype(BF16)


def _out_proj_kernel(*refs, n_experts):
    if n_experts:
        attn_ref, conv_ref, wo_ref, x_ref, g_ref, rt_ref, xo_ref, h_ref, gate_ref = refs
    else:
        attn_ref, conv_ref, wo_ref, x_ref, g_ref, xo_ref, h_ref = refs
    aw = attn_ref.shape[1]
    x = x_ref[...] + _dot(attn_ref[...], wo_ref[0:aw, :]) + _dot(conv_ref[...], wo_ref[aw:, :])
    xo_ref[...] = x
    h = _rms_rows(x, g_ref[...])
    if not n_experts:
        h_ref[...] = h.astype(h_ref.dtype)
    else:
        h_ref[...] = _pack_bf16_halves(h)
        h_hi, h_lo = _split_bf16(h)
        router = rt_ref[...]
        by_hi = _dot(h_hi, router)
        logits = by_hi[:, :LANES] + (by_hi[:, LANES:] + _dot(h_lo, router[:, :LANES]))
        gate_ref[...] = _top2_gates(logits, n_experts)


def _out_proj(attn, conv, wo, x2, g, router_split=None, n_experts=0):
    t, d = x2.shape
    tm = min(TM_OUT, t)
    row = lambda i: (i, 0)
    const = lambda i: (0, 0)
    in_specs = [pl.BlockSpec((tm, attn.shape[1]), row), pl.BlockSpec((tm, conv.shape[1]), row),
                pl.BlockSpec(wo.shape, const), pl.BlockSpec((tm, d), row), pl.BlockSpec((1, d), const)]
    out_shape = [jax.ShapeDtypeStruct((t, d), F32), jax.ShapeDtypeStruct((t, d), BF16)]
    out_specs = [pl.BlockSpec((tm, d), row), pl.BlockSpec((tm, d), row)]
    args = [attn, conv, wo, x2, g]
    if n_experts:
        out_shape[1] = jax.ShapeDtypeStruct((t, d // 2), jnp.uint32)
        out_specs[1] = pl.BlockSpec((tm, d // 2), row)
        in_specs.append(pl.BlockSpec(router_split.shape, const))
        out_shape.append(jax.ShapeDtypeStruct((t, LANES), F32))
        out_specs.append(pl.BlockSpec((tm, LANES), row))
        args.append(router_split)
    return pl.pallas_call(
        functools.partial(_out_proj_kernel, n_experts=n_experts),
        grid=(t // tm,), in_specs=in_specs, out_specs=out_specs, out_shape=out_shape,
        compiler_params=_cparams(("parallel",)), name="out_proj",
    )(*args)


def _swiglu_tiles(h, acc, wg, wu, wd, dff, tf):
    for f in range(dff // tf):
        cols = slice(f * tf, (f + 1) * tf)
        a = _dot(h, wg(cols))
        u = _dot(h, wu(cols))
        acc = acc + _dot(((a * jax.nn.sigmoid(a)) * u).astype(BF16), wd(cols))
    return acc


def _ffn_kernel(h_ref, x_ref, wg_ref, wu_ref, wd_ref, o_ref, *, tf):
    o_ref[...] = _swiglu_tiles(h_ref[...], x_ref[...], lambda c: wg_ref[:, c], lambda c: wu_ref[:, c],
                               lambda c: wd_ref[c, :], wg_ref.shape[1], tf)


def _ffn(h, x2, wg, wu, wd):
    t, d = x2.shape
    tm = min(TM_FFN, t)
    row = lambda i: (i, 0)
    resident = lambda w: pl.BlockSpec(w.shape, lambda i: (0, 0), pipeline_mode=pl.Buffered(1))
    return pl.pallas_call(
        functools.partial(_ffn_kernel, tf=TF_FFN), grid=(t // tm,),
        in_specs=[pl.BlockSpec((tm, d), row), pl.BlockSpec((tm, d), row),
                  resident(wg), resident(wu), resident(wd)],
        out_specs=pl.BlockSpec((tm, d), row),
        out_shape=jax.ShapeDtypeStruct((t, d), F32),
        compiler_params=_cparams(("parallel",)), name="ffn",
    )(h, x2, wg, wu, wd)


def _route_scan_kernel(g_ref, pos_ref, tot_ref, carry_ref):
    c = pl.program_id(0)

    @pl.when(c == 0)
    def _():
        carry_ref[...] = jnp.zeros_like(carry_ref)

    ct = g_ref.shape[0]
    routed = g_ref[...] > 0.0
    a = jnp.where(routed, 1.0, 0.0)
    earlier = lax.broadcasted_iota(jnp.int32, (ct, ct), 1) < lax.broadcasted_iota(jnp.int32, (ct, ct), 0)
    base = carry_ref[...]
    pos = jnp.where(routed, _dot(jnp.where(earlier, 1.0, 0.0).astype(BF16), a.astype(BF16)) + base, -1.0)
    pos_ref[...] = pos
    total = base + jnp.sum(a, axis=0, keepdims=True)
    carry_ref[...] = total
    tot_ref[...] = total


def _route_scan(gates):
    t = gates.shape[0]
    ct = CT_MOE
    nch = t // ct
    return pl.pallas_call(
        _route_scan_kernel, grid=(nch,),
        in_specs=[pl.BlockSpec((ct, LANES), lambda c: (c, 0))],
        out_specs=[pl.BlockSpec((ct, LANES), lambda c: (c, 0)),
                   pl.BlockSpec((1, LANES), lambda c: (0, 0))],
        out_shape=[jax.ShapeDtypeStruct((t, LANES), F32),
                   jax.ShapeDtypeStruct((1, LANES), F32)],
        scratch_shapes=[pltpu.VMEM((1, LANES), F32)],
        compiler_params=_cparams(("arbitrary",)), name="route_scan",
    )(gates)


def _sc_scatter_rows(rows, idx_a, idx_b, n_slots):
    t, w = rows.shape
    mesh = plsc.VectorSubcoreMesh(core_axis_name="core", subcore_axis_name="subcore")

    @pl.kernel(out_type=jax.ShapeDtypeStruct((n_slots, w), rows.dtype), mesh=mesh, scratch_types=[])
    def scatter(x_hbm, ia_hbm, ib_hbm, o_hbm):
        def body(x_vmem, ia_vmem, ib_vmem):
            pltpu.sync_copy(x_vmem, o_hbm.at[ia_vmem.at[0]])
            pltpu.sync_copy(x_vmem, o_hbm.at[ib_vmem.at[0]])

        pltpu.emit_pipeline(
            body, grid=(t // SC_WINDOW,),
            in_specs=[pl.BlockSpec((SC_WINDOW, w), lambda i: (i, 0)),
                      pl.BlockSpec((1, SC_WINDOW), lambda i: (i, 0)),
                      pl.BlockSpec((1, SC_WINDOW), lambda i: (i, 0))],
            out_specs=[], core_axis_name=("core", "subcore"),
            dimension_semantics=(pltpu.PARALLEL,))(x_hbm, ia_hbm, ib_hbm)

    return scatter(rows, idx_a, idx_b)


def _moe_ffn_kernel(exp_ref, rows_ref, xs_ref, wg_ref, wu_ref, wd_ref, ys_ref, *, tf):
    j = pl.program_id(0)
    n_rows = rows_ref[j]

    @pl.when(n_rows > 0)
    def _():
        xs = _unpack_bf16_halves(xs_ref[...])
        row = lax.broadcasted_iota(jnp.int32, xs.shape, 0)
        xs = jnp.where(row < n_rows, xs, jnp.zeros_like(xs))
        zero = jnp.zeros(xs.shape, F32)
        ys = _swiglu_tiles(xs, zero, lambda c: wg_ref[0, :, c], lambda c: wu_ref[0, :, c],
                           lambda c: wd_ref[0, c, :], wg_ref.shape[2], tf)
        ys_ref[...] = _pack_bf16_halves(ys)

    @pl.when(n_rows == 0)
    def _():
        ys_ref[...] = jnp.zeros_like(ys_ref)


def _moe_ffn(xs, blk_expert, blk_rows, wg, wu, wd):
    n_slots = xs.shape[0]
    d = wg.shape[1]
    expert = lambda w: pl.BlockSpec((1,) + w.shape[1:], lambda j, e, v: (e[j], 0, 0))
    grid_spec = pltpu.PrefetchScalarGridSpec(
        num_scalar_prefetch=2, grid=(n_slots // BM_MOE,),
        in_specs=[pl.BlockSpec((BM_MOE, d // 2), lambda j, e, v: (j, 0)), expert(wg), expert(wu), expert(wd)],
        out_specs=pl.BlockSpec((BM_MOE, d // 2), lambda j, e, v: (j, 0)))
    return pl.pallas_call(
        functools.partial(_moe_ffn_kernel, tf=TF_FFN), grid_spec=grid_spec,
        out_shape=jax.ShapeDtypeStruct((n_slots, d // 2), jnp.uint32),
        compiler_params=_cparams(("arbitrary",)), name="moe_ffn",
    )(blk_expert, blk_rows, xs, wg, wu, wd)


def _sc_gather_row_pairs(table, idx_a, idx_b):
    w = table.shape[1]
    t = idx_a.shape[0] * SC_WINDOW
    mesh = plsc.VectorSubcoreMesh(core_axis_name="core", subcore_axis_name="subcore")
    out = jax.ShapeDtypeStruct((t, w), table.dtype)

    @pl.kernel(out_type=(out, out), mesh=mesh, scratch_types=[])
    def gather(x_hbm, ia_hbm, ib_hbm, oa_hbm, ob_hbm):
        def body(i_vmem, o_vmem):
            pltpu.sync_copy(x_hbm.at[i_vmem.at[0]], o_vmem)

        for i_hbm, o_hbm in ((ia_hbm, oa_hbm), (ib_hbm, ob_hbm)):
            pltpu.emit_pipeline(
                body, grid=(t // SC_WINDOW,),
                in_specs=[pl.BlockSpec((1, SC_WINDOW), lambda i: (i, 0))],
                out_specs=[pl.BlockSpec((SC_WINDOW, w), lambda i: (i, 0))],
                core_axis_name=("core", "subcore"),
                dimension_semantics=(pltpu.PARALLEL,))(i_hbm, o_hbm)

    return gather(table, idx_a, idx_b)


def _token_slots(pos, pstart_row, unrouted):
    routed = pos >= 0.0
    slot = pos + pstart_row
    return (jnp.min(jnp.where(routed, slot, unrouted), axis=-1, keepdims=True),
            jnp.max(jnp.where(routed, slot, -1.0), axis=-1, keepdims=True), slot, routed)


def _moe_combine_kernel(x_ref, pos_ref, g_ref, pstart_ref, ya_ref, yb_ref, o_ref, *, n_slots):
    slot_a, slot_b, slot, routed = _token_slots(pos_ref[...], pstart_ref[...], float(n_slots))
    gates = g_ref[...]
    gate_a = jnp.sum(jnp.where(routed & (slot == slot_a), gates, 0.0), axis=-1, keepdims=True)
    gate_b = jnp.sum(jnp.where(routed & (slot == slot_b), gates, 0.0), axis=-1, keepdims=True)
    gate_b = jnp.where(slot_b == slot_a, 0.0, gate_b)

    def rows_f32(ref):
        p = ref[...]
        return jnp.concatenate([lax.bitcast_convert_type(p << 16, F32),
                                lax.bitcast_convert_type(p & jnp.uint32(0xFFFF0000), F32)], axis=-1)

    o_ref[...] = x_ref[...] + gate_a * rows_f32(ya_ref) + gate_b * rows_f32(yb_ref)


def _moe_combine(x2, pos, gates, pstart_row, ya, yb, n_slots):
    t, d = x2.shape
    tm = min(CT_MOE, t)
    row = lambda i: (i, 0)
    return pl.pallas_call(
        functools.partial(_moe_combine_kernel, n_slots=n_slots), grid=(t // tm,),
        in_specs=[pl.BlockSpec((tm, d), row), pl.BlockSpec((tm, LANES), row), pl.BlockSpec((tm, LANES), row),
                  pl.BlockSpec((1, LANES), lambda i: (0, 0)),
                  pl.BlockSpec((tm, d // 2), row), pl.BlockSpec((tm, d // 2), row)],
        out_specs=pl.BlockSpec((tm, d), row),
        out_shape=jax.ShapeDtypeStruct((t, d), F32),
        compiler_params=_cparams(("parallel",)), name="moe_combine",
    )(x2, pos, gates, pstart_row, ya, yb)


def _count_le(ascending, x):
    return jnp.sum(ascending[None, :] <= x[:, None], axis=1).astype(jnp.int32)


def _moe_routed(h, x2, gates, wg, wu, wd):
    t, d = x2.shape
    n_e = wg.shape[0]
    n_slots = t * TOP_K + n_e * BM_MOE
    pos, tot = _route_scan(gates)

    counts = tot[0, :n_e].astype(jnp.int32)
    padded = (counts + BM_MOE - 1) // BM_MOE * BM_MOE
    pend = jnp.cumsum(padded)
    pstart = pend - padded

    pstart_row = jnp.zeros((1, LANES), F32).at[0, :n_e].set(pstart.astype(F32))
    slot_a, slot_b, _, _ = _token_slots(pos, pstart_row, float(n_slots))
    slot_a = slot_a.astype(jnp.int32).reshape(-1, SC_WINDOW)
    slot_b = slot_b.astype(jnp.int32).reshape(-1, SC_WINDOW)
    xs = _sc_scatter_rows(h, slot_a, slot_b, n_slots)

    mb = jnp.arange(n_slots // BM_MOE, dtype=jnp.int32) * BM_MOE
    mb_e = jnp.minimum(_count_le(pend, mb), n_e - 1)
    mb_rows = jnp.where(mb < pend[-1], jnp.clip(counts[mb_e] - (mb - pstart[mb_e]), 0, BM_MOE), 0)
    ys = _moe_ffn(xs, mb_e, mb_rows.astype(jnp.int32), wg, wu, wd)

    ya, yb = _sc_gather_row_pairs(ys, slot_a, slot_b)
    return _moe_combine(x2, pos, gates, pstart_row, ya, yb, n_slots)


def _rope_tables(pos):
    half = ROPE_DIM // 2
    inv_freq = ROPE_THETA ** (-2.0 * jnp.arange(half, dtype=F32) / ROPE_DIM)
    ang = pos.astype(F32).reshape(-1, 1) * inv_freq
    cos, sin = jnp.cos(ang), jnp.sin(ang)
    n = ang.shape[0]
    rest = HEAD_DIM - ROPE_DIM
    cos_h = jnp.concatenate([cos, cos, jnp.ones((n, rest), F32)], axis=-1)
    sin_h = jnp.concatenate([-sin, sin, jnp.zeros((n, rest), F32)], axis=-1)
    return jnp.tile(cos_h, (1, HEADS_PER_VREG)), jnp.tile(sin_h, (1, HEADS_PER_VREG))


def _permute_w_in(w, conv_w):
    d = w.shape[0]
    kv_end = Q_W + 6 * KV_W
    g = w[:, kv_end:kv_end + N_HEADS * N_BRANCH]
    pad = jnp.zeros((d, LANES - GATES_PER_KV), w.dtype)
    gate_cols = []
    for kh in range(N_KV_HEADS):
        gate_cols += [g[:, kh * GATES_PER_KV:(kh + 1) * GATES_PER_KV], pad]
    u = w[:, kv_end + N_HEADS * N_BRANCH:]
    return jnp.concatenate([w[:, :kv_end]] + gate_cols + [u], axis=1).astype(BF16)


def _compress_weights(pos_emb, w1, w2):
    hidden = w1.shape[1]
    eye = jnp.eye(N_KV_HEADS, dtype=w1.dtype)
    w1r = w1.reshape(CMP_LEN, HEAD_DIM, hidden)
    halves = []
    for part in (w1r[:CMP_STRIDE], w1r[CMP_STRIDE:]):
        full = jnp.einsum('ldj,hg->lhdgj', part, eye)
        halves.append(full.reshape(CMP_STRIDE * N_KV_HEADS * HEAD_DIM, N_KV_HEADS * hidden).astype(BF16))
    w2p = jnp.einsum('jd,hg->hjgd', w2, eye).reshape(N_KV_HEADS * hidden, N_KV_HEADS * HEAD_DIM).astype(BF16)
    pos = []
    for part in (pos_emb[:CMP_STRIDE], pos_emb[CMP_STRIDE:]):
        pos.append(jnp.broadcast_to(part[:, None, :], (CMP_STRIDE, N_KV_HEADS, HEAD_DIM)).reshape(1, -1))
    return [halves[0], halves[1], w2p, pos[0], pos[1]]


def _selection_map_t(seq):
    ncp = seq // CMP_STRIDE
    n_cmp = (seq - CMP_LEN) // CMP_STRIDE + 1
    c0 = np.arange(ncp) * CMP_STRIDE
    s0 = np.arange(seq // SEL_LEN) * SEL_LEN
    ov = np.minimum(c0[None, :] + CMP_LEN, s0[:, None] + SEL_LEN) - np.maximum(c0[None, :], s0[:, None])
    m = np.clip(ov, 0, None) / CMP_LEN
    m[:, n_cmp:] = 0.0
    return jnp.asarray(m, dtype=BF16)


def kernel(x, positions, attn_norm_g, ffn_norm_g, w_in, w_out, q_norm_g, k_norm_g, cmp_pos_k, cmp_w1_k, cmp_w2_k, cmp_pos_v, cmp_w1_v, cmp_w2_v, conv_w, conv_b, conv_ln_g, conv_ln_b, ffn_w_gate, ffn_w_up, ffn_w_down, moe_router, moe_w_gate, moe_w_up, moe_w_down):
    b, seq, d = x.shape
    t = b * seq
    depth = w_in.shape[0]
    cw = conv_w.shape[2]
    ncp = seq // CMP_STRIDE
    n_cmp = (seq - CMP_LEN) // CMP_STRIDE + 1
    assert seq % max(TQ, KC, TM_PROJ) == 0 and seq >= WINDOW + TQ

    cos_t, sin_t = _rope_tables(positions)
    cmp_end = np.minimum(np.arange(ncp) * CMP_STRIDE + CMP_LEN - 1, seq - 1)
    cos_c, sin_c = _rope_tables(positions[:, cmp_end])
    selmap_t = _selection_map_t(seq)
    tile2 = lambda v: jnp.tile(v.reshape(1, HEAD_DIM), (1, HEADS_PER_VREG))

    x2 = x.reshape(t, d)
    for layer in range(depth):
        w_perm = _permute_w_in(w_in[layer], cw)
        conv_params = (conv_w[layer], conv_b[layer].reshape(1, cw), conv_ln_g[layer].reshape(1, cw),
                       conv_ln_b[layer].reshape(1, cw))
        q, kc, vc, ks, vs, kw, vw, gates, conv = _in_proj(
            x2, attn_norm_g[layer].reshape(1, d), w_perm, cos_t, sin_t,
            tile2(q_norm_g[layer]) * (HEAD_DIM ** -0.5 * LOG2_E),
            tile2(k_norm_g[layer, 1]), tile2(k_norm_g[layer, 2]), conv_params, seq)
        kcmp, vcmp = _compress(
            kc, vc,
            _compress_weights(cmp_pos_k[layer], cmp_w1_k[layer], cmp_w2_k[layer]),
            _compress_weights(cmp_pos_v[layer], cmp_w1_v[layer], cmp_w2_v[layer]),
            tile2(k_norm_g[layer, 0]), cos_c, sin_c, b, ncp)
        score_bound = (HEAD_DIM ** 0.5 * LOG2_E * SCORE_BOUND_MARGIN * jnp.max(jnp.abs(q_norm_g[layer]))
                       * jnp.max(jnp.abs(k_norm_g[layer]))).astype(F32).reshape(1)
        attn = _attention(score_bound, q, kcmp, vcmp, ks, vs, kw, vw, gates, selmap_t, b, seq)
        wo = w_out[layer].astype(BF16)
        g2 = ffn_norm_g[layer].reshape(1, d)
        i = layer // 2
        if layer % 2 == 0:
            x2, h = _out_proj(attn, conv, wo, x2, g2)
            x2 = _ffn(h, x2, ffn_w_gate[i].astype(BF16), ffn_w_up[i].astype(BF16), ffn_w_down[i].astype(BF16))
        else:
            n_e = moe_router.shape[2]
            r = jnp.pad(moe_router[i], ((0, 0), (0, LANES - n_e)))
            r_hi = r.astype(BF16)
            r_lo = (r - r_hi.astype(F32)).astype(BF16)
            x2, h, route = _out_proj(attn, conv, wo, x2, g2, jnp.concatenate([r_hi, r_lo], axis=1), n_e)
            x2 = _moe_routed(h, x2, route, moe_w_gate[i].astype(BF16), moe_w_up[i].astype(BF16),
                             moe_w_down[i].astype(BF16))
    return x2.reshape(b, seq, d)
```

```python
import functools
import math

import jax
import jax.numpy as jnp
import numpy as np
from jax import lax
from jax.experimental import pallas as pl
from jax.experimental.pallas import tpu as pltpu
from jax.experimental.pallas import tpu_sc as plsc

F32 = jnp.float32
BF16 = jnp.bfloat16

N_HEADS = 8
N_KV_HEADS = 2
Q_PER_KV = N_HEADS // N_KV_HEADS
HEAD_DIM = 64
N_BRANCH = 3
CMP_LEN = 32
CMP_STRIDE = 16
SEL_LEN = 64
SEL_TOPK = 16
N_LOCAL_SEL = 2
WINDOW = 512
CONV_KERNEL = 31
ROPE_THETA = 500000.0
ROPE_DIM = HEAD_DIM // 4
TOP_K = 2
EPS = 1e-6

LANES = 128
SUBLANES = 8
LOG2_E = math.log2(math.e)
NEG_BIG = -(2.0 ** 100)
MAX_SCORE_BOUND = 50.0
SCORE_BOUND_MARGIN = 1.02
HEADS_PER_VREG = LANES // HEAD_DIM
VMEM_LIMIT = 56 * 1024 * 1024

TM_PROJ = 512
TM_OUT = 1024
TQ = 512
KC = 512
CH_CONV = 16
HALO = 32
TM_FFN = 1024
TF_FFN = 256
CT_MOE = 1024
BM_MOE = 512
SC_WINDOW = 64


def _cparams(sem):
    return pltpu.CompilerParams(dimension_semantics=sem, vmem_limit_bytes=VMEM_LIMIT)


def _dot(a, b):
    return jnp.dot(a, b, preferred_element_type=F32)


def _dot_nt(a, b):
    return lax.dot_general(a, b, (((1,), (1,)), ((), ())), preferred_element_type=F32)


def _split_bf16(x):
    hi = x.astype(BF16)
    lo = (x - hi.astype(F32)).astype(BF16)
    return hi, lo


def _rms_rows(x, g):
    ms = jnp.mean(x * x, axis=-1, keepdims=True)
    return x * lax.rsqrt(ms + EPS) * g


def _head_block_ones(width):
    r = lax.broadcasted_iota(jnp.int32, (width, width), 0) // HEAD_DIM
    c = lax.broadcasted_iota(jnp.int32, (width, width), 1) // HEAD_DIM
    return jnp.where(r == c, 1.0 / HEAD_DIM, 0.0).astype(BF16)


def _norm_rope(xg, ms, gain, cos, sin):
    y = xg * lax.rsqrt(ms + EPS) * gain
    lane = lax.broadcasted_iota(jnp.int32, y.shape, 1) % HEAD_DIM
    half = ROPE_DIM // 2
    partner = jnp.where(lane < half, pltpu.roll(y, LANES - half, 1), pltpu.roll(y, half, 1))
    return y * cos + partner * sin


def _head_norm_rope(xg, gain, cos, sin, ones_bd):
    ms = _dot((xg * xg).astype(BF16), ones_bd)
    return _norm_rope(xg, ms, gain, cos, sin)


def _head_norm_rope_pair(xa, xb, gain_a, gain_b, cos, sin, ones_bd2):
    sq = jnp.concatenate([xa * xa, xb * xb], axis=-1).astype(BF16)
    ms = _dot(sq, ones_bd2)
    return (_norm_rope(xa, ms[:, :LANES], gain_a, cos, sin),
            _norm_rope(xb, ms[:, LANES:], gain_b, cos, sin))


Q_W = N_HEADS * HEAD_DIM
KV_W = N_KV_HEADS * HEAD_DIM
SEG_Q = 0
SEG_KV = Q_W
SEG_GATE = SEG_KV + 6 * KV_W
SEG_UA = SEG_GATE + N_KV_HEADS * LANES
GATES_PER_KV = Q_PER_KV * N_BRANCH


def _causal_conv_tile(glu, seq_start, w_ref, b_ref, lg_ref, lb_ref, o_ref, ext_ref, shift_ref):
    ts = glu.shape[0]
    ext_ref[0:HALO, :] = jnp.where(seq_start, 0.0, ext_ref[ts:ts + HALO, :])
    ext_ref[HALO:HALO + ts, :] = glu
    n_shift = shift_ref.shape[1]
    for r in range(1, SUBLANES):
        shift_ref[r - 1] = ext_ref[r:r + n_shift, :]

    def rows_from(o):
        r = o % SUBLANES
        if r == 0:
            return ext_ref[o:o + CH_CONV, :]
        return shift_ref[r - 1, o - r:o - r + CH_CONV, :]

    w = w_ref[...]
    first_tap = HALO - (CONV_KERNEL - 1)
    for c in range(ts // CH_CONV):
        base = c * CH_CONV + first_tap
        acc = w[0:1, :] * rows_from(base)
        for k in range(1, CONV_KERNEL):
            acc = acc + w[k:k + 1, :] * rows_from(base + k)
        y = acc + b_ref[...]
        yc = y - jnp.mean(y, axis=-1, keepdims=True)
        yn = yc * lax.rsqrt(jnp.mean(yc * yc, axis=-1, keepdims=True) + EPS)
        z = yn * lg_ref[...] + lb_ref[...]
        o_ref[c * CH_CONV:(c + 1) * CH_CONV, :] = (z * jax.nn.sigmoid(z)).astype(o_ref.dtype)


def _in_proj_kernel(x_ref, g_ref, w_ref, cos_ref, sin_ref, qg_ref, ksg_ref, kwg_ref,
                    cw_ref, cb_ref, clg_ref, clb_ref,
                    q_ref, kc_ref, vc_ref, ks_ref, vs_ref, kw_ref, vw_ref, gate_ref, conv_ref,
                    glu_ref, ext_ref, shift_ref, *, conv_w, seq, n_tiles):
    i = pl.program_id(0)
    tm = x_ref.shape[0]

    @pl.when(i == 0)
    def _():
        glu_ref[...] = jnp.zeros_like(glu_ref)
        ext_ref[...] = jnp.zeros_like(ext_ref)

    _causal_conv_tile(glu_ref[...], lax.rem((i - 1) * tm, seq) == 0, cw_ref, cb_ref, clg_ref, clb_ref,
                      conv_ref, ext_ref, shift_ref)

    h = _rms_rows(x_ref[...], g_ref[...]).astype(BF16)
    cos = cos_ref[...]
    sin = sin_ref[...]
    ones_bd = _head_block_ones(2 * LANES)
    lane = lax.broadcasted_iota(jnp.int32, (tm, LANES), 1)
    tile_start = lax.rem(jnp.minimum(i, n_tiles - 1) * tm, seq)
    tok = tile_start + lax.broadcasted_iota(jnp.int32, (tm, 1), 0)
    block_aug = jnp.where(lane - HEAD_DIM == tok // SEL_LEN, NEG_BIG, 0.0)
    ones_aug = jnp.where(lane == HEAD_DIM, 1.0, 0.0)

    def put_heads(ref, first, val, aug):
        for j in range(HEADS_PER_VREG):
            head = val if j == 0 else pltpu.roll(val, LANES - j * HEAD_DIM, 1)
            ref[first + j] = jnp.where(lane < HEAD_DIM, head, aug).astype(ref.dtype)

    qkv = _dot(h, w_ref[:, SEG_Q:SEG_GATE])
    group = lambda j: qkv[:, j * LANES:(j + 1) * LANES]
    first_kv = Q_W // LANES
    qg = qg_ref[...]
    for c in range(0, first_kv, 2):
        pair = _head_norm_rope_pair(group(c), group(c + 1), qg, qg, cos, sin, ones_bd)
        for j, y in enumerate(pair):
            put_heads(q_ref, (c + j) * HEADS_PER_VREG, y, 0.0)
    kc_ref[...] = group(first_kv)
    vc_ref[...] = group(first_kv + 1)
    k_sel, k_win = _head_norm_rope_pair(group(first_kv + 2), group(first_kv + 4), ksg_ref[...], kwg_ref[...],
                                        cos, sin, ones_bd)
    put_heads(ks_ref, 0, k_sel, block_aug)
    put_heads(vs_ref, 0, group(first_kv + 3), ones_aug)
    put_heads(kw_ref, 0, k_win, 0.0)
    put_heads(vw_ref, 0, group(first_kv + 5), ones_aug)
    gate_ref[...] = jax.nn.sigmoid(_dot(h, w_ref[:, SEG_GATE:SEG_UA]))
    u = _dot(h, w_ref[:, SEG_UA:])
    glu_ref[...] = u[:, :conv_w] * jax.nn.sigmoid(u[:, conv_w:])


def _in_proj(x2, g, w_perm, cos_t, sin_t, qg, ksg, kwg, conv_params, seq):
    t, d = x2.shape
    conv_w = (w_perm.shape[1] - SEG_UA) // 2
    tm = min(TM_PROJ, seq)
    n_tiles = t // tm
    row = lambda i: (jnp.minimum(i, n_tiles - 1), 0)
    const = lambda i: (0, 0)
    head_row = lambda i: (0, jnp.minimum(i, n_tiles - 1), 0)
    out_shape = [
        jax.ShapeDtypeStruct((N_HEADS, t, LANES), BF16),
        jax.ShapeDtypeStruct((t, KV_W), F32),
        jax.ShapeDtypeStruct((t, KV_W), F32),
        jax.ShapeDtypeStruct((N_KV_HEADS, t, LANES), BF16),
        jax.ShapeDtypeStruct((N_KV_HEADS, t, LANES), BF16),
        jax.ShapeDtypeStruct((N_KV_HEADS, t, LANES), BF16),
        jax.ShapeDtypeStruct((N_KV_HEADS, t, LANES), BF16),
        jax.ShapeDtypeStruct((t, N_KV_HEADS * LANES), F32),
        jax.ShapeDtypeStruct((t, conv_w), BF16),
    ]
    kv_spec = pl.BlockSpec((N_KV_HEADS, tm, LANES), head_row)
    out_specs = [
        pl.BlockSpec((N_HEADS, tm, LANES), head_row),
        pl.BlockSpec((tm, KV_W), row), pl.BlockSpec((tm, KV_W), row),
        kv_spec, kv_spec, kv_spec, kv_spec,
        pl.BlockSpec((tm, N_KV_HEADS * LANES), row),
        pl.BlockSpec((tm, conv_w), lambda i: (jnp.maximum(i - 1, 0), 0)),
    ]
    in_specs = [
        pl.BlockSpec((tm, d), row), pl.BlockSpec((1, d), const),
        pl.BlockSpec(w_perm.shape, const),
        pl.BlockSpec((tm, LANES), row), pl.BlockSpec((tm, LANES), row),
        pl.BlockSpec((1, LANES), const), pl.BlockSpec((1, LANES), const), pl.BlockSpec((1, LANES), const),
    ] + [pl.BlockSpec(p.shape, const) for p in conv_params]
    return pl.pallas_call(
        functools.partial(_in_proj_kernel, conv_w=conv_w, seq=seq, n_tiles=n_tiles),
        grid=(n_tiles + 1,), in_specs=in_specs, out_specs=out_specs, out_shape=out_shape,
        scratch_shapes=[pltpu.VMEM((tm, conv_w), F32),
                        pltpu.VMEM((tm + HALO, conv_w), F32),
                        pltpu.VMEM((SUBLANES - 1, tm + HALO - SUBLANES, conv_w), F32)],
        compiler_params=_cparams(("arbitrary",)), name="in_proj",
    )(x2, g, w_perm, cos_t, sin_t, qg, ksg, kwg, *conv_params)


def _gelu_tanh(x):
    c = math.sqrt(2.0 / math.pi)
    return 0.5 * x * (1.0 + jnp.tanh(c * (x + 0.044715 * (x * x * x))))


def _compress_kernel(k_ref, v_ref, w1ak_ref, w1bk_ref, w2k_ref, pak_ref, pbk_ref,
                     w1av_ref, w1bv_ref, w2v_ref, pav_ref, pbv_ref,
                     kg_ref, cos_ref, sin_ref, ko_ref, vo_ref):
    def mlp(x_ref, w1a_ref, w1b_ref, w2_ref, pa_ref, pb_ref):
        n = x_ref.shape[0] // CMP_STRIDE
        first = second = None
        for l in range(CMP_STRIDE):
            x = x_ref[pl.ds(l, n, stride=CMP_STRIDE), :]
            cols = slice(l * KV_W, (l + 1) * KV_W)
            fa = _dot((x + pa_ref[:, cols]).astype(BF16), w1a_ref[cols, :])
            fb = _dot((x + pb_ref[:, cols]).astype(BF16), w1b_ref[cols, :])
            first = fa if first is None else first + fa
            second = fb if second is None else second + fb
        hid = first + pltpu.roll(second, n - 1, 0)
        return _dot(_gelu_tanh(hid).astype(BF16), w2_ref[...])

    kc = mlp(k_ref, w1ak_ref, w1bk_ref, w2k_ref, pak_ref, pbk_ref)
    kc = _head_norm_rope(kc, kg_ref[...], cos_ref[...], sin_ref[...], _head_block_ones(LANES))
    vc = mlp(v_ref, w1av_ref, w1bv_ref, w2v_ref, pav_ref, pbv_ref)
    lane = lax.broadcasted_iota(jnp.int32, kc.shape, 1)
    for j in range(N_KV_HEADS):
        for val, ref in ((kc, ko_ref), (vc, vo_ref)):
            head = val if j == 0 else pltpu.roll(val, LANES - j * HEAD_DIM, 1)
            ref[0, j] = jnp.where(lane < HEAD_DIM, head, 0.0).astype(ref.dtype)


def _compress(kc, vc, wk, wv, kg, cosc, sinc, b, ncp):
    seq = kc.shape[0] // b
    const = lambda i: (0, 0)
    row = lambda i: (i, 0)

    def wspecs(ws):
        return [pl.BlockSpec(w.shape, const) for w in ws]

    out = jax.ShapeDtypeStruct((b, N_KV_HEADS, ncp, LANES), BF16)
    ospec = pl.BlockSpec((1, N_KV_HEADS, ncp, LANES), lambda i: (i, 0, 0, 0))
    return pl.pallas_call(
        _compress_kernel, grid=(b,),
        in_specs=[pl.BlockSpec((seq, KV_W), row), pl.BlockSpec((seq, KV_W), row)]
        + wspecs(wk) + wspecs(wv)
        + [pl.BlockSpec((1, LANES), const), pl.BlockSpec((ncp, LANES), row), pl.BlockSpec((ncp, LANES), row)],
        out_specs=[ospec, ospec], out_shape=[out, out],
        compiler_params=_cparams(("parallel",)), name="compress",
    )(kc, vc, *wk, *wv, kg, cosc, sinc)


def _attn_kernel(off_ref, q_ref, kc_ref, vc_ref, ks_ref, vs_ref, kw_ref, vw_ref, gate_ref, selmap_ref,
                 o_ref, *, seq, tq, kc_len, top_n, bounded):
    i = pl.program_id(2)
    t0 = i * tq
    rows = Q_PER_KV * tq
    n_sel = seq // SEL_LEN
    q2 = q_ref[...].reshape(rows, LANES)
    t_row = t0 + (lax.broadcasted_iota(jnp.int32, (rows, 1), 0) & (tq - 1))
    t_tok = t0 + lax.broadcasted_iota(jnp.int32, (tq, 1), 0)
    neg_offset = -off_ref[0] if bounded else 0.0

    def add_bias(s, bias):
        return (s.reshape(s.shape[0] // tq, tq, s.shape[1]) + bias[None]).reshape(s.shape)

    kcmp = kc_ref[0, 0]
    ncp = kcmp.shape[0]
    s_c = _dot_nt(q2, kcmp)
    cmp_end = lax.broadcasted_iota(jnp.int32, (1, ncp), 1) * CMP_STRIDE + (CMP_LEN - 1)
    if bounded:
        e_c = jnp.exp2(add_bias(s_c, jnp.where(cmp_end <= t_tok, neg_offset, NEG_BIG)))
    else:
        s_c = jnp.where(cmp_end <= t_row, s_c, -jnp.inf)
        m_c = jnp.max(s_c, axis=-1, keepdims=True)
        e_c = jnp.exp2(s_c - jnp.where(m_c == -jnp.inf, 0.0, m_c))
    p_c = e_c * (1.0 / jnp.maximum(jnp.sum(e_c, axis=-1, keepdims=True), jnp.finfo(F32).tiny))
    o_c = _dot(p_c.astype(BF16), vc_ref[0, 0])

    def denominator(acc):
        return acc[:, HEAD_DIM:HEAD_DIM + 1]

    p_hi, p_lo = _split_bf16(jnp.sum(p_c.reshape(Q_PER_KV, tq, ncp), axis=0))
    selmap = selmap_ref[...]
    imp = _dot_nt(selmap, p_hi) + _dot_nt(selmap, p_lo)
    blk = lax.broadcasted_iota(jnp.int32, (n_sel, tq), 0)
    cur = (t0 + lax.broadcasted_iota(jnp.int32, (n_sel, tq), 1)) // SEL_LEN
    causal_blk = blk <= cur
    forced = (blk == 0) | (causal_blk & (blk > cur - N_LOCAL_SEL))
    score = jnp.where(forced, jnp.inf, jnp.where(causal_blk, imp, -jnp.inf))
    sub = lax.broadcasted_iota(jnp.int32, (SUBLANES, tq), 0)
    groups = [score[g * SUBLANES:(g + 1) * SUBLANES, :] for g in range(n_sel // SUBLANES)]
    ranks = [jnp.zeros((SUBLANES, tq), F32) for _ in groups]
    for jp in range(n_sel):
        other = jnp.broadcast_to(score[jp:jp + 1, :], (SUBLANES, tq))
        for g, sg in enumerate(groups):
            first = g * SUBLANES
            if first > jp:
                inc = jnp.where(other >= sg, 1.0, 0.0)
            elif first + SUBLANES - 1 <= jp:
                inc = jnp.where(other > sg, 1.0, 0.0)
            else:
                inc = jnp.where(other > sg, 1.0, jnp.where((other == sg) & (sub > jp - first), 1.0, 0.0))
            ranks[g] = ranks[g] + inc
    chosen_flag = neg_offset * (1.0 / NEG_BIG)
    block_flags = jnp.where(jnp.concatenate(ranks, axis=0) < top_n, chosen_flag, 1.0)
    flag_rows = [jnp.zeros((HEAD_DIM, tq), F32), block_flags]
    if HEAD_DIM + n_sel < LANES:
        flag_rows.append(jnp.zeros((LANES - HEAD_DIM - n_sel, tq), F32))
    flags = jnp.transpose(jnp.concatenate(flag_rows, axis=0))
    q_sel = (q2.reshape(Q_PER_KV, tq, LANES) + flags.astype(BF16)[None]).reshape(rows, LANES)

    span = min(WINDOW + tq, seq)
    w0 = pl.multiple_of(jnp.maximum(t0 - WINDOW, 0), tq)
    key_w = w0 + lax.broadcasted_iota(jnp.int32, (1, span), 1)
    bias_w = jnp.where((key_w <= t_tok) & (key_w > t_tok - WINDOW), neg_offset, NEG_BIG)
    s_w = add_bias(_dot_nt(q2, kw_ref[0, pl.ds(w0, span), :]), bias_w)
    if bounded:
        p_w = jnp.exp2(s_w).astype(BF16)
    else:
        p_w = jnp.exp2((s_w - jnp.max(s_w, axis=-1, keepdims=True)).astype(BF16))
    acc_w = _dot(p_w, vw_ref[0, pl.ds(w0, span), :])
    o_w = acc_w * (1.0 / denominator(acc_w))

    def sel_chunk(k0, carry, bias):
        s = _dot_nt(q_sel, ks_ref[0, pl.ds(k0, kc_len), :])
        if bias is not None:
            s = add_bias(s, bias)
        v = vs_ref[0, pl.ds(k0, kc_len), :]
        if bounded:
            (acc,) = carry
            return (acc + _dot(jnp.exp2(s).astype(BF16), v),)
        m, acc = carry
        m_new = jnp.maximum(m, jnp.max(s, axis=-1, keepdims=True))
        p = jnp.exp2((s - m_new).astype(BF16))
        return m_new, jnp.exp2(m - m_new) * acc + _dot(p, v)

    n_full = t0 // kc_len
    init = (jnp.zeros((rows, LANES), F32),)
    if not bounded:
        init = (jnp.full((rows, 1), -jnp.inf, F32),) + init
    carry = lax.fori_loop(
        0, n_full, lambda c, cr: sel_chunk(pl.multiple_of(c * kc_len, kc_len), cr, None), init)
    kd = pl.multiple_of(n_full * kc_len, kc_len)
    key_d = kd + lax.broadcasted_iota(jnp.int32, (1, kc_len), 1)
    acc_s = sel_chunk(kd, carry, jnp.where(key_d <= t_tok, 0.0, NEG_BIG))[-1]
    o_s = acc_s * (1.0 / denominator(acc_s))

    gates = gate_ref[...]

    def gate_col(br):
        cols = [gates[:, g * N_BRANCH + br:g * N_BRANCH + br + 1] for g in range(Q_PER_KV)]
        return jnp.concatenate(cols, axis=0)

    o = gate_col(0) * o_c + gate_col(1) * o_s + gate_col(2) * o_w
    o3 = o.reshape(Q_PER_KV, tq, LANES)
    lane = lax.broadcasted_iota(jnp.int32, (tq, LANES), 1)
    pairs = [jnp.where(lane < HEAD_DIM, o3[g], pltpu.roll(o3[g + 1], HEAD_DIM, 1))
             for g in range(0, Q_PER_KV, HEADS_PER_VREG)]
    o_ref[...] = jnp.concatenate(pairs, axis=-1).astype(o_ref.dtype)


def _attention(score_bound, q, kcmp, vcmp, ks, vs, kw, vw, gates, selmap_t, b, seq):
    t = b * seq
    tq = min(TQ, seq)
    kc_len = min(KC, seq)
    nq = seq // tq
    ncp = kcmp.shape[2]
    n_sel = seq // SEL_LEN
    assert HEAD_DIM + n_sel <= LANES, "selection-block flags must fit beside the head dims"
    top_n = min(SEL_TOPK, n_sel)
    cmp_spec = pl.BlockSpec((1, 1, ncp, LANES), lambda bi, kh, i: (bi, kh, 0, 0))
    seq_spec = pl.BlockSpec((1, seq, LANES), lambda bi, kh, i: (kh, bi, 0))

    def run(bounded):
        return pl.pallas_call(
            functools.partial(_attn_kernel, seq=seq, tq=tq, kc_len=kc_len, top_n=top_n, bounded=bounded),
            grid=(b, N_KV_HEADS, nq),
            in_specs=[
                pl.BlockSpec(memory_space=pltpu.SMEM),
                pl.BlockSpec((Q_PER_KV, tq, LANES), lambda bi, kh, i: (kh, bi * nq + i, 0)),
                cmp_spec, cmp_spec, seq_spec, seq_spec, seq_spec, seq_spec,
                pl.BlockSpec((tq, LANES), lambda bi, kh, i: (bi * nq + i, kh)),
                pl.BlockSpec(selmap_t.shape, lambda bi, kh, i: (0, 0)),
            ],
            out_specs=pl.BlockSpec((tq, Q_PER_KV * HEAD_DIM), lambda bi, kh, i: (bi * nq + i, kh)),
            out_shape=jax.ShapeDtypeStruct((t, N_HEADS * HEAD_DIM), BF16),
            compiler_params=_cparams(("parallel", "parallel", "arbitrary")),
            name="nsa_attention" if bounded else "nsa_attention_running_max",
        )(score_bound, q, kcmp, vcmp, ks, vs, kw, vw, gates, selmap_t)

    return lax.cond(score_bound[0] < MAX_SCORE_BOUND, lambda: run(True), lambda: run(False))


def _top2_gates(logits, n_experts):
    lane = lax.broadcasted_iota(jnp.int32, logits.shape, 1)
    x = jnp.where(lane < n_experts, logits, -jnp.inf)
    m1 = jnp.max(x, axis=-1, keepdims=True)
    i1 = jnp.min(jnp.where(x == m1, lane, LANES), axis=-1, keepdims=True)
    x2 = jnp.where(lane == i1, -jnp.inf, x)
    m2 = jnp.max(x2, axis=-1, keepdims=True)
    i2 = jnp.min(jnp.where(x2 == m2, lane, LANES), axis=-1, keepdims=True)
    e2 = jnp.exp(m2 - m1)
    inv = 1.0 / (1.0 + e2)
    return jnp.where(lane == i1, inv, jnp.where(lane == i2, e2 * inv, 0.0))


def _pack_bf16_halves(x):
    w = x.shape[1] // 2
    bits = lax.bitcast_convert_type(x.astype(BF16).astype(F32), jnp.uint32)
    return (bits[:, :w] >> 16) | (bits[:, w:] & jnp.uint32(0xFFFF0000))


def _unpack_bf16_halves(p):
    lo = lax.bitcast_convert_type(p << 16, F32)
    hi = lax.bitcast_convert_type(p & jnp.uint32(0xFFFF0000), F32)
    return jnp.concatenate([lo, hi], axis=-1).astype(BF16)


def _out_proj_kernel(*refs, n_experts):
    if n_experts:
        attn_ref, conv_ref, wo_ref, x_ref, g_ref, rt_ref, xo_ref, h_ref, gate_ref = refs
    else:
        attn_ref, conv_ref, wo_ref, x_ref, g_ref, xo_ref, h_ref = refs
    aw = attn_ref.shape[1]
    x = x_ref[...] + _dot(attn_ref[...], wo_ref[0:aw, :]) + _dot(conv_ref[...], wo_ref[aw:, :])
    xo_ref[...] = x
    h = _rms_rows(x, g_ref[...])
    if not n_experts:
        h_ref[...] = h.astype(h_ref.dtype)
    else:
        h_ref[...] = _pack_bf16_halves(h)
        h_hi, h_lo = _split_bf16(h)
        router = rt_ref[...]
        by_hi = _dot(h_hi, router)
        logits = by_hi[:, :LANES] + (by_hi[:, LANES:] + _dot(h_lo, router[:, :LANES]))
        gate_ref[...] = _top2_gates(logits, n_experts)


def _out_proj(attn, conv, wo, x2, g, router_split=None, n_experts=0):
    t, d = x2.shape
    tm = min(TM_OUT, t)
    row = lambda i: (i, 0)
    const = lambda i: (0, 0)
    in_specs = [pl.BlockSpec((tm, attn.shape[1]), row), pl.BlockSpec((tm, conv.shape[1]), row),
                pl.BlockSpec(wo.shape, const), pl.BlockSpec((tm, d), row), pl.BlockSpec((1, d), const)]
    out_shape = [jax.ShapeDtypeStruct((t, d), F32), jax.ShapeDtypeStruct((t, d), BF16)]
    out_specs = [pl.BlockSpec((tm, d), row), pl.BlockSpec((tm, d), row)]
    args = [attn, conv, wo, x2, g]
    if n_experts:
        out_shape[1] = jax.ShapeDtypeStruct((t, d // 2), jnp.uint32)
        out_specs[1] = pl.BlockSpec((tm, d // 2), row)
        in_specs.append(pl.BlockSpec(router_split.shape, const))
        out_shape.append(jax.ShapeDtypeStruct((t, LANES), F32))
        out_specs.append(pl.BlockSpec((tm, LANES), row))
        args.append(router_split)
    return pl.pallas_call(
        functools.partial(_out_proj_kernel, n_experts=n_experts),
        grid=(t // tm,), in_specs=in_specs, out_specs=out_specs, out_shape=out_shape,
        compiler_params=_cparams(("parallel",)), name="out_proj",
    )(*args)


def _swiglu_tiles(h, acc, wg, wu, wd, dff, tf):
    for f in range(dff // tf):
        cols = slice(f * tf, (f + 1) * tf)
        a = _dot(h, wg(cols))
        u = _dot(h, wu(cols))
        acc = acc + _dot(((a * jax.nn.sigmoid(a)) * u).astype(BF16), wd(cols))
    return acc


def _ffn_kernel(h_ref, x_ref, wg_ref, wu_ref, wd_ref, o_ref, *, tf):
    o_ref[...] = _swiglu_tiles(h_ref[...], x_ref[...], lambda c: wg_ref[:, c], lambda c: wu_ref[:, c],
                               lambda c: wd_ref[c, :], wg_ref.shape[1], tf)


def _ffn(h, x2, wg, wu, wd):
    t, d = x2.shape
    tm = min(TM_FFN, t)
    row = lambda i: (i, 0)
    resident = lambda w: pl.BlockSpec(w.shape, lambda i: (0, 0), pipeline_mode=pl.Buffered(1))
    return pl.pallas_call(
        functools.partial(_ffn_kernel, tf=TF_FFN), grid=(t // tm,),
        in_specs=[pl.BlockSpec((tm, d), row), pl.BlockSpec((tm, d), row),
                  resident(wg), resident(wu), resident(wd)],
        out_specs=pl.BlockSpec((tm, d), row),
        out_shape=jax.ShapeDtypeStruct((t, d), F32),
        compiler_params=_cparams(("parallel",)), name="ffn",
    )(h, x2, wg, wu, wd)


def _route_scan_kernel(g_ref, pos_ref, tot_ref, carry_ref):
    c = pl.program_id(0)

    @pl.when(c == 0)
    def _():
        carry_ref[...] = jnp.zeros_like(carry_ref)

    ct = g_ref.shape[0]
    routed = g_ref[...] > 0.0
    a = jnp.where(routed, 1.0, 0.0)
    earlier = lax.broadcasted_iota(jnp.int32, (ct, ct), 1) < lax.broadcasted_iota(jnp.int32, (ct, ct), 0)
    base = carry_ref[...]
    pos = jnp.where(routed, _dot(jnp.where(earlier, 1.0, 0.0).astype(BF16), a.astype(BF16)) + base, -1.0)
    pos_ref[...] = pos
    total = base + jnp.sum(a, axis=0, keepdims=True)
    carry_ref[...] = total
    tot_ref[...] = total


def _route_scan(gates):
    t = gates.shape[0]
    ct = CT_MOE
    nch = t // ct
    return pl.pallas_call(
        _route_scan_kernel, grid=(nch,),
        in_specs=[pl.BlockSpec((ct, LANES), lambda c: (c, 0))],
        out_specs=[pl.BlockSpec((ct, LANES), lambda c: (c, 0)),
                   pl.BlockSpec((1, LANES), lambda c: (0, 0))],
        out_shape=[jax.ShapeDtypeStruct((t, LANES), F32),
                   jax.ShapeDtypeStruct((1, LANES), F32)],
        scratch_shapes=[pltpu.VMEM((1, LANES), F32)],
        compiler_params=_cparams(("arbitrary",)), name="route_scan",
    )(gates)


def _sc_scatter_rows(rows, idx_a, idx_b, n_slots):
    t, w = rows.shape
    mesh = plsc.VectorSubcoreMesh(core_axis_name="core", subcore_axis_name="subcore")

    @pl.kernel(out_type=jax.ShapeDtypeStruct((n_slots, w), rows.dtype), mesh=mesh, scratch_types=[])
    def scatter(x_hbm, ia_hbm, ib_hbm, o_hbm):
        def body(x_vmem, ia_vmem, ib_vmem):
            pltpu.sync_copy(x_vmem, o_hbm.at[ia_vmem.at[0]])
            pltpu.sync_copy(x_vmem, o_hbm.at[ib_vmem.at[0]])

        pltpu.emit_pipeline(
            body, grid=(t // SC_WINDOW,),
            in_specs=[pl.BlockSpec((SC_WINDOW, w), lambda i: (i, 0)),
                      pl.BlockSpec((1, SC_WINDOW), lambda i: (i, 0)),
                      pl.BlockSpec((1, SC_WINDOW), lambda i: (i, 0))],
            out_specs=[], core_axis_name=("core", "subcore"),
            dimension_semantics=(pltpu.PARALLEL,))(x_hbm, ia_hbm, ib_hbm)

    return scatter(rows, idx_a, idx_b)


def _moe_ffn_kernel(exp_ref, rows_ref, xs_ref, wg_ref, wu_ref, wd_ref, ys_ref, *, tf):
    j = pl.program_id(0)
    n_rows = rows_ref[j]

    @pl.when(n_rows > 0)
    def _():
        xs = _unpack_bf16_halves(xs_ref[...])
        row = lax.broadcasted_iota(jnp.int32, xs.shape, 0)
        xs = jnp.where(row < n_rows, xs, jnp.zeros_like(xs))
        zero = jnp.zeros(xs.shape, F32)
        ys = _swiglu_tiles(xs, zero, lambda c: wg_ref[0, :, c], lambda c: wu_ref[0, :, c],
                           lambda c: wd_ref[0, c, :], wg_ref.shape[2], tf)
        ys_ref[...] = _pack_bf16_halves(ys)

    @pl.when(n_rows == 0)
    def _():
        ys_ref[...] = jnp.zeros_like(ys_ref)


def _moe_ffn(xs, blk_expert, blk_rows, wg, wu, wd):
    n_slots = xs.shape[0]
    d = wg.shape[1]
    expert = lambda w: pl.BlockSpec((1,) + w.shape[1:], lambda j, e, v: (e[j], 0, 0))
    grid_spec = pltpu.PrefetchScalarGridSpec(
        num_scalar_prefetch=2, grid=(n_slots // BM_MOE,),
        in_specs=[pl.BlockSpec((BM_MOE, d // 2), lambda j, e, v: (j, 0)), expert(wg), expert(wu), expert(wd)],
        out_specs=pl.BlockSpec((BM_MOE, d // 2), lambda j, e, v: (j, 0)))
    return pl.pallas_call(
        functools.partial(_moe_ffn_kernel, tf=TF_FFN), grid_spec=grid_spec,
        out_shape=jax.ShapeDtypeStruct((n_slots, d // 2), jnp.uint32),
        compiler_params=_cparams(("arbitrary",)), name="moe_ffn",
    )(blk_expert, blk_rows, xs, wg, wu, wd)


def _sc_gather_row_pairs(table, idx_a, idx_b):
    w = table.shape[1]
    t = idx_a.shape[0] * SC_WINDOW
    mesh = plsc.VectorSubcoreMesh(core_axis_name="core", subcore_axis_name="subcore")
    out = jax.ShapeDtypeStruct((t, w), table.dtype)

    @pl.kernel(out_type=(out, out), mesh=mesh, scratch_types=[])
    def gather(x_hbm, ia_hbm, ib_hbm, oa_hbm, ob_hbm):
        def body(i_vmem, o_vmem):
            pltpu.sync_copy(x_hbm.at[i_vmem.at[0]], o_vmem)

        for i_hbm, o_hbm in ((ia_hbm, oa_hbm), (ib_hbm, ob_hbm)):
            pltpu.emit_pipeline(
                body, grid=(t // SC_WINDOW,),
                in_specs=[pl.BlockSpec((1, SC_WINDOW), lambda i: (i, 0))],
                out_specs=[pl.BlockSpec((SC_WINDOW, w), lambda i: (i, 0))],
                core_axis_name=("core", "subcore"),
                dimension_semantics=(pltpu.PARALLEL,))(i_hbm, o_hbm)

    return gather(table, idx_a, idx_b)


def _token_slots(pos, pstart_row, unrouted):
    routed = pos >= 0.0
    slot = pos + pstart_row
    return (jnp.min(jnp.where(routed, slot, unrouted), axis=-1, keepdims=True),
            jnp.max(jnp.where(routed, slot, -1.0), axis=-1, keepdims=True), slot, routed)


def _moe_combine_kernel(x_ref, pos_ref, g_ref, pstart_ref, ya_ref, yb_ref, o_ref, *, n_slots):
    slot_a, slot_b, slot, routed = _token_slots(pos_ref[...], pstart_ref[...], float(n_slots))
    gates = g_ref[...]
    gate_a = jnp.sum(jnp.where(routed & (slot == slot_a), gates, 0.0), axis=-1, keepdims=True)
    gate_b = jnp.sum(jnp.where(routed & (slot == slot_b), gates, 0.0), axis=-1, keepdims=True)
    gate_b = jnp.where(slot_b == slot_a, 0.0, gate_b)

    def rows_f32(ref):
        p = ref[...]
        return jnp.concatenate([lax.bitcast_convert_type(p << 16, F32),
                                lax.bitcast_convert_type(p & jnp.uint32(0xFFFF0000), F32)], axis=-1)

    o_ref[...] = x_ref[...] + gate_a * rows_f32(ya_ref) + gate_b * rows_f32(yb_ref)


def _moe_combine(x2, pos, gates, pstart_row, ya, yb, n_slots):
    t, d = x2.shape
    tm = min(CT_MOE, t)
    row = lambda i: (i, 0)
    return pl.pallas_call(
        functools.partial(_moe_combine_kernel, n_slots=n_slots), grid=(t // tm,),
        in_specs=[pl.BlockSpec((tm, d), row), pl.BlockSpec((tm, LANES), row), pl.BlockSpec((tm, LANES), row),
                  pl.BlockSpec((1, LANES), lambda i: (0, 0)),
                  pl.BlockSpec((tm, d // 2), row), pl.BlockSpec((tm, d // 2), row)],
        out_specs=pl.BlockSpec((tm, d), row),
        out_shape=jax.ShapeDtypeStruct((t, d), F32),
        compiler_params=_cparams(("parallel",)), name="moe_combine",
    )(x2, pos, gates, pstart_row, ya, yb)


def _count_le(ascending, x):
    return jnp.sum(ascending[None, :] <= x[:, None], axis=1).astype(jnp.int32)


def _moe_routed(h, x2, gates, wg, wu, wd):
    t, d = x2.shape
    n_e = wg.shape[0]
    n_slots = t * TOP_K + n_e * BM_MOE
    pos, tot = _route_scan(gates)

    counts = tot[0, :n_e].astype(jnp.int32)
    padded = (counts + BM_MOE - 1) // BM_MOE * BM_MOE
    pend = jnp.cumsum(padded)
    pstart = pend - padded

    pstart_row = jnp.zeros((1, LANES), F32).at[0, :n_e].set(pstart.astype(F32))
    slot_a, slot_b, _, _ = _token_slots(pos, pstart_row, float(n_slots))
    slot_a = slot_a.astype(jnp.int32).reshape(-1, SC_WINDOW)
    slot_b = slot_b.astype(jnp.int32).reshape(-1, SC_WINDOW)
    xs = _sc_scatter_rows(h, slot_a, slot_b, n_slots)

    mb = jnp.arange(n_slots // BM_MOE, dtype=jnp.int32) * BM_MOE
    mb_e = jnp.minimum(_count_le(pend, mb), n_e - 1)
    mb_rows = jnp.where(mb < pend[-1], jnp.clip(counts[mb_e] - (mb - pstart[mb_e]), 0, BM_MOE), 0)
    ys = _moe_ffn(xs, mb_e, mb_rows.astype(jnp.int32), wg, wu, wd)

    ya, yb = _sc_gather_row_pairs(ys, slot_a, slot_b)
    return _moe_combine(x2, pos, gates, pstart_row, ya, yb, n_slots)


def _rope_tables(pos):
    half = ROPE_DIM // 2
    inv_freq = ROPE_THETA ** (-2.0 * jnp.arange(half, dtype=F32) / ROPE_DIM)
    dim = np.arange(LANES) % HEAD_DIM
    freq = jnp.where(dim < ROPE_DIM, inv_freq[dim % half], 0.0)
    sign = np.where(dim < half, -1.0, np.where(dim < ROPE_DIM, 1.0, 0.0)).astype(np.float32)
    ang = pos.astype(F32).reshape(-1, 1) * freq[None, :]
    return jnp.cos(ang), jnp.sin(ang) * sign[None, :]


def _permute_w_in(w, conv_w):
    d = w.shape[0]
    kv_end = Q_W + 6 * KV_W
    g = w[:, kv_end:kv_end + N_HEADS * N_BRANCH]
    pad = jnp.zeros((d, LANES - GATES_PER_KV), w.dtype)
    gate_cols = []
    for kh in range(N_KV_HEADS):
        gate_cols += [g[:, kh * GATES_PER_KV:(kh + 1) * GATES_PER_KV], pad]
    u = w[:, kv_end + N_HEADS * N_BRANCH:]
    return jnp.concatenate([w[:, :kv_end]] + gate_cols + [u], axis=1).astype(BF16)


def _compress_weights(pos_emb, w1, w2):
    hidden = w1.shape[1]
    eye = jnp.eye(N_KV_HEADS, dtype=w1.dtype)
    w1r = w1.reshape(CMP_LEN, HEAD_DIM, hidden)
    halves = []
    for part in (w1r[:CMP_STRIDE], w1r[CMP_STRIDE:]):
        full = jnp.einsum('ldj,hg->lhdgj', part, eye)
        halves.append(full.reshape(CMP_STRIDE * N_KV_HEADS * HEAD_DIM, N_KV_HEADS * hidden).astype(BF16))
    w2p = jnp.einsum('jd,hg->hjgd', w2, eye).reshape(N_KV_HEADS * hidden, N_KV_HEADS * HEAD_DIM).astype(BF16)
    pos = []
    for part in (pos_emb[:CMP_STRIDE], pos_emb[CMP_STRIDE:]):
        pos.append(jnp.broadcast_to(part[:, None, :], (CMP_STRIDE, N_KV_HEADS, HEAD_DIM)).reshape(1, -1))
    return [halves[0], halves[1], w2p, pos[0], pos[1]]


def _selection_map_t(seq):
    ncp = seq // CMP_STRIDE
    n_cmp = (seq - CMP_LEN) // CMP_STRIDE + 1
    c0 = np.arange(ncp) * CMP_STRIDE
    s0 = np.arange(seq // SEL_LEN) * SEL_LEN
    ov = np.minimum(c0[None, :] + CMP_LEN, s0[:, None] + SEL_LEN) - np.maximum(c0[None, :], s0[:, None])
    m = np.clip(ov, 0, None) / CMP_LEN
    m[:, n_cmp:] = 0.0
    return jnp.asarray(m, dtype=BF16)


def kernel(x, positions, attn_norm_g, ffn_norm_g, w_in, w_out, q_norm_g, k_norm_g, cmp_pos_k, cmp_w1_k, cmp_w2_k, cmp_pos_v, cmp_w1_v, cmp_w2_v, conv_w, conv_b, conv_ln_g, conv_ln_b, ffn_w_gate, ffn_w_up, ffn_w_down, moe_router, moe_w_gate, moe_w_up, moe_w_down):
    b, seq, d = x.shape
    t = b * seq
    depth = w_in.shape[0]
    cw = conv_w.shape[2]
    ncp = seq // CMP_STRIDE
    n_cmp = (seq - CMP_LEN) // CMP_STRIDE + 1
    assert seq % max(TQ, KC, TM_PROJ) == 0 and seq >= WINDOW + TQ

    cos_t, sin_t = _rope_tables(positions)
    cmp_end = np.minimum(np.arange(ncp) * CMP_STRIDE + CMP_LEN - 1, seq - 1)
    cos_c, sin_c = _rope_tables(positions[:, cmp_end])
    selmap_t = _selection_map_t(seq)
    tile2 = lambda v: jnp.tile(v.reshape(1, HEAD_DIM), (1, HEADS_PER_VREG))

    x2 = x.reshape(t, d)
    for layer in range(depth):
        w_perm = _permute_w_in(w_in[layer], cw)
        conv_params = (conv_w[layer], conv_b[layer].reshape(1, cw), conv_ln_g[layer].reshape(1, cw),
                       conv_ln_b[layer].reshape(1, cw))
        q, kc, vc, ks, vs, kw, vw, gates, conv = _in_proj(
            x2, attn_norm_g[layer].reshape(1, d), w_perm, cos_t, sin_t,
            tile2(q_norm_g[layer]) * (HEAD_DIM ** -0.5 * LOG2_E),
            tile2(k_norm_g[layer, 1]), tile2(k_norm_g[layer, 2]), conv_params, seq)
        kcmp, vcmp = _compress(
            kc, vc,
            _compress_weights(cmp_pos_k[layer], cmp_w1_k[layer], cmp_w2_k[layer]),
            _compress_weights(cmp_pos_v[layer], cmp_w1_v[layer], cmp_w2_v[layer]),
            tile2(k_norm_g[layer, 0]), cos_c, sin_c, b, ncp)
        score_bound = (HEAD_DIM ** 0.5 * LOG2_E * SCORE_BOUND_MARGIN * jnp.max(jnp.abs(q_norm_g[layer]))
                       * jnp.max(jnp.abs(k_norm_g[layer]))).astype(F32).reshape(1)
        attn = _attention(score_bound, q, kcmp, vcmp, ks, vs, kw, vw, gates, selmap_t, b, seq)
        wo = w_out[layer].astype(BF16)
        g2 = ffn_norm_g[layer].reshape(1, d)
        i = layer // 2
        if layer % 2 == 0:
            x2, h = _out_proj(attn, conv, wo, x2, g2)
            x2 = _ffn(h, x2, ffn_w_gate[i].astype(BF16), ffn_w_up[i].astype(BF16), ffn_w_down[i].astype(BF16))
        else:
            n_e = moe_router.shape[2]
            r = jnp.pad(moe_router[i], ((0, 0), (0, LANES - n_e)))
            r_hi = r.astype(BF16)
            r_lo = (r - r_hi.astype(F32)).astype(BF16)
            x2, h, route = _out_proj(attn, conv, wo, x2, g2, jnp.concatenate([r_hi, r_lo], axis=1), n_e)
            x2 = _moe_routed(h, x2, route, moe_w_gate[i].astype(BF16), moe_w_up[i].astype(BF16),
                             moe_w_down[i].astype(BF16))
    return x2.reshape(b, seq, d)
```
